```python
import math
import jax
import jax.numpy as jnp
from jax import lax
import numpy as np


D_MODEL = 2048
BATCH = 4
SEQ = 4096
DEPTH = 4

GRID_W = 64
CTX_LEN = 256
HEAD_DIM = 128
HY_WIDTH = D_MODEL // 2
HY_ORDER = 2
HY_EMB = 33
HY_BANDS = (HY_EMB - 1) // 2
HY_FILTER_HIDDEN = 64
HY_FAST_DECAY = 0.3
HY_SLOW_DECAY = 1.5
HY_DECAY_TARGET = 1e-2
HY_MAX_DECAY = math.log(HY_DECAY_TARGET) / HY_FAST_DECAY
HY_MIN_DECAY = math.log(HY_DECAY_TARGET) / HY_SLOW_DECAY
NA_HEADS = D_MODEL // (4 * HEAD_DIM)
NA_WIN_R = 8
NA_WIN_C = 16
GQA_Q_HEADS = D_MODEL // (4 * HEAD_DIM)
GQA_KV_HEADS = GQA_Q_HEADS // 2
GQA_GROUP = GQA_Q_HEADS // GQA_KV_HEADS
GQA_WINDOW = 128
GQA_BLOCK = 128
ROPE_BASE = 10000.0
D_FF = 5632
N_BRANCH = 3
NORM_EPS = 1e-6
NA_W = NA_HEADS * HEAD_DIM
GQA_QW = GQA_Q_HEADS * HEAD_DIM
GQA_KVW = GQA_KV_HEADS * HEAD_DIM
MIX_WIDTH = HY_WIDTH + NA_W + GQA_QW
KV_COLS = 2 * NA_W + 2 * GQA_KVW
IN_SIZES = (NA_W, NA_W, GQA_KVW, GQA_KVW, 3 * HY_WIDTH, NA_W, GQA_QW, N_BRANCH * D_MODEL)
N_IN = sum(IN_SIZES)

kernel_name = 'hybrid_hyena_natten_swa_dit_block'

F32 = jnp.float32
NEG_INF = -1e30


def _split(a, sizes, axis=-1):
    out, o = [], 0
    for n in sizes:
        out.append(lax.slice_in_dim(a, o, o + n, axis=axis))
        o += n
    return out


def _heads(t, n):
    return t.reshape(t.shape[:-1] + (n, HEAD_DIM))


def rms_norm(x, g):
    xf = x.astype(F32)
    y = xf * lax.rsqrt(jnp.mean(xf * xf, axis=-1, keepdims=True) + NORM_EPS)
    return (y * g.astype(F32)).astype(x.dtype)


def modulate(xn, shift, scale):
    return xn * (1 + scale) + shift


def dwconv3(x, w, b):
    xp = jnp.pad(x, ((0, 0), (1, 1), (0, 0)))
    return xp[:, :-2] * w[0] + xp[:, 1:-1] * w[1] + xp[:, 2:] * w[2] + b


def axial_rope_tables(n):
    t = jnp.arange(n)
    row = (t // GRID_W).astype(F32)
    col = (t % GRID_W).astype(F32)
    per_axis = HEAD_DIM // 2
    inv = ROPE_BASE ** (-jnp.arange(0, per_axis, 2, dtype=F32) / per_axis)
    ang = jnp.stack([row[:, None] * inv, col[:, None] * inv], axis=1)
    return jnp.cos(ang), jnp.sin(ang)


def apply_axial_rope(x, cos, sin):
    qd = HEAD_DIM // 4
    n = x.shape[1]
    shp = (1, n) + (1,) * (x.ndim - 3) + (2, qd)
    c, s = cos.reshape(shp), sin.reshape(shp)
    xr = x.astype(F32).reshape(x.shape[:-1] + (2, 2, qd))
    x1, x2 = xr[..., 0, :], xr[..., 1, :]
    out = jnp.stack([x1 * c - x2 * s, x2 * c + x1 * s], axis=-2)
    return out.reshape(x.shape).astype(x.dtype)


def hyena_filters(n, w1, b1, freq, w2, b2, w3):
    t = jnp.linspace(0.0, 1.0, n, dtype=F32)[:, None]
    w = (2.0 * math.pi / n) * jnp.arange(n, dtype=F32)[:, None]
    f = jnp.linspace(1e-4, HY_BANDS - 1, HY_BANDS, dtype=F32)[None, :]
    z = jnp.concatenate([t, jnp.cos(f * w), -jnp.sin(f * w)], axis=-1)
    a = jnp.sin(freq[0].astype(F32) * (z @ w1.astype(F32) + b1.astype(F32)))
    a = jnp.sin(freq[1].astype(F32) * (a @ w2.astype(F32) + b2.astype(F32)))
    hf = (a @ w3.astype(F32)).reshape(n, 2, HY_ORDER * HY_WIDTH)
    deltas = jnp.linspace(HY_MIN_DECAY, HY_MAX_DECAY, HY_ORDER * HY_WIDTH, dtype=F32)
    hf = hf * jnp.exp(-t[:, :, None] * jnp.abs(deltas))
    fwd, bwd = hf[:, 0], hf[:, 1]
    k = jnp.concatenate([fwd, jnp.zeros_like(fwd[:1]), bwd[:0:-1]], axis=0)
    k = k / jnp.sum(jnp.abs(k), axis=0, keepdims=True)
    return k.reshape(2 * n, HY_ORDER, HY_WIDTH)


def hyena_mixer(u, conv_w, conv_b, w1, b1, freq, w2, b2, w3, skip):
    n = u.shape[1]
    u = dwconv3(u, conv_w, conv_b)
    v, x1, x2 = jnp.split(u.astype(F32), 3, axis=-1)
    kf = jnp.fft.rfft(hyena_filters(n, w1, b1, freq, w2, b2, w3), axis=0)

    def long_conv(z, o):
        zf = jnp.fft.rfft(z, n=2 * n, axis=1)
        y = jnp.fft.irfft(zf * kf[None, :, o], n=2 * n, axis=1)[:, :n]
        return y + skip[o].astype(F32) * z

    z = x1 * long_conv(v, 0)
    y = x2 * long_conv(z, 1)
    return y.astype(u.dtype)


def na_latent(q, k, v, k_ctx, v_ctx, rpb):
    b, s, h, d = q.shape
    rows = s // GRID_W
    kr = min(NA_WIN_R, rows)
    kc = NA_WIN_C
    r = jnp.arange(rows)
    row_idx = jnp.clip(r - kr // 2, 0, rows - kr)[:, None] + jnp.arange(kr)[None, :]
    cidx = jnp.arange(GRID_W)
    col_start = jnp.clip(cidx - kc // 2, 0, GRID_W - kc)
    col_ok = (cidx[None, :] >= col_start[:, None]) & (cidx[None, :] < col_start[:, None] + kc)
    qg = q.reshape(b, rows, GRID_W, h, d)
    kg = k.reshape(b, rows, GRID_W, h, d)[:, row_idx]
    vg = v.reshape(b, rows, GRID_W, h, d)[:, row_idx]
    scale = HEAD_DIM ** -0.5
    s_loc = jnp.einsum('brqhd,brikhd->brhqik', qg, kg).astype(F32) * scale
    dr = row_idx - r[:, None] + (NA_WIN_R - 1)
    dc = jnp.clip(cidx[None, :] - cidx[:, None] + (NA_WIN_C - 1), 0, 2 * NA_WIN_C - 2)
    bias = rpb.astype(F32)[:, dr[:, None, :, None], dc[None, :, None, :]]
    s_loc = jnp.where(col_ok[:, None, :], s_loc + jnp.moveaxis(bias, 0, 1)[None], NEG_INF)
    s_ctx = jnp.einsum('brqhd,bchd->brhqc', qg, k_ctx).astype(F32) * scale
    nloc = kr * GRID_W
    p = jax.nn.softmax(jnp.concatenate([s_loc.reshape(b, rows, h, GRID_W, nloc), s_ctx], axis=-1), axis=-1)
    p_loc = p[..., :nloc].reshape(b, rows, h, GRID_W, kr, GRID_W).astype(v.dtype)
    p_ctx = p[..., nloc:].astype(v.dtype)
    out = jnp.einsum('brhqik,brikhd->brqhd', p_loc, vg) + jnp.einsum('brhqc,bchd->brqhd', p_ctx, v_ctx)
    return out.reshape(b, s, h * d)


def swa_latent(q, k, v, k_ctx, v_ctx, sink):
    b, s, hk, g, d = q.shape
    bl = GQA_BLOCK
    nb = s // bl
    qb = q.reshape(b, nb, bl, hk, g, d)

    def band(t):
        tp = jnp.pad(t, ((0, 0), (bl, bl), (0, 0), (0, 0))).reshape(b, nb + 2, bl, hk, d)
        return jnp.concatenate([tp[:, :-2], tp[:, 1:-1], tp[:, 2:]], axis=2)

    kb, vb = band(k), band(v)
    blk = jnp.arange(nb)[:, None, None] * bl
    qpos = blk + jnp.arange(bl)[None, :, None]
    kpos = blk + jnp.arange(3 * bl)[None, None, :] - bl
    ok = (jnp.abs(qpos - kpos) <= GQA_WINDOW) & (kpos >= 0) & (kpos < s)
    scale = HEAD_DIM ** -0.5
    s_loc = jnp.einsum('bnqhgd,bnkhd->bnhgqk', qb, kb).astype(F32) * scale
    s_loc = jnp.where(ok[None, :, None, None], s_loc, NEG_INF)
    s_ctx = jnp.einsum('bnqhgd,bchd->bnhgqc', qb, k_ctx).astype(F32) * scale
    sk = jnp.broadcast_to(sink.astype(F32).reshape(1, 1, hk, g, 1, 1), s_loc.shape[:-1] + (1,))
    p = jax.nn.softmax(jnp.concatenate([s_loc, s_ctx, sk], axis=-1), axis=-1)
    n_ctx = k_ctx.shape[1]
    p_loc = p[..., :3 * bl].astype(v.dtype)
    p_ctx = p[..., 3 * bl:3 * bl + n_ctx].astype(v.dtype)
    out = jnp.einsum('bnhgqk,bnkhd->bnqhgd', p_loc, vb) + jnp.einsum('bnhgqc,bchd->bnqhgd', p_ctx, v_ctx)
    return out.reshape(b, s, hk * g * d)


def dense_ctx_attn(q, k, v, sink):
    b, n, hk, g, d = q.shape
    sc = jnp.einsum('bqhgd,bkhd->bhgqk', q, k).astype(F32) * (HEAD_DIM ** -0.5)
    if sink is not None:
        sk = jnp.broadcast_to(sink.astype(F32).reshape(1, hk, g, 1, 1), sc.shape[:-1] + (1,))
        sc = jnp.concatenate([sc, sk], axis=-1)
    p = jax.nn.softmax(sc, axis=-1)[..., :n].astype(v.dtype)
    return jnp.einsum('bhgqk,bkhd->bqhgd', p, v).reshape(b, n, hk * g * d)


def merge_branches(gate_logits, y_hy, y_na, y_sw, w_branch, w_out):
    g_hy, g_na, g_sw = jnp.split(jax.nn.sigmoid(gate_logits.astype(F32)).astype(y_hy.dtype), N_BRANCH, axis=-1)
    wb_hy, wb_na, wb_sw = _split(w_branch, (HY_WIDTH, NA_W, GQA_QW), axis=0)
    m = g_hy * (y_hy @ wb_hy) + g_na * (y_na @ wb_na) + g_sw * (y_sw @ wb_sw)
    return m @ w_out


def context_kv(hc, w_in):
    na_k, na_v, sw_k, sw_v = _split(hc @ w_in[:, :KV_COLS], IN_SIZES[:4])
    return (_heads(na_k, NA_HEADS), _heads(na_v, NA_HEADS), _heads(sw_k, GQA_KV_HEADS), _heads(sw_v, GQA_KV_HEADS))


def latent_mixers(h, kv_c, w_in, hy_p, rpb, sink, w_branch, w_out, cos, sin):
    b, s, _ = h.shape
    na_k, na_v, sw_k, sw_v, hy_u, na_q, sw_q, gates = _split(h @ w_in, IN_SIZES)
    na_kc, na_vc, sw_kc, sw_vc = kv_c
    y_hy = hyena_mixer(hy_u, *hy_p)
    y_na = na_latent(_heads(na_q, NA_HEADS), _heads(na_k, NA_HEADS), _heads(na_v, NA_HEADS), na_kc, na_vc, rpb)
    q = apply_axial_rope(sw_q.reshape(b, s, GQA_KV_HEADS, GQA_GROUP, HEAD_DIM), cos, sin)
    k = apply_axial_rope(_heads(sw_k, GQA_KV_HEADS), cos, sin)
    y_sw = swa_latent(q, k, _heads(sw_v, GQA_KV_HEADS), sw_kc, sw_vc, sink)
    return merge_branches(gates, y_hy, y_na, y_sw, w_branch, w_out)


def context_mixers(hc, kv_c, w_in, hy_p, sink, w_branch, w_out):
    b, n, _ = hc.shape
    hy_u, na_q, sw_q, gates = _split(hc @ w_in[:, KV_COLS:], IN_SIZES[4:])
    na_kc, na_vc, sw_kc, sw_vc = kv_c
    y_hy = hyena_mixer(hy_u, *hy_p)
    y_na = dense_ctx_attn(_heads(na_q, NA_HEADS)[:, :, :, None, :], na_kc, na_vc, None)
    y_sw = dense_ctx_attn(sw_q.reshape(b, n, GQA_KV_HEADS, GQA_GROUP, HEAD_DIM), sw_kc, sw_vc, sink)
    return merge_branches(gates, y_hy, y_na, y_sw, w_branch, w_out)


def conv_glu(h, w_up, conv_w, conv_b, w_down):
    a, u = jnp.split(h @ w_up, 2, axis=-1)
    return (jax.nn.silu(dwconv3(a, conv_w, conv_b)) * u) @ w_down


def setup_inputs(seed: int = 0) -> dict:
    key = jax.random.key(seed)
    ks = jax.random.split(key, 32)
    L, D = DEPTH, D_MODEL

    def nrm(k, shape, s):
        return jax.random.normal(k, shape, jnp.float32) * s

    return {
        'x': nrm(ks[0], (BATCH, SEQ, D), 1.0),
        'c': nrm(ks[1], (BATCH, D), 1.0),
        'ctx': nrm(ks[2], (BATCH, CTX_LEN, D), 1.0),
        'c_ctx': nrm(ks[3], (D,), 1.0),
        'w_mod': nrm(ks[4], (L, D, 6 * D), 0.5 * D ** -0.5),
        'b_mod': nrm(ks[5], (L, 6 * D), 0.02),
        'norm_gains': 1.0 + nrm(ks[6], (L, 4, D), 0.02),
        'w_in': nrm(ks[7], (L, D, N_IN), D ** -0.5),
        'hy_conv_w': nrm(ks[8], (L, 3, 3 * HY_WIDTH), 3 ** -0.5),
        'hy_conv_b': nrm(ks[9], (L, 3 * HY_WIDTH), 0.02),
        'hy_w1': nrm(ks[10], (L, HY_EMB, HY_FILTER_HIDDEN), HY_EMB ** -0.5),
        'hy_b1': nrm(ks[11], (L, HY_FILTER_HIDDEN), 0.02),
        'hy_freq': 1.0 + nrm(ks[12], (L, 2, HY_FILTER_HIDDEN), 0.02),
        'hy_w2': nrm(ks[13], (L, HY_FILTER_HIDDEN, HY_FILTER_HIDDEN), HY_FILTER_HIDDEN ** -0.5),
        'hy_b2': nrm(ks[14], (L, HY_FILTER_HIDDEN), 0.02),
        'hy_w3': nrm(ks[15], (L, HY_FILTER_HIDDEN, 2 * HY_ORDER * HY_WIDTH), HY_FILTER_HIDDEN ** -0.5),
        'hy_skip': nrm(ks[16], (L, HY_ORDER, HY_WIDTH), 0.1),
        'na_rpb': nrm(ks[17], (L, NA_HEADS, 2 * NA_WIN_R - 1, 2 * NA_WIN_C - 1), 0.1),
        'swa_sink': nrm(ks[18], (L, GQA_Q_HEADS), 0.5),
        'w_branch': jnp.concatenate([nrm(ks[19], (L, HY_WIDTH, D), HY_WIDTH ** -0.5),
                                     nrm(ks[20], (L, NA_W, D), NA_W ** -0.5),
                                     nrm(ks[21], (L, GQA_QW, D), GQA_QW ** -0.5)], axis=1),
        'w_out': nrm(ks[22], (L, D, D), D ** -0.5),
        'ffn_w_up': nrm(ks[23], (L, D, 2 * D_FF), D ** -0.5),
        'ffn_conv_w': nrm(ks[24], (L, 3, D_FF), 3 ** -0.5),
        'ffn_conv_b': nrm(ks[25], (L, D_FF), 0.02),
        'ffn_w_down': nrm(ks[26], (L, D_FF, D), D_FF ** -0.5),
    }


def reference(x, c, ctx, c_ctx, w_mod, b_mod, norm_gains, w_in, hy_conv_w, hy_conv_b, hy_w1, hy_b1,
              hy_freq, hy_w2, hy_b2, hy_w3, hy_skip, na_rpb, swa_sink, w_branch, w_out,
              ffn_w_up, ffn_conv_w, ffn_conv_b, ffn_w_down):
    cos, sin = axial_rope_tables(x.shape[1])
    xc = ctx
    silu_c = jax.nn.silu(c)
    silu_cc = jax.nn.silu(c_ctx)
    for l in range(DEPTH):
        mod = [m[:, None, :] for m in _split(silu_c @ w_mod[l] + b_mod[l], (D_MODEL,) * 6)]
        mod_c = _split(silu_cc @ w_mod[l] + b_mod[l], (D_MODEL,) * 6)
        g = norm_gains[l]
        hy_p = (hy_conv_w[l], hy_conv_b[l], hy_w1[l], hy_b1[l], hy_freq[l], hy_w2[l], hy_b2[l], hy_w3[l], hy_skip[l])
        hc = modulate(rms_norm(xc, g[0]), mod_c[0], mod_c[1])
        kv_c = context_kv(hc, w_in[l])
        h = modulate(rms_norm(x, g[0]), mod[0], mod[1])
        mix = latent_mixers(h, kv_c, w_in[l], hy_p, na_rpb[l], swa_sink[l], w_branch[l], w_out[l], cos, sin)
        x = x + mod[2] * rms_norm(mix, g[1])
        hf = modulate(rms_norm(x, g[2]), mod[3], mod[4])
        x = x + mod[5] * rms_norm(conv_glu(hf, ffn_w_up[l], ffn_conv_w[l], ffn_conv_b[l], ffn_w_down[l]), g[3])
        if l < DEPTH - 1:
            mix_c = context_mixers(hc, kv_c, w_in[l], hy_p, swa_sink[l], w_branch[l], w_out[l])
            xc = xc + mod_c[2] * rms_norm(mix_c, g[1])
            hfc = modulate(rms_norm(xc, g[2]), mod_c[3], mod_c[4])
            xc = xc + mod_c[5] * rms_norm(conv_glu(hfc, ffn_w_up[l], ffn_conv_w[l], ffn_conv_b[l], ffn_w_down[l]), g[3])
    return x
```

```python
import functools
import math

import numpy as np
import jax
import jax.numpy as jnp
from jax import lax
from jax.experimental import pallas as pl
from jax.experimental.pallas import tpu as pltpu

F32 = jnp.float32
BF16 = jnp.bfloat16
NEG_INF = -1e30

GRID_W = 64
HEAD_DIM = 128
HY_EMB = 33
HY_BANDS = (HY_EMB - 1) // 2
HY_FAST_DECAY = 0.3
HY_SLOW_DECAY = 1.5
HY_DECAY_TARGET = 1e-2
HY_MAX_DECAY = math.log(HY_DECAY_TARGET) / HY_FAST_DECAY
HY_MIN_DECAY = math.log(HY_DECAY_TARGET) / HY_SLOW_DECAY
NA_WIN_R = 8
NA_WIN_C = 16
GQA_GROUP = 2
GQA_WINDOW = 128
ROPE_BASE = 10000.0
NORM_EPS = 1e-6

LANES = 128
FFT_N2 = 128
VMEM_LIMIT = 56 * 1024 * 1024
HP = lax.Precision.HIGHEST


def _call(body, *, grid, in_specs, out_specs, out_shape, scratch=(), sem, name):
    return pl.pallas_call(
        body, grid=grid, in_specs=in_specs, out_specs=out_specs, out_shape=out_shape,
        scratch_shapes=list(scratch),
        compiler_params=pltpu.CompilerParams(dimension_semantics=sem, vmem_limit_bytes=VMEM_LIMIT),
        name=name)


def _tile(n, cands):
    for c in cands:
        if n % c == 0:
            return c
    raise ValueError(f"no tile for {n} in {cands}")


def _sigmoid(x):
    return 1.0 / (1.0 + jnp.exp(-x))


def _mod_body(c_ref, w_ref, b_ref, o_ref):
    c = c_ref[...]
    s = (c * _sigmoid(c)).astype(BF16)
    o_ref[0] = jnp.dot(s, w_ref[0].astype(BF16), preferred_element_type=F32) + b_ref[0]


def _modulation(cc, w_mod, b_mod):
    L, D, N = w_mod.shape
    R = cc.shape[0]
    tn = _tile(N, (1024, 512, 256, 128))
    return _call(
        _mod_body, grid=(L, N // tn),
        in_specs=[pl.BlockSpec((R, D), lambda l, j: (0, 0)),
                  pl.BlockSpec((1, D, tn), lambda l, j: (l, 0, j)),
                  pl.BlockSpec((1, 1, tn), lambda l, j: (l, 0, j))],
        out_specs=pl.BlockSpec((1, R, tn), lambda l, j: (l, 0, j)),
        out_shape=jax.ShapeDtypeStruct((L, R, N), F32),
        sem=("parallel", "parallel"), name="modulation")(cc, w_mod, b_mod.reshape(L, 1, N))


def _swap32(a):
    lane = lax.broadcasted_iota(jnp.int32, a.shape, 1)
    return jnp.where((lane & 32) == 0, pltpu.roll(a, 96, 1), pltpu.roll(a, 32, 1))


def _nm_body(*refs, rope_tiles, eps):
    if rope_tiles:
        x_ref, g_ref, sh_ref, sc_ref, w_ref, cos_ref, sin_ref, o_ref, h_ref = refs
    else:
        x_ref, g_ref, sh_ref, sc_ref, w_ref, o_ref, h_ref = refs
    j = pl.program_id(2)

    @pl.when(j == 0)
    def _():
        x = x_ref[0]
        y = x * lax.rsqrt(jnp.mean(x * x, axis=-1, keepdims=True) + eps) * g_ref[...]
        h_ref[...] = (y * (1.0 + sc_ref[0]) + sh_ref[0]).astype(BF16)

    acc = jnp.dot(h_ref[...], w_ref[...], preferred_element_type=F32)
    if not rope_tiles:
        o_ref[0] = acc.astype(o_ref.dtype)
        return
    is_rope = functools.reduce(jnp.logical_or, [j == t for t in rope_tiles])

    @pl.when(is_rope)
    def _():
        c, s = cos_ref[...], sin_ref[...]
        for k in range(acc.shape[1] // LANES):
            a = acc[:, k * LANES:(k + 1) * LANES]
            o_ref[0, :, k * LANES:(k + 1) * LANES] = (a * c + _swap32(a) * s).astype(o_ref.dtype)

    @pl.when(jnp.logical_not(is_rope))
    def _():
        o_ref[0] = acc.astype(o_ref.dtype)


def _colmap(segs, tn):
    bounds, o = [], 0
    for s, w in segs:
        assert s % tn == 0 and w % tn == 0
        bounds.append(((o + w) // tn, (s - o) // tn))
        o += w

    def f(j):
        r = j + bounds[-1][1]
        for hi, off in reversed(bounds[:-1]):
            r = jnp.where(j < hi, j + off, r)
        return r
    return f, o


def _norm_mod_matmul(x, gain, shift, scale, w, l, segs, out_dtype, tn, rope=None, name="norm_mod_matmul"):
    B, S, D = x.shape
    tm = _tile(S, (512, 256, 128))
    cmap, n_out = _colmap(segs, tn)
    in_specs = [pl.BlockSpec((1, tm, D), lambda b, i, j: (b, i, 0)),
                pl.BlockSpec((1, D), lambda b, i, j: (0, 0)),
                pl.BlockSpec((1, 1, D), lambda b, i, j: (b, 0, 0)),
                pl.BlockSpec((1, 1, D), lambda b, i, j: (b, 0, 0)),
                pl.BlockSpec((None, D, tn), lambda b, i, j: (l, 0, cmap(j)))]
    args = [x, gain.reshape(1, D), shift, scale, w]
    rope_tiles = ()
    if rope is not None:
        cos_t, sin_t, rope_tiles = rope
        in_specs += [pl.BlockSpec((tm, LANES), lambda b, i, j: (i, 0)),
                     pl.BlockSpec((tm, LANES), lambda b, i, j: (i, 0))]
        args += [cos_t, sin_t]
    return _call(
        functools.partial(_nm_body, rope_tiles=tuple(rope_tiles), eps=NORM_EPS),
        grid=(B, S // tm, n_out // tn), in_specs=in_specs,
        out_specs=pl.BlockSpec((1, tm, tn), lambda b, i, j: (b, i, j)),
        out_shape=jax.ShapeDtypeStruct((B, S, n_out), out_dtype),
        scratch=[pltpu.VMEM((tm, D), BF16)],
        sem=("parallel", "parallel", "arbitrary"), name=name)(*args)


def _conv3(u, w_ref, b_ref):
    n = u.shape[0]
    r = lax.broadcasted_iota(jnp.int32, u.shape, 0)
    up = jnp.where(r == 0, 0.0, pltpu.roll(u, 1, 0))
    un = jnp.where(r == n - 1, 0.0, pltpu.roll(u, n - 1, 0))
    return up * w_ref[0:1, :] + u * w_ref[1:2, :] + un * w_ref[2:3, :] + b_ref[...]


def _hyconv_body(u0, u1, u2, w0, w1, w2, b0, b1, b2, o0, o1, o2):
    for u, w, b, o in ((u0, w0, b0, o0), (u1, w1, b1, o1), (u2, w2, b2, o2)):
        o[0] = _conv3(u[0], w, b)


def _hyena_dwconv(u, conv_w, conv_b):
    B, n, C3 = u.shape
    C = C3 // 3
    tc = _tile(C, (256, 128))
    nc = C // tc
    cb = conv_b.reshape(1, C3)
    uspec = [pl.BlockSpec((1, n, tc), functools.partial(lambda b, c, k: (b, 0, k * nc + c), k=k)) for k in range(3)]
    wspec = [pl.BlockSpec((3, tc), functools.partial(lambda b, c, k: (0, k * nc + c), k=k)) for k in range(3)]
    bspec = [pl.BlockSpec((1, tc), functools.partial(lambda b, c, k: (0, k * nc + c), k=k)) for k in range(3)]
    ospec = [pl.BlockSpec((1, n, tc), lambda b, c: (b, 0, c)) for _ in range(3)]
    return _call(
        _hyconv_body, grid=(B, nc), in_specs=uspec + wspec + bspec, out_specs=ospec,
        out_shape=[jax.ShapeDtypeStruct((B, n, C), F32)] * 3,
        sem=("parallel", "parallel"), name="hyena_dwconv")(u, u, u, conv_w, conv_w, conv_w, cb, cb, cb)


def _glu_body(a_ref, u_ref, w_ref, b_ref, o_ref):
    c = _conv3(a_ref[0], w_ref, b_ref)
    o_ref[0] = (c * _sigmoid(c) * u_ref[0]).astype(o_ref.dtype)


def _conv_glu_gate(au, conv_w, conv_b):
    B, n, F2 = au.shape
    Fd = F2 // 2
    tc = _tile(Fd, (256, 128))
    nc = Fd // tc
    return _call(
        _glu_body, grid=(B, nc),
        in_specs=[pl.BlockSpec((1, n, tc), lambda b, c: (b, 0, c)),
                  pl.BlockSpec((1, n, tc), lambda b, c: (b, 0, nc + c)),
                  pl.BlockSpec((3, tc), lambda b, c: (0, c)),
                  pl.BlockSpec((1, tc), lambda b, c: (0, c))],
        out_specs=pl.BlockSpec((1, n, tc), lambda b, c: (b, 0, c)),
        out_shape=jax.ShapeDtypeStruct((B, n, Fd), BF16),
        sem=("parallel", "parallel"), name="conv_glu_gate")(au, au, conv_w, conv_b.reshape(1, Fd))


def _filt_trunk_body(z_ref, w1_ref, b1_ref, fr_ref, w2_ref, b2_ref, o_ref):
    a = jnp.dot(z_ref[...], w1_ref[...], precision=HP, preferred_element_type=F32) + b1_ref[...]
    a = jnp.sin(fr_ref[0:1, :] * a)
    a = jnp.dot(a, w2_ref[...], precision=HP, preferred_element_type=F32) + b2_ref[...]
    o_ref[...] = jnp.sin(fr_ref[1:2, :] * a)


def _filt_main_body(a_ref, wf_ref, wb_ref, t_ref, dl_ref, o_ref, *, n):
    a = a_ref[...]
    hf = jnp.dot(a, wf_ref[...], precision=HP, preferred_element_type=F32)
    hb = jnp.dot(a, wb_ref[...], precision=HP, preferred_element_type=F32)
    r = lax.broadcasted_iota(jnp.int32, hf.shape, 0)
    k = jnp.where(r < n, hf, jnp.where(r > n, hb, 0.0)) * jnp.exp(-t_ref[...] * dl_ref[...])
    o_ref[...] = k / jnp.sum(jnp.abs(k), axis=0, keepdims=True)


def _hyena_filters(n, w1, b1, freq, w2, b2, w3):
    Hd = w1.shape[1]
    OC = w3.shape[1] // 2
    t = jnp.linspace(0.0, 1.0, n, dtype=F32)[:, None]
    w = (2.0 * math.pi / n) * jnp.arange(n, dtype=F32)[:, None]
    f = jnp.linspace(1e-4, HY_BANDS - 1, HY_BANDS, dtype=F32)[None, :]
    z = jnp.concatenate([t, jnp.cos(f * w), -jnp.sin(f * w)], axis=-1)
    fold = lambda a: jnp.concatenate([a, jnp.zeros_like(a[:1]), a[:0:-1]], axis=0)
    EP = 64
    z2 = jnp.pad(fold(z), ((0, 0), (0, EP - HY_EMB)))
    w1p = jnp.pad(w1, ((0, EP - HY_EMB), (0, 0)))
    rt = _tile(2 * n, (1024, 512, 256))
    a2 = _call(
        _filt_trunk_body, grid=(2 * n // rt,),
        in_specs=[pl.BlockSpec((rt, EP), lambda i: (i, 0)),
                  pl.BlockSpec((EP, Hd), lambda i: (0, 0)),
                  pl.BlockSpec((1, Hd), lambda i: (0, 0)),
                  pl.BlockSpec((2, Hd), lambda i: (0, 0)),
                  pl.BlockSpec((Hd, Hd), lambda i: (0, 0)),
                  pl.BlockSpec((1, Hd), lambda i: (0, 0))],
        out_specs=pl.BlockSpec((rt, Hd), lambda i: (i, 0)),
        out_shape=jax.ShapeDtypeStruct((2 * n, Hd), F32),
        sem=("parallel",), name="hyena_filter_trunk")(z2, w1p, b1.reshape(1, Hd), freq, w2, b2.reshape(1, Hd))
    tc = LANES
    t2 = jnp.broadcast_to(fold(t), (2 * n, tc))
    deltas = jnp.abs(jnp.linspace(HY_MIN_DECAY, HY_MAX_DECAY, OC, dtype=F32))[None, :]
    nc = OC // tc
    return _call(
        functools.partial(_filt_main_body, n=n), grid=(nc,),
        in_specs=[pl.BlockSpec((2 * n, Hd), lambda c: (0, 0)),
                  pl.BlockSpec((Hd, tc), lambda c: (0, c)),
                  pl.BlockSpec((Hd, tc), lambda c: (0, nc + c)),
                  pl.BlockSpec((2 * n, tc), lambda c: (0, 0)),
                  pl.BlockSpec((1, tc), lambda c: (0, c))],
        out_specs=pl.BlockSpec((2 * n, tc), lambda c: (0, c)),
        out_shape=jax.ShapeDtypeStruct((2 * n, OC), F32),
        sem=("parallel",), name="hyena_filter")(a2, w3, w3, t2, deltas)


def _embed(re, im):
    return np.block([[re, -im], [im, re]])


@functools.lru_cache(maxsize=None)
def _fft_tables(N1, N2):
    N = N1 * N2
    S1 = N1 // 2
    i1 = np.arange(N1)
    ang = -2.0 * np.pi * ((i1[:, None] * i1[None, :]) % N1) / N1
    fr, fi = np.cos(ang), np.sin(ang)
    f1_pair = _embed(fr[:, :S1], fi[:, :S1])
    f1_real = np.concatenate([fr, fi], axis=0)
    i2 = np.arange(N2)
    fidx = i1[:, None, None] + N1 * i2[None, :, None]
    ang = -2.0 * np.pi * ((fidx * i2[None, None, :]) % N) / N
    mr, mi = np.cos(ang), np.sin(ang)
    m_fwd = np.stack([_embed(mr[a], mi[a]) for a in range(N1)])
    m_inv = np.stack([_embed(mr[a].T, -mi[a].T) for a in range(N1)])
    ang = 2.0 * np.pi * ((i1[:S1, None] * i1[None, :]) % N1) / N1
    f1_inv = _embed(np.cos(ang) / N, np.sin(ang) / N)
    cvt = lambda a: jnp.asarray(a, dtype=BF16)
    return cvt(f1_pair), cvt(f1_real), cvt(m_fwd), cvt(m_inv), cvt(f1_inv)


def _lmat_body(f_ref, x_ref, o_ref):
    o_ref[0] = jnp.dot(f_ref[...], x_ref[0].astype(BF16), preferred_element_type=F32).astype(o_ref.dtype)


def _left_matmul(fm, x, out_dtype, name):
    P, K, W = x.shape
    R = fm.shape[0]
    tw = _tile(W, (4096, 2048, 1024, 512, 256, 128))
    return _call(
        _lmat_body, grid=(P, W // tw),
        in_specs=[pl.BlockSpec((R, K), lambda p, j: (0, 0)),
                  pl.BlockSpec((1, K, tw), lambda p, j: (p, 0, j))],
        out_specs=pl.BlockSpec((1, R, tw), lambda p, j: (p, 0, j)),
        out_shape=jax.ShapeDtypeStruct((P, R, W), out_dtype),
        sem=("parallel", "parallel"), name=name)(fm, x)


def _cmul(xr, xi, kr, ki):
    return xr * kr - xi * ki, xr * ki + xi * kr


def _fftmid_body(a_ref, m_ref, mi_ref, k_ref, o_ref, *, FB, N2):
    for t in range(FB):
        a = a_ref[0, :, t].reshape(2 * N2, a_ref.shape[-1])
        x = jnp.dot(m_ref[t], a, preferred_element_type=F32)
        yr, yi = _cmul(x[:N2], x[N2:], k_ref[0, t], k_ref[1, t])
        y = jnp.concatenate([yr, yi], axis=0).astype(BF16)
        g = jnp.dot(mi_ref[t], y, preferred_element_type=F32)
        o_ref[0, :, t] = g.reshape(2, N2, g.shape[-1]).astype(o_ref.dtype)


def _fftfwd_body(a_ref, m_ref, o_ref, *, FB, N2):
    for t in range(FB):
        a = a_ref[:, t].reshape(2 * N2, a_ref.shape[-1])
        x = jnp.dot(m_ref[t], a, preferred_element_type=F32)
        o_ref[:, t] = x.reshape(2, N2, x.shape[-1])


def _fftfin_body(f_ref, g_ref, z_ref, x1_ref, sk_ref, o_ref):
    y = jnp.dot(f_ref[...], g_ref[0], preferred_element_type=F32)
    o_ref[0] = (x1_ref[0] * (y + sk_ref[...] * z_ref[0])).astype(o_ref.dtype)


def _filter_spectrum_2stage(k, N1, N2):
    N, OC = k.shape
    _, f1_real, m_fwd, _, _ = _fft_tables(N1, N2)
    a = _left_matmul(f1_real, k.reshape(1, N1, N2 * OC), BF16, "filter_dft1")
    a = a.reshape(2, N1, N2, OC)
    FB = _tile(N1, (4, 2, 1))
    tc = _tile(OC, (512, 256, 128))
    return _call(
        functools.partial(_fftfwd_body, FB=FB, N2=N2), grid=(N1 // FB, OC // tc),
        in_specs=[pl.BlockSpec((2, FB, N2, tc), lambda f, c: (0, f, 0, c)),
                  pl.BlockSpec((FB, 2 * N2, 2 * N2), lambda f, c: (f, 0, 0))],
        out_specs=pl.BlockSpec((2, FB, N2, tc), lambda f, c: (0, f, 0, c)),
        out_shape=jax.ShapeDtypeStruct((2, N1, N2, OC), F32),
        sem=("parallel", "parallel"), name="filter_dft2")(a, m_fwd)


def _long_conv_gate_2stage(z, x1, skip, kf, order, out_dtype, N1, N2):
    B, n, C = z.shape
    assert B % 2 == 0
    P, S1 = B // 2, N1 // 2
    f1_pair, _, m_fwd, m_inv, f1_inv = _fft_tables(N1, N2)
    W = N2 * C
    zr = z.reshape(P, 2 * S1, W)
    a = _left_matmul(f1_pair, zr, BF16, "conv_dft1").reshape(P, 2, N1, N2, C)
    FB = _tile(N1, (4, 2, 1))
    tc = _tile(C, (512, 256, 128))
    oc = order * (C // tc)
    g = _call(
        functools.partial(_fftmid_body, FB=FB, N2=N2), grid=(N1 // FB, C // tc, P),
        in_specs=[pl.BlockSpec((1, 2, FB, N2, tc), lambda f, c, p: (p, 0, f, 0, c)),
                  pl.BlockSpec((FB, 2 * N2, 2 * N2), lambda f, c, p: (f, 0, 0)),
                  pl.BlockSpec((FB, 2 * N2, 2 * N2), lambda f, c, p: (f, 0, 0)),
                  pl.BlockSpec((2, FB, N2, tc), lambda f, c, p: (0, f, 0, oc + c))],
        out_specs=pl.BlockSpec((1, 2, FB, N2, tc), lambda f, c, p: (p, 0, f, 0, c)),
        out_shape=jax.ShapeDtypeStruct((P, 2, N1, N2, C), BF16),
        sem=("parallel", "parallel", "arbitrary"), name="conv_dft2_mul_idft2")(a, m_fwd, m_inv, kf)
    tw = _tile(W, (4096, 2048, 1024, 512, 256, 128))
    out = _call(
        _fftfin_body, grid=(P, W // tw),
        in_specs=[pl.BlockSpec((2 * S1, 2 * N1), lambda p, j: (0, 0)),
                  pl.BlockSpec((1, 2 * N1, tw), lambda p, j: (p, 0, j)),
                  pl.BlockSpec((1, 2 * S1, tw), lambda p, j: (p, 0, j)),
                  pl.BlockSpec((1, 2 * S1, tw), lambda p, j: (p, 0, j)),
                  pl.BlockSpec((1, tw), lambda p, j: (0, j))],
        out_specs=pl.BlockSpec((1, 2 * S1, tw), lambda p, j: (p, 0, j)),
        out_shape=jax.ShapeDtypeStruct((P, 2 * S1, W), out_dtype),
        sem=("parallel", "parallel"), name="conv_idft1_gate")(
            f1_inv, g.reshape(P, 2 * N1, W), zr, x1.reshape(P, 2 * S1, W), jnp.tile(skip, N2)[None, :])
    return out.reshape(B, n, C)


@functools.lru_cache(maxsize=None)
def _dft_tables(n):
    N = 2 * n
    f = np.arange(N)
    ang = -2.0 * np.pi * ((f[:, None] * f[None, :]) % N) / N
    fr, fi = np.cos(ang), np.sin(ang)
    fwd_full = np.concatenate([fr, fi], axis=0)
    fwd_half = fwd_full[:, :n]
    inv = np.concatenate([fr[:n, :], fi[:n, :]], axis=1) / N
    cvt = lambda a: jnp.asarray(a, dtype=BF16)
    return cvt(fwd_full), cvt(fwd_half), cvt(inv)


def _dftconv_body(z_ref, x1_ref, sk_ref, f_ref, fi_ref, k_ref, o_ref, *, N):
    z = z_ref[0]
    x = jnp.dot(f_ref[...], z.astype(BF16), preferred_element_type=F32)
    yr, yi = _cmul(x[:N], x[N:], k_ref[0], k_ref[1])
    y = jnp.concatenate([yr, yi], axis=0).astype(BF16)
    y = jnp.dot(fi_ref[...], y, preferred_element_type=F32)
    o_ref[0] = (x1_ref[0] * (y + sk_ref[...] * z)).astype(o_ref.dtype)


def _long_conv_gate_dense(z, x1, skip, kf, order, out_dtype):
    B, n, C = z.shape
    N = 2 * n
    _, fwd_half, inv = _dft_tables(n)
    tc = _tile(C, (256, 128))
    oc = order * (C // tc)
    return _call(
        functools.partial(_dftconv_body, N=N), grid=(C // tc, B),
        in_specs=[pl.BlockSpec((1, n, tc), lambda c, b: (b, 0, c)),
                  pl.BlockSpec((1, n, tc), lambda c, b: (b, 0, c)),
                  pl.BlockSpec((1, tc), lambda c, b: (0, c)),
                  pl.BlockSpec((2 * N, n), lambda c, b: (0, 0)),
                  pl.BlockSpec((n, 2 * N), lambda c, b: (0, 0)),
                  pl.BlockSpec((2, N, tc), lambda c, b: (0, 0, oc + c))],
        out_specs=pl.BlockSpec((1, n, tc), lambda c, b: (b, 0, c)),
        out_shape=jax.ShapeDtypeStruct((B, n, C), out_dtype),
        sem=("parallel", "arbitrary"), name="conv_dense_dft")(z, x1, skip[None, :], fwd_half, inv, kf)


def _hyena_mixer(u, conv_w, conv_b, w1, b1, freq, w2, b2, w3, skip):
    B, n, _ = u.shape
    v, x1, x2 = _hyena_dwconv(u, conv_w, conv_b)
    k = _hyena_filters(n, w1, b1, freq, w2, b2, w3)
    N = 2 * n
    if N % FFT_N2 == 0 and (N // FFT_N2) >= 16:
        N1 = N // FFT_N2
        kf = _filter_spectrum_2stage(k, N1, FFT_N2)
        z = _long_conv_gate_2stage(v, x1, skip[0], kf, 0, F32, N1, FFT_N2)
        return _long_conv_gate_2stage(z, x2, skip[1], kf, 1, BF16, N1, FFT_N2)
    fwd_full, _, _ = _dft_tables(n)
    kf = _left_matmul(fwd_full, k[None], F32, "filter_dense_dft").reshape(2, N, k.shape[1])
    z = _long_conv_gate_dense(v, x1, skip[0], kf, 0, F32)
    return _long_conv_gate_dense(z, x2, skip[1], kf, 1, BF16)


def _nt(a, b):
    return lax.dot_general(a, b, (((1,), (1,)), ((), ())), preferred_element_type=F32)


def _na_body(q_ref, k_ref, v_ref, kc_ref, vc_ref, b_ref, o_ref, *, R, KR, rows, scale):
    j = pl.program_id(2)
    start = pl.multiple_of(jnp.clip(j * R - NA_WIN_R // 2, 0, rows - KR) * GRID_W, GRID_W)
    q = q_ref[0]
    kw = k_ref[0, pl.ds(start, KR * GRID_W), :]
    vw = v_ref[0, pl.ds(start, KR * GRID_W), :]
    s = _nt(q, kw) * scale + b_ref[0, 0]
    sc = _nt(q, kc_ref[0]) * scale
    m = jnp.maximum(jnp.max(s, axis=-1, keepdims=True), jnp.max(sc, axis=-1, keepdims=True))
    p = jnp.exp(s - m)
    pc = jnp.exp(sc - m)
    l = jnp.sum(p, axis=-1, keepdims=True) + jnp.sum(pc, axis=-1, keepdims=True)
    o = jnp.dot(p.astype(BF16), vw, preferred_element_type=F32)
    o = o + jnp.dot(pc.astype(BF16), vc_ref[0], preferred_element_type=F32)
    o_ref[0] = (o / l).astype(o_ref.dtype)


def _na_geometry(S):
    rows = S // GRID_W
    kr = min(NA_WIN_R, rows)
    R = min(8, rows)
    KR = min(rows, R + kr)
    nb = rows // R
    assert rows % R == 0
    types = sorted({0, min(1, nb - 1), nb - 1})
    if nb > 3:
        offs = {int(np.clip(j * R - NA_WIN_R // 2, 0, rows - KR)) - j * R for j in range(1, nb - 1)}
        assert len(offs) == 1
    return rows, kr, R, KR, nb, types


def _na_bias_tables(rpb, S):
    rows, kr, R, KR, nb, types = _na_geometry(S)
    W = GRID_W
    nd = 2 * NA_WIN_C - 1
    cidx = np.arange(W)
    cstart = np.clip(cidx - NA_WIN_C // 2, 0, W - NA_WIN_C)
    col_ok = (cidx[None, :] >= cstart[:, None]) & (cidx[None, :] < cstart[:, None] + NA_WIN_C)
    dc = np.clip(cidx[None, :] - cidx[:, None] + (NA_WIN_C - 1), 0, nd - 1)
    idx, ok = [], []
    for jt in types:
        qr = jt * R + np.arange(R)
        kra = int(np.clip(jt * R - NA_WIN_R // 2, 0, rows - KR)) + np.arange(KR)
        ws = np.clip(qr - kr // 2, 0, rows - kr)
        rok = (kra[None, :] >= ws[:, None]) & (kra[None, :] < ws[:, None] + kr)
        dr = np.clip(kra[None, :] - qr[:, None] + (NA_WIN_R - 1), 0, 2 * NA_WIN_R - 2)
        idx.append((dr[:, None, :, None] * nd + dc[None, :, None, :]).reshape(R * W, KR * W))
        ok.append((rok[:, None, :, None] & col_ok[None, :, None, :]).reshape(R * W, KR * W))
    idx = jnp.asarray(np.stack(idx), dtype=jnp.int32)
    ok = jnp.asarray(np.stack(ok))
    H = rpb.shape[0]
    tab = jnp.take(rpb.reshape(H, -1).astype(F32), idx, axis=1)
    return jnp.where(ok, tab, NEG_INF)


def _na_attention(qkv, qkv_c, bias, offs, H):
    B, S, _ = qkv.shape
    CTX = qkv_c.shape[1]
    rows, kr, R, KR, nb, types = _na_geometry(S)
    T = len(types)
    QB, KB = R * GRID_W, KR * GRID_W
    ok_, ov_, oq_ = (offs[n] // HEAD_DIM for n in ("na_k", "na_v", "na_q"))

    def btype(j):
        if T == nb:
            return j
        return jnp.where(j == 0, 0, jnp.where(j == nb - 1, T - 1, 1))

    return _call(
        functools.partial(_na_body, R=R, KR=KR, rows=rows, scale=HEAD_DIM ** -0.5),
        grid=(B, H, nb),
        in_specs=[pl.BlockSpec((1, QB, HEAD_DIM), lambda b, h, j: (b, j, oq_ + h)),
                  pl.BlockSpec((1, S, HEAD_DIM), lambda b, h, j: (b, 0, ok_ + h)),
                  pl.BlockSpec((1, S, HEAD_DIM), lambda b, h, j: (b, 0, ov_ + h)),
                  pl.BlockSpec((1, CTX, HEAD_DIM), lambda b, h, j: (b, 0, ok_ + h)),
                  pl.BlockSpec((1, CTX, HEAD_DIM), lambda b, h, j: (b, 0, ov_ + h)),
                  pl.BlockSpec((1, 1, QB, KB), lambda b, h, j: (h, btype(j), 0, 0))],
        out_specs=pl.BlockSpec((1, QB, HEAD_DIM), lambda b, h, j: (b, j, h)),
        out_shape=jax.ShapeDtypeStruct((B, S, H * HEAD_DIM), BF16),
        sem=("parallel", "parallel", "arbitrary"), name="na_attention")(qkv, qkv, qkv, qkv_c, qkv_c, bias)


def _stack_heads(q2, G):
    return jnp.concatenate([q2[:, g * HEAD_DIM:(g + 1) * HEAD_DIM] for g in range(G)], axis=0)


def _unstack_heads(o, G, n):
    return jnp.concatenate([o[g * n:(g + 1) * n] for g in range(G)], axis=1)


def _sink_column(sink_ref, h0, G, n):
    return jnp.concatenate([jnp.full((n, 1), sink_ref[h0 + g], F32) for g in range(G)], axis=0)


def _swa_body(sink_ref, q_ref, k_ref, v_ref, kc_ref, vc_ref, o_ref, *, QB, KB, S, G, scale):
    kvh, j = pl.program_id(1), pl.program_id(2)
    start = pl.multiple_of(jnp.clip(j * QB - GQA_WINDOW, 0, S - KB), LANES)
    q = _stack_heads(q_ref[0], G)
    kw = k_ref[0, pl.ds(start, KB), :]
    vw = v_ref[0, pl.ds(start, KB), :]
    qpos = j * QB + lax.broadcasted_iota(jnp.int32, (QB, KB), 0)
    kpos = start + lax.broadcasted_iota(jnp.int32, (QB, KB), 1)
    mask = jnp.where(jnp.abs(qpos - kpos) <= GQA_WINDOW, 0.0, NEG_INF)
    s = _nt(q, kw) * scale + jnp.concatenate([mask] * G, axis=0)
    sc = _nt(q, kc_ref[0]) * scale
    sk = _sink_column(sink_ref, kvh * G, G, QB)
    m = jnp.maximum(jnp.maximum(jnp.max(s, axis=-1, keepdims=True), jnp.max(sc, axis=-1, keepdims=True)), sk)
    p = jnp.exp(s - m)
    pc = jnp.exp(sc - m)
    l = jnp.sum(p, axis=-1, keepdims=True) + jnp.sum(pc, axis=-1, keepdims=True) + jnp.exp(sk - m)
    o = jnp.dot(p.astype(BF16), vw, preferred_element_type=F32)
    o = o + jnp.dot(pc.astype(BF16), vc_ref[0], preferred_element_type=F32)
    o_ref[0] = _unstack_heads(o / l, G, QB).astype(o_ref.dtype)


def _swa_attention(qkv, qkv_c, sink, offs, KVH):
    B, S, _ = qkv.shape
    CTX = qkv_c.shape[1]
    G = GQA_GROUP
    QB = _tile(S, (512, 256, 128))
    KB = min(S, QB + 2 * GQA_WINDOW)
    ok_, ov_ = offs["sw_k"] // HEAD_DIM, offs["sw_v"] // HEAD_DIM
    oq_ = offs["sw_q"] // (G * HEAD_DIM)
    return _call(
        functools.partial(_swa_body, QB=QB, KB=KB, S=S, G=G, scale=HEAD_DIM ** -0.5),
        grid=(B, KVH, S // QB),
        in_specs=[pl.BlockSpec(memory_space=pltpu.SMEM),
                  pl.BlockSpec((1, QB, G * HEAD_DIM), lambda b, h, j: (b, j, oq_ + h)),
                  pl.BlockSpec((1, S, HEAD_DIM), lambda b, h, j: (b, 0, ok_ + h)),
                  pl.BlockSpec((1, S, HEAD_DIM), lambda b, h, j: (b, 0, ov_ + h)),
                  pl.BlockSpec((1, CTX, HEAD_DIM), lambda b, h, j: (b, 0, ok_ + h)),
                  pl.BlockSpec((1, CTX, HEAD_DIM), lambda b, h, j: (b, 0, ov_ + h))],
        out_specs=pl.BlockSpec((1, QB, G * HEAD_DIM), lambda b, h, j: (b, j, h)),
        out_shape=jax.ShapeDtypeStruct((B, S, KVH * G * HEAD_DIM), BF16),
        sem=("parallel", "parallel", "arbitrary"), name="swa_attention")(sink, qkv, qkv, qkv, qkv_c, qkv_c)


def _cattn_body(sink_ref, q_ref, k_ref, v_ref, o_ref, *, G, use_sink, scale):
    h = pl.program_id(1)
    n = q_ref.shape[1]
    q = _stack_heads(q_ref[0], G)
    s = _nt(q, k_ref[0]) * scale
    m = jnp.max(s, axis=-1, keepdims=True)
    if use_sink:
        sk = _sink_column(sink_ref, h * G, G, n)
        m = jnp.maximum(m, sk)
    p = jnp.exp(s - m)
    l = jnp.sum(p, axis=-1, keepdims=True)
    if use_sink:
        l = l + jnp.exp(sk - m)
    o = jnp.dot(p.astype(BF16), v_ref[0], preferred_element_type=F32)
    o_ref[0] = _unstack_heads(o / l, G, n).astype(o_ref.dtype)


def _ctx_attention(qkv_c, sink, oq, ok, ov, KVH, G, use_sink):
    B, n, _ = qkv_c.shape
    oq_, ok_, ov_ = oq // (G * HEAD_DIM), ok // HEAD_DIM, ov // HEAD_DIM
    return _call(
        functools.partial(_cattn_body, G=G, use_sink=use_sink, scale=HEAD_DIM ** -0.5),
        grid=(B, KVH),
        in_specs=[pl.BlockSpec(memory_space=pltpu.SMEM),
                  pl.BlockSpec((1, n, G * HEAD_DIM), lambda b, h: (b, 0, oq_ + h)),
                  pl.BlockSpec((1, n, HEAD_DIM), lambda b, h: (b, 0, ok_ + h)),
                  pl.BlockSpec((1, n, HEAD_DIM), lambda b, h: (b, 0, ov_ + h))],
        out_specs=pl.BlockSpec((1, n, G * HEAD_DIM), lambda b, h: (b, 0, h)),
        out_shape=jax.ShapeDtypeStruct((B, n, KVH * G * HEAD_DIM), BF16),
        sem=("parallel", "parallel"), name="ctx_attention")(sink, qkv_c, qkv_c, qkv_c)


def _merge_body(yh, yn, ys, gh, gn, gs, wh, wn, ws, o_ref):
    m = _sigmoid(gh[0]) * jnp.dot(yh[0], wh[...], preferred_element_type=F32)
    m = m + _sigmoid(gn[0]) * jnp.dot(yn[0], wn[...], preferred_element_type=F32)
    m = m + _sigmoid(gs[0]) * jnp.dot(ys[0], ws[...], preferred_element_type=F32)
    o_ref[0] = m.astype(o_ref.dtype)


def _merge_branches(y_hy, y_na, y_sw, gates, w_br, l):
    B, S, _ = y_hy.shape
    D = w_br.shape[2]
    widths = (y_hy.shape[2], y_na.shape[2], y_sw.shape[2])
    starts = (0, widths[0], widths[0] + widths[1])
    assert all(s % w == 0 for s, w in zip(starts, widths))
    tm = _tile(S, (512, 256, 128))
    tn = _tile(D, (512, 256, 128))
    nj = D // tn
    yspec = lambda y: pl.BlockSpec((1, tm, y.shape[2]), lambda b, i, j: (b, i, 0))
    gspec = lambda k: pl.BlockSpec((1, tm, tn), lambda b, i, j: (b, i, k * nj + j))
    wspec = lambda k: pl.BlockSpec((None, widths[k], tn), lambda b, i, j: (l, starts[k] // widths[k], j))
    return _call(
        _merge_body, grid=(B, S // tm, nj),
        in_specs=[yspec(y_hy), yspec(y_na), yspec(y_sw), gspec(0), gspec(1), gspec(2),
                  wspec(0), wspec(1), wspec(2)],
        out_specs=pl.BlockSpec((1, tm, tn), lambda b, i, j: (b, i, j)),
        out_shape=jax.ShapeDtypeStruct((B, S, D), BF16),
        sem=("parallel", "parallel", "arbitrary"), name="merge_branches")(
            y_hy, y_na, y_sw, gates, gates, gates, w_br, w_br, w_br)


def _mmres_body(a_ref, w_ref, x_ref, g_ref, mg_ref, o_ref, acc_ref, *, nk, eps):
    k = pl.program_id(2)

    @pl.when(k == 0)
    def _():
        acc_ref[...] = jnp.zeros_like(acc_ref)

    acc_ref[...] += jnp.dot(a_ref[0], w_ref[...], preferred_element_type=F32)

    @pl.when(k == nk - 1)
    def _():
        y = acc_ref[...]
        yn = y * lax.rsqrt(jnp.mean(y * y, axis=-1, keepdims=True) + eps) * g_ref[...]
        o_ref[0] = x_ref[0] + mg_ref[0] * yn


def _matmul_norm_residual(a, w, l, x, gain, mgate, name):
    B, S, K = a.shape
    D = w.shape[2]
    tm = _tile(S, (512, 256, 128))
    tk = _tile(K, (512, 256, 128))
    nk = K // tk
    return _call(
        functools.partial(_mmres_body, nk=nk, eps=NORM_EPS), grid=(B, S // tm, nk),
        in_specs=[pl.BlockSpec((1, tm, tk), lambda b, i, k: (b, i, k)),
                  pl.BlockSpec((None, tk, D), lambda b, i, k: (l, k, 0)),
                  pl.BlockSpec((1, tm, D), lambda b, i, k: (b, i, 0)),
                  pl.BlockSpec((1, D), lambda b, i, k: (0, 0)),
                  pl.BlockSpec((1, 1, D), lambda b, i, k: (b, 0, 0))],
        out_specs=pl.BlockSpec((1, tm, D), lambda b, i, k: (b, i, 0)),
        out_shape=jax.ShapeDtypeStruct((B, S, D), F32),
        scratch=[pltpu.VMEM((tm, D), F32)],
        sem=("parallel", "parallel", "arbitrary"), name=name)(a, w, x, gain.reshape(1, D), mgate)


def _rope_tables(n):
    t = jnp.arange(n)
    row = (t // GRID_W).astype(F32)
    col = (t % GRID_W).astype(F32)
    per_axis = HEAD_DIM // 2
    inv = ROPE_BASE ** (-jnp.arange(0, per_axis, 2, dtype=F32) / per_axis)
    ar, ac = row[:, None] * inv, col[:, None] * inv
    cos_t = jnp.concatenate([jnp.cos(ar), jnp.cos(ar), jnp.cos(ac), jnp.cos(ac)], axis=1)
    sin_t = jnp.concatenate([-jnp.sin(ar), jnp.sin(ar), -jnp.sin(ac), jnp.sin(ac)], axis=1)
    return cos_t, sin_t


def kernel(x, c, ctx, c_ctx, w_mod, b_mod, norm_gains, w_in, hy_conv_w, hy_conv_b, hy_w1, hy_b1, hy_freq, hy_w2, hy_b2, hy_w3, hy_skip, na_rpb, swa_sink, w_branch, w_out, ffn_w_up, ffn_conv_w, ffn_conv_b, ffn_w_down):
    B, S, D = x.shape
    L = w_mod.shape[0]
    C = hy_skip.shape[-1]
    H_na = na_rpb.shape[1]
    H_q = swa_sink.shape[1]
    KVH = H_q // GQA_GROUP
    NA_W, QW, KVW = H_na * HEAD_DIM, H_q * HEAD_DIM, KVH * HEAD_DIM
    KV_COLS = 2 * NA_W + 2 * KVW
    qkv_segs = [(0, KV_COLS), (KV_COLS + 3 * C, NA_W + QW)]
    hy_segs = [(KV_COLS, 3 * C)]
    gate_segs = [(KV_COLS + 3 * C + NA_W + QW, 3 * D)]
    offs = {"na_k": 0, "na_v": NA_W, "sw_k": 2 * NA_W, "sw_v": 2 * NA_W + KVW,
            "na_q": KV_COLS, "sw_q": KV_COLS + NA_W}
    tq = _tile(math.gcd(NA_W, KVW), (256, 128))
    tw = lambda segs: _tile(functools.reduce(math.gcd, [v for seg in segs for v in seg if v]), (512, 256, 128))
    rope_tiles = (list(range(offs["sw_k"] // tq, (offs["sw_k"] + KVW) // tq))
                  + list(range(offs["sw_q"] // tq, (offs["sw_q"] + QW) // tq)))
    cos_t, sin_t = _rope_tables(S)

    w_in_b, w_out_b = w_in.astype(BF16), w_out.astype(BF16)
    w_br_b, w_up_b, w_dn_b = w_branch.astype(BF16), ffn_w_up.astype(BF16), ffn_w_down.astype(BF16)
    F2 = w_up_b.shape[2]

    R = -(-(B + 1) // 8) * 8
    cc = jnp.concatenate([c, c_ctx[None, :], jnp.zeros((R - B - 1, D), F32)], axis=0)
    mods = _modulation(cc, w_mod, b_mod)

    xc = ctx
    for l in range(L):
        mod = [mods[l, :B, k * D:(k + 1) * D].reshape(B, 1, D) for k in range(6)]
        mod_c = [jnp.broadcast_to(mods[l, B, k * D:(k + 1) * D].reshape(1, 1, D), (B, 1, D)) for k in range(6)]
        g = norm_gains[l]
        hy_p = (hy_conv_w[l], hy_conv_b[l], hy_w1[l], hy_b1[l], hy_freq[l], hy_w2[l], hy_b2[l], hy_w3[l], hy_skip[l])
        na_bias = _na_bias_tables(na_rpb[l], S)

        qkv_c = _norm_mod_matmul(xc, g[0], mod_c[0], mod_c[1], w_in_b, l, qkv_segs, BF16, tq, name="ctx_qkv_proj")
        qkv = _norm_mod_matmul(x, g[0], mod[0], mod[1], w_in_b, l, qkv_segs, BF16, tq,
                               rope=(cos_t, sin_t, rope_tiles), name="qkv_proj")
        hy_u = _norm_mod_matmul(x, g[0], mod[0], mod[1], w_in_b, l, hy_segs, F32, tw(hy_segs), name="hyena_proj")
        gates = _norm_mod_matmul(x, g[0], mod[0], mod[1], w_in_b, l, gate_segs, F32, tw(gate_segs + [(0, D)]), name="gate_proj")
        y_hy = _hyena_mixer(hy_u, *hy_p)
        y_na = _na_attention(qkv, qkv_c, na_bias, offs, H_na)
        y_sw = _swa_attention(qkv, qkv_c, swa_sink[l], offs, KVH)
        m = _merge_branches(y_hy, y_na, y_sw, gates, w_br_b, l)
        x = _matmul_norm_residual(m, w_out_b, l, x, g[1], mod[2], "out_proj_residual")
        au = _norm_mod_matmul(x, g[2], mod[3], mod[4], w_up_b, l, [(0, F2)], F32, tw([(0, F2 // 2)]), name="ffn_up")
        gl = _conv_glu_gate(au, ffn_conv_w[l], ffn_conv_b[l])
        x = _matmul_norm_residual(gl, w_dn_b, l, x, g[3], mod[5], "ffn_down_residual")

        if l < L - 1:
            hy_c = _norm_mod_matmul(xc, g[0], mod_c[0], mod_c[1], w_in_b, l, hy_segs, F32, tw(hy_segs), name="ctx_hyena_proj")
            gates_c = _norm_mod_matmul(xc, g[0], mod_c[0], mod_c[1], w_in_b, l, gate_segs, F32, tw(gate_segs + [(0, D)]), name="ctx_gate_proj")
            yc_hy = _hyena_mixer(hy_c, *hy_p)
            yc_na = _ctx_attention(qkv_c, swa_sink[l], offs["na_q"], offs["na_k"], offs["na_v"], H_na, 1, False)
            yc_sw = _ctx_attention(qkv_c, swa_sink[l], offs["sw_q"], offs["sw_k"], offs["sw_v"], KVH, GQA_GROUP, True)
            mc = _merge_branches(yc_hy, yc_na, yc_sw, gates_c, w_br_b, l)
            xc = _matmul_norm_residual(mc, w_out_b, l, xc, g[1], mod_c[2], "ctx_out_proj_residual")
            au_c = _norm_mod_matmul(xc, g[2], mod_c[3], mod_c[4], w_up_b, l, [(0, F2)], F32, tw([(0, F2 // 2)]), name="ctx_ffn_up")
            gl_c = _conv_glu_gate(au_c, ffn_conv_w[l], ffn_conv_b[l])
            xc = _matmul_norm_residual(gl_c, w_dn_b, l, xc, g[3], mod_c[5], "ctx_ffn_down_residual")
    return x
```

```python
import functools
import math

import numpy as np
import jax
import jax.numpy as jnp
from jax import lax
from jax.experimental import pallas as pl
from jax.experimental.pallas import tpu as pltpu

F32 = jnp.float32
BF16 = jnp.bfloat16
NEG_INF = -1e30

GRID_W = 64
HEAD_DIM = 128
HY_EMB = 33
HY_BANDS = (HY_EMB - 1) // 2
HY_FAST_DECAY = 0.3
HY_SLOW_DECAY = 1.5
HY_DECAY_TARGET = 1e-2
HY_MAX_DECAY = math.log(HY_DECAY_TARGET) / HY_FAST_DECAY
HY_MIN_DECAY = math.log(HY_DECAY_TARGET) / HY_SLOW_DECAY
NA_WIN_R = 8
NA_WIN_C = 16
GQA_GROUP = 2
GQA_WINDOW = 128
ROPE_BASE = 10000.0
NORM_EPS = 1e-6

LANES = 128
FFT_N2 = 128
VMEM_LIMIT = 56 * 1024 * 1024
HP = lax.Precision.HIGHEST


def _call(body, *, grid, in_specs, out_specs, out_shape, scratch=(), sem, name):
    return pl.pallas_call(
        body, grid=grid, in_specs=in_specs, out_specs=out_specs, out_shape=out_shape,
        scratch_shapes=list(scratch),
        compiler_params=pltpu.CompilerParams(dimension_semantics=sem, vmem_limit_bytes=VMEM_LIMIT),
        name=name)


def _tile(n, cands):
    for c in cands:
        if n % c == 0:
            return c
    raise ValueError(f"no tile for {n} in {cands}")


def _sigmoid(x):
    return 1.0 / (1.0 + jnp.exp(-x))


def _mod_body(c_ref, w_ref, b_ref, o_ref):
    c = c_ref[...]
    s = (c * _sigmoid(c)).astype(BF16)
    o_ref[0] = jnp.dot(s, w_ref[0].astype(BF16), preferred_element_type=F32) + b_ref[0]


def _modulation(cc, w_mod, b_mod):
    L, D, N = w_mod.shape
    R = cc.shape[0]
    tn = _tile(N, (1024, 512, 256, 128))
    return _call(
        _mod_body, grid=(L, N // tn),
        in_specs=[pl.BlockSpec((R, D), lambda l, j: (0, 0)),
                  pl.BlockSpec((1, D, tn), lambda l, j: (l, 0, j)),
                  pl.BlockSpec((1, 1, tn), lambda l, j: (l, 0, j))],
        out_specs=pl.BlockSpec((1, R, tn), lambda l, j: (l, 0, j)),
        out_shape=jax.ShapeDtypeStruct((L, R, N), F32),
        sem=("parallel", "parallel"), name="modulation")(cc, w_mod, b_mod.reshape(L, 1, N))


def _swap32(a):
    lane = lax.broadcasted_iota(jnp.int32, a.shape, 1)
    return jnp.where((lane & 32) == 0, pltpu.roll(a, 96, 1), pltpu.roll(a, 32, 1))


def _norm_modulate(x, g_ref, sh_ref, sc_ref, eps):
    y = x * lax.rsqrt(jnp.mean(x * x, axis=-1, keepdims=True) + eps) * g_ref[...]
    return y * (1.0 + sc_ref[0]) + sh_ref[0]


def _nm_body(*refs, rope_chunks, eps):
    if rope_chunks:
        x_ref, g_ref, sh_ref, sc_ref, w_ref, cos_ref, sin_ref, o_ref, h_ref = refs
    else:
        x_ref, g_ref, sh_ref, sc_ref, w_ref, o_ref, h_ref = refs
    j = pl.program_id(2)

    @pl.when(j == 0)
    def _():
        h_ref[...] = _norm_modulate(x_ref[0], g_ref, sh_ref, sc_ref, eps).astype(BF16)

    acc = jnp.dot(h_ref[...], w_ref[...], preferred_element_type=F32)
    if not rope_chunks:
        o_ref[0] = acc.astype(o_ref.dtype)
        return
    nch = acc.shape[1] // LANES
    rope_tiles = sorted({ch // nch for ch in rope_chunks})
    for jt in rope_tiles:
        @pl.when(j == jt)
        def _(jt=jt):
            c, s = cos_ref[...], sin_ref[...]
            for k in range(nch):
                a = acc[:, k * LANES:(k + 1) * LANES]
                if jt * nch + k in rope_chunks:
                    a = a * c + _swap32(a) * s
                o_ref[0, :, k * LANES:(k + 1) * LANES] = a.astype(o_ref.dtype)

    @pl.when(functools.reduce(jnp.logical_and, [j != jt for jt in rope_tiles]))
    def _():
        o_ref[0] = acc.astype(o_ref.dtype)


def _colmap(segs, tn):
    bounds, o = [], 0
    for s, w in segs:
        assert s % tn == 0 and w % tn == 0
        bounds.append(((o + w) // tn, (s - o) // tn))
        o += w

    def f(j):
        r = j + bounds[-1][1]
        for hi, off in reversed(bounds[:-1]):
            r = jnp.where(j < hi, j + off, r)
        return r
    return f, o


def _seg_tile(segs, cands):
    return _tile(functools.reduce(math.gcd, [v for seg in segs for v in seg if v]), cands)


def _norm_mod_matmul(x, gain, shift, scale, w, l, segs, out_dtype, rope=None, name="norm_mod_matmul"):
    B, S, D = x.shape
    tm = _tile(S, (1024, 512, 256, 128))
    tn = _seg_tile(segs, (1024, 512, 256, 128))
    cmap, n_out = _colmap(segs, tn)
    in_specs = [pl.BlockSpec((1, tm, D), lambda b, i, j: (b, i, 0)),
                pl.BlockSpec((1, D), lambda b, i, j: (0, 0)),
                pl.BlockSpec((1, 1, D), lambda b, i, j: (b, 0, 0)),
                pl.BlockSpec((1, 1, D), lambda b, i, j: (b, 0, 0)),
                pl.BlockSpec((None, D, tn), lambda b, i, j: (l, 0, cmap(j)))]
    args = [x, gain.reshape(1, D), shift, scale, w]
    rope_chunks = ()
    if rope is not None:
        cos_t, sin_t, rope_chunks = rope
        in_specs += [pl.BlockSpec((tm, LANES), lambda b, i, j: (i, 0)),
                     pl.BlockSpec((tm, LANES), lambda b, i, j: (i, 0))]
        args += [cos_t, sin_t]
    return _call(
        functools.partial(_nm_body, rope_chunks=frozenset(rope_chunks), eps=NORM_EPS),
        grid=(B, S // tm, n_out // tn), in_specs=in_specs,
        out_specs=pl.BlockSpec((1, tm, tn), lambda b, i, j: (b, i, j)),
        out_shape=jax.ShapeDtypeStruct((B, S, n_out), out_dtype),
        scratch=[pltpu.VMEM((tm, D), BF16)],
        sem=("parallel", "parallel", "arbitrary"), name=name)(*args)


def _conv3(u, w_ref, b_ref):
    n = u.shape[0]
    r = lax.broadcasted_iota(jnp.int32, u.shape, 0)
    up = jnp.where(r == 0, 0.0, pltpu.roll(u, 1, 0))
    un = jnp.where(r == n - 1, 0.0, pltpu.roll(u, n - 1, 0))
    return up * w_ref[0:1, :] + u * w_ref[1:2, :] + un * w_ref[2:3, :] + b_ref[...]


def _hyconv_body(u0, u1, u2, w0, w1, w2, b0, b1, b2, o0, o1, o2):
    for u, w, b, o in ((u0, w0, b0, o0), (u1, w1, b1, o1), (u2, w2, b2, o2)):
        o[0] = _conv3(u[0], w, b)


def _hyena_dwconv(u, conv_w, conv_b):
    B, n, C3 = u.shape
    C = C3 // 3
    tc = _tile(C, (256, 128))
    nc = C // tc
    cb = conv_b.reshape(1, C3)
    uspec = [pl.BlockSpec((1, n, tc), functools.partial(lambda b, c, k: (b, 0, k * nc + c), k=k)) for k in range(3)]
    wspec = [pl.BlockSpec((3, tc), functools.partial(lambda b, c, k: (0, k * nc + c), k=k)) for k in range(3)]
    bspec = [pl.BlockSpec((1, tc), functools.partial(lambda b, c, k: (0, k * nc + c), k=k)) for k in range(3)]
    ospec = [pl.BlockSpec((1, n, tc), lambda b, c: (b, 0, c)) for _ in range(3)]
    return _call(
        _hyconv_body, grid=(B, nc), in_specs=uspec + wspec + bspec, out_specs=ospec,
        out_shape=[jax.ShapeDtypeStruct((B, n, C), F32)] * 3,
        sem=("parallel", "parallel"), name="hyena_dwconv")(u, u, u, conv_w, conv_w, conv_w, cb, cb, cb)


HALO = 16


def _ffnup_body(x_ref, xp_ref, xn_ref, g_ref, sh_ref, sc_ref, wa_ref, wu_ref, cw_ref, cb_ref, o_ref, h_ref, *, eps):
    i, j = pl.program_id(1), pl.program_id(2)
    tm = x_ref.shape[1]

    @pl.when(j == 0)
    def _():
        nm = lambda x: _norm_modulate(x, g_ref, sh_ref, sc_ref, eps)
        keep_prev = (i > 0).astype(F32)
        keep_next = (i < pl.num_programs(1) - 1).astype(F32)
        h_ref[0:HALO, :] = (nm(xp_ref[0]) * keep_prev).astype(BF16)
        h_ref[HALO:HALO + tm, :] = nm(x_ref[0]).astype(BF16)
        h_ref[HALO + tm:, :] = (nm(xn_ref[0]) * keep_next).astype(BF16)

    a = jnp.dot(h_ref[...], wa_ref[...], preferred_element_type=F32)
    u = jnp.dot(h_ref[HALO:HALO + tm, :], wu_ref[...], preferred_element_type=F32)
    rows = a.shape[0]
    up = pltpu.roll(a, 1, 0)[HALO:HALO + tm]
    un = pltpu.roll(a, rows - 1, 0)[HALO:HALO + tm]
    c = up * cw_ref[0:1, :] + a[HALO:HALO + tm] * cw_ref[1:2, :] + un * cw_ref[2:3, :] + cb_ref[...]
    o_ref[0] = (c * _sigmoid(c) * u).astype(o_ref.dtype)


def _ffn_up_glu(x, gain, shift, scale, w_up, l, conv_w, conv_b):
    B, S, D = x.shape
    Fd = w_up.shape[2] // 2
    tm = _tile(S, (1024, 512, 256, 128))
    tn = _tile(Fd, (512, 256, 128))
    nj = Fd // tn
    hb, nh = tm // HALO, S // HALO
    return _call(
        functools.partial(_ffnup_body, eps=NORM_EPS), grid=(B, S // tm, nj),
        in_specs=[pl.BlockSpec((1, tm, D), lambda b, i, j: (b, i, 0)),
                  pl.BlockSpec((1, HALO, D), lambda b, i, j: (b, jnp.maximum(i * hb - 1, 0), 0)),
                  pl.BlockSpec((1, HALO, D), lambda b, i, j: (b, jnp.minimum((i + 1) * hb, nh - 1), 0)),
                  pl.BlockSpec((1, D), lambda b, i, j: (0, 0)),
                  pl.BlockSpec((1, 1, D), lambda b, i, j: (b, 0, 0)),
                  pl.BlockSpec((1, 1, D), lambda b, i, j: (b, 0, 0)),
                  pl.BlockSpec((None, D, tn), lambda b, i, j: (l, 0, j)),
                  pl.BlockSpec((None, D, tn), lambda b, i, j: (l, 0, nj + j)),
                  pl.BlockSpec((3, tn), lambda b, i, j: (0, j)),
                  pl.BlockSpec((1, tn), lambda b, i, j: (0, j))],
        out_specs=pl.BlockSpec((1, tm, tn), lambda b, i, j: (b, i, j)),
        out_shape=jax.ShapeDtypeStruct((B, S, Fd), BF16),
        scratch=[pltpu.VMEM((tm + 2 * HALO, D), BF16)],
        sem=("parallel", "parallel", "arbitrary"), name="ffn_up_glu")(
            x, x, x, gain.reshape(1, D), shift, scale, w_up, w_up, conv_w, conv_b.reshape(1, Fd))


def _filt_trunk_body(z_ref, w1_ref, b1_ref, fr_ref, w2_ref, b2_ref, o_ref):
    a = jnp.dot(z_ref[...], w1_ref[...], precision=HP, preferred_element_type=F32) + b1_ref[...]
    a = jnp.sin(fr_ref[0:1, :] * a)
    a = jnp.dot(a, w2_ref[...], precision=HP, preferred_element_type=F32) + b2_ref[...]
    o_ref[...] = jnp.sin(fr_ref[1:2, :] * a)


def _filt_main_body(a_ref, wf_ref, wb_ref, t_ref, dl_ref, o_ref, *, n):
    a = a_ref[...]
    hf = jnp.dot(a, wf_ref[...], precision=HP, preferred_element_type=F32)
    hb = jnp.dot(a, wb_ref[...], precision=HP, preferred_element_type=F32)
    r = lax.broadcasted_iota(jnp.int32, hf.shape, 0)
    k = jnp.where(r < n, hf, jnp.where(r > n, hb, 0.0)) * jnp.exp(-t_ref[...] * dl_ref[...])
    o_ref[...] = k / jnp.sum(jnp.abs(k), axis=0, keepdims=True)


def _hyena_filters(n, w1, b1, freq, w2, b2, w3):
    Hd = w1.shape[1]
    OC = w3.shape[1] // 2
    t = jnp.linspace(0.0, 1.0, n, dtype=F32)[:, None]
    w = (2.0 * math.pi / n) * jnp.arange(n, dtype=F32)[:, None]
    f = jnp.linspace(1e-4, HY_BANDS - 1, HY_BANDS, dtype=F32)[None, :]
    z = jnp.concatenate([t, jnp.cos(f * w), -jnp.sin(f * w)], axis=-1)
    fold = lambda a: jnp.concatenate([a, jnp.zeros_like(a[:1]), a[:0:-1]], axis=0)
    EP = 64
    z2 = jnp.pad(fold(z), ((0, 0), (0, EP - HY_EMB)))
    w1p = jnp.pad(w1, ((0, EP - HY_EMB), (0, 0)))
    rt = _tile(2 * n, (1024, 512, 256))
    a2 = _call(
        _filt_trunk_body, grid=(2 * n // rt,),
        in_specs=[pl.BlockSpec((rt, EP), lambda i: (i, 0)),
                  pl.BlockSpec((EP, Hd), lambda i: (0, 0)),
                  pl.BlockSpec((1, Hd), lambda i: (0, 0)),
                  pl.BlockSpec((2, Hd), lambda i: (0, 0)),
                  pl.BlockSpec((Hd, Hd), lambda i: (0, 0)),
                  pl.BlockSpec((1, Hd), lambda i: (0, 0))],
        out_specs=pl.BlockSpec((rt, Hd), lambda i: (i, 0)),
        out_shape=jax.ShapeDtypeStruct((2 * n, Hd), F32),
        sem=("parallel",), name="hyena_filter_trunk")(z2, w1p, b1.reshape(1, Hd), freq, w2, b2.reshape(1, Hd))
    tc = LANES
    t2 = jnp.broadcast_to(fold(t), (2 * n, tc))
    deltas = jnp.abs(jnp.linspace(HY_MIN_DECAY, HY_MAX_DECAY, OC, dtype=F32))[None, :]
    nc = OC // tc
    return _call(
        functools.partial(_filt_main_body, n=n), grid=(nc,),
        in_specs=[pl.BlockSpec((2 * n, Hd), lambda c: (0, 0)),
                  pl.BlockSpec((Hd, tc), lambda c: (0, c)),
                  pl.BlockSpec((Hd, tc), lambda c: (0, nc + c)),
                  pl.BlockSpec((2 * n, tc), lambda c: (0, 0)),
                  pl.BlockSpec((1, tc), lambda c: (0, c))],
        out_specs=pl.BlockSpec((2 * n, tc), lambda c: (0, c)),
        out_shape=jax.ShapeDtypeStruct((2 * n, OC), F32),
        sem=("parallel",), name="hyena_filter")(a2, w3, w3, t2, deltas)


def _embed(re, im):
    return np.block([[re, -im], [im, re]])


@functools.lru_cache(maxsize=None)
def _fft_tables(N1, N2):
    N = N1 * N2
    S1 = N1 // 2
    i1 = np.arange(N1)
    ang = -2.0 * np.pi * ((i1[:, None] * i1[None, :]) % N1) / N1
    fr, fi = np.cos(ang), np.sin(ang)
    f1_pair = _embed(fr[:, :S1], fi[:, :S1])
    f1_real = np.concatenate([fr, fi], axis=0)
    i2 = np.arange(N2)
    fidx = i1[:, None, None] + N1 * i2[None, :, None]
    ang = -2.0 * np.pi * ((fidx * i2[None, None, :]) % N) / N
    mr, mi = np.cos(ang), np.sin(ang)
    m_fwd = np.stack([_embed(mr[a], mi[a]) for a in range(N1)])
    m_inv = np.stack([_embed(mr[a].T, -mi[a].T) for a in range(N1)])
    ang = 2.0 * np.pi * ((i1[:S1, None] * i1[None, :]) % N1) / N1
    f1_inv = _embed(np.cos(ang) / N, np.sin(ang) / N)
    cvt = lambda a: jnp.asarray(a, dtype=BF16)
    return cvt(f1_pair), cvt(f1_real), cvt(m_fwd), cvt(m_inv), cvt(f1_inv)


def _lmat_body(f_ref, x_ref, o_ref):
    o_ref[0] = jnp.dot(f_ref[...], x_ref[0].astype(BF16), preferred_element_type=F32).astype(o_ref.dtype)


def _left_matmul(fm, x, out_dtype, name):
    P, K, W = x.shape
    R = fm.shape[0]
    tw = _tile(W, (4096, 2048, 1024, 512, 256, 128))
    return _call(
        _lmat_body, grid=(P, W // tw),
        in_specs=[pl.BlockSpec((R, K), lambda p, j: (0, 0)),
                  pl.BlockSpec((1, K, tw), lambda p, j: (p, 0, j))],
        out_specs=pl.BlockSpec((1, R, tw), lambda p, j: (p, 0, j)),
        out_shape=jax.ShapeDtypeStruct((P, R, W), out_dtype),
        sem=("parallel", "parallel"), name=name)(fm, x)


def _cmul(xr, xi, kr, ki):
    return xr * kr - xi * ki, xr * ki + xi * kr


def _fftmid_body(a_ref, m_ref, mi_ref, k_ref, o_ref, *, FB, N2):
    for t in range(FB):
        a = a_ref[0, :, t].reshape(2 * N2, a_ref.shape[-1])
        x = jnp.dot(m_ref[t], a, preferred_element_type=F32)
        yr, yi = _cmul(x[:N2], x[N2:], k_ref[0, t], k_ref[1, t])
        y = jnp.concatenate([yr, yi], axis=0).astype(BF16)
        g = jnp.dot(mi_ref[t], y, preferred_element_type=F32)
        o_ref[0, :, t] = g.reshape(2, N2, g.shape[-1]).astype(o_ref.dtype)


def _fftfwd_body(a_ref, m_ref, o_ref, *, FB, N2):
    for t in range(FB):
        a = a_ref[:, t].reshape(2 * N2, a_ref.shape[-1])
        x = jnp.dot(m_ref[t], a, preferred_element_type=F32)
        o_ref[:, t] = x.reshape(2, N2, x.shape[-1])


def _fftfin_body(f_ref, g_ref, z_ref, x1_ref, sk_ref, o_ref):
    y = jnp.dot(f_ref[...], g_ref[0], preferred_element_type=F32)
    o_ref[0] = (x1_ref[0] * (y + sk_ref[...] * z_ref[0])).astype(o_ref.dtype)


def _filter_spectrum_2stage(k, N1, N2):
    N, OC = k.shape
    _, f1_real, m_fwd, _, _ = _fft_tables(N1, N2)
    a = _left_matmul(f1_real, k.reshape(1, N1, N2 * OC), BF16, "filter_dft1")
    a = a.reshape(2, N1, N2, OC)
    FB = _tile(N1, (4, 2, 1))
    tc = _tile(OC, (512, 256, 128))
    return _call(
        functools.partial(_fftfwd_body, FB=FB, N2=N2), grid=(N1 // FB, OC // tc),
        in_specs=[pl.BlockSpec((2, FB, N2, tc), lambda f, c: (0, f, 0, c)),
                  pl.BlockSpec((FB, 2 * N2, 2 * N2), lambda f, c: (f, 0, 0))],
        out_specs=pl.BlockSpec((2, FB, N2, tc), lambda f, c: (0, f, 0, c)),
        out_shape=jax.ShapeDtypeStruct((2, N1, N2, OC), F32),
        sem=("parallel", "parallel"), name="filter_dft2")(a, m_fwd)


def _long_conv_gate_2stage(z, x1, skip, kf, order, out_dtype, N1, N2):
    B, n, C = z.shape
    assert B % 2 == 0
    P, S1 = B // 2, N1 // 2
    f1_pair, _, m_fwd, m_inv, f1_inv = _fft_tables(N1, N2)
    W = N2 * C
    zr = z.reshape(P, 2 * S1, W)
    a = _left_matmul(f1_pair, zr, BF16, "conv_dft1").reshape(P, 2, N1, N2, C)
    FB = _tile(N1, (4, 2, 1))
    tc = _tile(C, (512, 256, 128))
    oc = order * (C // tc)
    g = _call(
        functools.partial(_fftmid_body, FB=FB, N2=N2), grid=(N1 // FB, C // tc, P),
        in_specs=[pl.BlockSpec((1, 2, FB, N2, tc), lambda f, c, p: (p, 0, f, 0, c)),
                  pl.BlockSpec((FB, 2 * N2, 2 * N2), lambda f, c, p: (f, 0, 0)),
                  pl.BlockSpec((FB, 2 * N2, 2 * N2), lambda f, c, p: (f, 0, 0)),
                  pl.BlockSpec((2, FB, N2, tc), lambda f, c, p: (0, f, 0, oc + c))],
        out_specs=pl.BlockSpec((1, 2, FB, N2, tc), lambda f, c, p: (p, 0, f, 0, c)),
        out_shape=jax.ShapeDtypeStruct((P, 2, N1, N2, C), BF16),
        sem=("parallel", "parallel", "arbitrary"), name="conv_dft2_mul_idft2")(a, m_fwd, m_inv, kf)
    tw = _tile(W, (4096, 2048, 1024, 512, 256, 128))
    out = _call(
        _fftfin_body, grid=(P, W // tw),
        in_specs=[pl.BlockSpec((2 * S1, 2 * N1), lambda p, j: (0, 0)),
                  pl.BlockSpec((1, 2 * N1, tw), lambda p, j: (p, 0, j)),
                  pl.BlockSpec((1, 2 * S1, tw), lambda p, j: (p, 0, j)),
                  pl.BlockSpec((1, 2 * S1, tw), lambda p, j: (p, 0, j)),
                  pl.BlockSpec((1, tw), lambda p, j: (0, j))],
        out_specs=pl.BlockSpec((1, 2 * S1, tw), lambda p, j: (p, 0, j)),
        out_shape=jax.ShapeDtypeStruct((P, 2 * S1, W), out_dtype),
        sem=("parallel", "parallel"), name="conv_idft1_gate")(
            f1_inv, g.reshape(P, 2 * N1, W), zr, x1.reshape(P, 2 * S1, W), jnp.tile(skip, N2)[None, :])
    return out.reshape(B, n, C)


@functools.lru_cache(maxsize=None)
def _dft_tables(n):
    N = 2 * n
    f = np.arange(N)
    ang = -2.0 * np.pi * ((f[:, None] * f[None, :]) % N) / N
    fr, fi = np.cos(ang), np.sin(ang)
    fwd_full = np.concatenate([fr, fi], axis=0)
    fwd_half = fwd_full[:, :n]
    inv = np.concatenate([fr[:n, :], fi[:n, :]], axis=1) / N
    cvt = lambda a: jnp.asarray(a, dtype=BF16)
    return cvt(fwd_full), cvt(fwd_half), cvt(inv)


def _dftconv_body(z_ref, x1_ref, sk_ref, f_ref, fi_ref, k_ref, o_ref, *, N):
    z = z_ref[0]
    x = jnp.dot(f_ref[...], z.astype(BF16), preferred_element_type=F32)
    yr, yi = _cmul(x[:N], x[N:], k_ref[0], k_ref[1])
    y = jnp.concatenate([yr, yi], axis=0).astype(BF16)
    y = jnp.dot(fi_ref[...], y, preferred_element_type=F32)
    o_ref[0] = (x1_ref[0] * (y + sk_ref[...] * z)).astype(o_ref.dtype)


def _long_conv_gate_dense(z, x1, skip, kf, order, out_dtype):
    B, n, C = z.shape
    N = 2 * n
    _, fwd_half, inv = _dft_tables(n)
    tc = _tile(C, (256, 128))
    oc = order * (C // tc)
    return _call(
        functools.partial(_dftconv_body, N=N), grid=(C // tc, B),
        in_specs=[pl.BlockSpec((1, n, tc), lambda c, b: (b, 0, c)),
                  pl.BlockSpec((1, n, tc), lambda c, b: (b, 0, c)),
                  pl.BlockSpec((1, tc), lambda c, b: (0, c)),
                  pl.BlockSpec((2 * N, n), lambda c, b: (0, 0)),
                  pl.BlockSpec((n, 2 * N), lambda c, b: (0, 0)),
                  pl.BlockSpec((2, N, tc), lambda c, b: (0, 0, oc + c))],
        out_specs=pl.BlockSpec((1, n, tc), lambda c, b: (b, 0, c)),
        out_shape=jax.ShapeDtypeStruct((B, n, C), out_dtype),
        sem=("parallel", "arbitrary"), name="conv_dense_dft")(z, x1, skip[None, :], fwd_half, inv, kf)


def _hyena_mixer(u, conv_w, conv_b, w1, b1, freq, w2, b2, w3, skip):
    B, n, _ = u.shape
    v, x1, x2 = _hyena_dwconv(u, conv_w, conv_b)
    k = _hyena_filters(n, w1, b1, freq, w2, b2, w3)
    N = 2 * n
    if N % FFT_N2 == 0 and (N // FFT_N2) >= 16:
        N1 = N // FFT_N2
        kf = _filter_spectrum_2stage(k, N1, FFT_N2)
        z = _long_conv_gate_2stage(v, x1, skip[0], kf, 0, F32, N1, FFT_N2)
        return _long_conv_gate_2stage(z, x2, skip[1], kf, 1, BF16, N1, FFT_N2)
    fwd_full, _, _ = _dft_tables(n)
    kf = _left_matmul(fwd_full, k[None], F32, "filter_dense_dft").reshape(2, N, k.shape[1])
    z = _long_conv_gate_dense(v, x1, skip[0], kf, 0, F32)
    return _long_conv_gate_dense(z, x2, skip[1], kf, 1, BF16)


def _nt(a, b):
    return lax.dot_general(a, b, (((1,), (1,)), ((), ())), preferred_element_type=F32)


def _na_body(q_ref, k_ref, v_ref, kc_ref, vc_ref, b_ref, o_ref, *, R, KR, rows, scale):
    j = pl.program_id(2)
    start = pl.multiple_of(jnp.clip(j * R - NA_WIN_R // 2, 0, rows - KR) * GRID_W, GRID_W)
    q = q_ref[0]
    kw = k_ref[0, pl.ds(start, KR * GRID_W), :]
    vw = v_ref[0, pl.ds(start, KR * GRID_W), :]
    s = _nt(q, kw) * scale + b_ref[0, 0]
    sc = _nt(q, kc_ref[0]) * scale
    m = jnp.maximum(jnp.max(s, axis=-1, keepdims=True), jnp.max(sc, axis=-1, keepdims=True))
    p = jnp.exp(s - m)
    pc = jnp.exp(sc - m)
    l = jnp.sum(p, axis=-1, keepdims=True) + jnp.sum(pc, axis=-1, keepdims=True)
    o = jnp.dot(p.astype(BF16), vw, preferred_element_type=F32)
    o = o + jnp.dot(pc.astype(BF16), vc_ref[0], preferred_element_type=F32)
    o_ref[0] = (o / l).astype(o_ref.dtype)


def _na_geometry(S):
    rows = S // GRID_W
    kr = min(NA_WIN_R, rows)
    R = min(8, rows)
    KR = min(rows, R + kr)
    nb = rows // R
    assert rows % R == 0
    types = sorted({0, min(1, nb - 1), nb - 1})
    if nb > 3:
        offs = {int(np.clip(j * R - NA_WIN_R // 2, 0, rows - KR)) - j * R for j in range(1, nb - 1)}
        assert len(offs) == 1
    return rows, kr, R, KR, nb, types


def _nabias_body(rpb_ref, o_ref, tw_ref, *, plan, R, KR):
    W = GRID_W
    nd_r, nd_c = 2 * NA_WIN_R - 1, 2 * NA_WIN_C - 1
    base = pl.program_id(0) * (nd_r * nd_c)
    qc = lax.broadcasted_iota(jnp.int32, (W, 2 * W), 0)
    lane = lax.broadcasted_iota(jnp.int32, (W, 2 * W), 1)
    kc = lane % W
    cs = jnp.clip(qc - NA_WIN_C // 2, 0, W - NA_WIN_C)
    col_ok = jnp.logical_and(kc >= cs, kc < cs + NA_WIN_C)
    dcm = kc - qc + (NA_WIN_C - 1)
    neg = jnp.full((W, 2 * W), NEG_INF, F32)
    for dr in range(nd_r):
        acc = neg
        for dc in range(nd_c):
            acc = jnp.where(dcm == dc, rpb_ref[base + dr * nd_c + dc], acc)
        tw_ref[dr] = jnp.where(col_ok, acc, NEG_INF)
    left = lane < W
    for t, per_q in enumerate(plan):
        for qr in range(R):
            for kp in range(KR // 2):
                d0, d1 = per_q[qr][kp]
                a = neg if d0 is None else tw_ref[d0]
                b = neg if d1 is None else tw_ref[d1]
                blk = neg if (d0 is None and d1 is None) else jnp.where(left, a, b)
                o_ref[0, t, qr * W:(qr + 1) * W, kp * 2 * W:(kp + 1) * 2 * W] = blk


def _na_bias_tables(rpb, S):
    rows, kr, R, KR, nb, types = _na_geometry(S)
    assert 2 * GRID_W == LANES and KR % 2 == 0
    plan = []
    for jt in types:
        start = int(np.clip(jt * R - NA_WIN_R // 2, 0, rows - KR))
        per_q = []
        for q in range(R):
            qra = jt * R + q
            ws = int(np.clip(qra - kr // 2, 0, rows - kr))
            d = [(start + k) - qra + (NA_WIN_R - 1) if ws <= start + k < ws + kr else None for k in range(KR)]
            per_q.append([(d[2 * p], d[2 * p + 1]) for p in range(KR // 2)])
        plan.append(per_q)
    L, H, nd_r, nd_c = rpb.shape
    T, QB, KB = len(types), R * GRID_W, KR * GRID_W
    return _call(
        functools.partial(_nabias_body, plan=plan, R=R, KR=KR), grid=(L * H,),
        in_specs=[pl.BlockSpec(memory_space=pltpu.SMEM)],
        out_specs=pl.BlockSpec((1, T, QB, KB), lambda i: (i, 0, 0, 0)),
        out_shape=jax.ShapeDtypeStruct((L * H, T, QB, KB), F32),
        scratch=[pltpu.VMEM((nd_r, GRID_W, 2 * GRID_W), F32)],
        sem=("parallel",), name="na_bias_table")(rpb.reshape(-1).astype(F32))


def _na_attention(qkv, qkv_c, bias, l, offs, H):
    B, S, _ = qkv.shape
    CTX = qkv_c.shape[1]
    rows, kr, R, KR, nb, types = _na_geometry(S)
    T = len(types)
    QB, KB = R * GRID_W, KR * GRID_W
    ok_, ov_, oq_ = (offs[n] // HEAD_DIM for n in ("na_k", "na_v", "na_q"))

    def btype(j):
        if T == nb:
            return j
        return jnp.where(j == 0, 0, jnp.where(j == nb - 1, T - 1, 1))

    return _call(
        functools.partial(_na_body, R=R, KR=KR, rows=rows, scale=HEAD_DIM ** -0.5),
        grid=(B, H, nb),
        in_specs=[pl.BlockSpec((1, QB, HEAD_DIM), lambda b, h, j: (b, j, oq_ + h)),
                  pl.BlockSpec((1, S, HEAD_DIM), lambda b, h, j: (b, 0, ok_ + h)),
                  pl.BlockSpec((1, S, HEAD_DIM), lambda b, h, j: (b, 0, ov_ + h)),
                  pl.BlockSpec((1, CTX, HEAD_DIM), lambda b, h, j: (b, 0, ok_ + h)),
                  pl.BlockSpec((1, CTX, HEAD_DIM), lambda b, h, j: (b, 0, ov_ + h)),
                  pl.BlockSpec((1, 1, QB, KB), lambda b, h, j: (l * H + h, btype(j), 0, 0))],
        out_specs=pl.BlockSpec((1, QB, HEAD_DIM), lambda b, h, j: (b, j, h)),
        out_shape=jax.ShapeDtypeStruct((B, S, H * HEAD_DIM), BF16),
        sem=("parallel", "parallel", "arbitrary"), name="na_attention")(qkv, qkv, qkv, qkv_c, qkv_c, bias)


def _stack_heads(q2, G):
    return jnp.concatenate([q2[:, g * HEAD_DIM:(g + 1) * HEAD_DIM] for g in range(G)], axis=0)


def _unstack_heads(o, G, n):
    return jnp.concatenate([o[g * n:(g + 1) * n] for g in range(G)], axis=1)


def _sink_column(sink_ref, h0, G, n):
    return jnp.concatenate([jnp.full((n, 1), sink_ref[h0 + g], F32) for g in range(G)], axis=0)


def _swa_body(sink_ref, q_ref, k_ref, v_ref, kc_ref, vc_ref, o_ref, *, QB, KB, S, G, scale):
    kvh, j = pl.program_id(1), pl.program_id(2)
    start = pl.multiple_of(jnp.clip(j * QB - GQA_WINDOW, 0, S - KB), LANES)
    q = _stack_heads(q_ref[0], G)
    kw = k_ref[0, pl.ds(start, KB), :]
    vw = v_ref[0, pl.ds(start, KB), :]
    qpos = j * QB + lax.broadcasted_iota(jnp.int32, (QB, KB), 0)
    kpos = start + lax.broadcasted_iota(jnp.int32, (QB, KB), 1)
    mask = jnp.where(jnp.abs(qpos - kpos) <= GQA_WINDOW, 0.0, NEG_INF)
    s = _nt(q, kw) * scale + jnp.concatenate([mask] * G, axis=0)
    sc = _nt(q, kc_ref[0]) * scale
    sk = _sink_column(sink_ref, kvh * G, G, QB)
    m = jnp.maximum(jnp.maximum(jnp.max(s, axis=-1, keepdims=True), jnp.max(sc, axis=-1, keepdims=True)), sk)
    p = jnp.exp(s - m)
    pc = jnp.exp(sc - m)
    l = jnp.sum(p, axis=-1, keepdims=True) + jnp.sum(pc, axis=-1, keepdims=True) + jnp.exp(sk - m)
    o = jnp.dot(p.astype(BF16), vw, preferred_element_type=F32)
    o = o + jnp.dot(pc.astype(BF16), vc_ref[0], preferred_element_type=F32)
    o_ref[0] = _unstack_heads(o / l, G, QB).astype(o_ref.dtype)


def _swa_attention(qkv, qkv_c, sink, offs, KVH):
    B, S, _ = qkv.shape
    CTX = qkv_c.shape[1]
    G = GQA_GROUP
    QB = _tile(S, (512, 256, 128))
    KB = min(S, QB + 2 * GQA_WINDOW)
    ok_, ov_ = offs["sw_k"] // HEAD_DIM, offs["sw_v"] // HEAD_DIM
    oq_ = offs["sw_q"] // (G * HEAD_DIM)
    return _call(
        functools.partial(_swa_body, QB=QB, KB=KB, S=S, G=G, scale=HEAD_DIM ** -0.5),
        grid=(B, KVH, S // QB),
        in_specs=[pl.BlockSpec(memory_space=pltpu.SMEM),
                  pl.BlockSpec((1, QB, G * HEAD_DIM), lambda b, h, j: (b, j, oq_ + h)),
                  pl.BlockSpec((1, S, HEAD_DIM), lambda b, h, j: (b, 0, ok_ + h)),
                  pl.BlockSpec((1, S, HEAD_DIM), lambda b, h, j: (b, 0, ov_ + h)),
                  pl.BlockSpec((1, CTX, HEAD_DIM), lambda b, h, j: (b, 0, ok_ + h)),
                  pl.BlockSpec((1, CTX, HEAD_DIM), lambda b, h, j: (b, 0, ov_ + h))],
        out_specs=pl.BlockSpec((1, QB, G * HEAD_DIM), lambda b, h, j: (b, j, h)),
        out_shape=jax.ShapeDtypeStruct((B, S, KVH * G * HEAD_DIM), BF16),
        sem=("parallel", "parallel", "arbitrary"), name="swa_attention")(sink, qkv, qkv, qkv, qkv_c, qkv_c)


def _cattn_body(sink_ref, q_ref, k_ref, v_ref, o_ref, *, G, use_sink, scale):
    h = pl.program_id(1)
    n = q_ref.shape[1]
    q = _stack_heads(q_ref[0], G)
    s = _nt(q, k_ref[0]) * scale
    m = jnp.max(s, axis=-1, keepdims=True)
    if use_sink:
        sk = _sink_column(sink_ref, h * G, G, n)
        m = jnp.maximum(m, sk)
    p = jnp.exp(s - m)
    l = jnp.sum(p, axis=-1, keepdims=True)
    if use_sink:
        l = l + jnp.exp(sk - m)
    o = jnp.dot(p.astype(BF16), v_ref[0], preferred_element_type=F32)
    o_ref[0] = _unstack_heads(o / l, G, n).astype(o_ref.dtype)


def _ctx_attention(qkv_c, sink, oq, ok, ov, KVH, G, use_sink):
    B, n, _ = qkv_c.shape
    oq_, ok_, ov_ = oq // (G * HEAD_DIM), ok // HEAD_DIM, ov // HEAD_DIM
    return _call(
        functools.partial(_cattn_body, G=G, use_sink=use_sink, scale=HEAD_DIM ** -0.5),
        grid=(B, KVH),
        in_specs=[pl.BlockSpec(memory_space=pltpu.SMEM),
                  pl.BlockSpec((1, n, G * HEAD_DIM), lambda b, h: (b, 0, oq_ + h)),
                  pl.BlockSpec((1, n, HEAD_DIM), lambda b, h: (b, 0, ok_ + h)),
                  pl.BlockSpec((1, n, HEAD_DIM), lambda b, h: (b, 0, ov_ + h))],
        out_specs=pl.BlockSpec((1, n, G * HEAD_DIM), lambda b, h: (b, 0, h)),
        out_shape=jax.ShapeDtypeStruct((B, n, KVH * G * HEAD_DIM), BF16),
        sem=("parallel", "parallel"), name="ctx_attention")(sink, qkv_c, qkv_c, qkv_c)


def _merge_body(x_ref, g_ref, sh_ref, sc_ref, yh, yn, ys, wgh, wgn, wgs, wh, wn, ws, o_ref, h_ref, *, eps):
    @pl.when(pl.program_id(2) == 0)
    def _():
        h_ref[...] = _norm_modulate(x_ref[0], g_ref, sh_ref, sc_ref, eps).astype(BF16)

    h = h_ref[...]
    gate = lambda wg: _sigmoid(jnp.dot(h, wg[...], preferred_element_type=F32))
    m = gate(wgh) * jnp.dot(yh[0], wh[...], preferred_element_type=F32)
    m = m + gate(wgn) * jnp.dot(yn[0], wn[...], preferred_element_type=F32)
    m = m + gate(wgs) * jnp.dot(ys[0], ws[...], preferred_element_type=F32)
    o_ref[0] = m.astype(o_ref.dtype)


def _merge_branches(x, gain, shift, scale, w_in, gate_start, y_hy, y_na, y_sw, w_br, l):
    B, S, D = x.shape
    widths = (y_hy.shape[2], y_na.shape[2], y_sw.shape[2])
    starts = (0, widths[0], widths[0] + widths[1])
    assert all(s % w == 0 for s, w in zip(starts, widths))
    tm = _tile(S, (512, 256, 128))
    tn = _seg_tile([(gate_start, D)], (512, 256, 128))
    nj, gj = D // tn, gate_start // tn
    yspec = lambda y: pl.BlockSpec((1, tm, y.shape[2]), lambda b, i, j: (b, i, 0))
    gspec = lambda k: pl.BlockSpec((None, D, tn), lambda b, i, j: (l, 0, gj + k * nj + j))
    wspec = lambda k: pl.BlockSpec((None, widths[k], tn), lambda b, i, j: (l, starts[k] // widths[k], j))
    return _call(
        functools.partial(_merge_body, eps=NORM_EPS), grid=(B, S // tm, nj),
        in_specs=[pl.BlockSpec((1, tm, D), lambda b, i, j: (b, i, 0)),
                  pl.BlockSpec((1, D), lambda b, i, j: (0, 0)),
                  pl.BlockSpec((1, 1, D), lambda b, i, j: (b, 0, 0)),
                  pl.BlockSpec((1, 1, D), lambda b, i, j: (b, 0, 0)),
                  yspec(y_hy), yspec(y_na), yspec(y_sw), gspec(0), gspec(1), gspec(2),
                  wspec(0), wspec(1), wspec(2)],
        out_specs=pl.BlockSpec((1, tm, tn), lambda b, i, j: (b, i, j)),
        out_shape=jax.ShapeDtypeStruct((B, S, D), BF16),
        scratch=[pltpu.VMEM((tm, D), BF16)],
        sem=("parallel", "parallel", "arbitrary"), name="gated_merge")(
            x, gain.reshape(1, D), shift, scale, y_hy, y_na, y_sw, w_in, w_in, w_in, w_br, w_br, w_br)


def _mmres_body(a_ref, w_ref, x_ref, g_ref, mg_ref, o_ref, acc_ref, *, nk, eps):
    k = pl.program_id(2)

    @pl.when(k == 0)
    def _():
        acc_ref[...] = jnp.zeros_like(acc_ref)

    acc_ref[...] += jnp.dot(a_ref[0], w_ref[...], preferred_element_type=F32)

    @pl.when(k == nk - 1)
    def _():
        y = acc_ref[...]
        yn = y * lax.rsqrt(jnp.mean(y * y, axis=-1, keepdims=True) + eps) * g_ref[...]
        o_ref[0] = x_ref[0] + mg_ref[0] * yn


def _matmul_norm_residual(a, w, l, x, gain, mgate, name):
    B, S, K = a.shape
    D = w.shape[2]
    tm = _tile(S, (1024, 512, 256, 128))
    tk = _tile(K, (512, 256, 128))
    nk = K // tk
    return _call(
        functools.partial(_mmres_body, nk=nk, eps=NORM_EPS), grid=(B, S // tm, nk),
        in_specs=[pl.BlockSpec((1, tm, tk), lambda b, i, k: (b, i, k)),
                  pl.BlockSpec((None, tk, D), lambda b, i, k: (l, k, 0)),
                  pl.BlockSpec((1, tm, D), lambda b, i, k: (b, i, 0)),
                  pl.BlockSpec((1, D), lambda b, i, k: (0, 0)),
                  pl.BlockSpec((1, 1, D), lambda b, i, k: (b, 0, 0))],
        out_specs=pl.BlockSpec((1, tm, D), lambda b, i, k: (b, i, 0)),
        out_shape=jax.ShapeDtypeStruct((B, S, D), F32),
        scratch=[pltpu.VMEM((tm, D), F32)],
        sem=("parallel", "parallel", "arbitrary"), name=name)(a, w, x, gain.reshape(1, D), mgate)


def _rope_tables(n):
    t = jnp.arange(n)
    row = (t // GRID_W).astype(F32)
    col = (t % GRID_W).astype(F32)
    per_axis = HEAD_DIM // 2
    inv = ROPE_BASE ** (-jnp.arange(0, per_axis, 2, dtype=F32) / per_axis)
    ar, ac = row[:, None] * inv, col[:, None] * inv
    cos_t = jnp.concatenate([jnp.cos(ar), jnp.cos(ar), jnp.cos(ac), jnp.cos(ac)], axis=1)
    sin_t = jnp.concatenate([-jnp.sin(ar), jnp.sin(ar), -jnp.sin(ac), jnp.sin(ac)], axis=1)
    return cos_t, sin_t


def kernel(x, c, ctx, c_ctx, w_mod, b_mod, norm_gains, w_in, hy_conv_w, hy_conv_b, hy_w1, hy_b1, hy_freq, hy_w2, hy_b2, hy_w3, hy_skip, na_rpb, swa_sink, w_branch, w_out, ffn_w_up, ffn_conv_w, ffn_conv_b, ffn_w_down):
    B, S, D = x.shape
    L = w_mod.shape[0]
    C = hy_skip.shape[-1]
    H_na = na_rpb.shape[1]
    H_q = swa_sink.shape[1]
    KVH = H_q // GQA_GROUP
    NA_W, QW, KVW = H_na * HEAD_DIM, H_q * HEAD_DIM, KVH * HEAD_DIM
    KV_COLS = 2 * NA_W + 2 * KVW
    qkv_segs = [(0, KV_COLS), (KV_COLS + 3 * C, NA_W + QW)]
    hy_segs = [(KV_COLS, 3 * C)]
    gate_start = KV_COLS + 3 * C + NA_W + QW
    offs = {"na_k": 0, "na_v": NA_W, "sw_k": 2 * NA_W, "sw_v": 2 * NA_W + KVW,
            "na_q": KV_COLS, "sw_q": KV_COLS + NA_W}
    rope_chunks = (list(range(offs["sw_k"] // LANES, (offs["sw_k"] + KVW) // LANES))
                   + list(range(offs["sw_q"] // LANES, (offs["sw_q"] + QW) // LANES)))
    cos_t, sin_t = _rope_tables(S)

    w_in_b, w_out_b = w_in.astype(BF16), w_out.astype(BF16)
    w_br_b, w_up_b, w_dn_b = w_branch.astype(BF16), ffn_w_up.astype(BF16), ffn_w_down.astype(BF16)
    na_bias = _na_bias_tables(na_rpb, S)

    R = -(-(B + 1) // 8) * 8
    cc = jnp.concatenate([c, c_ctx[None, :], jnp.zeros((R - B - 1, D), F32)], axis=0)
    mods = _modulation(cc, w_mod, b_mod)

    xc = ctx
    for l in range(L):
        mod = [mods[l, :B, k * D:(k + 1) * D].reshape(B, 1, D) for k in range(6)]
        mod_c = [jnp.broadcast_to(mods[l, B, k * D:(k + 1) * D].reshape(1, 1, D), (B, 1, D)) for k in range(6)]
        g = norm_gains[l]
        hy_p = (hy_conv_w[l], hy_conv_b[l], hy_w1[l], hy_b1[l], hy_freq[l], hy_w2[l], hy_b2[l], hy_w3[l], hy_skip[l])

        qkv_c = _norm_mod_matmul(xc, g[0], mod_c[0], mod_c[1], w_in_b, l, qkv_segs, BF16, name="ctx_qkv_proj")
        qkv = _norm_mod_matmul(x, g[0], mod[0], mod[1], w_in_b, l, qkv_segs, BF16,
                               rope=(cos_t, sin_t, rope_chunks), name="qkv_proj")
        hy_u = _norm_mod_matmul(x, g[0], mod[0], mod[1], w_in_b, l, hy_segs, F32, name="hyena_proj")
        y_hy = _hyena_mixer(hy_u, *hy_p)
        y_na = _na_attention(qkv, qkv_c, na_bias, l, offs, H_na)
        y_sw = _swa_attention(qkv, qkv_c, swa_sink[l], offs, KVH)
        m = _merge_branches(x, g[0], mod[0], mod[1], w_in_b, gate_start, y_hy, y_na, y_sw, w_br_b, l)
        x = _matmul_norm_residual(m, w_out_b, l, x, g[1], mod[2], "out_proj_residual")
        gl = _ffn_up_glu(x, g[2], mod[3], mod[4], w_up_b, l, ffn_conv_w[l], ffn_conv_b[l])
        x = _matmul_norm_residual(gl, w_dn_b, l, x, g[3], mod[5], "ffn_down_residual")

        if l < L - 1:
            hy_c = _norm_mod_matmul(xc, g[0], mod_c[0], mod_c[1], w_in_b, l, hy_segs, F32, name="ctx_hyena_proj")
            yc_hy = _hyena_mixer(hy_c, *hy_p)
            yc_na = _ctx_attention(qkv_c, swa_sink[l], offs["na_q"], offs["na_k"], offs["na_v"], H_na, 1, False)
            yc_sw = _ctx_attention(qkv_c, swa_sink[l], offs["sw_q"], offs["sw_k"], offs["sw_v"], KVH, GQA_GROUP, True)
            mc = _merge_branches(xc, g[0], mod_c[0], mod_c[1], w_in_b, gate_start, yc_hy, yc_na, yc_sw, w_br_b, l)
            xc = _matmul_norm_residual(mc, w_out_b, l, xc, g[1], mod_c[2], "ctx_out_proj_residual")
            gl_c = _ffn_up_glu(xc, g[2], mod_c[3], mod_c[4], w_up_b, l, ffn_conv_w[l], ffn_conv_b[l])
            xc = _matmul_norm_residual(gl_c, w_dn_b, l, xc, g[3], mod_c[5], "ctx_ffn_down_residual")
    return x
```

```python
import functools
import math

import numpy as np
import jax
import jax.numpy as jnp
from jax import lax
from jax.experimental import pallas as pl
from jax.experimental.pallas import tpu as pltpu

F32 = jnp.float32
BF16 = jnp.bfloat16
NEG_INF = -1e30

GRID_W = 64
HEAD_DIM = 128
HY_EMB = 33
HY_BANDS = (HY_EMB - 1) // 2
HY_FAST_DECAY = 0.3
HY_SLOW_DECAY = 1.5
HY_DECAY_TARGET = 1e-2
HY_MAX_DECAY = math.log(HY_DECAY_TARGET) / HY_FAST_DECAY
HY_MIN_DECAY = math.log(HY_DECAY_TARGET) / HY_SLOW_DECAY
NA_WIN_R = 8
NA_WIN_C = 16
GQA_GROUP = 2
GQA_WINDOW = 128
ROPE_BASE = 10000.0
NORM_EPS = 1e-6

LANES = 128
FFT_N2 = 128
VMEM_LIMIT = 56 * 1024 * 1024
HP = lax.Precision.HIGHEST


def _call(body, *, grid, in_specs, out_specs, out_shape, scratch=(), sem, name):
    return pl.pallas_call(
        body, grid=grid, in_specs=in_specs, out_specs=out_specs, out_shape=out_shape,
        scratch_shapes=list(scratch),
        compiler_params=pltpu.CompilerParams(dimension_semantics=sem, vmem_limit_bytes=VMEM_LIMIT),
        name=name)


def _tile(n, cands):
    for c in cands:
        if n % c == 0:
            return c
    raise ValueError(f"no tile for {n} in {cands}")


def _sigmoid(x):
    return 1.0 / (1.0 + jnp.exp(-x))


def _mod_body(c_ref, w_ref, b_ref, o_ref):
    c = c_ref[...]
    s = (c * _sigmoid(c)).astype(BF16)
    o_ref[0] = jnp.dot(s, w_ref[0].astype(BF16), preferred_element_type=F32) + b_ref[0]


def _modulation(cc, w_mod, b_mod):
    L, D, N = w_mod.shape
    R = cc.shape[0]
    tn = _tile(N, (1024, 512, 256, 128))
    return _call(
        _mod_body, grid=(L, N // tn),
        in_specs=[pl.BlockSpec((R, D), lambda l, j: (0, 0)),
                  pl.BlockSpec((1, D, tn), lambda l, j: (l, 0, j)),
                  pl.BlockSpec((1, 1, tn), lambda l, j: (l, 0, j))],
        out_specs=pl.BlockSpec((1, R, tn), lambda l, j: (l, 0, j)),
        out_shape=jax.ShapeDtypeStruct((L, R, N), F32),
        sem=("parallel", "parallel"), name="modulation")(cc, w_mod, b_mod.reshape(L, 1, N))


def _swap32(a):
    lane = lax.broadcasted_iota(jnp.int32, a.shape, 1)
    return jnp.where((lane & 32) == 0, pltpu.roll(a, 96, 1), pltpu.roll(a, 32, 1))


def _norm_modulate(x, g_ref, sh_ref, sc_ref, eps):
    y = x * lax.rsqrt(jnp.mean(x * x, axis=-1, keepdims=True) + eps) * g_ref[...]
    return y * (1.0 + sc_ref[0]) + sh_ref[0]


HALO = 16


def _halo_prologue(h_ref, x_ref, xp_ref, xn_ref, g_ref, sh_ref, sc_ref, eps):
    i, tm = pl.program_id(1), x_ref.shape[1]
    nm = lambda x: _norm_modulate(x, g_ref, sh_ref, sc_ref, eps)
    keep_prev = (i > 0).astype(F32)
    keep_next = (i < pl.num_programs(1) - 1).astype(F32)
    h_ref[0:HALO, :] = (nm(xp_ref[0]) * keep_prev).astype(BF16)
    h_ref[HALO:HALO + tm, :] = nm(x_ref[0]).astype(BF16)
    h_ref[HALO + tm:, :] = (nm(xn_ref[0]) * keep_next).astype(BF16)


def _conv3_halo(a, cw_ref, cb_ref, tm):
    rows = a.shape[0]
    up = pltpu.roll(a, 1, 0)[HALO:HALO + tm]
    un = pltpu.roll(a, rows - 1, 0)[HALO:HALO + tm]
    return up * cw_ref[0:1, :] + a[HALO:HALO + tm] * cw_ref[1:2, :] + un * cw_ref[2:3, :] + cb_ref[...]


def _halo_specs(S, tm, D):
    hb, nh = tm // HALO, S // HALO
    return [pl.BlockSpec((1, tm, D), lambda b, i, j: (b, i, 0)),
            pl.BlockSpec((1, HALO, D), lambda b, i, j: (b, jnp.maximum(i * hb - 1, 0), 0)),
            pl.BlockSpec((1, HALO, D), lambda b, i, j: (b, jnp.minimum((i + 1) * hb, nh - 1), 0)),
            pl.BlockSpec((1, D), lambda b, i, j: (0, 0)),
            pl.BlockSpec((1, 1, D), lambda b, i, j: (b, 0, 0)),
            pl.BlockSpec((1, 1, D), lambda b, i, j: (b, 0, 0))]


def _inproj_body(*refs, nq, nc, rope_chunks, eps):
    if rope_chunks:
        x_ref, xp_ref, xn_ref, g_ref, sh_ref, sc_ref, w_ref, cw_ref, cb_ref, cos_ref, sin_ref, q_o, o0, o1, o2, h_ref = refs
    else:
        x_ref, xp_ref, xn_ref, g_ref, sh_ref, sc_ref, w_ref, cw_ref, cb_ref, q_o, o0, o1, o2, h_ref = refs
    j = pl.program_id(2)
    tm = x_ref.shape[1]

    @pl.when(j == 0)
    def _():
        _halo_prologue(h_ref, x_ref, xp_ref, xn_ref, g_ref, sh_ref, sc_ref, eps)

    @pl.when(j < nq)
    def _():
        acc = jnp.dot(h_ref[HALO:HALO + tm, :], w_ref[...], preferred_element_type=F32)
        nch = acc.shape[1] // LANES
        rope_tiles = sorted({ch // nch for ch in rope_chunks})
        for jt in rope_tiles:
            @pl.when(j == jt)
            def _(jt=jt):
                c, s = cos_ref[...], sin_ref[...]
                for k in range(nch):
                    a = acc[:, k * LANES:(k + 1) * LANES]
                    if jt * nch + k in rope_chunks:
                        a = a * c + _swap32(a) * s
                    q_o[0, :, k * LANES:(k + 1) * LANES] = a.astype(q_o.dtype)

        if rope_tiles:
            @pl.when(functools.reduce(jnp.logical_and, [j != jt for jt in rope_tiles]))
            def _():
                q_o[0] = acc.astype(q_o.dtype)
        else:
            q_o[0] = acc.astype(q_o.dtype)

    for k, o in enumerate((o0, o1, o2)):
        @pl.when(jnp.logical_and(j >= nq + k * nc, j < nq + (k + 1) * nc))
        def _(o=o):
            a = jnp.dot(h_ref[...], w_ref[...], preferred_element_type=F32)
            o[0] = _conv3_halo(a, cw_ref, cb_ref, tm)


def _colmap(segs, tn):
    bounds, o = [], 0
    for s, w in segs:
        assert s % tn == 0 and w % tn == 0
        bounds.append(((o + w) // tn, (s - o) // tn))
        o += w

    def f(j):
        r = j + bounds[-1][1]
        for hi, off in reversed(bounds[:-1]):
            r = jnp.where(j < hi, j + off, r)
        return r
    return f, o


def _seg_tile(segs, cands):
    return _tile(functools.reduce(math.gcd, [v for seg in segs for v in seg if v]), cands)


def _in_proj(x, gain, shift, scale, w_in, l, qkv_segs, hy_start, C, conv_w, conv_b, rope, name):
    B, S, D = x.shape
    tm = _tile(S, (1024, 512, 256, 128))
    segs = list(qkv_segs) + [(hy_start, 3 * C)]
    tn = _seg_tile(segs + [(0, C)], (512, 256, 128))
    cmap, n_out = _colmap(segs, tn)
    nc = C // tn
    nq = n_out // tn - 3 * nc
    in_specs = _halo_specs(S, tm, D) + [
        pl.BlockSpec((None, D, tn), lambda b, i, j: (l, 0, cmap(j))),
        pl.BlockSpec((3, tn), lambda b, i, j: (0, jnp.clip(j - nq, 0, 3 * nc - 1))),
        pl.BlockSpec((1, tn), lambda b, i, j: (0, jnp.clip(j - nq, 0, 3 * nc - 1)))]
    args = [x, x, x, gain.reshape(1, D), shift, scale, w_in, conv_w, conv_b.reshape(1, 3 * C)]
    rope_chunks = ()
    if rope is not None:
        cos_t, sin_t, rope_chunks = rope
        in_specs += [pl.BlockSpec((tm, LANES), lambda b, i, j: (i, 0)),
                     pl.BlockSpec((tm, LANES), lambda b, i, j: (i, 0))]
        args += [cos_t, sin_t]
    hy_spec = lambda k: pl.BlockSpec((1, tm, tn), lambda b, i, j: (b, i, jnp.clip(j - nq - k * nc, 0, nc - 1)))
    return _call(
        functools.partial(_inproj_body, nq=nq, nc=nc, rope_chunks=frozenset(rope_chunks), eps=NORM_EPS),
        grid=(B, S // tm, n_out // tn), in_specs=in_specs,
        out_specs=[pl.BlockSpec((1, tm, tn), lambda b, i, j: (b, i, jnp.minimum(j, nq - 1))),
                   hy_spec(0), hy_spec(1), hy_spec(2)],
        out_shape=[jax.ShapeDtypeStruct((B, S, nq * tn), BF16)] + [jax.ShapeDtypeStruct((B, S, C), F32)] * 3,
        scratch=[pltpu.VMEM((tm + 2 * HALO, D), BF16)],
        sem=("parallel", "parallel", "arbitrary"), name=name)(*args)


def _ffnup_body(x_ref, xp_ref, xn_ref, g_ref, sh_ref, sc_ref, wa_ref, wu_ref, cw_ref, cb_ref, o_ref, h_ref, *, eps):
    tm = x_ref.shape[1]

    @pl.when(pl.program_id(2) == 0)
    def _():
        _halo_prologue(h_ref, x_ref, xp_ref, xn_ref, g_ref, sh_ref, sc_ref, eps)

    a = jnp.dot(h_ref[...], wa_ref[...], preferred_element_type=F32)
    u = jnp.dot(h_ref[HALO:HALO + tm, :], wu_ref[...], preferred_element_type=F32)
    c = _conv3_halo(a, cw_ref, cb_ref, tm)
    o_ref[0] = (c * _sigmoid(c) * u).astype(o_ref.dtype)


def _ffn_up_glu(x, gain, shift, scale, w_up, l, conv_w, conv_b):
    B, S, D = x.shape
    Fd = w_up.shape[2] // 2
    tm = _tile(S, (1024, 512, 256, 128))
    tn = _tile(Fd, (512, 256, 128))
    nj = Fd // tn
    return _call(
        functools.partial(_ffnup_body, eps=NORM_EPS), grid=(B, S // tm, nj),
        in_specs=_halo_specs(S, tm, D) + [
            pl.BlockSpec((None, D, tn), lambda b, i, j: (l, 0, j)),
            pl.BlockSpec((None, D, tn), lambda b, i, j: (l, 0, nj + j)),
            pl.BlockSpec((3, tn), lambda b, i, j: (0, j)),
            pl.BlockSpec((1, tn), lambda b, i, j: (0, j))],
        out_specs=pl.BlockSpec((1, tm, tn), lambda b, i, j: (b, i, j)),
        out_shape=jax.ShapeDtypeStruct((B, S, Fd), BF16),
        scratch=[pltpu.VMEM((tm + 2 * HALO, D), BF16)],
        sem=("parallel", "parallel", "arbitrary"), name="ffn_up_glu")(
            x, x, x, gain.reshape(1, D), shift, scale, w_up, w_up, conv_w, conv_b.reshape(1, Fd))


def _filt_trunk_body(z_ref, w1_ref, b1_ref, fr_ref, w2_ref, b2_ref, o_ref):
    a = jnp.dot(z_ref[...], w1_ref[...], precision=HP, preferred_element_type=F32) + b1_ref[...]
    a = jnp.sin(fr_ref[0:1, :] * a)
    a = jnp.dot(a, w2_ref[...], precision=HP, preferred_element_type=F32) + b2_ref[...]
    o_ref[...] = jnp.sin(fr_ref[1:2, :] * a)


def _filt_main_body(a_ref, wf_ref, wb_ref, t_ref, dl_ref, o_ref, *, n):
    a = a_ref[...]
    hf = jnp.dot(a, wf_ref[...], precision=HP, preferred_element_type=F32)
    hb = jnp.dot(a, wb_ref[...], precision=HP, preferred_element_type=F32)
    r = lax.broadcasted_iota(jnp.int32, hf.shape, 0)
    k = jnp.where(r < n, hf, jnp.where(r > n, hb, 0.0)) * jnp.exp(-t_ref[...] * dl_ref[...])
    o_ref[...] = k / jnp.sum(jnp.abs(k), axis=0, keepdims=True)


def _hyena_filters(n, w1, b1, freq, w2, b2, w3):
    Hd = w1.shape[1]
    OC = w3.shape[1] // 2
    t = jnp.linspace(0.0, 1.0, n, dtype=F32)[:, None]
    w = (2.0 * math.pi / n) * jnp.arange(n, dtype=F32)[:, None]
    f = jnp.linspace(1e-4, HY_BANDS - 1, HY_BANDS, dtype=F32)[None, :]
    z = jnp.concatenate([t, jnp.cos(f * w), -jnp.sin(f * w)], axis=-1)
    fold = lambda a: jnp.concatenate([a, jnp.zeros_like(a[:1]), a[:0:-1]], axis=0)
    EP = 64
    z2 = jnp.pad(fold(z), ((0, 0), (0, EP - HY_EMB)))
    w1p = jnp.pad(w1, ((0, EP - HY_EMB), (0, 0)))
    rt = _tile(2 * n, (1024, 512, 256))
    a2 = _call(
        _filt_trunk_body, grid=(2 * n // rt,),
        in_specs=[pl.BlockSpec((rt, EP), lambda i: (i, 0)),
                  pl.BlockSpec((EP, Hd), lambda i: (0, 0)),
                  pl.BlockSpec((1, Hd), lambda i: (0, 0)),
                  pl.BlockSpec((2, Hd), lambda i: (0, 0)),
                  pl.BlockSpec((Hd, Hd), lambda i: (0, 0)),
                  pl.BlockSpec((1, Hd), lambda i: (0, 0))],
        out_specs=pl.BlockSpec((rt, Hd), lambda i: (i, 0)),
        out_shape=jax.ShapeDtypeStruct((2 * n, Hd), F32),
        sem=("parallel",), name="hyena_filter_trunk")(z2, w1p, b1.reshape(1, Hd), freq, w2, b2.reshape(1, Hd))
    tc = LANES
    t2 = jnp.broadcast_to(fold(t), (2 * n, tc))
    deltas = jnp.abs(jnp.linspace(HY_MIN_DECAY, HY_MAX_DECAY, OC, dtype=F32))[None, :]
    nc = OC // tc
    return _call(
        functools.partial(_filt_main_body, n=n), grid=(nc,),
        in_specs=[pl.BlockSpec((2 * n, Hd), lambda c: (0, 0)),
                  pl.BlockSpec((Hd, tc), lambda c: (0, c)),
                  pl.BlockSpec((Hd, tc), lambda c: (0, nc + c)),
                  pl.BlockSpec((2 * n, tc), lambda c: (0, 0)),
                  pl.BlockSpec((1, tc), lambda c: (0, c))],
        out_specs=pl.BlockSpec((2 * n, tc), lambda c: (0, c)),
        out_shape=jax.ShapeDtypeStruct((2 * n, OC), F32),
        sem=("parallel",), name="hyena_filter")(a2, w3, w3, t2, deltas)


def _embed(re, im):
    return np.block([[re, -im], [im, re]])


@functools.lru_cache(maxsize=None)
def _fft_tables(N1, N2):
    N = N1 * N2
    S1 = N1 // 2
    i1 = np.arange(N1)
    ang = -2.0 * np.pi * ((i1[:, None] * i1[None, :]) % N1) / N1
    fr, fi = np.cos(ang), np.sin(ang)
    f1_pair = _embed(fr[:, :S1], fi[:, :S1])
    f1_real = np.concatenate([fr, fi], axis=0)
    i2 = np.arange(N2)
    fidx = i1[:, None, None] + N1 * i2[None, :, None]
    ang = -2.0 * np.pi * ((fidx * i2[None, None, :]) % N) / N
    mr, mi = np.cos(ang), np.sin(ang)
    m_fwd = np.stack([_embed(mr[a], mi[a]) for a in range(N1)])
    m_inv = np.stack([_embed(mr[a].T, -mi[a].T) for a in range(N1)])
    ang = 2.0 * np.pi * ((i1[:S1, None] * i1[None, :]) % N1) / N1
    f1_inv = _embed(np.cos(ang) / N, np.sin(ang) / N)
    cvt = lambda a: jnp.asarray(a, dtype=BF16)
    return cvt(f1_pair), cvt(f1_real), cvt(m_fwd), cvt(m_inv), cvt(f1_inv)


def _lmat_body(f_ref, x_ref, o_ref):
    o_ref[0] = jnp.dot(f_ref[...], x_ref[0].astype(BF16), preferred_element_type=F32).astype(o_ref.dtype)


def _left_matmul(fm, x, out_dtype, name):
    P, K, W = x.shape
    R = fm.shape[0]
    tw = _tile(W, (4096, 2048, 1024, 512, 256, 128))
    return _call(
        _lmat_body, grid=(P, W // tw),
        in_specs=[pl.BlockSpec((R, K), lambda p, j: (0, 0)),
                  pl.BlockSpec((1, K, tw), lambda p, j: (p, 0, j))],
        out_specs=pl.BlockSpec((1, R, tw), lambda p, j: (p, 0, j)),
        out_shape=jax.ShapeDtypeStruct((P, R, W), out_dtype),
        sem=("parallel", "parallel"), name=name)(fm, x)


def _cmul(xr, xi, kr, ki):
    return xr * kr - xi * ki, xr * ki + xi * kr


def _fftmid_body(a_ref, m_ref, mi_ref, k_ref, o_ref, *, FB, N2):
    for t in range(FB):
        a = a_ref[0, :, t].reshape(2 * N2, a_ref.shape[-1])
        x = jnp.dot(m_ref[t], a, preferred_element_type=F32)
        yr, yi = _cmul(x[:N2], x[N2:], k_ref[0, t], k_ref[1, t])
        y = jnp.concatenate([yr, yi], axis=0).astype(BF16)
        g = jnp.dot(mi_ref[t], y, preferred_element_type=F32)
        o_ref[0, :, t] = g.reshape(2, N2, g.shape[-1]).astype(o_ref.dtype)


def _fftfwd_body(a_ref, m_ref, o_ref, *, FB, N2):
    for t in range(FB):
        a = a_ref[:, t].reshape(2 * N2, a_ref.shape[-1])
        x = jnp.dot(m_ref[t], a, preferred_element_type=F32)
        o_ref[:, t] = x.reshape(2, N2, x.shape[-1])


def _fftfin_body(f_ref, g_ref, z_ref, x1_ref, sk_ref, o_ref):
    y = jnp.dot(f_ref[...], g_ref[0], preferred_element_type=F32)
    o_ref[0] = (x1_ref[0] * (y + sk_ref[...] * z_ref[0])).astype(o_ref.dtype)


def _filter_spectrum_2stage(k, N1, N2):
    N, OC = k.shape
    _, f1_real, m_fwd, _, _ = _fft_tables(N1, N2)
    a = _left_matmul(f1_real, k.reshape(1, N1, N2 * OC), BF16, "filter_dft1")
    a = a.reshape(2, N1, N2, OC)
    FB = _tile(N1, (4, 2, 1))
    tc = _tile(OC, (512, 256, 128))
    return _call(
        functools.partial(_fftfwd_body, FB=FB, N2=N2), grid=(N1 // FB, OC // tc),
        in_specs=[pl.BlockSpec((2, FB, N2, tc), lambda f, c: (0, f, 0, c)),
                  pl.BlockSpec((FB, 2 * N2, 2 * N2), lambda f, c: (f, 0, 0))],
        out_specs=pl.BlockSpec((2, FB, N2, tc), lambda f, c: (0, f, 0, c)),
        out_shape=jax.ShapeDtypeStruct((2, N1, N2, OC), F32),
        sem=("parallel", "parallel"), name="filter_dft2")(a, m_fwd)


def _long_conv_gate_2stage(z, x1, skip, kf, order, out_dtype, N1, N2):
    B, n, C = z.shape
    assert B % 2 == 0
    P, S1 = B // 2, N1 // 2
    f1_pair, _, m_fwd, m_inv, f1_inv = _fft_tables(N1, N2)
    W = N2 * C
    zr = z.reshape(P, 2 * S1, W)
    a = _left_matmul(f1_pair, zr, BF16, "conv_dft1").reshape(P, 2, N1, N2, C)
    FB = _tile(N1, (4, 2, 1))
    tc = _tile(C, (512, 256, 128))
    oc = order * (C // tc)
    g = _call(
        functools.partial(_fftmid_body, FB=FB, N2=N2), grid=(N1 // FB, C // tc, P),
        in_specs=[pl.BlockSpec((1, 2, FB, N2, tc), lambda f, c, p: (p, 0, f, 0, c)),
                  pl.BlockSpec((FB, 2 * N2, 2 * N2), lambda f, c, p: (f, 0, 0)),
                  pl.BlockSpec((FB, 2 * N2, 2 * N2), lambda f, c, p: (f, 0, 0)),
                  pl.BlockSpec((2, FB, N2, tc), lambda f, c, p: (0, f, 0, oc + c))],
        out_specs=pl.BlockSpec((1, 2, FB, N2, tc), lambda f, c, p: (p, 0, f, 0, c)),
        out_shape=jax.ShapeDtypeStruct((P, 2, N1, N2, C), BF16),
        sem=("parallel", "parallel", "arbitrary"), name="conv_dft2_mul_idft2")(a, m_fwd, m_inv, kf)
    tw = _tile(W, (4096, 2048, 1024, 512, 256, 128))
    out = _call(
        _fftfin_body, grid=(P, W // tw),
        in_specs=[pl.BlockSpec((2 * S1, 2 * N1), lambda p, j: (0, 0)),
                  pl.BlockSpec((1, 2 * N1, tw), lambda p, j: (p, 0, j)),
                  pl.BlockSpec((1, 2 * S1, tw), lambda p, j: (p, 0, j)),
                  pl.BlockSpec((1, 2 * S1, tw), lambda p, j: (p, 0, j)),
                  pl.BlockSpec((1, tw), lambda p, j: (0, j))],
        out_specs=pl.BlockSpec((1, 2 * S1, tw), lambda p, j: (p, 0, j)),
        out_shape=jax.ShapeDtypeStruct((P, 2 * S1, W), out_dtype),
        sem=("parallel", "parallel"), name="conv_idft1_gate")(
            f1_inv, g.reshape(P, 2 * N1, W), zr, x1.reshape(P, 2 * S1, W), jnp.tile(skip, N2)[None, :])
    return out.reshape(B, n, C)


@functools.lru_cache(maxsize=None)
def _dft_tables(n):
    N = 2 * n
    f = np.arange(N)
    ang = -2.0 * np.pi * ((f[:, None] * f[None, :]) % N) / N
    fr, fi = np.cos(ang), np.sin(ang)
    fwd_full = np.concatenate([fr, fi], axis=0)
    fwd_half = fwd_full[:, :n]
    inv = np.concatenate([fr[:n, :], fi[:n, :]], axis=1) / N
    cvt = lambda a: jnp.asarray(a, dtype=BF16)
    return cvt(fwd_full), cvt(fwd_half), cvt(inv)


def _dftconv_body(z_ref, x1_ref, sk_ref, f_ref, fi_ref, k_ref, o_ref, *, N):
    z = z_ref[0]
    x = jnp.dot(f_ref[...], z.astype(BF16), preferred_element_type=F32)
    yr, yi = _cmul(x[:N], x[N:], k_ref[0], k_ref[1])
    y = jnp.concatenate([yr, yi], axis=0).astype(BF16)
    y = jnp.dot(fi_ref[...], y, preferred_element_type=F32)
    o_ref[0] = (x1_ref[0] * (y + sk_ref[...] * z)).astype(o_ref.dtype)


def _long_conv_gate_dense(z, x1, skip, kf, order, out_dtype):
    B, n, C = z.shape
    N = 2 * n
    _, fwd_half, inv = _dft_tables(n)
    tc = _tile(C, (256, 128))
    oc = order * (C // tc)
    return _call(
        functools.partial(_dftconv_body, N=N), grid=(C // tc, B),
        in_specs=[pl.BlockSpec((1, n, tc), lambda c, b: (b, 0, c)),
                  pl.BlockSpec((1, n, tc), lambda c, b: (b, 0, c)),
                  pl.BlockSpec((1, tc), lambda c, b: (0, c)),
                  pl.BlockSpec((2 * N, n), lambda c, b: (0, 0)),
                  pl.BlockSpec((n, 2 * N), lambda c, b: (0, 0)),
                  pl.BlockSpec((2, N, tc), lambda c, b: (0, 0, oc + c))],
        out_specs=pl.BlockSpec((1, n, tc), lambda c, b: (b, 0, c)),
        out_shape=jax.ShapeDtypeStruct((B, n, C), out_dtype),
        sem=("parallel", "arbitrary"), name="conv_dense_dft")(z, x1, skip[None, :], fwd_half, inv, kf)


def _hyena_mixer(v, x1, x2, w1, b1, freq, w2, b2, w3, skip):
    B, n, _ = v.shape
    k = _hyena_filters(n, w1, b1, freq, w2, b2, w3)
    N = 2 * n
    if N % FFT_N2 == 0 and (N // FFT_N2) >= 16:
        N1 = N // FFT_N2
        kf = _filter_spectrum_2stage(k, N1, FFT_N2)
        z = _long_conv_gate_2stage(v, x1, skip[0], kf, 0, F32, N1, FFT_N2)
        return _long_conv_gate_2stage(z, x2, skip[1], kf, 1, BF16, N1, FFT_N2)
    fwd_full, _, _ = _dft_tables(n)
    kf = _left_matmul(fwd_full, k[None], F32, "filter_dense_dft").reshape(2, N, k.shape[1])
    z = _long_conv_gate_dense(v, x1, skip[0], kf, 0, F32)
    return _long_conv_gate_dense(z, x2, skip[1], kf, 1, BF16)


def _nt(a, b):
    return lax.dot_general(a, b, (((1,), (1,)), ((), ())), preferred_element_type=F32)


def _na_body(q_ref, k_ref, v_ref, kc_ref, vc_ref, b_ref, o_ref, *, R, KR, rows, HPS, scale):
    j = pl.program_id(2)
    start = pl.multiple_of(jnp.clip(j * R - NA_WIN_R // 2, 0, rows - KR) * GRID_W, GRID_W)
    for h in range(HPS):
        hs = slice(h * HEAD_DIM, (h + 1) * HEAD_DIM)
        q = q_ref[0, :, hs]
        kw = k_ref[0, pl.ds(start, KR * GRID_W), hs]
        vw = v_ref[0, pl.ds(start, KR * GRID_W), hs]
        s = _nt(q, kw) * scale + b_ref[h, 0]
        sc = _nt(q, kc_ref[0, :, hs]) * scale
        m = jnp.maximum(jnp.max(s, axis=-1, keepdims=True), jnp.max(sc, axis=-1, keepdims=True))
        p = jnp.exp(s - m)
        pc = jnp.exp(sc - m)
        l = jnp.sum(p, axis=-1, keepdims=True) + jnp.sum(pc, axis=-1, keepdims=True)
        o = jnp.dot(p.astype(BF16), vw, preferred_element_type=F32)
        o = o + jnp.dot(pc.astype(BF16), vc_ref[0, :, hs], preferred_element_type=F32)
        o_ref[0, :, hs] = (o / l).astype(o_ref.dtype)


def _na_geometry(S):
    rows = S // GRID_W
    kr = min(NA_WIN_R, rows)
    R = min(8, rows)
    KR = min(rows, R + kr)
    nb = rows // R
    assert rows % R == 0
    types = sorted({0, min(1, nb - 1), nb - 1})
    if nb > 3:
        offs = {int(np.clip(j * R - NA_WIN_R // 2, 0, rows - KR)) - j * R for j in range(1, nb - 1)}
        assert len(offs) == 1
    return rows, kr, R, KR, nb, types


def _nabias_body(rpb_ref, o_ref, tw_ref, *, plan, R, KR):
    W = GRID_W
    nd_r, nd_c = 2 * NA_WIN_R - 1, 2 * NA_WIN_C - 1
    base = pl.program_id(0) * (nd_r * nd_c)
    qc = lax.broadcasted_iota(jnp.int32, (W, 2 * W), 0)
    lane = lax.broadcasted_iota(jnp.int32, (W, 2 * W), 1)
    kc = lane % W
    cs = jnp.clip(qc - NA_WIN_C // 2, 0, W - NA_WIN_C)
    col_ok = jnp.logical_and(kc >= cs, kc < cs + NA_WIN_C)
    dcm = kc - qc + (NA_WIN_C - 1)
    neg = jnp.full((W, 2 * W), NEG_INF, F32)
    for dr in range(nd_r):
        acc = neg
        for dc in range(nd_c):
            acc = jnp.where(dcm == dc, rpb_ref[base + dr * nd_c + dc], acc)
        tw_ref[dr] = jnp.where(col_ok, acc, NEG_INF)
    left = lane < W
    for t, per_q in enumerate(plan):
        for qr in range(R):
            for kp in range(KR // 2):
                d0, d1 = per_q[qr][kp]
                a = neg if d0 is None else tw_ref[d0]
                b = neg if d1 is None else tw_ref[d1]
                blk = neg if (d0 is None and d1 is None) else jnp.where(left, a, b)
                o_ref[0, t, qr * W:(qr + 1) * W, kp * 2 * W:(kp + 1) * 2 * W] = blk


def _na_bias_tables(rpb, S):
    rows, kr, R, KR, nb, types = _na_geometry(S)
    assert 2 * GRID_W == LANES and KR % 2 == 0
    plan = []
    for jt in types:
        start = int(np.clip(jt * R - NA_WIN_R // 2, 0, rows - KR))
        per_q = []
        for q in range(R):
            qra = jt * R + q
            ws = int(np.clip(qra - kr // 2, 0, rows - kr))
            d = [(start + k) - qra + (NA_WIN_R - 1) if ws <= start + k < ws + kr else None for k in range(KR)]
            per_q.append([(d[2 * p], d[2 * p + 1]) for p in range(KR // 2)])
        plan.append(per_q)
    L, H, nd_r, nd_c = rpb.shape
    T, QB, KB = len(types), R * GRID_W, KR * GRID_W
    return _call(
        functools.partial(_nabias_body, plan=plan, R=R, KR=KR), grid=(L * H,),
        in_specs=[pl.BlockSpec(memory_space=pltpu.SMEM)],
        out_specs=pl.BlockSpec((1, T, QB, KB), lambda i: (i, 0, 0, 0)),
        out_shape=jax.ShapeDtypeStruct((L * H, T, QB, KB), F32),
        scratch=[pltpu.VMEM((nd_r, GRID_W, 2 * GRID_W), F32)],
        sem=("parallel",), name="na_bias_table")(rpb.reshape(-1).astype(F32))


def _na_attention(qkv, qkv_c, bias, l, offs, H):
    B, S, _ = qkv.shape
    CTX = qkv_c.shape[1]
    rows, kr, R, KR, nb, types = _na_geometry(S)
    T = len(types)
    QB, KB = R * GRID_W, KR * GRID_W
    HPS = 2 if H % 2 == 0 else 1
    HW = HPS * HEAD_DIM
    assert all(offs[n] % HW == 0 for n in ("na_k", "na_v", "na_q"))
    ok_, ov_, oq_ = (offs[n] // HW for n in ("na_k", "na_v", "na_q"))

    def btype(j):
        if T == nb:
            return j
        return jnp.where(j == 0, 0, jnp.where(j == nb - 1, T - 1, 1))

    return _call(
        functools.partial(_na_body, R=R, KR=KR, rows=rows, HPS=HPS, scale=HEAD_DIM ** -0.5),
        grid=(B, H // HPS, nb),
        in_specs=[pl.BlockSpec((1, QB, HW), lambda b, h, j: (b, j, oq_ + h)),
                  pl.BlockSpec((1, S, HW), lambda b, h, j: (b, 0, ok_ + h)),
                  pl.BlockSpec((1, S, HW), lambda b, h, j: (b, 0, ov_ + h)),
                  pl.BlockSpec((1, CTX, HW), lambda b, h, j: (b, 0, ok_ + h)),
                  pl.BlockSpec((1, CTX, HW), lambda b, h, j: (b, 0, ov_ + h)),
                  pl.BlockSpec((HPS, 1, QB, KB), lambda b, h, j: (l * (H // HPS) + h, btype(j), 0, 0))],
        out_specs=pl.BlockSpec((1, QB, HW), lambda b, h, j: (b, j, h)),
        out_shape=jax.ShapeDtypeStruct((B, S, H * HEAD_DIM), BF16),
        sem=("parallel", "parallel", "arbitrary"), name="na_attention")(qkv, qkv, qkv, qkv_c, qkv_c, bias)


def _stack_heads(q2, G):
    return jnp.concatenate([q2[:, g * HEAD_DIM:(g + 1) * HEAD_DIM] for g in range(G)], axis=0)


def _unstack_heads(o, G, n):
    return jnp.concatenate([o[g * n:(g + 1) * n] for g in range(G)], axis=1)


def _sink_column(sink_ref, h0, G, n):
    return jnp.concatenate([jnp.full((n, 1), sink_ref[h0 + g], F32) for g in range(G)], axis=0)


def _swa_body(sink_ref, q_ref, k_ref, v_ref, kc_ref, vc_ref, o_ref, *, QB, KB, S, G, HPS, scale):
    hb, j = pl.program_id(1), pl.program_id(2)
    start = pl.multiple_of(jnp.clip(j * QB - GQA_WINDOW, 0, S - KB), LANES)
    qpos = j * QB + lax.broadcasted_iota(jnp.int32, (QB, KB), 0)
    kpos = start + lax.broadcasted_iota(jnp.int32, (QB, KB), 1)
    mask = jnp.where(jnp.abs(qpos - kpos) <= GQA_WINDOW, 0.0, NEG_INF)
    mask = jnp.concatenate([mask] * G, axis=0)
    for h in range(HPS):
        hs = slice(h * HEAD_DIM, (h + 1) * HEAD_DIM)
        qs = slice(h * G * HEAD_DIM, (h + 1) * G * HEAD_DIM)
        q = _stack_heads(q_ref[0, :, qs], G)
        kw = k_ref[0, pl.ds(start, KB), hs]
        vw = v_ref[0, pl.ds(start, KB), hs]
        s = _nt(q, kw) * scale + mask
        sc = _nt(q, kc_ref[0, :, hs]) * scale
        sk = _sink_column(sink_ref, (hb * HPS + h) * G, G, QB)
        m = jnp.maximum(jnp.maximum(jnp.max(s, axis=-1, keepdims=True), jnp.max(sc, axis=-1, keepdims=True)), sk)
        p = jnp.exp(s - m)
        pc = jnp.exp(sc - m)
        l = jnp.sum(p, axis=-1, keepdims=True) + jnp.sum(pc, axis=-1, keepdims=True) + jnp.exp(sk - m)
        o = jnp.dot(p.astype(BF16), vw, preferred_element_type=F32)
        o = o + jnp.dot(pc.astype(BF16), vc_ref[0, :, hs], preferred_element_type=F32)
        o_ref[0, :, qs] = _unstack_heads(o / l, G, QB).astype(o_ref.dtype)


def _swa_attention(qkv, qkv_c, sink, offs, KVH):
    B, S, _ = qkv.shape
    CTX = qkv_c.shape[1]
    G = GQA_GROUP
    QB = _tile(S, (512, 256, 128))
    KB = min(S, QB + 2 * GQA_WINDOW)
    HPS = 2 if KVH % 2 == 0 else 1
    HW = HPS * HEAD_DIM
    assert offs["sw_k"] % HW == 0 and offs["sw_v"] % HW == 0 and offs["sw_q"] % (G * HW) == 0
    ok_, ov_, oq_ = offs["sw_k"] // HW, offs["sw_v"] // HW, offs["sw_q"] // (G * HW)
    return _call(
        functools.partial(_swa_body, QB=QB, KB=KB, S=S, G=G, HPS=HPS, scale=HEAD_DIM ** -0.5),
        grid=(B, KVH // HPS, S // QB),
        in_specs=[pl.BlockSpec(memory_space=pltpu.SMEM),
                  pl.BlockSpec((1, QB, G * HW), lambda b, h, j: (b, j, oq_ + h)),
                  pl.BlockSpec((1, S, HW), lambda b, h, j: (b, 0, ok_ + h)),
                  pl.BlockSpec((1, S, HW), lambda b, h, j: (b, 0, ov_ + h)),
                  pl.BlockSpec((1, CTX, HW), lambda b, h, j: (b, 0, ok_ + h)),
                  pl.BlockSpec((1, CTX, HW), lambda b, h, j: (b, 0, ov_ + h))],
        out_specs=pl.BlockSpec((1, QB, G * HW), lambda b, h, j: (b, j, h)),
        out_shape=jax.ShapeDtypeStruct((B, S, KVH * G * HEAD_DIM), BF16),
        sem=("parallel", "parallel", "arbitrary"), name="swa_attention")(sink, qkv, qkv, qkv, qkv_c, qkv_c)


def _cattn_body(sink_ref, q_ref, k_ref, v_ref, o_ref, *, G, use_sink, scale):
    h = pl.program_id(1)
    n = q_ref.shape[1]
    q = _stack_heads(q_ref[0], G)
    s = _nt(q, k_ref[0]) * scale
    m = jnp.max(s, axis=-1, keepdims=True)
    if use_sink:
        sk = _sink_column(sink_ref, h * G, G, n)
        m = jnp.maximum(m, sk)
    p = jnp.exp(s - m)
    l = jnp.sum(p, axis=-1, keepdims=True)
    if use_sink:
        l = l + jnp.exp(sk - m)
    o = jnp.dot(p.astype(BF16), v_ref[0], preferred_element_type=F32)
    o_ref[0] = _unstack_heads(o / l, G, n).astype(o_ref.dtype)


def _ctx_attention(qkv_c, sink, oq, ok, ov, KVH, G, use_sink):
    B, n, _ = qkv_c.shape
    oq_, ok_, ov_ = oq // (G * HEAD_DIM), ok // HEAD_DIM, ov // HEAD_DIM
    return _call(
        functools.partial(_cattn_body, G=G, use_sink=use_sink, scale=HEAD_DIM ** -0.5),
        grid=(B, KVH),
        in_specs=[pl.BlockSpec(memory_space=pltpu.SMEM),
                  pl.BlockSpec((1, n, G * HEAD_DIM), lambda b, h: (b, 0, oq_ + h)),
                  pl.BlockSpec((1, n, HEAD_DIM), lambda b, h: (b, 0, ok_ + h)),
                  pl.BlockSpec((1, n, HEAD_DIM), lambda b, h: (b, 0, ov_ + h))],
        out_specs=pl.BlockSpec((1, n, G * HEAD_DIM), lambda b, h: (b, 0, h)),
        out_shape=jax.ShapeDtypeStruct((B, n, KVH * G * HEAD_DIM), BF16),
        sem=("parallel", "parallel"), name="ctx_attention")(sink, qkv_c, qkv_c, qkv_c)


def _merge_body(x_ref, g_ref, sh_ref, sc_ref, yh, yn, ys, wgh, wgn, wgs, wh, wn, ws, o_ref, h_ref, *, eps):
    @pl.when(pl.program_id(2) == 0)
    def _():
        h_ref[...] = _norm_modulate(x_ref[0], g_ref, sh_ref, sc_ref, eps).astype(BF16)

    h = h_ref[...]
    gate = lambda wg: _sigmoid(jnp.dot(h, wg[...], preferred_element_type=F32))
    m = gate(wgh) * jnp.dot(yh[0], wh[...], preferred_element_type=F32)
    m = m + gate(wgn) * jnp.dot(yn[0], wn[...], preferred_element_type=F32)
    m = m + gate(wgs) * jnp.dot(ys[0], ws[...], preferred_element_type=F32)
    o_ref[0] = m.astype(o_ref.dtype)


def _merge_branches(x, gain, shift, scale, w_in, gate_start, y_hy, y_na, y_sw, w_br, l):
    B, S, D = x.shape
    widths = (y_hy.shape[2], y_na.shape[2], y_sw.shape[2])
    starts = (0, widths[0], widths[0] + widths[1])
    assert all(s % w == 0 for s, w in zip(starts, widths))
    tm = _tile(S, (512, 256, 128))
    tn = _seg_tile([(gate_start, D)], (512, 256, 128))
    nj, gj = D // tn, gate_start // tn
    yspec = lambda y: pl.BlockSpec((1, tm, y.shape[2]), lambda b, i, j: (b, i, 0))
    gspec = lambda k: pl.BlockSpec((None, D, tn), lambda b, i, j: (l, 0, gj + k * nj + j))
    wspec = lambda k: pl.BlockSpec((None, widths[k], tn), lambda b, i, j: (l, starts[k] // widths[k], j))
    return _call(
        functools.partial(_merge_body, eps=NORM_EPS), grid=(B, S // tm, nj),
        in_specs=[pl.BlockSpec((1, tm, D), lambda b, i, j: (b, i, 0)),
                  pl.BlockSpec((1, D), lambda b, i, j: (0, 0)),
                  pl.BlockSpec((1, 1, D), lambda b, i, j: (b, 0, 0)),
                  pl.BlockSpec((1, 1, D), lambda b, i, j: (b, 0, 0)),
                  yspec(y_hy), yspec(y_na), yspec(y_sw), gspec(0), gspec(1), gspec(2),
                  wspec(0), wspec(1), wspec(2)],
        out_specs=pl.BlockSpec((1, tm, tn), lambda b, i, j: (b, i, j)),
        out_shape=jax.ShapeDtypeStruct((B, S, D), BF16),
        scratch=[pltpu.VMEM((tm, D), BF16)],
        sem=("parallel", "parallel", "arbitrary"), name="gated_merge")(
            x, gain.reshape(1, D), shift, scale, y_hy, y_na, y_sw, w_in, w_in, w_in, w_br, w_br, w_br)


def _mmres_body(a_ref, w_ref, x_ref, g_ref, mg_ref, o_ref, *, nk, eps):
    k = pl.program_id(2)
    part = lambda: jnp.dot(a_ref[0], w_ref[...], preferred_element_type=F32)

    def finish():
        y = o_ref[0]
        yn = y * lax.rsqrt(jnp.mean(y * y, axis=-1, keepdims=True) + eps) * g_ref[...]
        o_ref[0] = x_ref[0] + mg_ref[0] * yn

    if nk == 1:
        o_ref[0] = part()
        finish()
        return

    @pl.when(k == 0)
    def _():
        o_ref[0] = part()

    @pl.when(k > 0)
    def _():
        o_ref[0] += part()

    @pl.when(k == nk - 1)
    def _():
        finish()


def _matmul_norm_residual(a, w, l, x, gain, mgate, name):
    B, S, K = a.shape
    D = w.shape[2]
    if K <= 2048:
        tm, tk = _tile(S, (512, 256, 128)), K
    else:
        tm = _tile(S, (1024, 512, 256, 128))
        tk = _tile(K, (512, 256, 128))
    nk = K // tk
    return _call(
        functools.partial(_mmres_body, nk=nk, eps=NORM_EPS), grid=(B, S // tm, nk),
        in_specs=[pl.BlockSpec((1, tm, tk), lambda b, i, k: (b, i, k)),
                  pl.BlockSpec((None, tk, D), lambda b, i, k: (l, k, 0)),
                  pl.BlockSpec((1, tm, D), lambda b, i, k: (b, i, 0)),
                  pl.BlockSpec((1, D), lambda b, i, k: (0, 0)),
                  pl.BlockSpec((1, 1, D), lambda b, i, k: (b, 0, 0))],
        out_specs=pl.BlockSpec((1, tm, D), lambda b, i, k: (b, i, 0)),
        out_shape=jax.ShapeDtypeStruct((B, S, D), F32),
        sem=("parallel", "parallel", "arbitrary"), name=name)(a, w, x, gain.reshape(1, D), mgate)


def _rope_tables(n):
    t = jnp.arange(n)
    row = (t // GRID_W).astype(F32)
    col = (t % GRID_W).astype(F32)
    per_axis = HEAD_DIM // 2
    inv = ROPE_BASE ** (-jnp.arange(0, per_axis, 2, dtype=F32) / per_axis)
    ar, ac = row[:, None] * inv, col[:, None] * inv
    cos_t = jnp.concatenate([jnp.cos(ar), jnp.cos(ar), jnp.cos(ac), jnp.cos(ac)], axis=1)
    sin_t = jnp.concatenate([-jnp.sin(ar), jnp.sin(ar), -jnp.sin(ac), jnp.sin(ac)], axis=1)
    return cos_t, sin_t


def kernel(x, c, ctx, c_ctx, w_mod, b_mod, norm_gains, w_in, hy_conv_w, hy_conv_b, hy_w1, hy_b1, hy_freq, hy_w2, hy_b2, hy_w3, hy_skip, na_rpb, swa_sink, w_branch, w_out, ffn_w_up, ffn_conv_w, ffn_conv_b, ffn_w_down):
    B, S, D = x.shape
    L = w_mod.shape[0]
    C = hy_skip.shape[-1]
    H_na = na_rpb.shape[1]
    H_q = swa_sink.shape[1]
    KVH = H_q // GQA_GROUP
    NA_W, QW, KVW = H_na * HEAD_DIM, H_q * HEAD_DIM, KVH * HEAD_DIM
    KV_COLS = 2 * NA_W + 2 * KVW
    qkv_segs = [(0, KV_COLS), (KV_COLS + 3 * C, NA_W + QW)]
    gate_start = KV_COLS + 3 * C + NA_W + QW
    offs = {"na_k": 0, "na_v": NA_W, "sw_k": 2 * NA_W, "sw_v": 2 * NA_W + KVW,
            "na_q": KV_COLS, "sw_q": KV_COLS + NA_W}
    rope_chunks = (list(range(offs["sw_k"] // LANES, (offs["sw_k"] + KVW) // LANES))
                   + list(range(offs["sw_q"] // LANES, (offs["sw_q"] + QW) // LANES)))
    cos_t, sin_t = _rope_tables(S)

    w_in_b, w_out_b = w_in.astype(BF16), w_out.astype(BF16)
    w_br_b, w_up_b, w_dn_b = w_branch.astype(BF16), ffn_w_up.astype(BF16), ffn_w_down.astype(BF16)
    na_bias = _na_bias_tables(na_rpb, S)

    R = -(-(B + 1) // 8) * 8
    cc = jnp.concatenate([c, c_ctx[None, :], jnp.zeros((R - B - 1, D), F32)], axis=0)
    mods = _modulation(cc, w_mod, b_mod)

    xc = ctx
    for l in range(L):
        mod = [mods[l, :B, k * D:(k + 1) * D].reshape(B, 1, D) for k in range(6)]
        mod_c = [jnp.broadcast_to(mods[l, B, k * D:(k + 1) * D].reshape(1, 1, D), (B, 1, D)) for k in range(6)]
        g = norm_gains[l]
        hy_p = (hy_w1[l], hy_b1[l], hy_freq[l], hy_w2[l], hy_b2[l], hy_w3[l], hy_skip[l])

        qkv_c, vc, x1c, x2c = _in_proj(xc, g[0], mod_c[0], mod_c[1], w_in_b, l, qkv_segs, KV_COLS, C,
                                       hy_conv_w[l], hy_conv_b[l], None, "ctx_in_proj")
        qkv, v, x1, x2 = _in_proj(x, g[0], mod[0], mod[1], w_in_b, l, qkv_segs, KV_COLS, C,
                                  hy_conv_w[l], hy_conv_b[l], (cos_t, sin_t, rope_chunks), "in_proj")
        y_hy = _hyena_mixer(v, x1, x2, *hy_p)
        y_na = _na_attention(qkv, qkv_c, na_bias, l, offs, H_na)
        y_sw = _swa_attention(qkv, qkv_c, swa_sink[l], offs, KVH)
        m = _merge_branches(x, g[0], mod[0], mod[1], w_in_b, gate_start, y_hy, y_na, y_sw, w_br_b, l)
        x = _matmul_norm_residual(m, w_out_b, l, x, g[1], mod[2], "out_proj_residual")
        gl = _ffn_up_glu(x, g[2], mod[3], mod[4], w_up_b, l, ffn_conv_w[l], ffn_conv_b[l])
        x = _matmul_norm_residual(gl, w_dn_b, l, x, g[3], mod[5], "ffn_down_residual")

        if l < L - 1:
            yc_hy = _hyena_mixer(vc, x1c, x2c, *hy_p)
            yc_na = _ctx_attention(qkv_c, swa_sink[l], offs["na_q"], offs["na_k"], offs["na_v"], H_na, 1, False)
            yc_sw = _ctx_attention(qkv_c, swa_sink[l], offs["sw_q"], offs["sw_k"], offs["sw_v"], KVH, GQA_GROUP, True)
            mc = _merge_branches(xc, g[0], mod_c[0], mod_c[1], w_in_b, gate_start, yc_hy, yc_na, yc_sw, w_br_b, l)
            xc = _matmul_norm_residual(mc, w_out_b, l, xc, g[1], mod_c[2], "ctx_out_proj_residual")
            gl_c = _ffn_up_glu(xc, g[2], mod_c[3], mod_c[4], w_up_b, l, ffn_conv_w[l], ffn_conv_b[l])
            xc = _matmul_norm_residual(gl_c, w_dn_b, l, xc, g[3], mod_c[5], "ctx_ffn_down_residual")
    return x
```

```python
import functools
import math

import numpy as np
import jax
import jax.numpy as jnp
from jax import lax
from jax.experimental import pallas as pl
from jax.experimental.pallas import tpu as pltpu

F32 = jnp.float32
BF16 = jnp.bfloat16
NEG_INF = -1e30

GRID_W = 64
HEAD_DIM = 128
HY_EMB = 33
HY_BANDS = (HY_EMB - 1) // 2
HY_FAST_DECAY = 0.3
HY_SLOW_DECAY = 1.5
HY_DECAY_TARGET = 1e-2
HY_MAX_DECAY = math.log(HY_DECAY_TARGET) / HY_FAST_DECAY
HY_MIN_DECAY = math.log(HY_DECAY_TARGET) / HY_SLOW_DECAY
NA_WIN_R = 8
NA_WIN_C = 16
GQA_GROUP = 2
GQA_WINDOW = 128
ROPE_BASE = 10000.0
NORM_EPS = 1e-6

LANES = 128
FFT_N2 = 128
VMEM_LIMIT = 56 * 1024 * 1024
HP = lax.Precision.HIGHEST


def _call(body, *, grid, in_specs, out_specs, out_shape, scratch=(), sem, name):
    return pl.pallas_call(
        body, grid=grid, in_specs=in_specs, out_specs=out_specs, out_shape=out_shape,
        scratch_shapes=list(scratch),
        compiler_params=pltpu.CompilerParams(dimension_semantics=sem, vmem_limit_bytes=VMEM_LIMIT),
        name=name)


def _tile(n, cands):
    for c in cands:
        if n % c == 0:
            return c
    raise ValueError(f"no tile for {n} in {cands}")


def _sigmoid(x):
    return 1.0 / (1.0 + jnp.exp(-x))


def _mod_body(c_ref, w_ref, b_ref, o_ref):
    c = c_ref[...]
    s = (c * _sigmoid(c)).astype(BF16)
    o_ref[0] = jnp.dot(s, w_ref[0].astype(BF16), preferred_element_type=F32) + b_ref[0]


def _modulation(cc, w_mod, b_mod):
    L, D, N = w_mod.shape
    R = cc.shape[0]
    tn = _tile(N, (1024, 512, 256, 128))
    return _call(
        _mod_body, grid=(L, N // tn),
        in_specs=[pl.BlockSpec((R, D), lambda l, j: (0, 0)),
                  pl.BlockSpec((1, D, tn), lambda l, j: (l, 0, j)),
                  pl.BlockSpec((1, 1, tn), lambda l, j: (l, 0, j))],
        out_specs=pl.BlockSpec((1, R, tn), lambda l, j: (l, 0, j)),
        out_shape=jax.ShapeDtypeStruct((L, R, N), F32),
        sem=("parallel", "parallel"), name="modulation")(cc, w_mod, b_mod.reshape(L, 1, N))


def _swap32(a):
    lane = lax.broadcasted_iota(jnp.int32, a.shape, 1)
    return jnp.where((lane & 32) == 0, pltpu.roll(a, 96, 1), pltpu.roll(a, 32, 1))


def _norm_modulate(x, g_ref, sh_ref, sc_ref, eps):
    y = x * lax.rsqrt(jnp.mean(x * x, axis=-1, keepdims=True) + eps) * g_ref[...]
    return y * (1.0 + sc_ref[0]) + sh_ref[0]


HALO = 16


def _halo_prologue(h_ref, x_ref, xp_ref, xn_ref, g_ref, sh_ref, sc_ref, eps):
    i, tm = pl.program_id(1), x_ref.shape[1]
    nm = lambda x: _norm_modulate(x, g_ref, sh_ref, sc_ref, eps)
    keep_prev = (i > 0).astype(F32)
    keep_next = (i < pl.num_programs(1) - 1).astype(F32)
    h_ref[0:HALO, :] = (nm(xp_ref[0]) * keep_prev).astype(BF16)
    h_ref[HALO:HALO + tm, :] = nm(x_ref[0]).astype(BF16)
    h_ref[HALO + tm:, :] = (nm(xn_ref[0]) * keep_next).astype(BF16)


def _conv3_halo(a, cw_ref, cb_ref, tm):
    rows = a.shape[0]
    up = pltpu.roll(a, 1, 0)[HALO:HALO + tm]
    un = pltpu.roll(a, rows - 1, 0)[HALO:HALO + tm]
    return up * cw_ref[0:1, :] + a[HALO:HALO + tm] * cw_ref[1:2, :] + un * cw_ref[2:3, :] + cb_ref[...]


def _halo_specs(S, tm, D):
    hb, nh = tm // HALO, S // HALO
    return [pl.BlockSpec((1, tm, D), lambda b, i, j: (b, i, 0)),
            pl.BlockSpec((1, HALO, D), lambda b, i, j: (b, jnp.maximum(i * hb - 1, 0), 0)),
            pl.BlockSpec((1, HALO, D), lambda b, i, j: (b, jnp.minimum((i + 1) * hb, nh - 1), 0)),
            pl.BlockSpec((1, D), lambda b, i, j: (0, 0)),
            pl.BlockSpec((1, 1, D), lambda b, i, j: (b, 0, 0)),
            pl.BlockSpec((1, 1, D), lambda b, i, j: (b, 0, 0))]


def _inproj_body(*refs, nq, nc, rope_chunks, eps):
    if rope_chunks:
        x_ref, xp_ref, xn_ref, g_ref, sh_ref, sc_ref, w_ref, cw_ref, cb_ref, cos_ref, sin_ref, q_o, o0, o1, o2, h_ref = refs
    else:
        x_ref, xp_ref, xn_ref, g_ref, sh_ref, sc_ref, w_ref, cw_ref, cb_ref, q_o, o0, o1, o2, h_ref = refs
    j = pl.program_id(2)
    tm = x_ref.shape[1]

    @pl.when(j == 0)
    def _():
        _halo_prologue(h_ref, x_ref, xp_ref, xn_ref, g_ref, sh_ref, sc_ref, eps)

    qkv_dot = lambda: jnp.dot(h_ref[HALO:HALO + tm, :], w_ref[...], preferred_element_type=F32)
    nch = w_ref.shape[1] // LANES
    rope_tiles = sorted({ch // nch for ch in rope_chunks})
    for jt in rope_tiles:
        @pl.when(j == jt)
        def _(jt=jt):
            acc = qkv_dot()
            c, s = cos_ref[...], sin_ref[...]
            for k in range(nch):
                a = acc[:, k * LANES:(k + 1) * LANES]
                if jt * nch + k in rope_chunks:
                    a = a * c + _swap32(a) * s
                q_o[0, :, k * LANES:(k + 1) * LANES] = a.astype(q_o.dtype)

    @pl.when(functools.reduce(jnp.logical_and, [j != jt for jt in rope_tiles], j < nq))
    def _():
        q_o[0] = qkv_dot().astype(q_o.dtype)

    for k, o in enumerate((o0, o1, o2)):
        @pl.when(jnp.logical_and(j >= nq + k * nc, j < nq + (k + 1) * nc))
        def _(o=o):
            a = jnp.dot(h_ref[...], w_ref[...], preferred_element_type=F32)
            o[0] = _conv3_halo(a, cw_ref, cb_ref, tm)


def _colmap(segs, tn):
    bounds, o = [], 0
    for s, w in segs:
        assert s % tn == 0 and w % tn == 0
        bounds.append(((o + w) // tn, (s - o) // tn))
        o += w

    def f(j):
        r = j + bounds[-1][1]
        for hi, off in reversed(bounds[:-1]):
            r = jnp.where(j < hi, j + off, r)
        return r
    return f, o


def _seg_tile(segs, cands):
    return _tile(functools.reduce(math.gcd, [v for seg in segs for v in seg if v]), cands)


def _in_proj(x, gain, shift, scale, w_in, l, qkv_segs, hy_start, C, conv_w, conv_b, rope, name):
    B, S, D = x.shape
    tm = _tile(S, (1024, 512, 256, 128))
    segs = list(qkv_segs) + [(hy_start, 3 * C)]
    tn = _seg_tile(segs + [(0, C)], (512, 256, 128))
    cmap, n_out = _colmap(segs, tn)
    nc = C // tn
    nq = n_out // tn - 3 * nc
    in_specs = _halo_specs(S, tm, D) + [
        pl.BlockSpec((None, D, tn), lambda b, i, j: (l, 0, cmap(j))),
        pl.BlockSpec((3, tn), lambda b, i, j: (0, jnp.clip(j - nq, 0, 3 * nc - 1))),
        pl.BlockSpec((1, tn), lambda b, i, j: (0, jnp.clip(j - nq, 0, 3 * nc - 1)))]
    args = [x, x, x, gain.reshape(1, D), shift, scale, w_in, conv_w, conv_b.reshape(1, 3 * C)]
    rope_chunks = ()
    if rope is not None:
        cos_t, sin_t, rope_chunks = rope
        in_specs += [pl.BlockSpec((tm, LANES), lambda b, i, j: (i, 0)),
                     pl.BlockSpec((tm, LANES), lambda b, i, j: (i, 0))]
        args += [cos_t, sin_t]
    hy_spec = lambda k: pl.BlockSpec((1, tm, tn), lambda b, i, j: (b, i, jnp.clip(j - nq - k * nc, 0, nc - 1)))
    return _call(
        functools.partial(_inproj_body, nq=nq, nc=nc, rope_chunks=frozenset(rope_chunks), eps=NORM_EPS),
        grid=(B, S // tm, n_out // tn), in_specs=in_specs,
        out_specs=[pl.BlockSpec((1, tm, tn), lambda b, i, j: (b, i, jnp.minimum(j, nq - 1))),
                   hy_spec(0), hy_spec(1), hy_spec(2)],
        out_shape=[jax.ShapeDtypeStruct((B, S, nq * tn), BF16)] + [jax.ShapeDtypeStruct((B, S, C), F32)] * 3,
        scratch=[pltpu.VMEM((tm + 2 * HALO, D), BF16)],
        sem=("parallel", "parallel", "arbitrary"), name=name)(*args)


def _ffnup_body(x_ref, xp_ref, xn_ref, g_ref, sh_ref, sc_ref, wa_ref, wu_ref, cw_ref, cb_ref, o_ref, h_ref, *, eps):
    tm = x_ref.shape[1]

    @pl.when(pl.program_id(2) == 0)
    def _():
        _halo_prologue(h_ref, x_ref, xp_ref, xn_ref, g_ref, sh_ref, sc_ref, eps)

    a = jnp.dot(h_ref[...], wa_ref[...], preferred_element_type=F32)
    u = jnp.dot(h_ref[HALO:HALO + tm, :], wu_ref[...], preferred_element_type=F32)
    c = _conv3_halo(a, cw_ref, cb_ref, tm)
    o_ref[0] = (c * _sigmoid(c) * u).astype(o_ref.dtype)


def _ffn_up_glu(x, gain, shift, scale, w_up, l, conv_w, conv_b):
    B, S, D = x.shape
    Fd = w_up.shape[2] // 2
    tm = _tile(S, (1024, 512, 256, 128))
    tn = _tile(Fd, (512, 256, 128))
    nj = Fd // tn
    return _call(
        functools.partial(_ffnup_body, eps=NORM_EPS), grid=(B, S // tm, nj),
        in_specs=_halo_specs(S, tm, D) + [
            pl.BlockSpec((None, D, tn), lambda b, i, j: (l, 0, j)),
            pl.BlockSpec((None, D, tn), lambda b, i, j: (l, 0, nj + j)),
            pl.BlockSpec((3, tn), lambda b, i, j: (0, j)),
            pl.BlockSpec((1, tn), lambda b, i, j: (0, j))],
        out_specs=pl.BlockSpec((1, tm, tn), lambda b, i, j: (b, i, j)),
        out_shape=jax.ShapeDtypeStruct((B, S, Fd), BF16),
        scratch=[pltpu.VMEM((tm + 2 * HALO, D), BF16)],
        sem=("parallel", "parallel", "arbitrary"), name="ffn_up_glu")(
            x, x, x, gain.reshape(1, D), shift, scale, w_up, w_up, conv_w, conv_b.reshape(1, Fd))


def _filt_trunk_body(z_ref, w1_ref, b1_ref, fr_ref, w2_ref, b2_ref, o_ref):
    a = jnp.dot(z_ref[...], w1_ref[...], precision=HP, preferred_element_type=F32) + b1_ref[...]
    a = jnp.sin(fr_ref[0:1, :] * a)
    a = jnp.dot(a, w2_ref[...], precision=HP, preferred_element_type=F32) + b2_ref[...]
    o_ref[...] = jnp.sin(fr_ref[1:2, :] * a)


def _filt_main_body(a_ref, wf_ref, wb_ref, t_ref, dl_ref, o_ref, *, n):
    hf = jnp.dot(a_ref[0:n, :], wf_ref[...], precision=HP, preferred_element_type=F32)
    hb = jnp.dot(a_ref[n:2 * n, :], wb_ref[...], precision=HP, preferred_element_type=F32)
    r = lax.broadcasted_iota(jnp.int32, hb.shape, 0)
    k = jnp.concatenate([hf, jnp.where(r == 0, 0.0, hb)], axis=0) * jnp.exp(-t_ref[...] * dl_ref[...])
    o_ref[...] = k / jnp.sum(jnp.abs(k), axis=0, keepdims=True)


def _hyena_filters(n, w1, b1, freq, w2, b2, w3):
    Hd = w1.shape[1]
    OC = w3.shape[1] // 2
    t = jnp.linspace(0.0, 1.0, n, dtype=F32)[:, None]
    w = (2.0 * math.pi / n) * jnp.arange(n, dtype=F32)[:, None]
    f = jnp.linspace(1e-4, HY_BANDS - 1, HY_BANDS, dtype=F32)[None, :]
    z = jnp.concatenate([t, jnp.cos(f * w), -jnp.sin(f * w)], axis=-1)
    fold = lambda a: jnp.concatenate([a, jnp.zeros_like(a[:1]), a[:0:-1]], axis=0)
    EP = 64
    z2 = jnp.pad(fold(z), ((0, 0), (0, EP - HY_EMB)))
    w1p = jnp.pad(w1, ((0, EP - HY_EMB), (0, 0)))
    rt = _tile(2 * n, (1024, 512, 256))
    a2 = _call(
        _filt_trunk_body, grid=(2 * n // rt,),
        in_specs=[pl.BlockSpec((rt, EP), lambda i: (i, 0)),
                  pl.BlockSpec((EP, Hd), lambda i: (0, 0)),
                  pl.BlockSpec((1, Hd), lambda i: (0, 0)),
                  pl.BlockSpec((2, Hd), lambda i: (0, 0)),
                  pl.BlockSpec((Hd, Hd), lambda i: (0, 0)),
                  pl.BlockSpec((1, Hd), lambda i: (0, 0))],
        out_specs=pl.BlockSpec((rt, Hd), lambda i: (i, 0)),
        out_shape=jax.ShapeDtypeStruct((2 * n, Hd), F32),
        sem=("parallel",), name="hyena_filter_trunk")(z2, w1p, b1.reshape(1, Hd), freq, w2, b2.reshape(1, Hd))
    tc = LANES
    t2 = jnp.broadcast_to(fold(t), (2 * n, tc))
    deltas = jnp.abs(jnp.linspace(HY_MIN_DECAY, HY_MAX_DECAY, OC, dtype=F32))[None, :]
    nc = OC // tc
    return _call(
        functools.partial(_filt_main_body, n=n), grid=(nc,),
        in_specs=[pl.BlockSpec((2 * n, Hd), lambda c: (0, 0)),
                  pl.BlockSpec((Hd, tc), lambda c: (0, c)),
                  pl.BlockSpec((Hd, tc), lambda c: (0, nc + c)),
                  pl.BlockSpec((2 * n, tc), lambda c: (0, 0)),
                  pl.BlockSpec((1, tc), lambda c: (0, c))],
        out_specs=pl.BlockSpec((2 * n, tc), lambda c: (0, c)),
        out_shape=jax.ShapeDtypeStruct((2 * n, OC), F32),
        sem=("parallel",), name="hyena_filter")(a2, w3, w3, t2, deltas)


def _embed(re, im):
    return np.block([[re, -im], [im, re]])


@functools.lru_cache(maxsize=None)
def _fft_tables(N1, N2):
    N = N1 * N2
    S1 = N1 // 2
    i1 = np.arange(N1)
    ang = -2.0 * np.pi * ((i1[:, None] * i1[None, :]) % N1) / N1
    fr, fi = np.cos(ang), np.sin(ang)
    f1_pair = _embed(fr[:, :S1], fi[:, :S1])
    f1_real = np.concatenate([fr, fi], axis=0)
    i2 = np.arange(N2)
    fidx = i1[:, None, None] + N1 * i2[None, :, None]
    ang = -2.0 * np.pi * ((fidx * i2[None, None, :]) % N) / N
    mr, mi = np.cos(ang), np.sin(ang)
    m_fwd = np.stack([_embed(mr[a], mi[a]) for a in range(N1)])
    m_inv = np.stack([_embed(mr[a].T, -mi[a].T) for a in range(N1)])
    ang = 2.0 * np.pi * ((i1[:S1, None] * i1[None, :]) % N1) / N1
    f1_inv = _embed(np.cos(ang) / N, np.sin(ang) / N)
    cvt = lambda a: jnp.asarray(a, dtype=BF16)
    return cvt(f1_pair), cvt(f1_real), cvt(m_fwd), cvt(m_inv), cvt(f1_inv)


def _lmat_body(f_ref, x_ref, o_ref):
    o_ref[0] = jnp.dot(f_ref[...], x_ref[0].astype(BF16), preferred_element_type=F32).astype(o_ref.dtype)


def _left_matmul(fm, x, out_dtype, name):
    P, K, W = x.shape
    R = fm.shape[0]
    tw = _tile(W, (4096, 2048, 1024, 512, 256, 128))
    return _call(
        _lmat_body, grid=(P, W // tw),
        in_specs=[pl.BlockSpec((R, K), lambda p, j: (0, 0)),
                  pl.BlockSpec((1, K, tw), lambda p, j: (p, 0, j))],
        out_specs=pl.BlockSpec((1, R, tw), lambda p, j: (p, 0, j)),
        out_shape=jax.ShapeDtypeStruct((P, R, W), out_dtype),
        sem=("parallel", "parallel"), name=name)(fm, x)


def _cmul(xr, xi, kr, ki):
    return xr * kr - xi * ki, xr * ki + xi * kr


def _fftmid_body(a_ref, m_ref, mi_ref, k_ref, o_ref, *, FB, N2):
    for t in range(FB):
        a = a_ref[0, :, t].reshape(2 * N2, a_ref.shape[-1])
        x = jnp.dot(m_ref[t], a, preferred_element_type=F32)
        yr, yi = _cmul(x[:N2], x[N2:], k_ref[0, t], k_ref[1, t])
        y = jnp.concatenate([yr, yi], axis=0).astype(BF16)
        g = jnp.dot(mi_ref[t], y, preferred_element_type=F32)
        o_ref[0, :, t] = g.reshape(2, N2, g.shape[-1]).astype(o_ref.dtype)


def _fftfwd_body(a_ref, m_ref, o_ref, *, FB, N2):
    for t in range(FB):
        a = a_ref[:, t].reshape(2 * N2, a_ref.shape[-1])
        x = jnp.dot(m_ref[t], a, preferred_element_type=F32)
        o_ref[:, t] = x.reshape(2, N2, x.shape[-1])


SUB_BLOCK = 16


def _dft1_body(f_ref, x_ref, o_ref):
    xt = pltpu.einshape("ksc->skc", x_ref[0])
    r = jnp.stack([jnp.dot(f_ref[...], xt[k].astype(BF16), preferred_element_type=F32)
                   for k in range(xt.shape[0])], axis=0)
    o_ref[0] = pltpu.einshape("skc->ksc", r).astype(o_ref.dtype)


def _outer_dft(fm, x4, out_dtype, name):
    P, K, N2, C = x4.shape
    R = fm.shape[0]
    tc = _tile(C, (512, 256, 128))
    return _call(
        _dft1_body, grid=(P, N2 // SUB_BLOCK, C // tc),
        in_specs=[pl.BlockSpec((R, K), lambda p, s, c: (0, 0)),
                  pl.BlockSpec((1, K, SUB_BLOCK, tc), lambda p, s, c: (p, 0, s, c))],
        out_specs=pl.BlockSpec((1, R, SUB_BLOCK, tc), lambda p, s, c: (p, 0, s, c)),
        out_shape=jax.ShapeDtypeStruct((P, R, N2, C), out_dtype),
        sem=("parallel", "parallel", "parallel"), name=name)(fm, x4)


def _idft1_gate_body(f_ref, g_ref, z_ref, x1_ref, sk_ref, o_ref):
    gt = pltpu.einshape("ksc->skc", g_ref[0])
    y = jnp.stack([jnp.dot(f_ref[...], gt[k], preferred_element_type=F32) for k in range(gt.shape[0])], axis=0)
    y = pltpu.einshape("skc->ksc", y)
    o_ref[0] = (x1_ref[0] * (y + sk_ref[...] * z_ref[0])).astype(o_ref.dtype)


def _filter_spectrum_2stage(k, N1, N2):
    N, OC = k.shape
    _, f1_real, m_fwd, _, _ = _fft_tables(N1, N2)
    a = _outer_dft(f1_real, k.reshape(1, N1, N2, OC), BF16, "filter_dft1").reshape(2, N1, N2, OC)
    FB = _tile(N1, (8, 4, 2, 1))
    tc = _tile(OC, (512, 256, 128))
    return _call(
        functools.partial(_fftfwd_body, FB=FB, N2=N2), grid=(N1 // FB, OC // tc),
        in_specs=[pl.BlockSpec((2, FB, N2, tc), lambda f, c: (0, f, 0, c)),
                  pl.BlockSpec((FB, 2 * N2, 2 * N2), lambda f, c: (f, 0, 0))],
        out_specs=pl.BlockSpec((2, FB, N2, tc), lambda f, c: (0, f, 0, c)),
        out_shape=jax.ShapeDtypeStruct((2, N1, N2, OC), F32),
        sem=("parallel", "parallel"), name="filter_dft2")(a, m_fwd)


def _long_conv_gate_2stage(z, x1, skip, kf, order, out_dtype, N1, N2):
    B, n, C = z.shape
    assert B % 2 == 0
    P, S1 = B // 2, N1 // 2
    f1_pair, _, m_fwd, m_inv, f1_inv = _fft_tables(N1, N2)
    z4 = z.reshape(P, 2 * S1, N2, C)
    a = _outer_dft(f1_pair, z4, BF16, "conv_dft1").reshape(P, 2, N1, N2, C)
    FB = _tile(N1, (8, 4, 2, 1))
    tc = _tile(C, (512, 256, 128))
    oc = order * (C // tc)
    g = _call(
        functools.partial(_fftmid_body, FB=FB, N2=N2), grid=(N1 // FB, C // tc, P),
        in_specs=[pl.BlockSpec((1, 2, FB, N2, tc), lambda f, c, p: (p, 0, f, 0, c)),
                  pl.BlockSpec((FB, 2 * N2, 2 * N2), lambda f, c, p: (f, 0, 0)),
                  pl.BlockSpec((FB, 2 * N2, 2 * N2), lambda f, c, p: (f, 0, 0)),
                  pl.BlockSpec((2, FB, N2, tc), lambda f, c, p: (0, f, 0, oc + c))],
        out_specs=pl.BlockSpec((1, 2, FB, N2, tc), lambda f, c, p: (p, 0, f, 0, c)),
        out_shape=jax.ShapeDtypeStruct((P, 2, N1, N2, C), BF16),
        sem=("parallel", "parallel", "arbitrary"), name="conv_dft2_mul_idft2")(a, m_fwd, m_inv, kf)
    blk = lambda rows: pl.BlockSpec((1, rows, SUB_BLOCK, tc), lambda p, s, c: (p, 0, s, c))
    out = _call(
        _idft1_gate_body, grid=(P, N2 // SUB_BLOCK, C // tc),
        in_specs=[pl.BlockSpec((2 * S1, 2 * N1), lambda p, s, c: (0, 0)),
                  blk(2 * N1), blk(2 * S1), blk(2 * S1),
                  pl.BlockSpec((1, tc), lambda p, s, c: (0, c))],
        out_specs=blk(2 * S1),
        out_shape=jax.ShapeDtypeStruct((P, 2 * S1, N2, C), out_dtype),
        sem=("parallel", "parallel", "parallel"), name="conv_idft1_gate")(
            f1_inv, g.reshape(P, 2 * N1, N2, C), z4, x1.reshape(P, 2 * S1, N2, C), skip[None, :])
    return out.reshape(B, n, C)


@functools.lru_cache(maxsize=None)
def _dft_tables(n):
    N = 2 * n
    f = np.arange(N)
    ang = -2.0 * np.pi * ((f[:, None] * f[None, :]) % N) / N
    fr, fi = np.cos(ang), np.sin(ang)
    fwd_full = np.concatenate([fr, fi], axis=0)
    fwd_half = fwd_full[:, :n]
    inv = np.concatenate([fr[:n, :], fi[:n, :]], axis=1) / N
    cvt = lambda a: jnp.asarray(a, dtype=BF16)
    return cvt(fwd_full), cvt(fwd_half), cvt(inv)


def _dftconv_body(z_ref, x1_ref, sk_ref, f_ref, fi_ref, k_ref, o_ref, *, N):
    z = z_ref[0]
    x = jnp.dot(f_ref[...], z.astype(BF16), preferred_element_type=F32)
    yr, yi = _cmul(x[:N], x[N:], k_ref[0], k_ref[1])
    y = jnp.concatenate([yr, yi], axis=0).astype(BF16)
    y = jnp.dot(fi_ref[...], y, preferred_element_type=F32)
    o_ref[0] = (x1_ref[0] * (y + sk_ref[...] * z)).astype(o_ref.dtype)


def _long_conv_gate_dense(z, x1, skip, kf, order, out_dtype):
    B, n, C = z.shape
    N = 2 * n
    _, fwd_half, inv = _dft_tables(n)
    tc = _tile(C, (256, 128))
    oc = order * (C // tc)
    return _call(
        functools.partial(_dftconv_body, N=N), grid=(C // tc, B),
        in_specs=[pl.BlockSpec((1, n, tc), lambda c, b: (b, 0, c)),
                  pl.BlockSpec((1, n, tc), lambda c, b: (b, 0, c)),
                  pl.BlockSpec((1, tc), lambda c, b: (0, c)),
                  pl.BlockSpec((2 * N, n), lambda c, b: (0, 0)),
                  pl.BlockSpec((n, 2 * N), lambda c, b: (0, 0)),
                  pl.BlockSpec((2, N, tc), lambda c, b: (0, 0, oc + c))],
        out_specs=pl.BlockSpec((1, n, tc), lambda c, b: (b, 0, c)),
        out_shape=jax.ShapeDtypeStruct((B, n, C), out_dtype),
        sem=("parallel", "arbitrary"), name="conv_dense_dft")(z, x1, skip[None, :], fwd_half, inv, kf)


def _hyena_mixer(v, x1, x2, w1, b1, freq, w2, b2, w3, skip):
    B, n, _ = v.shape
    k = _hyena_filters(n, w1, b1, freq, w2, b2, w3)
    N = 2 * n
    if N % FFT_N2 == 0 and (N // FFT_N2) >= 16:
        N1 = N // FFT_N2
        kf = _filter_spectrum_2stage(k, N1, FFT_N2)
        z = _long_conv_gate_2stage(v, x1, skip[0], kf, 0, F32, N1, FFT_N2)
        return _long_conv_gate_2stage(z, x2, skip[1], kf, 1, BF16, N1, FFT_N2)
    fwd_full, _, _ = _dft_tables(n)
    kf = _left_matmul(fwd_full, k[None], F32, "filter_dense_dft").reshape(2, N, k.shape[1])
    z = _long_conv_gate_dense(v, x1, skip[0], kf, 0, F32)
    return _long_conv_gate_dense(z, x2, skip[1], kf, 1, BF16)


def _nt(a, b):
    return lax.dot_general(a, b, (((1,), (1,)), ((), ())), preferred_element_type=F32)


def _na_body(q_ref, k_ref, v_ref, kc_ref, vc_ref, b_ref, o_ref, *, R, KR, rows, HPS, scale):
    j = pl.program_id(2)
    start = pl.multiple_of(jnp.clip(j * R - NA_WIN_R // 2, 0, rows - KR) * GRID_W, GRID_W)
    for h in range(HPS):
        hs = slice(h * HEAD_DIM, (h + 1) * HEAD_DIM)
        q = q_ref[0, :, hs]
        kw = k_ref[0, pl.ds(start, KR * GRID_W), hs]
        vw = v_ref[0, pl.ds(start, KR * GRID_W), hs]
        s = _nt(q, kw) * scale + b_ref[h, 0]
        sc = _nt(q, kc_ref[0, :, hs]) * scale
        m = jnp.maximum(jnp.max(s, axis=-1, keepdims=True), jnp.max(sc, axis=-1, keepdims=True))
        p = jnp.exp(s - m)
        pc = jnp.exp(sc - m)
        l = jnp.sum(p, axis=-1, keepdims=True) + jnp.sum(pc, axis=-1, keepdims=True)
        o = jnp.dot(p.astype(BF16), vw, preferred_element_type=F32)
        o = o + jnp.dot(pc.astype(BF16), vc_ref[0, :, hs], preferred_element_type=F32)
        o_ref[0, :, hs] = (o / l).astype(o_ref.dtype)


def _na_geometry(S):
    rows = S // GRID_W
    kr = min(NA_WIN_R, rows)
    R = min(8, rows)
    KR = min(rows, R + kr)
    nb = rows // R
    assert rows % R == 0
    types = sorted({0, min(1, nb - 1), nb - 1})
    if nb > 3:
        offs = {int(np.clip(j * R - NA_WIN_R // 2, 0, rows - KR)) - j * R for j in range(1, nb - 1)}
        assert len(offs) == 1
    return rows, kr, R, KR, nb, types


def _nabias_body(rpb_ref, o_ref, tw_ref, *, plan, R, KR):
    W = GRID_W
    nd_r, nd_c = 2 * NA_WIN_R - 1, 2 * NA_WIN_C - 1
    base = pl.program_id(0) * (nd_r * nd_c)
    qc = lax.broadcasted_iota(jnp.int32, (W, 2 * W), 0)
    lane = lax.broadcasted_iota(jnp.int32, (W, 2 * W), 1)
    kc = lane % W
    cs = jnp.clip(qc - NA_WIN_C // 2, 0, W - NA_WIN_C)
    col_ok = jnp.logical_and(kc >= cs, kc < cs + NA_WIN_C)
    dcm = kc - qc + (NA_WIN_C - 1)
    neg = jnp.full((W, 2 * W), NEG_INF, F32)
    for dr in range(nd_r):
        acc = neg
        for dc in range(nd_c):
            acc = jnp.where(dcm == dc, rpb_ref[base + dr * nd_c + dc], acc)
        tw_ref[dr] = jnp.where(col_ok, acc, NEG_INF)
    left = lane < W
    for t, per_q in enumerate(plan):
        for qr in range(R):
            for kp in range(KR // 2):
                d0, d1 = per_q[qr][kp]
                a = neg if d0 is None else tw_ref[d0]
                b = neg if d1 is None else tw_ref[d1]
                blk = neg if (d0 is None and d1 is None) else jnp.where(left, a, b)
                o_ref[0, t, qr * W:(qr + 1) * W, kp * 2 * W:(kp + 1) * 2 * W] = blk


def _na_bias_tables(rpb, S):
    rows, kr, R, KR, nb, types = _na_geometry(S)
    assert 2 * GRID_W == LANES and KR % 2 == 0
    plan = []
    for jt in types:
        start = int(np.clip(jt * R - NA_WIN_R // 2, 0, rows - KR))
        per_q = []
        for q in range(R):
            qra = jt * R + q
            ws = int(np.clip(qra - kr // 2, 0, rows - kr))
            d = [(start + k) - qra + (NA_WIN_R - 1) if ws <= start + k < ws + kr else None for k in range(KR)]
            per_q.append([(d[2 * p], d[2 * p + 1]) for p in range(KR // 2)])
        plan.append(per_q)
    L, H, nd_r, nd_c = rpb.shape
    T, QB, KB = len(types), R * GRID_W, KR * GRID_W
    return _call(
        functools.partial(_nabias_body, plan=plan, R=R, KR=KR), grid=(L * H,),
        in_specs=[pl.BlockSpec(memory_space=pltpu.SMEM)],
        out_specs=pl.BlockSpec((1, T, QB, KB), lambda i: (i, 0, 0, 0)),
        out_shape=jax.ShapeDtypeStruct((L * H, T, QB, KB), F32),
        scratch=[pltpu.VMEM((nd_r, GRID_W, 2 * GRID_W), F32)],
        sem=("parallel",), name="na_bias_table")(rpb.reshape(-1).astype(F32))


def _na_attention(qkv, qkv_c, bias, l, offs, H):
    B, S, _ = qkv.shape
    CTX = qkv_c.shape[1]
    rows, kr, R, KR, nb, types = _na_geometry(S)
    T = len(types)
    QB, KB = R * GRID_W, KR * GRID_W
    HPS = 2 if H % 2 == 0 else 1
    HW = HPS * HEAD_DIM
    assert all(offs[n] % HW == 0 for n in ("na_k", "na_v", "na_q"))
    ok_, ov_, oq_ = (offs[n] // HW for n in ("na_k", "na_v", "na_q"))

    def btype(j):
        if T == nb:
            return j
        return jnp.where(j == 0, 0, jnp.where(j == nb - 1, T - 1, 1))

    return _call(
        functools.partial(_na_body, R=R, KR=KR, rows=rows, HPS=HPS, scale=HEAD_DIM ** -0.5),
        grid=(B, H // HPS, nb),
        in_specs=[pl.BlockSpec((1, QB, HW), lambda b, h, j: (b, j, oq_ + h)),
                  pl.BlockSpec((1, S, HW), lambda b, h, j: (b, 0, ok_ + h)),
                  pl.BlockSpec((1, S, HW), lambda b, h, j: (b, 0, ov_ + h)),
                  pl.BlockSpec((1, CTX, HW), lambda b, h, j: (b, 0, ok_ + h)),
                  pl.BlockSpec((1, CTX, HW), lambda b, h, j: (b, 0, ov_ + h)),
                  pl.BlockSpec((HPS, 1, QB, KB), lambda b, h, j: (l * (H // HPS) + h, btype(j), 0, 0))],
        out_specs=pl.BlockSpec((1, QB, HW), lambda b, h, j: (b, j, h)),
        out_shape=jax.ShapeDtypeStruct((B, S, H * HEAD_DIM), BF16),
        sem=("parallel", "parallel", "arbitrary"), name="na_attention")(qkv, qkv, qkv, qkv_c, qkv_c, bias)


def _stack_heads(q2, G):
    return jnp.concatenate([q2[:, g * HEAD_DIM:(g + 1) * HEAD_DIM] for g in range(G)], axis=0)


def _unstack_heads(o, G, n):
    return jnp.concatenate([o[g * n:(g + 1) * n] for g in range(G)], axis=1)


def _sink_column(sink_ref, h0, G, n):
    return jnp.concatenate([jnp.full((n, 1), sink_ref[h0 + g], F32) for g in range(G)], axis=0)


def _swa_body(sink_ref, q_ref, k_ref, v_ref, kc_ref, vc_ref, o_ref, *, QB, KB, S, G, HPS, scale):
    hb, j = pl.program_id(1), pl.program_id(2)
    start = pl.multiple_of(jnp.clip(j * QB - GQA_WINDOW, 0, S - KB), LANES)
    qpos = j * QB + lax.broadcasted_iota(jnp.int32, (QB, KB), 0)
    kpos = start + lax.broadcasted_iota(jnp.int32, (QB, KB), 1)
    mask = jnp.where(jnp.abs(qpos - kpos) <= GQA_WINDOW, 0.0, NEG_INF)
    mask = jnp.concatenate([mask] * G, axis=0)
    for h in range(HPS):
        hs = slice(h * HEAD_DIM, (h + 1) * HEAD_DIM)
        qs = slice(h * G * HEAD_DIM, (h + 1) * G * HEAD_DIM)
        q = _stack_heads(q_ref[0, :, qs], G)
        kw = k_ref[0, pl.ds(start, KB), hs]
        vw = v_ref[0, pl.ds(start, KB), hs]
        s = _nt(q, kw) * scale + mask
        sc = _nt(q, kc_ref[0, :, hs]) * scale
        sk = _sink_column(sink_ref, (hb * HPS + h) * G, G, QB)
        m = jnp.maximum(jnp.maximum(jnp.max(s, axis=-1, keepdims=True), jnp.max(sc, axis=-1, keepdims=True)), sk)
        p = jnp.exp(s - m)
        pc = jnp.exp(sc - m)
        l = jnp.sum(p, axis=-1, keepdims=True) + jnp.sum(pc, axis=-1, keepdims=True) + jnp.exp(sk - m)
        o = jnp.dot(p.astype(BF16), vw, preferred_element_type=F32)
        o = o + jnp.dot(pc.astype(BF16), vc_ref[0, :, hs], preferred_element_type=F32)
        o_ref[0, :, qs] = _unstack_heads(o / l, G, QB).astype(o_ref.dtype)


def _swa_attention(qkv, qkv_c, sink, offs, KVH):
    B, S, _ = qkv.shape
    CTX = qkv_c.shape[1]
    G = GQA_GROUP
    QB = _tile(S, (512, 256, 128))
    KB = min(S, QB + 2 * GQA_WINDOW)
    HPS = 2 if KVH % 2 == 0 else 1
    HW = HPS * HEAD_DIM
    assert offs["sw_k"] % HW == 0 and offs["sw_v"] % HW == 0 and offs["sw_q"] % (G * HW) == 0
    ok_, ov_, oq_ = offs["sw_k"] // HW, offs["sw_v"] // HW, offs["sw_q"] // (G * HW)
    return _call(
        functools.partial(_swa_body, QB=QB, KB=KB, S=S, G=G, HPS=HPS, scale=HEAD_DIM ** -0.5),
        grid=(B, KVH // HPS, S // QB),
        in_specs=[pl.BlockSpec(memory_space=pltpu.SMEM),
                  pl.BlockSpec((1, QB, G * HW), lambda b, h, j: (b, j, oq_ + h)),
                  pl.BlockSpec((1, S, HW), lambda b, h, j: (b, 0, ok_ + h)),
                  pl.BlockSpec((1, S, HW), lambda b, h, j: (b, 0, ov_ + h)),
                  pl.BlockSpec((1, CTX, HW), lambda b, h, j: (b, 0, ok_ + h)),
                  pl.BlockSpec((1, CTX, HW), lambda b, h, j: (b, 0, ov_ + h))],
        out_specs=pl.BlockSpec((1, QB, G * HW), lambda b, h, j: (b, j, h)),
        out_shape=jax.ShapeDtypeStruct((B, S, KVH * G * HEAD_DIM), BF16),
        sem=("parallel", "parallel", "arbitrary"), name="swa_attention")(sink, qkv, qkv, qkv, qkv_c, qkv_c)


def _cattn_body(sink_ref, q_ref, k_ref, v_ref, o_ref, *, G, use_sink, scale):
    h = pl.program_id(1)
    n = q_ref.shape[1]
    q = _stack_heads(q_ref[0], G)
    s = _nt(q, k_ref[0]) * scale
    m = jnp.max(s, axis=-1, keepdims=True)
    if use_sink:
        sk = _sink_column(sink_ref, h * G, G, n)
        m = jnp.maximum(m, sk)
    p = jnp.exp(s - m)
    l = jnp.sum(p, axis=-1, keepdims=True)
    if use_sink:
        l = l + jnp.exp(sk - m)
    o = jnp.dot(p.astype(BF16), v_ref[0], preferred_element_type=F32)
    o_ref[0] = _unstack_heads(o / l, G, n).astype(o_ref.dtype)


def _ctx_attention(qkv_c, sink, oq, ok, ov, KVH, G, use_sink):
    B, n, _ = qkv_c.shape
    oq_, ok_, ov_ = oq // (G * HEAD_DIM), ok // HEAD_DIM, ov // HEAD_DIM
    return _call(
        functools.partial(_cattn_body, G=G, use_sink=use_sink, scale=HEAD_DIM ** -0.5),
        grid=(B, KVH),
        in_specs=[pl.BlockSpec(memory_space=pltpu.SMEM),
                  pl.BlockSpec((1, n, G * HEAD_DIM), lambda b, h: (b, 0, oq_ + h)),
                  pl.BlockSpec((1, n, HEAD_DIM), lambda b, h: (b, 0, ok_ + h)),
                  pl.BlockSpec((1, n, HEAD_DIM), lambda b, h: (b, 0, ov_ + h))],
        out_specs=pl.BlockSpec((1, n, G * HEAD_DIM), lambda b, h: (b, 0, h)),
        out_shape=jax.ShapeDtypeStruct((B, n, KVH * G * HEAD_DIM), BF16),
        sem=("parallel", "parallel"), name="ctx_attention")(sink, qkv_c, qkv_c, qkv_c)


def _merge_body(x_ref, g_ref, sh_ref, sc_ref, yh, yn, ys, wgh, wgn, wgs, wh, wn, ws, o_ref, h_ref, *, eps):
    @pl.when(pl.program_id(2) == 0)
    def _():
        h_ref[...] = _norm_modulate(x_ref[0], g_ref, sh_ref, sc_ref, eps).astype(BF16)

    h = h_ref[...]
    gate = lambda wg: _sigmoid(jnp.dot(h, wg[...], preferred_element_type=F32))
    m = gate(wgh) * jnp.dot(yh[0], wh[...], preferred_element_type=F32)
    m = m + gate(wgn) * jnp.dot(yn[0], wn[...], preferred_element_type=F32)
    m = m + gate(wgs) * jnp.dot(ys[0], ws[...], preferred_element_type=F32)
    o_ref[0] = m.astype(o_ref.dtype)


def _merge_branches(x, gain, shift, scale, w_in, gate_start, y_hy, y_na, y_sw, w_br, l):
    B, S, D = x.shape
    widths = (y_hy.shape[2], y_na.shape[2], y_sw.shape[2])
    starts = (0, widths[0], widths[0] + widths[1])
    assert all(s % w == 0 for s, w in zip(starts, widths))
    tm = _tile(S, (512, 256, 128))
    tn = _seg_tile([(gate_start, D)], (512, 256, 128))
    nj, gj = D // tn, gate_start // tn
    yspec = lambda y: pl.BlockSpec((1, tm, y.shape[2]), lambda b, i, j: (b, i, 0))
    gspec = lambda k: pl.BlockSpec((None, D, tn), lambda b, i, j: (l, 0, gj + k * nj + j))
    wspec = lambda k: pl.BlockSpec((None, widths[k], tn), lambda b, i, j: (l, starts[k] // widths[k], j))
    return _call(
        functools.partial(_merge_body, eps=NORM_EPS), grid=(B, S // tm, nj),
        in_specs=[pl.BlockSpec((1, tm, D), lambda b, i, j: (b, i, 0)),
                  pl.BlockSpec((1, D), lambda b, i, j: (0, 0)),
                  pl.BlockSpec((1, 1, D), lambda b, i, j: (b, 0, 0)),
                  pl.BlockSpec((1, 1, D), lambda b, i, j: (b, 0, 0)),
                  yspec(y_hy), yspec(y_na), yspec(y_sw), gspec(0), gspec(1), gspec(2),
                  wspec(0), wspec(1), wspec(2)],
        out_specs=pl.BlockSpec((1, tm, tn), lambda b, i, j: (b, i, j)),
        out_shape=jax.ShapeDtypeStruct((B, S, D), BF16),
        scratch=[pltpu.VMEM((tm, D), BF16)],
        sem=("parallel", "parallel", "arbitrary"), name="gated_merge")(
            x, gain.reshape(1, D), shift, scale, y_hy, y_na, y_sw, w_in, w_in, w_in, w_br, w_br, w_br)


def _mmres_body(a_ref, w_ref, x_ref, g_ref, mg_ref, o_ref, *, nk, eps):
    k = pl.program_id(2)
    part = lambda: jnp.dot(a_ref[0], w_ref[...], preferred_element_type=F32)

    def finish():
        y = o_ref[0]
        yn = y * lax.rsqrt(jnp.mean(y * y, axis=-1, keepdims=True) + eps) * g_ref[...]
        o_ref[0] = x_ref[0] + mg_ref[0] * yn

    if nk == 1:
        o_ref[0] = part()
        finish()
        return

    @pl.when(k == 0)
    def _():
        o_ref[0] = part()

    @pl.when(k > 0)
    def _():
        o_ref[0] += part()

    @pl.when(k == nk - 1)
    def _():
        finish()


def _matmul_norm_residual(a, w, l, x, gain, mgate, name):
    B, S, K = a.shape
    D = w.shape[2]
    if K <= 2048:
        tm, tk = _tile(S, (512, 256, 128)), K
    else:
        tm = _tile(S, (1024, 512, 256, 128))
        tk = _tile(K, (512, 256, 128))
    nk = K // tk
    return _call(
        functools.partial(_mmres_body, nk=nk, eps=NORM_EPS), grid=(B, S // tm, nk),
        in_specs=[pl.BlockSpec((1, tm, tk), lambda b, i, k: (b, i, k)),
                  pl.BlockSpec((None, tk, D), lambda b, i, k: (l, k, 0)),
                  pl.BlockSpec((1, tm, D), lambda b, i, k: (b, i, 0)),
                  pl.BlockSpec((1, D), lambda b, i, k: (0, 0)),
                  pl.BlockSpec((1, 1, D), lambda b, i, k: (b, 0, 0))],
        out_specs=pl.BlockSpec((1, tm, D), lambda b, i, k: (b, i, 0)),
        out_shape=jax.ShapeDtypeStruct((B, S, D), F32),
        sem=("parallel", "parallel", "arbitrary"), name=name)(a, w, x, gain.reshape(1, D), mgate)


def _rope_tables(n):
    t = jnp.arange(n)
    row = (t // GRID_W).astype(F32)
    col = (t % GRID_W).astype(F32)
    per_axis = HEAD_DIM // 2
    inv = ROPE_BASE ** (-jnp.arange(0, per_axis, 2, dtype=F32) / per_axis)
    ar, ac = row[:, None] * inv, col[:, None] * inv
    cos_t = jnp.concatenate([jnp.cos(ar), jnp.cos(ar), jnp.cos(ac), jnp.cos(ac)], axis=1)
    sin_t = jnp.concatenate([-jnp.sin(ar), jnp.sin(ar), -jnp.sin(ac), jnp.sin(ac)], axis=1)
    return cos_t, sin_t


def kernel(x, c, ctx, c_ctx, w_mod, b_mod, norm_gains, w_in, hy_conv_w, hy_conv_b, hy_w1, hy_b1, hy_freq, hy_w2, hy_b2, hy_w3, hy_skip, na_rpb, swa_sink, w_branch, w_out, ffn_w_up, ffn_conv_w, ffn_conv_b, ffn_w_down):
    B, S, D = x.shape
    L = w_mod.shape[0]
    C = hy_skip.shape[-1]
    H_na = na_rpb.shape[1]
    H_q = swa_sink.shape[1]
    KVH = H_q // GQA_GROUP
    NA_W, QW, KVW = H_na * HEAD_DIM, H_q * HEAD_DIM, KVH * HEAD_DIM
    KV_COLS = 2 * NA_W + 2 * KVW
    qkv_segs = [(0, KV_COLS), (KV_COLS + 3 * C, NA_W + QW)]
    gate_start = KV_COLS + 3 * C + NA_W + QW
    offs = {"na_k": 0, "na_v": NA_W, "sw_k": 2 * NA_W, "sw_v": 2 * NA_W + KVW,
            "na_q": KV_COLS, "sw_q": KV_COLS + NA_W}
    rope_chunks = (list(range(offs["sw_k"] // LANES, (offs["sw_k"] + KVW) // LANES))
                   + list(range(offs["sw_q"] // LANES, (offs["sw_q"] + QW) // LANES)))
    cos_t, sin_t = _rope_tables(S)

    w_in_b, w_out_b = w_in.astype(BF16), w_out.astype(BF16)
    w_br_b, w_up_b, w_dn_b = w_branch.astype(BF16), ffn_w_up.astype(BF16), ffn_w_down.astype(BF16)
    na_bias = _na_bias_tables(na_rpb, S)

    R = -(-(B + 1) // 8) * 8
    cc = jnp.concatenate([c, c_ctx[None, :], jnp.zeros((R - B - 1, D), F32)], axis=0)
    mods = _modulation(cc, w_mod, b_mod)

    xc = ctx
    for l in range(L):
        mod = [mods[l, :B, k * D:(k + 1) * D].reshape(B, 1, D) for k in range(6)]
        mod_c = [jnp.broadcast_to(mods[l, B, k * D:(k + 1) * D].reshape(1, 1, D), (B, 1, D)) for k in range(6)]
        g = norm_gains[l]
        hy_p = (hy_w1[l], hy_b1[l], hy_freq[l], hy_w2[l], hy_b2[l], hy_w3[l], hy_skip[l])

        qkv_c, vc, x1c, x2c = _in_proj(xc, g[0], mod_c[0], mod_c[1], w_in_b, l, qkv_segs, KV_COLS, C,
                                       hy_conv_w[l], hy_conv_b[l], None, "ctx_in_proj")
        qkv, v, x1, x2 = _in_proj(x, g[0], mod[0], mod[1], w_in_b, l, qkv_segs, KV_COLS, C,
                                  hy_conv_w[l], hy_conv_b[l], (cos_t, sin_t, rope_chunks), "in_proj")
        y_hy = _hyena_mixer(v, x1, x2, *hy_p)
        y_na = _na_attention(qkv, qkv_c, na_bias, l, offs, H_na)
        y_sw = _swa_attention(qkv, qkv_c, swa_sink[l], offs, KVH)
        m = _merge_branches(x, g[0], mod[0], mod[1], w_in_b, gate_start, y_hy, y_na, y_sw, w_br_b, l)
        x = _matmul_norm_residual(m, w_out_b, l, x, g[1], mod[2], "out_proj_residual")
        gl = _ffn_up_glu(x, g[2], mod[3], mod[4], w_up_b, l, ffn_conv_w[l], ffn_conv_b[l])
        x = _matmul_norm_residual(gl, w_dn_b, l, x, g[3], mod[5], "ffn_down_residual")

        if l < L - 1:
            yc_hy = _hyena_mixer(vc, x1c, x2c, *hy_p)
            yc_na = _ctx_attention(qkv_c, swa_sink[l], offs["na_q"], offs["na_k"], offs["na_v"], H_na, 1, False)
            yc_sw = _ctx_attention(qkv_c, swa_sink[l], offs["sw_q"], offs["sw_k"], offs["sw_v"], KVH, GQA_GROUP, True)
            mc = _merge_branches(xc, g[0], mod_c[0], mod_c[1], w_in_b, gate_start, yc_hy, yc_na, yc_sw, w_br_b, l)
            xc = _matmul_norm_residual(mc, w_out_b, l, xc, g[1], mod_c[2], "ctx_out_proj_residual")
            gl_c = _ffn_up_glu(xc, g[2], mod_c[3], mod_c[4], w_up_b, l, ffn_conv_w[l], ffn_conv_b[l])
            xc = _matmul_norm_residual(gl_c, w_dn_b, l, xc, g[3], mod_c[5], "ctx_ffn_down_residual")
    return x
```

```python
import functools
import math

import numpy as np
import jax
import jax.numpy as jnp
from jax import lax
from jax.experimental import pallas as pl
from jax.experimental.pallas import tpu as pltpu

F32 = jnp.float32
BF16 = jnp.bfloat16
NEG_INF = -1e30

GRID_W = 64
HEAD_DIM = 128
HY_EMB = 33
HY_BANDS = (HY_EMB - 1) // 2
HY_FAST_DECAY = 0.3
HY_SLOW_DECAY = 1.5
HY_DECAY_TARGET = 1e-2
HY_MAX_DECAY = math.log(HY_DECAY_TARGET) / HY_FAST_DECAY
HY_MIN_DECAY = math.log(HY_DECAY_TARGET) / HY_SLOW_DECAY
NA_WIN_R = 8
NA_WIN_C = 16
GQA_GROUP = 2
GQA_WINDOW = 128
ROPE_BASE = 10000.0
NORM_EPS = 1e-6

LANES = 128
FFT_N2 = 128
VMEM_LIMIT = 56 * 1024 * 1024
VMEM_LIMIT_LARGE = 60 * 1024 * 1024
VMEM_TEMP_RESERVE = 8 * 1024 * 1024
HP = lax.Precision.HIGHEST


def _call(body, *, grid, in_specs, out_specs, out_shape, scratch=(), sem, name, vmem=VMEM_LIMIT):
    return pl.pallas_call(
        body, grid=grid, in_specs=in_specs, out_specs=out_specs, out_shape=out_shape,
        scratch_shapes=list(scratch),
        compiler_params=pltpu.CompilerParams(dimension_semantics=sem, vmem_limit_bytes=vmem),
        name=name)


def _tile(n, cands):
    for c in cands:
        if n % c == 0:
            return c
    raise ValueError(f"no tile for {n} in {cands}")


def _sigmoid(x):
    return 1.0 / (1.0 + jnp.exp(-x))


def _mod_body(c_ref, w_ref, b_ref, o_ref):
    c = c_ref[...]
    s = (c * _sigmoid(c)).astype(BF16)
    o_ref[0] = jnp.dot(s, w_ref[0].astype(BF16), preferred_element_type=F32) + b_ref[0]


def _modulation(cc, w_mod, b_mod):
    L, D, N = w_mod.shape
    R = cc.shape[0]
    tn = _tile(N, (1024, 512, 256, 128))
    return _call(
        _mod_body, grid=(L, N // tn),
        in_specs=[pl.BlockSpec((R, D), lambda l, j: (0, 0)),
                  pl.BlockSpec((1, D, tn), lambda l, j: (l, 0, j)),
                  pl.BlockSpec((1, 1, tn), lambda l, j: (l, 0, j))],
        out_specs=pl.BlockSpec((1, R, tn), lambda l, j: (l, 0, j)),
        out_shape=jax.ShapeDtypeStruct((L, R, N), F32),
        sem=("parallel", "parallel"), name="modulation")(cc, w_mod, b_mod.reshape(L, 1, N))


def _swap32(a):
    lane = lax.broadcasted_iota(jnp.int32, a.shape, 1)
    return jnp.where((lane & 32) == 0, pltpu.roll(a, 96, 1), pltpu.roll(a, 32, 1))


def _norm_modulate(x, g_ref, sh_ref, sc_ref, eps):
    y = x * lax.rsqrt(jnp.mean(x * x, axis=-1, keepdims=True) + eps) * g_ref[...]
    return y * (1.0 + sc_ref[0]) + sh_ref[0]


HALO = 16


def _halo_prologue(h_ref, x_ref, xp_ref, xn_ref, g_ref, sh_ref, sc_ref, eps):
    i, tm = pl.program_id(1), x_ref.shape[1]
    nm = lambda x: _norm_modulate(x, g_ref, sh_ref, sc_ref, eps)
    keep_prev = (i > 0).astype(F32)
    keep_next = (i < pl.num_programs(1) - 1).astype(F32)
    h_ref[0:HALO, :] = (nm(xp_ref[0]) * keep_prev).astype(BF16)
    h_ref[HALO:HALO + tm, :] = nm(x_ref[0]).astype(BF16)
    h_ref[HALO + tm:, :] = (nm(xn_ref[0]) * keep_next).astype(BF16)


def _conv3_halo(a, cw_ref, cb_ref, tm):
    rows = a.shape[0]
    up = pltpu.roll(a, 1, 0)[HALO:HALO + tm]
    un = pltpu.roll(a, rows - 1, 0)[HALO:HALO + tm]
    return up * cw_ref[0:1, :] + a[HALO:HALO + tm] * cw_ref[1:2, :] + un * cw_ref[2:3, :] + cb_ref[...]


def _halo_specs(S, tm, D):
    hb, nh = tm // HALO, S // HALO
    return [pl.BlockSpec((1, tm, D), lambda b, i, j: (b, i, 0)),
            pl.BlockSpec((1, HALO, D), lambda b, i, j: (b, jnp.maximum(i * hb - 1, 0), 0)),
            pl.BlockSpec((1, HALO, D), lambda b, i, j: (b, jnp.minimum((i + 1) * hb, nh - 1), 0)),
            pl.BlockSpec((1, D), lambda b, i, j: (0, 0)),
            pl.BlockSpec((1, 1, D), lambda b, i, j: (b, 0, 0)),
            pl.BlockSpec((1, 1, D), lambda b, i, j: (b, 0, 0))]


def _inproj_body(*refs, nq, nc, rope_chunks, eps):
    if rope_chunks:
        x_ref, xp_ref, xn_ref, g_ref, sh_ref, sc_ref, w_ref, cw_ref, cb_ref, cos_ref, sin_ref, q_o, o0, o1, o2, h_ref = refs
    else:
        x_ref, xp_ref, xn_ref, g_ref, sh_ref, sc_ref, w_ref, cw_ref, cb_ref, q_o, o0, o1, o2, h_ref = refs
    j = pl.program_id(2)
    tm = x_ref.shape[1]

    @pl.when(j == 0)
    def _():
        _halo_prologue(h_ref, x_ref, xp_ref, xn_ref, g_ref, sh_ref, sc_ref, eps)

    qkv_dot = lambda: jnp.dot(h_ref[HALO:HALO + tm, :], w_ref[...], preferred_element_type=F32)
    nch = w_ref.shape[1] // LANES
    rope_tiles = sorted({ch // nch for ch in rope_chunks})
    for jt in rope_tiles:
        @pl.when(j == jt)
        def _(jt=jt):
            acc = qkv_dot()
            c, s = cos_ref[...], sin_ref[...]
            for k in range(nch):
                a = acc[:, k * LANES:(k + 1) * LANES]
                if jt * nch + k in rope_chunks:
                    a = a * c + _swap32(a) * s
                q_o[0, :, k * LANES:(k + 1) * LANES] = a.astype(q_o.dtype)

    @pl.when(functools.reduce(jnp.logical_and, [j != jt for jt in rope_tiles], j < nq))
    def _():
        q_o[0] = qkv_dot().astype(q_o.dtype)

    for k, o in enumerate((o0, o1, o2)):
        @pl.when(jnp.logical_and(j >= nq + k * nc, j < nq + (k + 1) * nc))
        def _(o=o):
            a = jnp.dot(h_ref[...], w_ref[...], preferred_element_type=F32)
            o[0] = _conv3_halo(a, cw_ref, cb_ref, tm)


def _colmap(segs, tn):
    bounds, o = [], 0
    for s, w in segs:
        assert s % tn == 0 and w % tn == 0
        bounds.append(((o + w) // tn, (s - o) // tn))
        o += w

    def f(j):
        r = j + bounds[-1][1]
        for hi, off in reversed(bounds[:-1]):
            r = jnp.where(j < hi, j + off, r)
        return r
    return f, o


def _seg_tile(segs, cands):
    return _tile(functools.reduce(math.gcd, [v for seg in segs for v in seg if v]), cands)


def _in_proj(x, gain, shift, scale, w_in, l, qkv_segs, hy_start, C, conv_w, conv_b, rope, name):
    B, S, D = x.shape
    tm = _tile(S, (1024, 512, 256, 128))
    segs = list(qkv_segs) + [(hy_start, 3 * C)]
    tn = _seg_tile(segs + [(0, C)], (512, 256, 128))
    cmap, n_out = _colmap(segs, tn)
    nc = C // tn
    nq = n_out // tn - 3 * nc
    in_specs = _halo_specs(S, tm, D) + [
        pl.BlockSpec((None, D, tn), lambda b, i, j: (l, 0, cmap(j))),
        pl.BlockSpec((3, tn), lambda b, i, j: (0, jnp.clip(j - nq, 0, 3 * nc - 1))),
        pl.BlockSpec((1, tn), lambda b, i, j: (0, jnp.clip(j - nq, 0, 3 * nc - 1)))]
    args = [x, x, x, gain.reshape(1, D), shift, scale, w_in, conv_w, conv_b.reshape(1, 3 * C)]
    rope_chunks = ()
    if rope is not None:
        cos_t, sin_t, rope_chunks = rope
        in_specs += [pl.BlockSpec((tm, LANES), lambda b, i, j: (i, 0)),
                     pl.BlockSpec((tm, LANES), lambda b, i, j: (i, 0))]
        args += [cos_t, sin_t]
    hy_spec = lambda k: pl.BlockSpec((1, tm, tn), lambda b, i, j: (b, i, jnp.clip(j - nq - k * nc, 0, nc - 1)))
    return _call(
        functools.partial(_inproj_body, nq=nq, nc=nc, rope_chunks=frozenset(rope_chunks), eps=NORM_EPS),
        grid=(B, S // tm, n_out // tn), in_specs=in_specs,
        out_specs=[pl.BlockSpec((1, tm, tn), lambda b, i, j: (b, i, jnp.minimum(j, nq - 1))),
                   hy_spec(0), hy_spec(1), hy_spec(2)],
        out_shape=[jax.ShapeDtypeStruct((B, S, nq * tn), BF16)] + [jax.ShapeDtypeStruct((B, S, C), F32)] * 3,
        scratch=[pltpu.VMEM((tm + 2 * HALO, D), BF16)],
        sem=("parallel", "parallel", "arbitrary"), name=name)(*args)


def _ffnup_body(x_ref, xp_ref, xn_ref, g_ref, sh_ref, sc_ref, wa_ref, wu_ref, cw_ref, cb_ref, o_ref, h_ref, *, eps):
    tm = x_ref.shape[1]

    @pl.when(pl.program_id(2) == 0)
    def _():
        _halo_prologue(h_ref, x_ref, xp_ref, xn_ref, g_ref, sh_ref, sc_ref, eps)

    a = jnp.dot(h_ref[...], wa_ref[...], preferred_element_type=F32)
    u = jnp.dot(h_ref[HALO:HALO + tm, :], wu_ref[...], preferred_element_type=F32)
    c = _conv3_halo(a, cw_ref, cb_ref, tm)
    o_ref[0] = (c * _sigmoid(c) * u).astype(o_ref.dtype)


def _ffn_up_glu(x, gain, shift, scale, w_up, l, conv_w, conv_b):
    B, S, D = x.shape
    Fd = w_up.shape[2] // 2
    tm = _tile(S, (1024, 512, 256, 128))
    tn = _tile(Fd, (512, 256, 128))
    nj = Fd // tn
    return _call(
        functools.partial(_ffnup_body, eps=NORM_EPS), grid=(B, S // tm, nj),
        in_specs=_halo_specs(S, tm, D) + [
            pl.BlockSpec((None, D, tn), lambda b, i, j: (l, 0, j)),
            pl.BlockSpec((None, D, tn), lambda b, i, j: (l, 0, nj + j)),
            pl.BlockSpec((3, tn), lambda b, i, j: (0, j)),
            pl.BlockSpec((1, tn), lambda b, i, j: (0, j))],
        out_specs=pl.BlockSpec((1, tm, tn), lambda b, i, j: (b, i, j)),
        out_shape=jax.ShapeDtypeStruct((B, S, Fd), BF16),
        scratch=[pltpu.VMEM((tm + 2 * HALO, D), BF16)],
        sem=("parallel", "parallel", "arbitrary"), name="ffn_up_glu")(
            x, x, x, gain.reshape(1, D), shift, scale, w_up, w_up, conv_w, conv_b.reshape(1, Fd))


def _filt_trunk_body(z_ref, w1_ref, b1_ref, fr_ref, w2_ref, b2_ref, o_ref):
    a = jnp.dot(z_ref[...], w1_ref[...], precision=HP, preferred_element_type=F32) + b1_ref[...]
    a = jnp.sin(fr_ref[0:1, :] * a)
    a = jnp.dot(a, w2_ref[...], precision=HP, preferred_element_type=F32) + b2_ref[...]
    o_ref[...] = jnp.sin(fr_ref[1:2, :] * a)


def _filt_main_body(a_ref, wf_ref, wb_ref, t_ref, dl_ref, o_ref, *, n):
    hf = jnp.dot(a_ref[0:n, :], wf_ref[...], precision=HP, preferred_element_type=F32)
    hb = jnp.dot(a_ref[n:2 * n, :], wb_ref[...], precision=HP, preferred_element_type=F32)
    r = lax.broadcasted_iota(jnp.int32, hb.shape, 0)
    k = jnp.concatenate([hf, jnp.where(r == 0, 0.0, hb)], axis=0) * jnp.exp(-t_ref[...] * dl_ref[...])
    o_ref[...] = k / jnp.sum(jnp.abs(k), axis=0, keepdims=True)


def _hyena_filters(n, w1, b1, freq, w2, b2, w3):
    Hd = w1.shape[1]
    OC = w3.shape[1] // 2
    t = jnp.linspace(0.0, 1.0, n, dtype=F32)[:, None]
    w = (2.0 * math.pi / n) * jnp.arange(n, dtype=F32)[:, None]
    f = jnp.linspace(1e-4, HY_BANDS - 1, HY_BANDS, dtype=F32)[None, :]
    z = jnp.concatenate([t, jnp.cos(f * w), -jnp.sin(f * w)], axis=-1)
    fold = lambda a: jnp.concatenate([a, jnp.zeros_like(a[:1]), a[:0:-1]], axis=0)
    EP = 64
    z2 = jnp.pad(fold(z), ((0, 0), (0, EP - HY_EMB)))
    w1p = jnp.pad(w1, ((0, EP - HY_EMB), (0, 0)))
    rt = _tile(2 * n, (1024, 512, 256))
    a2 = _call(
        _filt_trunk_body, grid=(2 * n // rt,),
        in_specs=[pl.BlockSpec((rt, EP), lambda i: (i, 0)),
                  pl.BlockSpec((EP, Hd), lambda i: (0, 0)),
                  pl.BlockSpec((1, Hd), lambda i: (0, 0)),
                  pl.BlockSpec((2, Hd), lambda i: (0, 0)),
                  pl.BlockSpec((Hd, Hd), lambda i: (0, 0)),
                  pl.BlockSpec((1, Hd), lambda i: (0, 0))],
        out_specs=pl.BlockSpec((rt, Hd), lambda i: (i, 0)),
        out_shape=jax.ShapeDtypeStruct((2 * n, Hd), F32),
        sem=("parallel",), name="hyena_filter_trunk")(z2, w1p, b1.reshape(1, Hd), freq, w2, b2.reshape(1, Hd))
    tc = LANES
    t2 = jnp.broadcast_to(fold(t), (2 * n, tc))
    deltas = jnp.abs(jnp.linspace(HY_MIN_DECAY, HY_MAX_DECAY, OC, dtype=F32))[None, :]
    nc = OC // tc
    return _call(
        functools.partial(_filt_main_body, n=n), grid=(nc,),
        in_specs=[pl.BlockSpec((2 * n, Hd), lambda c: (0, 0)),
                  pl.BlockSpec((Hd, tc), lambda c: (0, c)),
                  pl.BlockSpec((Hd, tc), lambda c: (0, nc + c)),
                  pl.BlockSpec((2 * n, tc), lambda c: (0, 0)),
                  pl.BlockSpec((1, tc), lambda c: (0, c))],
        out_specs=pl.BlockSpec((2 * n, tc), lambda c: (0, c)),
        out_shape=jax.ShapeDtypeStruct((2 * n, OC), F32),
        sem=("parallel",), name="hyena_filter")(a2, w3, w3, t2, deltas)


def _embed(re, im):
    return np.block([[re, -im], [im, re]])


@functools.lru_cache(maxsize=None)
def _fft_tables(N1, N2):
    N = N1 * N2
    S1 = N1 // 2
    i1 = np.arange(N1)
    ang = -2.0 * np.pi * ((i1[:, None] * i1[None, :]) % N1) / N1
    fr, fi = np.cos(ang), np.sin(ang)
    f1_pair = _embed(fr[:, :S1], fi[:, :S1])
    f1_real = np.concatenate([fr, fi], axis=0)
    i2 = np.arange(N2)
    fidx = i1[:, None, None] + N1 * i2[None, :, None]
    ang = -2.0 * np.pi * ((fidx * i2[None, None, :]) % N) / N
    mr, mi = np.cos(ang), np.sin(ang)
    m_fwd = np.stack([_embed(mr[a], mi[a]) for a in range(N1)])
    m_inv = np.stack([_embed(mr[a].T, -mi[a].T) for a in range(N1)])
    ang = 2.0 * np.pi * ((i1[:S1, None] * i1[None, :]) % N1) / N1
    f1_inv = _embed(np.cos(ang) / N, np.sin(ang) / N)
    cvt = lambda a: jnp.asarray(a, dtype=BF16)
    return cvt(f1_pair), cvt(f1_real), cvt(m_fwd), cvt(m_inv), cvt(f1_inv)


def _lmat_body(f_ref, x_ref, o_ref):
    o_ref[0] = jnp.dot(f_ref[...], x_ref[0].astype(BF16), preferred_element_type=F32).astype(o_ref.dtype)


def _left_matmul(fm, x, out_dtype, name):
    P, K, W = x.shape
    R = fm.shape[0]
    tw = _tile(W, (4096, 2048, 1024, 512, 256, 128))
    return _call(
        _lmat_body, grid=(P, W // tw),
        in_specs=[pl.BlockSpec((R, K), lambda p, j: (0, 0)),
                  pl.BlockSpec((1, K, tw), lambda p, j: (p, 0, j))],
        out_specs=pl.BlockSpec((1, R, tw), lambda p, j: (p, 0, j)),
        out_shape=jax.ShapeDtypeStruct((P, R, W), out_dtype),
        sem=("parallel", "parallel"), name=name)(fm, x)


def _cmul(xr, xi, kr, ki):
    return xr * kr - xi * ki, xr * ki + xi * kr


def _fftmid_body(a_ref, m_ref, mi_ref, k_ref, o_ref, *, FB, N2):
    for t in range(FB):
        a = a_ref[0, :, t].reshape(2 * N2, a_ref.shape[-1])
        x = jnp.dot(m_ref[t], a, preferred_element_type=F32)
        yr, yi = _cmul(x[:N2], x[N2:], k_ref[0, t], k_ref[1, t])
        y = jnp.concatenate([yr, yi], axis=0).astype(BF16)
        g = jnp.dot(mi_ref[t], y, preferred_element_type=F32)
        o_ref[0, :, t] = g.reshape(2, N2, g.shape[-1]).astype(o_ref.dtype)


def _fftfwd_body(a_ref, m_ref, o_ref, *, FB, N2):
    for t in range(FB):
        a = a_ref[:, t].reshape(2 * N2, a_ref.shape[-1])
        x = jnp.dot(m_ref[t], a, preferred_element_type=F32)
        o_ref[:, t] = x.reshape(2, N2, x.shape[-1])


SUB_BLOCK = 16


def _dft1_body(f_ref, x_ref, o_ref):
    xt = pltpu.einshape("ksc->skc", x_ref[0])
    r = jnp.stack([jnp.dot(f_ref[...], xt[k].astype(BF16), preferred_element_type=F32)
                   for k in range(xt.shape[0])], axis=0)
    o_ref[0] = pltpu.einshape("skc->ksc", r).astype(o_ref.dtype)


def _outer_dft(fm, x4, out_dtype, name):
    P, K, N2, C = x4.shape
    R = fm.shape[0]
    tc = _tile(C, (512, 256, 128))
    return _call(
        _dft1_body, grid=(P, N2 // SUB_BLOCK, C // tc),
        in_specs=[pl.BlockSpec((R, K), lambda p, s, c: (0, 0)),
                  pl.BlockSpec((1, K, SUB_BLOCK, tc), lambda p, s, c: (p, 0, s, c))],
        out_specs=pl.BlockSpec((1, R, SUB_BLOCK, tc), lambda p, s, c: (p, 0, s, c)),
        out_shape=jax.ShapeDtypeStruct((P, R, N2, C), out_dtype),
        sem=("parallel", "parallel", "parallel"), name=name)(fm, x4)


def _idft1_gate_body(f_ref, g_ref, z_ref, x1_ref, sk_ref, o_ref):
    gt = pltpu.einshape("ksc->skc", g_ref[0])
    y = jnp.stack([jnp.dot(f_ref[...], gt[k], preferred_element_type=F32) for k in range(gt.shape[0])], axis=0)
    y = pltpu.einshape("skc->ksc", y)
    o_ref[0] = (x1_ref[0] * (y + sk_ref[...] * z_ref[0])).astype(o_ref.dtype)


def _filter_spectrum_2stage(k, N1, N2):
    N, OC = k.shape
    _, f1_real, m_fwd, _, _ = _fft_tables(N1, N2)
    a = _outer_dft(f1_real, k.reshape(1, N1, N2, OC), BF16, "filter_dft1").reshape(2, N1, N2, OC)
    FB = _tile(N1, (8, 4, 2, 1))
    tc = _tile(OC, (512, 256, 128))
    return _call(
        functools.partial(_fftfwd_body, FB=FB, N2=N2), grid=(N1 // FB, OC // tc),
        in_specs=[pl.BlockSpec((2, FB, N2, tc), lambda f, c: (0, f, 0, c)),
                  pl.BlockSpec((FB, 2 * N2, 2 * N2), lambda f, c: (f, 0, 0))],
        out_specs=pl.BlockSpec((2, FB, N2, tc), lambda f, c: (0, f, 0, c)),
        out_shape=jax.ShapeDtypeStruct((2, N1, N2, OC), F32),
        sem=("parallel", "parallel"), name="filter_dft2")(a, m_fwd)


def _long_conv_gate_2stage(z, x1, skip, kf, order, out_dtype, N1, N2):
    B, n, C = z.shape
    assert B % 2 == 0
    P, S1 = B // 2, N1 // 2
    f1_pair, _, m_fwd, m_inv, f1_inv = _fft_tables(N1, N2)
    z4 = z.reshape(P, 2 * S1, N2, C)
    a = _outer_dft(f1_pair, z4, BF16, "conv_dft1").reshape(P, 2, N1, N2, C)
    FB = _tile(N1, (8, 4, 2, 1))
    tc = _tile(C, (512, 256, 128))
    oc = order * (C // tc)
    g = _call(
        functools.partial(_fftmid_body, FB=FB, N2=N2), grid=(N1 // FB, C // tc, P),
        in_specs=[pl.BlockSpec((1, 2, FB, N2, tc), lambda f, c, p: (p, 0, f, 0, c)),
                  pl.BlockSpec((FB, 2 * N2, 2 * N2), lambda f, c, p: (f, 0, 0)),
                  pl.BlockSpec((FB, 2 * N2, 2 * N2), lambda f, c, p: (f, 0, 0)),
                  pl.BlockSpec((2, FB, N2, tc), lambda f, c, p: (0, f, 0, oc + c))],
        out_specs=pl.BlockSpec((1, 2, FB, N2, tc), lambda f, c, p: (p, 0, f, 0, c)),
        out_shape=jax.ShapeDtypeStruct((P, 2, N1, N2, C), BF16),
        sem=("parallel", "parallel", "arbitrary"), name="conv_dft2_mul_idft2")(a, m_fwd, m_inv, kf)
    blk = lambda rows: pl.BlockSpec((1, rows, SUB_BLOCK, tc), lambda p, s, c: (p, 0, s, c))
    out = _call(
        _idft1_gate_body, grid=(P, N2 // SUB_BLOCK, C // tc),
        in_specs=[pl.BlockSpec((2 * S1, 2 * N1), lambda p, s, c: (0, 0)),
                  blk(2 * N1), blk(2 * S1), blk(2 * S1),
                  pl.BlockSpec((1, tc), lambda p, s, c: (0, c))],
        out_specs=blk(2 * S1),
        out_shape=jax.ShapeDtypeStruct((P, 2 * S1, N2, C), out_dtype),
        sem=("parallel", "parallel", "parallel"), name="conv_idft1_gate")(
            f1_inv, g.reshape(P, 2 * N1, N2, C), z4, x1.reshape(P, 2 * S1, N2, C), skip[None, :])
    return out.reshape(B, n, C)


@functools.lru_cache(maxsize=None)
def _dft_tables(n):
    N = 2 * n
    f = np.arange(N)
    ang = -2.0 * np.pi * ((f[:, None] * f[None, :]) % N) / N
    fr, fi = np.cos(ang), np.sin(ang)
    fwd_full = np.concatenate([fr, fi], axis=0)
    fwd_half = fwd_full[:, :n]
    inv = np.concatenate([fr[:n, :], fi[:n, :]], axis=1) / N
    cvt = lambda a: jnp.asarray(a, dtype=BF16)
    return cvt(fwd_full), cvt(fwd_half), cvt(inv)


def _dftconv_body(z_ref, x1_ref, sk_ref, f_ref, fi_ref, k_ref, o_ref, *, N):
    z = z_ref[0]
    x = jnp.dot(f_ref[...], z.astype(BF16), preferred_element_type=F32)
    yr, yi = _cmul(x[:N], x[N:], k_ref[0], k_ref[1])
    y = jnp.concatenate([yr, yi], axis=0).astype(BF16)
    y = jnp.dot(fi_ref[...], y, preferred_element_type=F32)
    o_ref[0] = (x1_ref[0] * (y + sk_ref[...] * z)).astype(o_ref.dtype)


def _long_conv_gate_dense(z, x1, skip, kf, order, out_dtype):
    B, n, C = z.shape
    N = 2 * n
    _, fwd_half, inv = _dft_tables(n)
    tc = _tile(C, (256, 128))
    oc = order * (C // tc)
    return _call(
        functools.partial(_dftconv_body, N=N), grid=(C // tc, B),
        in_specs=[pl.BlockSpec((1, n, tc), lambda c, b: (b, 0, c)),
                  pl.BlockSpec((1, n, tc), lambda c, b: (b, 0, c)),
                  pl.BlockSpec((1, tc), lambda c, b: (0, c)),
                  pl.BlockSpec((2 * N, n), lambda c, b: (0, 0)),
                  pl.BlockSpec((n, 2 * N), lambda c, b: (0, 0)),
                  pl.BlockSpec((2, N, tc), lambda c, b: (0, 0, oc + c))],
        out_specs=pl.BlockSpec((1, n, tc), lambda c, b: (b, 0, c)),
        out_shape=jax.ShapeDtypeStruct((B, n, C), out_dtype),
        sem=("parallel", "arbitrary"), name="conv_dense_dft")(z, x1, skip[None, :], fwd_half, inv, kf)


def _hyena_mixer(v, x1, x2, w1, b1, freq, w2, b2, w3, skip):
    B, n, _ = v.shape
    k = _hyena_filters(n, w1, b1, freq, w2, b2, w3)
    N = 2 * n
    if N % FFT_N2 == 0 and (N // FFT_N2) >= 16:
        N1 = N // FFT_N2
        kf = _filter_spectrum_2stage(k, N1, FFT_N2)
        z = _long_conv_gate_2stage(v, x1, skip[0], kf, 0, F32, N1, FFT_N2)
        return _long_conv_gate_2stage(z, x2, skip[1], kf, 1, BF16, N1, FFT_N2)
    fwd_full, _, _ = _dft_tables(n)
    kf = _left_matmul(fwd_full, k[None], F32, "filter_dense_dft").reshape(2, N, k.shape[1])
    z = _long_conv_gate_dense(v, x1, skip[0], kf, 0, F32)
    return _long_conv_gate_dense(z, x2, skip[1], kf, 1, BF16)


def _nt(a, b):
    return lax.dot_general(a, b, (((1,), (1,)), ((), ())), preferred_element_type=F32)


def _na_body(q_ref, k_ref, v_ref, kc_ref, vc_ref, b_ref, o_ref, *, R, KR, rows, HPS, scale):
    j = pl.program_id(2)
    start = pl.multiple_of(jnp.clip(j * R - NA_WIN_R // 2, 0, rows - KR) * GRID_W, GRID_W)
    for h in range(HPS):
        hs = slice(h * HEAD_DIM, (h + 1) * HEAD_DIM)
        q = q_ref[0, :, hs]
        kw = k_ref[0, pl.ds(start, KR * GRID_W), hs]
        vw = v_ref[0, pl.ds(start, KR * GRID_W), hs]
        s = _nt(q, kw) * scale + b_ref[h, 0]
        sc = _nt(q, kc_ref[0, :, hs]) * scale
        m = jnp.maximum(jnp.max(s, axis=-1, keepdims=True), jnp.max(sc, axis=-1, keepdims=True))
        p = jnp.exp(s - m)
        pc = jnp.exp(sc - m)
        l = jnp.sum(p, axis=-1, keepdims=True) + jnp.sum(pc, axis=-1, keepdims=True)
        o = jnp.dot(p.astype(BF16), vw, preferred_element_type=F32)
        o = o + jnp.dot(pc.astype(BF16), vc_ref[0, :, hs], preferred_element_type=F32)
        o_ref[0, :, hs] = (o / l).astype(o_ref.dtype)


def _na_geometry(S):
    rows = S // GRID_W
    kr = min(NA_WIN_R, rows)
    R = min(8, rows)
    KR = min(rows, R + kr)
    nb = rows // R
    assert rows % R == 0
    types = sorted({0, min(1, nb - 1), nb - 1})
    if nb > 3:
        offs = {int(np.clip(j * R - NA_WIN_R // 2, 0, rows - KR)) - j * R for j in range(1, nb - 1)}
        assert len(offs) == 1
    return rows, kr, R, KR, nb, types


def _nabias_body(rpb_ref, o_ref, tw_ref, *, plan, R, KR):
    W = GRID_W
    nd_r, nd_c = 2 * NA_WIN_R - 1, 2 * NA_WIN_C - 1
    base = pl.program_id(0) * (nd_r * nd_c)
    qc = lax.broadcasted_iota(jnp.int32, (W, 2 * W), 0)
    lane = lax.broadcasted_iota(jnp.int32, (W, 2 * W), 1)
    kc = lane % W
    cs = jnp.clip(qc - NA_WIN_C // 2, 0, W - NA_WIN_C)
    col_ok = jnp.logical_and(kc >= cs, kc < cs + NA_WIN_C)
    dcm = kc - qc + (NA_WIN_C - 1)
    neg = jnp.full((W, 2 * W), NEG_INF, F32)
    for dr in range(nd_r):
        acc = neg
        for dc in range(nd_c):
            acc = jnp.where(dcm == dc, rpb_ref[base + dr * nd_c + dc], acc)
        tw_ref[dr] = jnp.where(col_ok, acc, NEG_INF)
    left = lane < W
    for t, per_q in enumerate(plan):
        for qr in range(R):
            for kp in range(KR // 2):
                d0, d1 = per_q[qr][kp]
                a = neg if d0 is None else tw_ref[d0]
                b = neg if d1 is None else tw_ref[d1]
                blk = neg if (d0 is None and d1 is None) else jnp.where(left, a, b)
                o_ref[0, t, qr * W:(qr + 1) * W, kp * 2 * W:(kp + 1) * 2 * W] = blk


def _na_bias_tables(rpb, S):
    rows, kr, R, KR, nb, types = _na_geometry(S)
    assert 2 * GRID_W == LANES and KR % 2 == 0
    plan = []
    for jt in types:
        start = int(np.clip(jt * R - NA_WIN_R // 2, 0, rows - KR))
        per_q = []
        for q in range(R):
            qra = jt * R + q
            ws = int(np.clip(qra - kr // 2, 0, rows - kr))
            d = [(start + k) - qra + (NA_WIN_R - 1) if ws <= start + k < ws + kr else None for k in range(KR)]
            per_q.append([(d[2 * p], d[2 * p + 1]) for p in range(KR // 2)])
        plan.append(per_q)
    L, H, nd_r, nd_c = rpb.shape
    T, QB, KB = len(types), R * GRID_W, KR * GRID_W
    return _call(
        functools.partial(_nabias_body, plan=plan, R=R, KR=KR), grid=(L * H,),
        in_specs=[pl.BlockSpec(memory_space=pltpu.SMEM)],
        out_specs=pl.BlockSpec((1, T, QB, KB), lambda i: (i, 0, 0, 0)),
        out_shape=jax.ShapeDtypeStruct((L * H, T, QB, KB), F32),
        scratch=[pltpu.VMEM((nd_r, GRID_W, 2 * GRID_W), F32)],
        sem=("parallel",), name="na_bias_table")(rpb.reshape(-1).astype(F32))


def _na_attention(qkv, qkv_c, bias, l, offs, H):
    B, S, _ = qkv.shape
    CTX = qkv_c.shape[1]
    rows, kr, R, KR, nb, types = _na_geometry(S)
    T = len(types)
    QB, KB = R * GRID_W, KR * GRID_W
    HPS = 2 if H % 2 == 0 else 1
    HW = HPS * HEAD_DIM
    assert all(offs[n] % HW == 0 for n in ("na_k", "na_v", "na_q"))
    ok_, ov_, oq_ = (offs[n] // HW for n in ("na_k", "na_v", "na_q"))

    def btype(j):
        if T == nb:
            return j
        return jnp.where(j == 0, 0, jnp.where(j == nb - 1, T - 1, 1))

    return _call(
        functools.partial(_na_body, R=R, KR=KR, rows=rows, HPS=HPS, scale=HEAD_DIM ** -0.5),
        grid=(B, H // HPS, nb),
        in_specs=[pl.BlockSpec((1, QB, HW), lambda b, h, j: (b, j, oq_ + h)),
                  pl.BlockSpec((1, S, HW), lambda b, h, j: (b, 0, ok_ + h)),
                  pl.BlockSpec((1, S, HW), lambda b, h, j: (b, 0, ov_ + h)),
                  pl.BlockSpec((1, CTX, HW), lambda b, h, j: (b, 0, ok_ + h)),
                  pl.BlockSpec((1, CTX, HW), lambda b, h, j: (b, 0, ov_ + h)),
                  pl.BlockSpec((HPS, 1, QB, KB), lambda b, h, j: (l * (H // HPS) + h, btype(j), 0, 0))],
        out_specs=pl.BlockSpec((1, QB, HW), lambda b, h, j: (b, j, h)),
        out_shape=jax.ShapeDtypeStruct((B, S, H * HEAD_DIM), BF16),
        sem=("parallel", "parallel", "arbitrary"), name="na_attention")(qkv, qkv, qkv, qkv_c, qkv_c, bias)


def _stack_heads(q2, G):
    return jnp.concatenate([q2[:, g * HEAD_DIM:(g + 1) * HEAD_DIM] for g in range(G)], axis=0)


def _unstack_heads(o, G, n):
    return jnp.concatenate([o[g * n:(g + 1) * n] for g in range(G)], axis=1)


def _sink_column(sink_ref, h0, G, n):
    return jnp.concatenate([jnp.full((n, 1), sink_ref[h0 + g], F32) for g in range(G)], axis=0)


def _swa_body(sink_ref, q_ref, k_ref, v_ref, kc_ref, vc_ref, o_ref, *, QB, KB, S, G, HPS, scale):
    hb, j = pl.program_id(1), pl.program_id(2)
    start = pl.multiple_of(jnp.clip(j * QB - GQA_WINDOW, 0, S - KB), LANES)
    qpos = j * QB + lax.broadcasted_iota(jnp.int32, (QB, KB), 0)
    kpos = start + lax.broadcasted_iota(jnp.int32, (QB, KB), 1)
    mask = jnp.where(jnp.abs(qpos - kpos) <= GQA_WINDOW, 0.0, NEG_INF)
    mask = jnp.concatenate([mask] * G, axis=0)
    for h in range(HPS):
        hs = slice(h * HEAD_DIM, (h + 1) * HEAD_DIM)
        qs = slice(h * G * HEAD_DIM, (h + 1) * G * HEAD_DIM)
        q = _stack_heads(q_ref[0, :, qs], G)
        kw = k_ref[0, pl.ds(start, KB), hs]
        vw = v_ref[0, pl.ds(start, KB), hs]
        s = _nt(q, kw) * scale + mask
        sc = _nt(q, kc_ref[0, :, hs]) * scale
        sk = _sink_column(sink_ref, (hb * HPS + h) * G, G, QB)
        m = jnp.maximum(jnp.maximum(jnp.max(s, axis=-1, keepdims=True), jnp.max(sc, axis=-1, keepdims=True)), sk)
        p = jnp.exp(s - m)
        pc = jnp.exp(sc - m)
        l = jnp.sum(p, axis=-1, keepdims=True) + jnp.sum(pc, axis=-1, keepdims=True) + jnp.exp(sk - m)
        o = jnp.dot(p.astype(BF16), vw, preferred_element_type=F32)
        o = o + jnp.dot(pc.astype(BF16), vc_ref[0, :, hs], preferred_element_type=F32)
        o_ref[0, :, qs] = _unstack_heads(o / l, G, QB).astype(o_ref.dtype)


def _swa_attention(qkv, qkv_c, sink, offs, KVH):
    B, S, _ = qkv.shape
    CTX = qkv_c.shape[1]
    G = GQA_GROUP
    QB = _tile(S, (256, 128))
    KB = min(S, QB + 2 * GQA_WINDOW)
    HPS = 2 if KVH % 2 == 0 else 1
    HW = HPS * HEAD_DIM
    assert offs["sw_k"] % HW == 0 and offs["sw_v"] % HW == 0 and offs["sw_q"] % (G * HW) == 0
    ok_, ov_, oq_ = offs["sw_k"] // HW, offs["sw_v"] // HW, offs["sw_q"] // (G * HW)
    return _call(
        functools.partial(_swa_body, QB=QB, KB=KB, S=S, G=G, HPS=HPS, scale=HEAD_DIM ** -0.5),
        grid=(B, KVH // HPS, S // QB),
        in_specs=[pl.BlockSpec(memory_space=pltpu.SMEM),
                  pl.BlockSpec((1, QB, G * HW), lambda b, h, j: (b, j, oq_ + h)),
                  pl.BlockSpec((1, S, HW), lambda b, h, j: (b, 0, ok_ + h)),
                  pl.BlockSpec((1, S, HW), lambda b, h, j: (b, 0, ov_ + h)),
                  pl.BlockSpec((1, CTX, HW), lambda b, h, j: (b, 0, ok_ + h)),
                  pl.BlockSpec((1, CTX, HW), lambda b, h, j: (b, 0, ov_ + h))],
        out_specs=pl.BlockSpec((1, QB, G * HW), lambda b, h, j: (b, j, h)),
        out_shape=jax.ShapeDtypeStruct((B, S, KVH * G * HEAD_DIM), BF16),
        sem=("parallel", "parallel", "arbitrary"), name="swa_attention")(sink, qkv, qkv, qkv, qkv_c, qkv_c)


def _cattn_body(sink_ref, q_ref, k_ref, v_ref, o_ref, *, G, use_sink, scale):
    h = pl.program_id(1)
    n = q_ref.shape[1]
    q = _stack_heads(q_ref[0], G)
    s = _nt(q, k_ref[0]) * scale
    m = jnp.max(s, axis=-1, keepdims=True)
    if use_sink:
        sk = _sink_column(sink_ref, h * G, G, n)
        m = jnp.maximum(m, sk)
    p = jnp.exp(s - m)
    l = jnp.sum(p, axis=-1, keepdims=True)
    if use_sink:
        l = l + jnp.exp(sk - m)
    o = jnp.dot(p.astype(BF16), v_ref[0], preferred_element_type=F32)
    o_ref[0] = _unstack_heads(o / l, G, n).astype(o_ref.dtype)


def _ctx_attention(qkv_c, sink, oq, ok, ov, KVH, G, use_sink):
    B, n, _ = qkv_c.shape
    oq_, ok_, ov_ = oq // (G * HEAD_DIM), ok // HEAD_DIM, ov // HEAD_DIM
    return _call(
        functools.partial(_cattn_body, G=G, use_sink=use_sink, scale=HEAD_DIM ** -0.5),
        grid=(B, KVH),
        in_specs=[pl.BlockSpec(memory_space=pltpu.SMEM),
                  pl.BlockSpec((1, n, G * HEAD_DIM), lambda b, h: (b, 0, oq_ + h)),
                  pl.BlockSpec((1, n, HEAD_DIM), lambda b, h: (b, 0, ok_ + h)),
                  pl.BlockSpec((1, n, HEAD_DIM), lambda b, h: (b, 0, ov_ + h))],
        out_specs=pl.BlockSpec((1, n, G * HEAD_DIM), lambda b, h: (b, 0, h)),
        out_shape=jax.ShapeDtypeStruct((B, n, KVH * G * HEAD_DIM), BF16),
        sem=("parallel", "parallel"), name="ctx_attention")(sink, qkv_c, qkv_c, qkv_c)


def _merge_body(x_ref, g_ref, sh_ref, sc_ref, yh, yn, ys, wgh, wgn, wgs, wh, wn, ws, o_ref, h_ref, *, eps):
    @pl.when(pl.program_id(2) == 0)
    def _():
        h_ref[...] = _norm_modulate(x_ref[0], g_ref, sh_ref, sc_ref, eps).astype(BF16)

    h = h_ref[...]
    gate = lambda wg: _sigmoid(jnp.dot(h, wg[...], preferred_element_type=F32))
    m = gate(wgh) * jnp.dot(yh[0], wh[...], preferred_element_type=F32)
    m = m + gate(wgn) * jnp.dot(yn[0], wn[...], preferred_element_type=F32)
    m = m + gate(wgs) * jnp.dot(ys[0], ws[...], preferred_element_type=F32)
    o_ref[0] = m.astype(o_ref.dtype)


def _merge_branches(x, gain, shift, scale, w_in, gate_start, y_hy, y_na, y_sw, w_br, l):
    B, S, D = x.shape
    widths = (y_hy.shape[2], y_na.shape[2], y_sw.shape[2])
    starts = (0, widths[0], widths[0] + widths[1])
    assert all(s % w == 0 for s, w in zip(starts, widths))
    tm = _tile(S, (512, 256, 128))
    tn = _seg_tile([(gate_start, D)], (512, 256, 128))
    nj, gj = D // tn, gate_start // tn
    yspec = lambda y: pl.BlockSpec((1, tm, y.shape[2]), lambda b, i, j: (b, i, 0))
    gspec = lambda k: pl.BlockSpec((None, D, tn), lambda b, i, j: (l, 0, gj + k * nj + j))
    wspec = lambda k: pl.BlockSpec((None, widths[k], tn), lambda b, i, j: (l, starts[k] // widths[k], j))
    return _call(
        functools.partial(_merge_body, eps=NORM_EPS), grid=(B, S // tm, nj),
        in_specs=[pl.BlockSpec((1, tm, D), lambda b, i, j: (b, i, 0)),
                  pl.BlockSpec((1, D), lambda b, i, j: (0, 0)),
                  pl.BlockSpec((1, 1, D), lambda b, i, j: (b, 0, 0)),
                  pl.BlockSpec((1, 1, D), lambda b, i, j: (b, 0, 0)),
                  yspec(y_hy), yspec(y_na), yspec(y_sw), gspec(0), gspec(1), gspec(2),
                  wspec(0), wspec(1), wspec(2)],
        out_specs=pl.BlockSpec((1, tm, tn), lambda b, i, j: (b, i, j)),
        out_shape=jax.ShapeDtypeStruct((B, S, D), BF16),
        scratch=[pltpu.VMEM((tm, D), BF16)],
        sem=("parallel", "parallel", "arbitrary"), name="gated_merge")(
            x, gain.reshape(1, D), shift, scale, y_hy, y_na, y_sw, w_in, w_in, w_in, w_br, w_br, w_br)


def _mmres_body(a_ref, w_ref, x_ref, g_ref, mg_ref, o_ref, *, nk, eps):
    k = pl.program_id(2)
    part = lambda: jnp.dot(a_ref[0], w_ref[...], preferred_element_type=F32)

    def finish():
        y = o_ref[0]
        yn = y * lax.rsqrt(jnp.mean(y * y, axis=-1, keepdims=True) + eps) * g_ref[...]
        o_ref[0] = x_ref[0] + mg_ref[0] * yn

    if nk == 1:
        o_ref[0] = part()
        finish()
        return

    @pl.when(k == 0)
    def _():
        o_ref[0] = part()

    @pl.when(k > 0)
    def _():
        o_ref[0] += part()

    @pl.when(k == nk - 1)
    def _():
        finish()


def _matmul_norm_residual(a, w, l, x, gain, mgate, name):
    B, S, K = a.shape
    D = w.shape[2]
    if K <= 2048:
        tm, tk = _tile(S, (512, 256, 128)), K
    else:
        tm = _tile(S, (1024, 512, 256, 128))
        tk = max(t for t in range(LANES, 1537, LANES) if K % t == 0)
    nk = K // tk
    est = 2 * (tm * tk * 2 + tk * D * 2 + 2 * tm * D * 4)
    return _call(
        functools.partial(_mmres_body, nk=nk, eps=NORM_EPS), grid=(B, S // tm, nk),
        vmem=VMEM_LIMIT_LARGE if est > VMEM_LIMIT - VMEM_TEMP_RESERVE else VMEM_LIMIT,
        in_specs=[pl.BlockSpec((1, tm, tk), lambda b, i, k: (b, i, k)),
                  pl.BlockSpec((None, tk, D), lambda b, i, k: (l, k, 0)),
                  pl.BlockSpec((1, tm, D), lambda b, i, k: (b, i, 0)),
                  pl.BlockSpec((1, D), lambda b, i, k: (0, 0)),
                  pl.BlockSpec((1, 1, D), lambda b, i, k: (b, 0, 0))],
        out_specs=pl.BlockSpec((1, tm, D), lambda b, i, k: (b, i, 0)),
        out_shape=jax.ShapeDtypeStruct((B, S, D), F32),
        sem=("parallel", "parallel", "arbitrary"), name=name)(a, w, x, gain.reshape(1, D), mgate)


def _rope_tables(n):
    t = jnp.arange(n)
    row = (t // GRID_W).astype(F32)
    col = (t % GRID_W).astype(F32)
    per_axis = HEAD_DIM // 2
    inv = ROPE_BASE ** (-jnp.arange(0, per_axis, 2, dtype=F32) / per_axis)
    ar, ac = row[:, None] * inv, col[:, None] * inv
    cos_t = jnp.concatenate([jnp.cos(ar), jnp.cos(ar), jnp.cos(ac), jnp.cos(ac)], axis=1)
    sin_t = jnp.concatenate([-jnp.sin(ar), jnp.sin(ar), -jnp.sin(ac), jnp.sin(ac)], axis=1)
    return cos_t, sin_t


def kernel(x, c, ctx, c_ctx, w_mod, b_mod, norm_gains, w_in, hy_conv_w, hy_conv_b, hy_w1, hy_b1, hy_freq, hy_w2, hy_b2, hy_w3, hy_skip, na_rpb, swa_sink, w_branch, w_out, ffn_w_up, ffn_conv_w, ffn_conv_b, ffn_w_down):
    B, S, D = x.shape
    L = w_mod.shape[0]
    C = hy_skip.shape[-1]
    H_na = na_rpb.shape[1]
    H_q = swa_sink.shape[1]
    KVH = H_q // GQA_GROUP
    NA_W, QW, KVW = H_na * HEAD_DIM, H_q * HEAD_DIM, KVH * HEAD_DIM
    KV_COLS = 2 * NA_W + 2 * KVW
    qkv_segs = [(0, KV_COLS), (KV_COLS + 3 * C, NA_W + QW)]
    gate_start = KV_COLS + 3 * C + NA_W + QW
    offs = {"na_k": 0, "na_v": NA_W, "sw_k": 2 * NA_W, "sw_v": 2 * NA_W + KVW,
            "na_q": KV_COLS, "sw_q": KV_COLS + NA_W}
    rope_chunks = (list(range(offs["sw_k"] // LANES, (offs["sw_k"] + KVW) // LANES))
                   + list(range(offs["sw_q"] // LANES, (offs["sw_q"] + QW) // LANES)))
    cos_t, sin_t = _rope_tables(S)

    w_in_b, w_out_b = w_in.astype(BF16), w_out.astype(BF16)
    w_br_b, w_up_b, w_dn_b = w_branch.astype(BF16), ffn_w_up.astype(BF16), ffn_w_down.astype(BF16)
    na_bias = _na_bias_tables(na_rpb, S)

    R = -(-(B + 1) // 8) * 8
    cc = jnp.concatenate([c, c_ctx[None, :], jnp.zeros((R - B - 1, D), F32)], axis=0)
    mods = _modulation(cc, w_mod, b_mod)

    xc = ctx
    for l in range(L):
        mod = [mods[l, :B, k * D:(k + 1) * D].reshape(B, 1, D) for k in range(6)]
        mod_c = [jnp.broadcast_to(mods[l, B, k * D:(k + 1) * D].reshape(1, 1, D), (B, 1, D)) for k in range(6)]
        g = norm_gains[l]
        hy_p = (hy_w1[l], hy_b1[l], hy_freq[l], hy_w2[l], hy_b2[l], hy_w3[l], hy_skip[l])

        qkv_c, vc, x1c, x2c = _in_proj(xc, g[0], mod_c[0], mod_c[1], w_in_b, l, qkv_segs, KV_COLS, C,
                                       hy_conv_w[l], hy_conv_b[l], None, "ctx_in_proj")
        qkv, v, x1, x2 = _in_proj(x, g[0], mod[0], mod[1], w_in_b, l, qkv_segs, KV_COLS, C,
                                  hy_conv_w[l], hy_conv_b[l], (cos_t, sin_t, rope_chunks), "in_proj")
        y_hy = _hyena_mixer(v, x1, x2, *hy_p)
        y_na = _na_attention(qkv, qkv_c, na_bias, l, offs, H_na)
        y_sw = _swa_attention(qkv, qkv_c, swa_sink[l], offs, KVH)
        m = _merge_branches(x, g[0], mod[0], mod[1], w_in_b, gate_start, y_hy, y_na, y_sw, w_br_b, l)
        x = _matmul_norm_residual(m, w_out_b, l, x, g[1], mod[2], "out_proj_residual")
        gl = _ffn_up_glu(x, g[2], mod[3], mod[4], w_up_b, l, ffn_conv_w[l], ffn_conv_b[l])
        x = _matmul_norm_residual(gl, w_dn_b, l, x, g[3], mod[5], "ffn_down_residual")

        if l < L - 1:
            yc_hy = _hyena_mixer(vc, x1c, x2c, *hy_p)
            yc_na = _ctx_attention(qkv_c, swa_sink[l], offs["na_q"], offs["na_k"], offs["na_v"], H_na, 1, False)
            yc_sw = _ctx_attention(qkv_c, swa_sink[l], offs["sw_q"], offs["sw_k"], offs["sw_v"], KVH, GQA_GROUP, True)
            flat = lambda t: t.reshape(1, -1, t.shape[-1])
            mc1 = [m_[:1] for m_ in mod_c]
            mc = _merge_branches(flat(xc), g[0], mc1[0], mc1[1], w_in_b, gate_start,
                                 flat(yc_hy), flat(yc_na), flat(yc_sw), w_br_b, l)
            xc = _matmul_norm_residual(mc, w_out_b, l, flat(xc), g[1], mc1[2], "ctx_out_proj_residual").reshape(ctx.shape)
            gl_c = _ffn_up_glu(xc, g[2], mod_c[3], mod_c[4], w_up_b, l, ffn_conv_w[l], ffn_conv_b[l])
            xc = _matmul_norm_residual(flat(gl_c), w_dn_b, l, flat(xc), g[3], mc1[5], "ctx_ffn_down_residual").reshape(ctx.shape)
    return x
```

```python
import functools
import math

import numpy as np
import jax
import jax.numpy as jnp
from jax import lax
from jax.experimental import pallas as pl
from jax.experimental.pallas import tpu as pltpu

F32 = jnp.float32
BF16 = jnp.bfloat16
NEG_INF = -1e30

GRID_W = 64
HEAD_DIM = 128
HY_EMB = 33
HY_BANDS = (HY_EMB - 1) // 2
HY_FAST_DECAY = 0.3
HY_SLOW_DECAY = 1.5
HY_DECAY_TARGET = 1e-2
HY_MAX_DECAY = math.log(HY_DECAY_TARGET) / HY_FAST_DECAY
HY_MIN_DECAY = math.log(HY_DECAY_TARGET) / HY_SLOW_DECAY
NA_WIN_R = 8
NA_WIN_C = 16
GQA_GROUP = 2
GQA_WINDOW = 128
ROPE_BASE = 10000.0
NORM_EPS = 1e-6

LANES = 128
FFT_N2 = 128
VMEM_LIMIT = 56 * 1024 * 1024
VMEM_LIMIT_LARGE = 60 * 1024 * 1024
VMEM_TEMP_RESERVE = 8 * 1024 * 1024
HP = lax.Precision.HIGHEST
LOG2_E = math.log2(math.e)


def _call(body, *, grid, in_specs, out_specs, out_shape, scratch=(), sem, name, vmem=VMEM_LIMIT):
    return pl.pallas_call(
        body, grid=grid, in_specs=in_specs, out_specs=out_specs, out_shape=out_shape,
        scratch_shapes=list(scratch),
        compiler_params=pltpu.CompilerParams(dimension_semantics=sem, vmem_limit_bytes=vmem),
        name=name)


def _tile(n, cands):
    for c in cands:
        if n % c == 0:
            return c
    raise ValueError(f"no tile for {n} in {cands}")


def _sigmoid(x):
    return 1.0 / (1.0 + jnp.exp(-x))


def _mod_body(c_ref, w_ref, b_ref, o_ref):
    c = c_ref[...]
    s = (c * _sigmoid(c)).astype(BF16)
    o_ref[0] = jnp.dot(s, w_ref[0].astype(BF16), preferred_element_type=F32) + b_ref[0]


def _modulation(cc, w_mod, b_mod):
    L, D, N = w_mod.shape
    R = cc.shape[0]
    tn = _tile(N, (1024, 512, 256, 128))
    return _call(
        _mod_body, grid=(L, N // tn),
        in_specs=[pl.BlockSpec((R, D), lambda l, j: (0, 0)),
                  pl.BlockSpec((1, D, tn), lambda l, j: (l, 0, j)),
                  pl.BlockSpec((1, 1, tn), lambda l, j: (l, 0, j))],
        out_specs=pl.BlockSpec((1, R, tn), lambda l, j: (l, 0, j)),
        out_shape=jax.ShapeDtypeStruct((L, R, N), F32),
        sem=("parallel", "parallel"), name="modulation")(cc, w_mod, b_mod.reshape(L, 1, N))


def _swap32(a):
    lane = lax.broadcasted_iota(jnp.int32, a.shape, 1)
    return jnp.where((lane & 32) == 0, pltpu.roll(a, 96, 1), pltpu.roll(a, 32, 1))


def _norm_modulate(x, g_ref, sh_ref, sc_ref, eps):
    y = x * lax.rsqrt(jnp.mean(x * x, axis=-1, keepdims=True) + eps) * g_ref[...]
    return y * (1.0 + sc_ref[0]) + sh_ref[0]


HALO = 16


def _halo_prologue(h_ref, x_ref, xp_ref, xn_ref, g_ref, sh_ref, sc_ref, eps):
    i, tm = pl.program_id(1), x_ref.shape[1]
    nm = lambda x: _norm_modulate(x, g_ref, sh_ref, sc_ref, eps)
    keep_prev = (i > 0).astype(F32)
    keep_next = (i < pl.num_programs(1) - 1).astype(F32)
    h_ref[0:HALO, :] = (nm(xp_ref[0]) * keep_prev).astype(BF16)
    h_ref[HALO:HALO + tm, :] = nm(x_ref[0]).astype(BF16)
    h_ref[HALO + tm:, :] = (nm(xn_ref[0]) * keep_next).astype(BF16)


def _conv3_halo(a, cw_ref, cb_ref, tm):
    rows = a.shape[0]
    up = pltpu.roll(a, 1, 0)[HALO:HALO + tm]
    un = pltpu.roll(a, rows - 1, 0)[HALO:HALO + tm]
    return up * cw_ref[0:1, :] + a[HALO:HALO + tm] * cw_ref[1:2, :] + un * cw_ref[2:3, :] + cb_ref[...]


def _halo_specs(S, tm, D):
    hb, nh = tm // HALO, S // HALO
    return [pl.BlockSpec((1, tm, D), lambda b, i, j: (b, i, 0)),
            pl.BlockSpec((1, HALO, D), lambda b, i, j: (b, jnp.maximum(i * hb - 1, 0), 0)),
            pl.BlockSpec((1, HALO, D), lambda b, i, j: (b, jnp.minimum((i + 1) * hb, nh - 1), 0)),
            pl.BlockSpec((1, D), lambda b, i, j: (0, 0)),
            pl.BlockSpec((1, 1, D), lambda b, i, j: (b, 0, 0)),
            pl.BlockSpec((1, 1, D), lambda b, i, j: (b, 0, 0))]


def _inproj_body(*refs, nq, nc, rope_chunks, eps):
    if rope_chunks:
        x_ref, xp_ref, xn_ref, g_ref, sh_ref, sc_ref, w_ref, cw_ref, cb_ref, cos_ref, sin_ref, q_o, o0, o1, o2, h_ref = refs
    else:
        x_ref, xp_ref, xn_ref, g_ref, sh_ref, sc_ref, w_ref, cw_ref, cb_ref, q_o, o0, o1, o2, h_ref = refs
    j = pl.program_id(2)
    tm = x_ref.shape[1]

    @pl.when(j == 0)
    def _():
        _halo_prologue(h_ref, x_ref, xp_ref, xn_ref, g_ref, sh_ref, sc_ref, eps)

    qkv_dot = lambda: jnp.dot(h_ref[HALO:HALO + tm, :], w_ref[...], preferred_element_type=F32)
    nch = w_ref.shape[1] // LANES
    rope_tiles = sorted({ch // nch for ch in rope_chunks})
    for jt in rope_tiles:
        @pl.when(j == jt)
        def _(jt=jt):
            acc = qkv_dot()
            c, s = cos_ref[...], sin_ref[...]
            for k in range(nch):
                a = acc[:, k * LANES:(k + 1) * LANES]
                if jt * nch + k in rope_chunks:
                    a = a * c + _swap32(a) * s
                q_o[0, :, k * LANES:(k + 1) * LANES] = a.astype(q_o.dtype)

    @pl.when(functools.reduce(jnp.logical_and, [j != jt for jt in rope_tiles], j < nq))
    def _():
        q_o[0] = qkv_dot().astype(q_o.dtype)

    for k, o in enumerate((o0, o1, o2)):
        @pl.when(jnp.logical_and(j >= nq + k * nc, j < nq + (k + 1) * nc))
        def _(o=o):
            a = jnp.dot(h_ref[...], w_ref[...], preferred_element_type=F32)
            o[0] = _conv3_halo(a, cw_ref, cb_ref, tm)


def _colmap(segs, tn):
    bounds, o = [], 0
    for s, w in segs:
        assert s % tn == 0 and w % tn == 0
        bounds.append(((o + w) // tn, (s - o) // tn))
        o += w

    def f(j):
        r = j + bounds[-1][1]
        for hi, off in reversed(bounds[:-1]):
            r = jnp.where(j < hi, j + off, r)
        return r
    return f, o


def _seg_tile(segs, cands):
    return _tile(functools.reduce(math.gcd, [v for seg in segs for v in seg if v]), cands)


def _in_proj(x, gain, shift, scale, w_in, l, qkv_segs, hy_start, C, conv_w, conv_b, rope, name):
    B, S, D = x.shape
    tm = _tile(S, (1024, 512, 256, 128))
    segs = list(qkv_segs) + [(hy_start, 3 * C)]
    tn = _seg_tile(segs + [(0, C)], (512, 256, 128))
    cmap, n_out = _colmap(segs, tn)
    nc = C // tn
    nq = n_out // tn - 3 * nc
    in_specs = _halo_specs(S, tm, D) + [
        pl.BlockSpec((None, D, tn), lambda b, i, j: (l, 0, cmap(j))),
        pl.BlockSpec((3, tn), lambda b, i, j: (0, jnp.clip(j - nq, 0, 3 * nc - 1))),
        pl.BlockSpec((1, tn), lambda b, i, j: (0, jnp.clip(j - nq, 0, 3 * nc - 1)))]
    args = [x, x, x, gain.reshape(1, D), shift, scale, w_in, conv_w, conv_b.reshape(1, 3 * C)]
    rope_chunks = ()
    if rope is not None:
        cos_t, sin_t, rope_chunks = rope
        in_specs += [pl.BlockSpec((tm, LANES), lambda b, i, j: (i, 0)),
                     pl.BlockSpec((tm, LANES), lambda b, i, j: (i, 0))]
        args += [cos_t, sin_t]
    hy_spec = lambda k: pl.BlockSpec((1, tm, tn), lambda b, i, j: (b, i, jnp.clip(j - nq - k * nc, 0, nc - 1)))
    return _call(
        functools.partial(_inproj_body, nq=nq, nc=nc, rope_chunks=frozenset(rope_chunks), eps=NORM_EPS),
        grid=(B, S // tm, n_out // tn), in_specs=in_specs,
        out_specs=[pl.BlockSpec((1, tm, tn), lambda b, i, j: (b, i, jnp.minimum(j, nq - 1))),
                   hy_spec(0), hy_spec(1), hy_spec(2)],
        out_shape=[jax.ShapeDtypeStruct((B, S, nq * tn), BF16)] + [jax.ShapeDtypeStruct((B, S, C), F32)] * 3,
        scratch=[pltpu.VMEM((tm + 2 * HALO, D), BF16)],
        sem=("parallel", "parallel", "arbitrary"), name=name)(*args)


def _ffnup_body(x_ref, xp_ref, xn_ref, g_ref, sh_ref, sc_ref, wa_ref, wu_ref, cw_ref, cb_ref, o_ref, h_ref, *, eps):
    tm = x_ref.shape[1]

    @pl.when(pl.program_id(2) == 0)
    def _():
        _halo_prologue(h_ref, x_ref, xp_ref, xn_ref, g_ref, sh_ref, sc_ref, eps)

    a = jnp.dot(h_ref[...], wa_ref[...], preferred_element_type=F32)
    u = jnp.dot(h_ref[HALO:HALO + tm, :], wu_ref[...], preferred_element_type=F32)
    c = _conv3_halo(a, cw_ref, cb_ref, tm)
    o_ref[0] = (c * _sigmoid(c) * u).astype(o_ref.dtype)


def _ffn_up_glu(x, gain, shift, scale, w_up, l, conv_w, conv_b):
    B, S, D = x.shape
    Fd = w_up.shape[2] // 2
    tm = _tile(S, (1024, 512, 256, 128))
    tn = _tile(Fd, (512, 256, 128))
    nj = Fd // tn
    return _call(
        functools.partial(_ffnup_body, eps=NORM_EPS), grid=(B, S // tm, nj),
        in_specs=_halo_specs(S, tm, D) + [
            pl.BlockSpec((None, D, tn), lambda b, i, j: (l, 0, j)),
            pl.BlockSpec((None, D, tn), lambda b, i, j: (l, 0, nj + j)),
            pl.BlockSpec((3, tn), lambda b, i, j: (0, j)),
            pl.BlockSpec((1, tn), lambda b, i, j: (0, j))],
        out_specs=pl.BlockSpec((1, tm, tn), lambda b, i, j: (b, i, j)),
        out_shape=jax.ShapeDtypeStruct((B, S, Fd), BF16),
        scratch=[pltpu.VMEM((tm + 2 * HALO, D), BF16)],
        sem=("parallel", "parallel", "arbitrary"), name="ffn_up_glu")(
            x, x, x, gain.reshape(1, D), shift, scale, w_up, w_up, conv_w, conv_b.reshape(1, Fd))


def _filt_trunk_body(z_ref, w1_ref, b1_ref, fr_ref, w2_ref, b2_ref, o_ref):
    a = jnp.dot(z_ref[...], w1_ref[...], precision=HP, preferred_element_type=F32) + b1_ref[...]
    a = jnp.sin(fr_ref[0:1, :] * a)
    a = jnp.dot(a, w2_ref[...], precision=HP, preferred_element_type=F32) + b2_ref[...]
    o_ref[...] = jnp.sin(fr_ref[1:2, :] * a)


def _filt_main_body(a_ref, wf_ref, wb_ref, t_ref, dl_ref, o_ref, *, n):
    hf = jnp.dot(a_ref[0:n, :], wf_ref[...], precision=HP, preferred_element_type=F32)
    hb = jnp.dot(a_ref[n:2 * n, :], wb_ref[...], precision=HP, preferred_element_type=F32)
    r = lax.broadcasted_iota(jnp.int32, hb.shape, 0)
    k = jnp.concatenate([hf, jnp.where(r == 0, 0.0, hb)], axis=0) * jnp.exp(-t_ref[...] * dl_ref[...])
    o_ref[...] = k / jnp.sum(jnp.abs(k), axis=0, keepdims=True)


def _hyena_filters(n, w1, b1, freq, w2, b2, w3):
    Hd = w1.shape[1]
    OC = w3.shape[1] // 2
    t = jnp.linspace(0.0, 1.0, n, dtype=F32)[:, None]
    w = (2.0 * math.pi / n) * jnp.arange(n, dtype=F32)[:, None]
    f = jnp.linspace(1e-4, HY_BANDS - 1, HY_BANDS, dtype=F32)[None, :]
    z = jnp.concatenate([t, jnp.cos(f * w), -jnp.sin(f * w)], axis=-1)
    fold = lambda a: jnp.concatenate([a, jnp.zeros_like(a[:1]), a[:0:-1]], axis=0)
    EP = 64
    z2 = jnp.pad(fold(z), ((0, 0), (0, EP - HY_EMB)))
    w1p = jnp.pad(w1, ((0, EP - HY_EMB), (0, 0)))
    rt = _tile(2 * n, (1024, 512, 256))
    a2 = _call(
        _filt_trunk_body, grid=(2 * n // rt,),
        in_specs=[pl.BlockSpec((rt, EP), lambda i: (i, 0)),
                  pl.BlockSpec((EP, Hd), lambda i: (0, 0)),
                  pl.BlockSpec((1, Hd), lambda i: (0, 0)),
                  pl.BlockSpec((2, Hd), lambda i: (0, 0)),
                  pl.BlockSpec((Hd, Hd), lambda i: (0, 0)),
                  pl.BlockSpec((1, Hd), lambda i: (0, 0))],
        out_specs=pl.BlockSpec((rt, Hd), lambda i: (i, 0)),
        out_shape=jax.ShapeDtypeStruct((2 * n, Hd), F32),
        sem=("parallel",), name="hyena_filter_trunk")(z2, w1p, b1.reshape(1, Hd), freq, w2, b2.reshape(1, Hd))
    tc = LANES
    t2 = jnp.broadcast_to(fold(t), (2 * n, tc))
    deltas = jnp.abs(jnp.linspace(HY_MIN_DECAY, HY_MAX_DECAY, OC, dtype=F32))[None, :]
    nc = OC // tc
    return _call(
        functools.partial(_filt_main_body, n=n), grid=(nc,),
        in_specs=[pl.BlockSpec((2 * n, Hd), lambda c: (0, 0)),
                  pl.BlockSpec((Hd, tc), lambda c: (0, c)),
                  pl.BlockSpec((Hd, tc), lambda c: (0, nc + c)),
                  pl.BlockSpec((2 * n, tc), lambda c: (0, 0)),
                  pl.BlockSpec((1, tc), lambda c: (0, c))],
        out_specs=pl.BlockSpec((2 * n, tc), lambda c: (0, c)),
        out_shape=jax.ShapeDtypeStruct((2 * n, OC), F32),
        sem=("parallel",), name="hyena_filter")(a2, w3, w3, t2, deltas)


def _embed(re, im):
    return np.block([[re, -im], [im, re]])


@functools.lru_cache(maxsize=None)
def _fft_tables(N1, N2):
    N = N1 * N2
    S1 = N1 // 2
    i1 = np.arange(N1)
    ang = -2.0 * np.pi * ((i1[:, None] * i1[None, :]) % N1) / N1
    fr, fi = np.cos(ang), np.sin(ang)
    f1_pair = _embed(fr[:, :S1], fi[:, :S1])
    f1_real = np.concatenate([fr, fi], axis=0)
    i2 = np.arange(N2)
    fidx = i1[:, None, None] + N1 * i2[None, :, None]
    ang = -2.0 * np.pi * ((fidx * i2[None, None, :]) % N) / N
    mr, mi = np.cos(ang), np.sin(ang)
    m_fwd = np.stack([_embed(mr[a], mi[a]) for a in range(N1)])
    m_inv = np.stack([_embed(mr[a].T, -mi[a].T) for a in range(N1)])
    ang = 2.0 * np.pi * ((i1[:S1, None] * i1[None, :]) % N1) / N1
    f1_inv = _embed(np.cos(ang) / N, np.sin(ang) / N)
    cvt = lambda a: jnp.asarray(a, dtype=BF16)
    return cvt(f1_pair), cvt(f1_real), cvt(m_fwd), cvt(m_inv), cvt(f1_inv)


def _lmat_body(f_ref, x_ref, o_ref):
    o_ref[0] = jnp.dot(f_ref[...], x_ref[0].astype(BF16), preferred_element_type=F32).astype(o_ref.dtype)


def _left_matmul(fm, x, out_dtype, name):
    P, K, W = x.shape
    R = fm.shape[0]
    tw = _tile(W, (4096, 2048, 1024, 512, 256, 128))
    return _call(
        _lmat_body, grid=(P, W // tw),
        in_specs=[pl.BlockSpec((R, K), lambda p, j: (0, 0)),
                  pl.BlockSpec((1, K, tw), lambda p, j: (p, 0, j))],
        out_specs=pl.BlockSpec((1, R, tw), lambda p, j: (p, 0, j)),
        out_shape=jax.ShapeDtypeStruct((P, R, W), out_dtype),
        sem=("parallel", "parallel"), name=name)(fm, x)


def _cmul(xr, xi, kr, ki):
    return xr * kr - xi * ki, xr * ki + xi * kr


def _fftmid_body(a_ref, m_ref, mi_ref, k_ref, o_ref, *, FB, N2):
    for t in range(FB):
        a = a_ref[0, :, t].reshape(2 * N2, a_ref.shape[-1])
        x = jnp.dot(m_ref[t], a, preferred_element_type=F32)
        yr, yi = _cmul(x[:N2], x[N2:], k_ref[0, t].astype(F32), k_ref[1, t].astype(F32))
        y = jnp.concatenate([yr, yi], axis=0).astype(BF16)
        g = jnp.dot(mi_ref[t], y, preferred_element_type=F32)
        o_ref[0, :, t] = g.reshape(2, N2, g.shape[-1]).astype(o_ref.dtype)


def _fftfwd_body(a_ref, m_ref, o_ref, *, FB, N2):
    for t in range(FB):
        a = a_ref[:, t].reshape(2 * N2, a_ref.shape[-1])
        x = jnp.dot(m_ref[t], a, preferred_element_type=F32)
        o_ref[:, t] = x.reshape(2, N2, x.shape[-1]).astype(o_ref.dtype)


SUB_BLOCK = 16


def _dft1_body(f_ref, x_ref, o_ref):
    xt = pltpu.einshape("ksc->skc", x_ref[0])
    r = jnp.stack([jnp.dot(f_ref[...], xt[k].astype(BF16), preferred_element_type=F32)
                   for k in range(xt.shape[0])], axis=0)
    o_ref[0] = pltpu.einshape("skc->ksc", r).astype(o_ref.dtype)


def _outer_dft(fm, x4, out_dtype, name):
    P, K, N2, C = x4.shape
    R = fm.shape[0]
    tc = _tile(C, (512, 256, 128))
    return _call(
        _dft1_body, grid=(P, N2 // SUB_BLOCK, C // tc),
        in_specs=[pl.BlockSpec((R, K), lambda p, s, c: (0, 0)),
                  pl.BlockSpec((1, K, SUB_BLOCK, tc), lambda p, s, c: (p, 0, s, c))],
        out_specs=pl.BlockSpec((1, R, SUB_BLOCK, tc), lambda p, s, c: (p, 0, s, c)),
        out_shape=jax.ShapeDtypeStruct((P, R, N2, C), out_dtype),
        sem=("parallel", "parallel", "parallel"), name=name)(fm, x4)


def _idft1_gate_body(f_ref, g_ref, z_ref, x1_ref, sk_ref, o_ref):
    gt = pltpu.einshape("ksc->skc", g_ref[0])
    y = jnp.stack([jnp.dot(f_ref[...], gt[k], preferred_element_type=F32) for k in range(gt.shape[0])], axis=0)
    y = pltpu.einshape("skc->ksc", y)
    o_ref[0] = (x1_ref[0] * (y + sk_ref[...] * z_ref[0])).astype(o_ref.dtype)


def _filter_spectrum_2stage(k, N1, N2):
    N, OC = k.shape
    _, f1_real, m_fwd, _, _ = _fft_tables(N1, N2)
    a = _outer_dft(f1_real, k.reshape(1, N1, N2, OC), BF16, "filter_dft1").reshape(2, N1, N2, OC)
    FB = _tile(N1, (8, 4, 2, 1))
    tc = _tile(OC, (512, 256, 128))
    return _call(
        functools.partial(_fftfwd_body, FB=FB, N2=N2), grid=(N1 // FB, OC // tc),
        in_specs=[pl.BlockSpec((2, FB, N2, tc), lambda f, c: (0, f, 0, c)),
                  pl.BlockSpec((FB, 2 * N2, 2 * N2), lambda f, c: (f, 0, 0))],
        out_specs=pl.BlockSpec((2, FB, N2, tc), lambda f, c: (0, f, 0, c)),
        out_shape=jax.ShapeDtypeStruct((2, N1, N2, OC), BF16),
        sem=("parallel", "parallel"), name="filter_dft2")(a, m_fwd)


def _long_conv_gate_2stage(z, x1, skip, kf, order, out_dtype, N1, N2):
    B, n, C = z.shape
    assert B % 2 == 0
    P, S1 = B // 2, N1 // 2
    f1_pair, _, m_fwd, m_inv, f1_inv = _fft_tables(N1, N2)
    z4 = z.reshape(P, 2 * S1, N2, C)
    a = _outer_dft(f1_pair, z4, BF16, "conv_dft1").reshape(P, 2, N1, N2, C)
    FB = _tile(N1, (8, 4, 2, 1))
    tc = _tile(C, (512, 256, 128))
    oc = order * (C // tc)
    g = _call(
        functools.partial(_fftmid_body, FB=FB, N2=N2), grid=(N1 // FB, C // tc, P),
        in_specs=[pl.BlockSpec((1, 2, FB, N2, tc), lambda f, c, p: (p, 0, f, 0, c)),
                  pl.BlockSpec((FB, 2 * N2, 2 * N2), lambda f, c, p: (f, 0, 0)),
                  pl.BlockSpec((FB, 2 * N2, 2 * N2), lambda f, c, p: (f, 0, 0)),
                  pl.BlockSpec((2, FB, N2, tc), lambda f, c, p: (0, f, 0, oc + c))],
        out_specs=pl.BlockSpec((1, 2, FB, N2, tc), lambda f, c, p: (p, 0, f, 0, c)),
        out_shape=jax.ShapeDtypeStruct((P, 2, N1, N2, C), BF16),
        sem=("parallel", "parallel", "arbitrary"), name="conv_dft2_mul_idft2")(a, m_fwd, m_inv, kf)
    blk = lambda rows: pl.BlockSpec((1, rows, SUB_BLOCK, tc), lambda p, s, c: (p, 0, s, c))
    out = _call(
        _idft1_gate_body, grid=(P, N2 // SUB_BLOCK, C // tc),
        in_specs=[pl.BlockSpec((2 * S1, 2 * N1), lambda p, s, c: (0, 0)),
                  blk(2 * N1), blk(2 * S1), blk(2 * S1),
                  pl.BlockSpec((1, tc), lambda p, s, c: (0, c))],
        out_specs=blk(2 * S1),
        out_shape=jax.ShapeDtypeStruct((P, 2 * S1, N2, C), out_dtype),
        sem=("parallel", "parallel", "parallel"), name="conv_idft1_gate")(
            f1_inv, g.reshape(P, 2 * N1, N2, C), z4, x1.reshape(P, 2 * S1, N2, C), skip[None, :])
    return out.reshape(B, n, C)


@functools.lru_cache(maxsize=None)
def _dft_tables(n):
    N = 2 * n
    f = np.arange(N)
    ang = -2.0 * np.pi * ((f[:, None] * f[None, :]) % N) / N
    fr, fi = np.cos(ang), np.sin(ang)
    fwd_full = np.concatenate([fr, fi], axis=0)
    fwd_half = fwd_full[:, :n]
    inv = np.concatenate([fr[:n, :], fi[:n, :]], axis=1) / N
    cvt = lambda a: jnp.asarray(a, dtype=BF16)
    return cvt(fwd_full), cvt(fwd_half), cvt(inv)


def _dftconv_body(z_ref, x1_ref, sk_ref, f_ref, fi_ref, k_ref, o_ref, *, N):
    z = z_ref[0]
    x = jnp.dot(f_ref[...], z.astype(BF16), preferred_element_type=F32)
    yr, yi = _cmul(x[:N], x[N:], k_ref[0], k_ref[1])
    y = jnp.concatenate([yr, yi], axis=0).astype(BF16)
    y = jnp.dot(fi_ref[...], y, preferred_element_type=F32)
    o_ref[0] = (x1_ref[0] * (y + sk_ref[...] * z)).astype(o_ref.dtype)


def _long_conv_gate_dense(z, x1, skip, kf, order, out_dtype):
    B, n, C = z.shape
    N = 2 * n
    _, fwd_half, inv = _dft_tables(n)
    tc = _tile(C, (256, 128))
    oc = order * (C // tc)
    return _call(
        functools.partial(_dftconv_body, N=N), grid=(C // tc, B),
        in_specs=[pl.BlockSpec((1, n, tc), lambda c, b: (b, 0, c)),
                  pl.BlockSpec((1, n, tc), lambda c, b: (b, 0, c)),
                  pl.BlockSpec((1, tc), lambda c, b: (0, c)),
                  pl.BlockSpec((2 * N, n), lambda c, b: (0, 0)),
                  pl.BlockSpec((n, 2 * N), lambda c, b: (0, 0)),
                  pl.BlockSpec((2, N, tc), lambda c, b: (0, 0, oc + c))],
        out_specs=pl.BlockSpec((1, n, tc), lambda c, b: (b, 0, c)),
        out_shape=jax.ShapeDtypeStruct((B, n, C), out_dtype),
        sem=("parallel", "arbitrary"), name="conv_dense_dft")(z, x1, skip[None, :], fwd_half, inv, kf)


def _hyena_mixer(v, x1, x2, w1, b1, freq, w2, b2, w3, skip):
    B, n, _ = v.shape
    k = _hyena_filters(n, w1, b1, freq, w2, b2, w3)
    N = 2 * n
    if N % FFT_N2 == 0 and (N // FFT_N2) >= 16:
        N1 = N // FFT_N2
        kf = _filter_spectrum_2stage(k, N1, FFT_N2)
        z = _long_conv_gate_2stage(v, x1, skip[0], kf, 0, F32, N1, FFT_N2)
        return _long_conv_gate_2stage(z, x2, skip[1], kf, 1, BF16, N1, FFT_N2)
    fwd_full, _, _ = _dft_tables(n)
    kf = _left_matmul(fwd_full, k[None], F32, "filter_dense_dft").reshape(2, N, k.shape[1])
    z = _long_conv_gate_dense(v, x1, skip[0], kf, 0, F32)
    return _long_conv_gate_dense(z, x2, skip[1], kf, 1, BF16)


def _nt(a, b):
    return lax.dot_general(a, b, (((1,), (1,)), ((), ())), preferred_element_type=F32)


def _na_body(q_ref, k_ref, v_ref, kc_ref, vc_ref, b_ref, o_ref, *, R, KR, rows, HPS, scale):
    j = pl.program_id(2)
    start = pl.multiple_of(jnp.clip(j * R - NA_WIN_R // 2, 0, rows - KR) * GRID_W, GRID_W)
    for h in range(HPS):
        hs = slice(h * HEAD_DIM, (h + 1) * HEAD_DIM)
        q = q_ref[0, :, hs]
        kw = k_ref[0, pl.ds(start, KR * GRID_W), hs]
        vw = v_ref[0, pl.ds(start, KR * GRID_W), hs]
        s = _nt(q, kw) + b_ref[h, 0]
        sc = _nt(q, kc_ref[0, :, hs])
        m = jnp.maximum(jnp.max(s, axis=-1, keepdims=True), jnp.max(sc, axis=-1, keepdims=True))
        p = jnp.exp2((s - m) * (scale * LOG2_E))
        pc = jnp.exp2((sc - m) * (scale * LOG2_E))
        l = jnp.sum(p, axis=-1, keepdims=True) + jnp.sum(pc, axis=-1, keepdims=True)
        o = jnp.dot(p.astype(BF16), vw, preferred_element_type=F32)
        o = o + jnp.dot(pc.astype(BF16), vc_ref[0, :, hs], preferred_element_type=F32)
        o_ref[0, :, hs] = (o / l).astype(o_ref.dtype)


def _na_geometry(S):
    rows = S // GRID_W
    kr = min(NA_WIN_R, rows)
    R = min(8, rows)
    KR = min(rows, R + kr)
    nb = rows // R
    assert rows % R == 0
    types = sorted({0, min(1, nb - 1), nb - 1})
    if nb > 3:
        offs = {int(np.clip(j * R - NA_WIN_R // 2, 0, rows - KR)) - j * R for j in range(1, nb - 1)}
        assert len(offs) == 1
    return rows, kr, R, KR, nb, types


def _nabias_body(rpb_ref, o_ref, tw_ref, *, plan, R, KR, inv_scale):
    W = GRID_W
    nd_r, nd_c = 2 * NA_WIN_R - 1, 2 * NA_WIN_C - 1
    base = pl.program_id(0) * (nd_r * nd_c)
    qc = lax.broadcasted_iota(jnp.int32, (W, 2 * W), 0)
    lane = lax.broadcasted_iota(jnp.int32, (W, 2 * W), 1)
    kc = lane % W
    cs = jnp.clip(qc - NA_WIN_C // 2, 0, W - NA_WIN_C)
    col_ok = jnp.logical_and(kc >= cs, kc < cs + NA_WIN_C)
    dcm = kc - qc + (NA_WIN_C - 1)
    neg = jnp.full((W, 2 * W), NEG_INF, F32)
    for dr in range(nd_r):
        acc = neg
        for dc in range(nd_c):
            acc = jnp.where(dcm == dc, rpb_ref[base + dr * nd_c + dc] * inv_scale, acc)
        tw_ref[dr] = jnp.where(col_ok, acc, NEG_INF)
    left = lane < W
    for t, per_q in enumerate(plan):
        for qr in range(R):
            for kp in range(KR // 2):
                d0, d1 = per_q[qr][kp]
                a = neg if d0 is None else tw_ref[d0]
                b = neg if d1 is None else tw_ref[d1]
                blk = neg if (d0 is None and d1 is None) else jnp.where(left, a, b)
                o_ref[0, t, qr * W:(qr + 1) * W, kp * 2 * W:(kp + 1) * 2 * W] = blk


def _na_bias_tables(rpb, S):
    rows, kr, R, KR, nb, types = _na_geometry(S)
    assert 2 * GRID_W == LANES and KR % 2 == 0
    plan = []
    for jt in types:
        start = int(np.clip(jt * R - NA_WIN_R // 2, 0, rows - KR))
        per_q = []
        for q in range(R):
            qra = jt * R + q
            ws = int(np.clip(qra - kr // 2, 0, rows - kr))
            d = [(start + k) - qra + (NA_WIN_R - 1) if ws <= start + k < ws + kr else None for k in range(KR)]
            per_q.append([(d[2 * p], d[2 * p + 1]) for p in range(KR // 2)])
        plan.append(per_q)
    L, H, nd_r, nd_c = rpb.shape
    T, QB, KB = len(types), R * GRID_W, KR * GRID_W
    return _call(
        functools.partial(_nabias_body, plan=plan, R=R, KR=KR, inv_scale=HEAD_DIM ** 0.5), grid=(L * H,),
        in_specs=[pl.BlockSpec(memory_space=pltpu.SMEM)],
        out_specs=pl.BlockSpec((1, T, QB, KB), lambda i: (i, 0, 0, 0)),
        out_shape=jax.ShapeDtypeStruct((L * H, T, QB, KB), F32),
        scratch=[pltpu.VMEM((nd_r, GRID_W, 2 * GRID_W), F32)],
        sem=("parallel",), name="na_bias_table")(rpb.reshape(-1).astype(F32))


def _na_attention(qkv, qkv_c, bias, l, offs, H):
    B, S, _ = qkv.shape
    CTX = qkv_c.shape[1]
    rows, kr, R, KR, nb, types = _na_geometry(S)
    T = len(types)
    QB, KB = R * GRID_W, KR * GRID_W
    HPS = next(n for n in (4, 2, 1) if H % n == 0)
    HW = HPS * HEAD_DIM
    assert all(offs[n] % HW == 0 for n in ("na_k", "na_v", "na_q"))
    ok_, ov_, oq_ = (offs[n] // HW for n in ("na_k", "na_v", "na_q"))

    def btype(j):
        if T == nb:
            return j
        return jnp.where(j == 0, 0, jnp.where(j == nb - 1, T - 1, 1))

    return _call(
        functools.partial(_na_body, R=R, KR=KR, rows=rows, HPS=HPS, scale=HEAD_DIM ** -0.5),
        grid=(B, H // HPS, nb),
        in_specs=[pl.BlockSpec((1, QB, HW), lambda b, h, j: (b, j, oq_ + h)),
                  pl.BlockSpec((1, S, HW), lambda b, h, j: (b, 0, ok_ + h)),
                  pl.BlockSpec((1, S, HW), lambda b, h, j: (b, 0, ov_ + h)),
                  pl.BlockSpec((1, CTX, HW), lambda b, h, j: (b, 0, ok_ + h)),
                  pl.BlockSpec((1, CTX, HW), lambda b, h, j: (b, 0, ov_ + h)),
                  pl.BlockSpec((HPS, 1, QB, KB), lambda b, h, j: (l * (H // HPS) + h, btype(j), 0, 0))],
        out_specs=pl.BlockSpec((1, QB, HW), lambda b, h, j: (b, j, h)),
        out_shape=jax.ShapeDtypeStruct((B, S, H * HEAD_DIM), BF16),
        sem=("parallel", "parallel", "arbitrary"), name="na_attention")(qkv, qkv, qkv, qkv_c, qkv_c, bias)


def _stack_heads(q2, G):
    return jnp.concatenate([q2[:, g * HEAD_DIM:(g + 1) * HEAD_DIM] for g in range(G)], axis=0)


def _unstack_heads(o, G, n):
    return jnp.concatenate([o[g * n:(g + 1) * n] for g in range(G)], axis=1)


def _sink_column(sink_ref, h0, G, n):
    return jnp.concatenate([jnp.full((n, 1), sink_ref[h0 + g], F32) for g in range(G)], axis=0)


def _swa_body(sink_ref, q_ref, k_ref, v_ref, kc_ref, vc_ref, o_ref, *, QB, KB, S, G, HPS, scale):
    hb, j = pl.program_id(1), pl.program_id(2)
    start = pl.multiple_of(jnp.clip(j * QB - GQA_WINDOW, 0, S - KB), LANES)
    qpos = j * QB + lax.broadcasted_iota(jnp.int32, (QB, KB), 0)
    kpos = start + lax.broadcasted_iota(jnp.int32, (QB, KB), 1)
    mask = jnp.where(jnp.abs(qpos - kpos) <= GQA_WINDOW, 0.0, NEG_INF)
    mask = jnp.concatenate([mask] * G, axis=0)
    for h in range(HPS):
        hs = slice(h * HEAD_DIM, (h + 1) * HEAD_DIM)
        qs = slice(h * G * HEAD_DIM, (h + 1) * G * HEAD_DIM)
        q = _stack_heads(q_ref[0, :, qs], G)
        kw = k_ref[0, pl.ds(start, KB), hs]
        vw = v_ref[0, pl.ds(start, KB), hs]
        s = _nt(q, kw) + mask
        sc = _nt(q, kc_ref[0, :, hs])
        sk = _sink_column(sink_ref, (hb * HPS + h) * G, G, QB) * (1.0 / scale)
        m = jnp.maximum(jnp.maximum(jnp.max(s, axis=-1, keepdims=True), jnp.max(sc, axis=-1, keepdims=True)), sk)
        p = jnp.exp2((s - m) * (scale * LOG2_E))
        pc = jnp.exp2((sc - m) * (scale * LOG2_E))
        l = (jnp.sum(p, axis=-1, keepdims=True) + jnp.sum(pc, axis=-1, keepdims=True)
             + jnp.exp2((sk - m) * (scale * LOG2_E)))
        o = jnp.dot(p.astype(BF16), vw, preferred_element_type=F32)
        o = o + jnp.dot(pc.astype(BF16), vc_ref[0, :, hs], preferred_element_type=F32)
        o_ref[0, :, qs] = _unstack_heads(o / l, G, QB).astype(o_ref.dtype)


def _swa_attention(qkv, qkv_c, sink, offs, KVH):
    B, S, _ = qkv.shape
    CTX = qkv_c.shape[1]
    G = GQA_GROUP
    QB = _tile(S, (512, 256, 128))
    KB = min(S, QB + 2 * GQA_WINDOW)
    HPS = 2 if KVH % 2 == 0 else 1
    HW = HPS * HEAD_DIM
    assert offs["sw_k"] % HW == 0 and offs["sw_v"] % HW == 0 and offs["sw_q"] % (G * HW) == 0
    ok_, ov_, oq_ = offs["sw_k"] // HW, offs["sw_v"] // HW, offs["sw_q"] // (G * HW)
    return _call(
        functools.partial(_swa_body, QB=QB, KB=KB, S=S, G=G, HPS=HPS, scale=HEAD_DIM ** -0.5),
        grid=(B, KVH // HPS, S // QB),
        in_specs=[pl.BlockSpec(memory_space=pltpu.SMEM),
                  pl.BlockSpec((1, QB, G * HW), lambda b, h, j: (b, j, oq_ + h)),
                  pl.BlockSpec((1, S, HW), lambda b, h, j: (b, 0, ok_ + h)),
                  pl.BlockSpec((1, S, HW), lambda b, h, j: (b, 0, ov_ + h)),
                  pl.BlockSpec((1, CTX, HW), lambda b, h, j: (b, 0, ok_ + h)),
                  pl.BlockSpec((1, CTX, HW), lambda b, h, j: (b, 0, ov_ + h))],
        out_specs=pl.BlockSpec((1, QB, G * HW), lambda b, h, j: (b, j, h)),
        out_shape=jax.ShapeDtypeStruct((B, S, KVH * G * HEAD_DIM), BF16),
        sem=("parallel", "parallel", "arbitrary"), name="swa_attention")(sink, qkv, qkv, qkv, qkv_c, qkv_c)


def _cattn_body(sink_ref, q_ref, k_ref, v_ref, o_ref, *, G, use_sink, scale):
    h = pl.program_id(1)
    n = q_ref.shape[1]
    q = _stack_heads(q_ref[0], G)
    s = _nt(q, k_ref[0]) * scale
    m = jnp.max(s, axis=-1, keepdims=True)
    if use_sink:
        sk = _sink_column(sink_ref, h * G, G, n)
        m = jnp.maximum(m, sk)
    p = jnp.exp(s - m)
    l = jnp.sum(p, axis=-1, keepdims=True)
    if use_sink:
        l = l + jnp.exp(sk - m)
    o = jnp.dot(p.astype(BF16), v_ref[0], preferred_element_type=F32)
    o_ref[0] = _unstack_heads(o / l, G, n).astype(o_ref.dtype)


def _ctx_attention(qkv_c, sink, oq, ok, ov, KVH, G, use_sink):
    B, n, _ = qkv_c.shape
    oq_, ok_, ov_ = oq // (G * HEAD_DIM), ok // HEAD_DIM, ov // HEAD_DIM
    return _call(
        functools.partial(_cattn_body, G=G, use_sink=use_sink, scale=HEAD_DIM ** -0.5),
        grid=(B, KVH),
        in_specs=[pl.BlockSpec(memory_space=pltpu.SMEM),
                  pl.BlockSpec((1, n, G * HEAD_DIM), lambda b, h: (b, 0, oq_ + h)),
                  pl.BlockSpec((1, n, HEAD_DIM), lambda b, h: (b, 0, ok_ + h)),
                  pl.BlockSpec((1, n, HEAD_DIM), lambda b, h: (b, 0, ov_ + h))],
        out_specs=pl.BlockSpec((1, n, G * HEAD_DIM), lambda b, h: (b, 0, h)),
        out_shape=jax.ShapeDtypeStruct((B, n, KVH * G * HEAD_DIM), BF16),
        sem=("parallel", "parallel"), name="ctx_attention")(sink, qkv_c, qkv_c, qkv_c)


def _merge_body(x_ref, g_ref, sh_ref, sc_ref, yh, yn, ys, wgh, wgn, wgs, wh, wn, ws, o_ref, h_ref, *, eps):
    @pl.when(pl.program_id(2) == 0)
    def _():
        h_ref[...] = _norm_modulate(x_ref[0], g_ref, sh_ref, sc_ref, eps).astype(BF16)

    h = h_ref[...]
    gate = lambda wg: _sigmoid(jnp.dot(h, wg[...], preferred_element_type=F32))
    m = gate(wgh) * jnp.dot(yh[0], wh[...], preferred_element_type=F32)
    m = m + gate(wgn) * jnp.dot(yn[0], wn[...], preferred_element_type=F32)
    m = m + gate(wgs) * jnp.dot(ys[0], ws[...], preferred_element_type=F32)
    o_ref[0] = m.astype(o_ref.dtype)


def _merge_branches(x, gain, shift, scale, w_in, gate_start, y_hy, y_na, y_sw, w_br, l):
    B, S, D = x.shape
    widths = (y_hy.shape[2], y_na.shape[2], y_sw.shape[2])
    starts = (0, widths[0], widths[0] + widths[1])
    assert all(s % w == 0 for s, w in zip(starts, widths))
    tm = _tile(S, (512, 256, 128))
    tn = _seg_tile([(gate_start, D)], (512, 256, 128))
    nj, gj = D // tn, gate_start // tn
    yspec = lambda y: pl.BlockSpec((1, tm, y.shape[2]), lambda b, i, j: (b, i, 0))
    gspec = lambda k: pl.BlockSpec((None, D, tn), lambda b, i, j: (l, 0, gj + k * nj + j))
    wspec = lambda k: pl.BlockSpec((None, widths[k], tn), lambda b, i, j: (l, starts[k] // widths[k], j))
    return _call(
        functools.partial(_merge_body, eps=NORM_EPS), grid=(B, S // tm, nj),
        in_specs=[pl.BlockSpec((1, tm, D), lambda b, i, j: (b, i, 0)),
                  pl.BlockSpec((1, D), lambda b, i, j: (0, 0)),
                  pl.BlockSpec((1, 1, D), lambda b, i, j: (b, 0, 0)),
                  pl.BlockSpec((1, 1, D), lambda b, i, j: (b, 0, 0)),
                  yspec(y_hy), yspec(y_na), yspec(y_sw), gspec(0), gspec(1), gspec(2),
                  wspec(0), wspec(1), wspec(2)],
        out_specs=pl.BlockSpec((1, tm, tn), lambda b, i, j: (b, i, j)),
        out_shape=jax.ShapeDtypeStruct((B, S, D), BF16),
        scratch=[pltpu.VMEM((tm, D), BF16)],
        sem=("parallel", "parallel", "arbitrary"), name="gated_merge")(
            x, gain.reshape(1, D), shift, scale, y_hy, y_na, y_sw, w_in, w_in, w_in, w_br, w_br, w_br)


def _mmres_body(a_ref, w_ref, x_ref, g_ref, mg_ref, o_ref, *, nk, eps):
    k = pl.program_id(2)
    part = lambda: jnp.dot(a_ref[0], w_ref[...], preferred_element_type=F32)

    def finish():
        y = o_ref[0]
        yn = y * lax.rsqrt(jnp.mean(y * y, axis=-1, keepdims=True) + eps) * g_ref[...]
        o_ref[0] = x_ref[0] + mg_ref[0] * yn

    if nk == 1:
        o_ref[0] = part()
        finish()
        return

    @pl.when(k == 0)
    def _():
        o_ref[0] = part()

    @pl.when(k > 0)
    def _():
        o_ref[0] += part()

    @pl.when(k == nk - 1)
    def _():
        finish()


def _matmul_norm_residual(a, w, l, x, gain, mgate, name):
    B, S, K = a.shape
    D = w.shape[2]
    if K <= 2048:
        tm, tk = _tile(S, (512, 256, 128)), K
    else:
        tm = _tile(S, (1024, 512, 256, 128))
        tk = max(t for t in range(LANES, 1537, LANES) if K % t == 0)
    nk = K // tk
    est = 2 * (tm * tk * 2 + tk * D * 2 + 2 * tm * D * 4)
    return _call(
        functools.partial(_mmres_body, nk=nk, eps=NORM_EPS), grid=(B, S // tm, nk),
        vmem=VMEM_LIMIT_LARGE if est > VMEM_LIMIT - VMEM_TEMP_RESERVE else VMEM_LIMIT,
        in_specs=[pl.BlockSpec((1, tm, tk), lambda b, i, k: (b, i, k)),
                  pl.BlockSpec((None, tk, D), lambda b, i, k: (l, k, 0)),
                  pl.BlockSpec((1, tm, D), lambda b, i, k: (b, i, 0)),
                  pl.BlockSpec((1, D), lambda b, i, k: (0, 0)),
                  pl.BlockSpec((1, 1, D), lambda b, i, k: (b, 0, 0))],
        out_specs=pl.BlockSpec((1, tm, D), lambda b, i, k: (b, i, 0)),
        out_shape=jax.ShapeDtypeStruct((B, S, D), F32),
        sem=("parallel", "parallel", "arbitrary"), name=name)(a, w, x, gain.reshape(1, D), mgate)


def _rope_tables(n):
    t = jnp.arange(n)
    row = (t // GRID_W).astype(F32)
    col = (t % GRID_W).astype(F32)
    per_axis = HEAD_DIM // 2
    inv = ROPE_BASE ** (-jnp.arange(0, per_axis, 2, dtype=F32) / per_axis)
    ar, ac = row[:, None] * inv, col[:, None] * inv
    cos_t = jnp.concatenate([jnp.cos(ar), jnp.cos(ar), jnp.cos(ac), jnp.cos(ac)], axis=1)
    sin_t = jnp.concatenate([-jnp.sin(ar), jnp.sin(ar), -jnp.sin(ac), jnp.sin(ac)], axis=1)
    return cos_t, sin_t


def kernel(x, c, ctx, c_ctx, w_mod, b_mod, norm_gains, w_in, hy_conv_w, hy_conv_b, hy_w1, hy_b1, hy_freq, hy_w2, hy_b2, hy_w3, hy_skip, na_rpb, swa_sink, w_branch, w_out, ffn_w_up, ffn_conv_w, ffn_conv_b, ffn_w_down):
    B, S, D = x.shape
    L = w_mod.shape[0]
    C = hy_skip.shape[-1]
    H_na = na_rpb.shape[1]
    H_q = swa_sink.shape[1]
    KVH = H_q // GQA_GROUP
    NA_W, QW, KVW = H_na * HEAD_DIM, H_q * HEAD_DIM, KVH * HEAD_DIM
    KV_COLS = 2 * NA_W + 2 * KVW
    qkv_segs = [(0, KV_COLS), (KV_COLS + 3 * C, NA_W + QW)]
    gate_start = KV_COLS + 3 * C + NA_W + QW
    offs = {"na_k": 0, "na_v": NA_W, "sw_k": 2 * NA_W, "sw_v": 2 * NA_W + KVW,
            "na_q": KV_COLS, "sw_q": KV_COLS + NA_W}
    rope_chunks = (list(range(offs["sw_k"] // LANES, (offs["sw_k"] + KVW) // LANES))
                   + list(range(offs["sw_q"] // LANES, (offs["sw_q"] + QW) // LANES)))
    cos_t, sin_t = _rope_tables(S)

    w_in_b, w_out_b = w_in.astype(BF16), w_out.astype(BF16)
    w_br_b, w_up_b, w_dn_b = w_branch.astype(BF16), ffn_w_up.astype(BF16), ffn_w_down.astype(BF16)
    na_bias = _na_bias_tables(na_rpb, S)

    R = -(-(B + 1) // 8) * 8
    cc = jnp.concatenate([c, c_ctx[None, :], jnp.zeros((R - B - 1, D), F32)], axis=0)
    mods = _modulation(cc, w_mod, b_mod)

    xc = ctx
    for l in range(L):
        mod = [mods[l, :B, k * D:(k + 1) * D].reshape(B, 1, D) for k in range(6)]
        mod_c = [jnp.broadcast_to(mods[l, B, k * D:(k + 1) * D].reshape(1, 1, D), (B, 1, D)) for k in range(6)]
        g = norm_gains[l]
        hy_p = (hy_w1[l], hy_b1[l], hy_freq[l], hy_w2[l], hy_b2[l], hy_w3[l], hy_skip[l])

        qkv_c, vc, x1c, x2c = _in_proj(xc, g[0], mod_c[0], mod_c[1], w_in_b, l, qkv_segs, KV_COLS, C,
                                       hy_conv_w[l], hy_conv_b[l], None, "ctx_in_proj")
        qkv, v, x1, x2 = _in_proj(x, g[0], mod[0], mod[1], w_in_b, l, qkv_segs, KV_COLS, C,
                                  hy_conv_w[l], hy_conv_b[l], (cos_t, sin_t, rope_chunks), "in_proj")
        y_hy = _hyena_mixer(v, x1, x2, *hy_p)
        y_na = _na_attention(qkv, qkv_c, na_bias, l, offs, H_na)
        y_sw = _swa_attention(qkv, qkv_c, swa_sink[l], offs, KVH)
        m = _merge_branches(x, g[0], mod[0], mod[1], w_in_b, gate_start, y_hy, y_na, y_sw, w_br_b, l)
        x = _matmul_norm_residual(m, w_out_b, l, x, g[1], mod[2], "out_proj_residual")
        gl = _ffn_up_glu(x, g[2], mod[3], mod[4], w_up_b, l, ffn_conv_w[l], ffn_conv_b[l])
        x = _matmul_norm_residual(gl, w_dn_b, l, x, g[3], mod[5], "ffn_down_residual")

        if l < L - 1:
            yc_hy = _hyena_mixer(vc, x1c, x2c, *hy_p)
            yc_na = _ctx_attention(qkv_c, swa_sink[l], offs["na_q"], offs["na_k"], offs["na_v"], H_na, 1, False)
            yc_sw = _ctx_attention(qkv_c, swa_sink[l], offs["sw_q"], offs["sw_k"], offs["sw_v"], KVH, GQA_GROUP, True)
            flat = lambda t: t.reshape(1, -1, t.shape[-1])
            mc1 = [m_[:1] for m_ in mod_c]
            mc = _merge_branches(flat(xc), g[0], mc1[0], mc1[1], w_in_b, gate_start,
                                 flat(yc_hy), flat(yc_na), flat(yc_sw), w_br_b, l)
            xc = _matmul_norm_residual(mc, w_out_b, l, flat(xc), g[1], mc1[2], "ctx_out_proj_residual").reshape(ctx.shape)
            gl_c = _ffn_up_glu(xc, g[2], mod_c[3], mod_c[4], w_up_b, l, ffn_conv_w[l], ffn_conv_b[l])
            xc = _matmul_norm_residual(flat(gl_c), w_dn_b, l, flat(xc), g[3], mc1[5], "ctx_ffn_down_residual").reshape(ctx.shape)
    return x
```

```python
import functools
import math

import numpy as np
import jax
import jax.numpy as jnp
from jax import lax
from jax.experimental import pallas as pl
from jax.experimental.pallas import tpu as pltpu

F32 = jnp.float32
BF16 = jnp.bfloat16
NEG_INF = -1e30

GRID_W = 64
HEAD_DIM = 128
HY_EMB = 33
HY_BANDS = (HY_EMB - 1) // 2
HY_FAST_DECAY = 0.3
HY_SLOW_DECAY = 1.5
HY_DECAY_TARGET = 1e-2
HY_MAX_DECAY = math.log(HY_DECAY_TARGET) / HY_FAST_DECAY
HY_MIN_DECAY = math.log(HY_DECAY_TARGET) / HY_SLOW_DECAY
NA_WIN_R = 8
NA_WIN_C = 16
GQA_GROUP = 2
GQA_WINDOW = 128
ROPE_BASE = 10000.0
NORM_EPS = 1e-6

LANES = 128
FFT_N2 = 128
VMEM_LIMIT = 56 * 1024 * 1024
VMEM_LIMIT_LARGE = 60 * 1024 * 1024
VMEM_TEMP_RESERVE = 8 * 1024 * 1024
HP = lax.Precision.HIGHEST
LOG2_E = math.log2(math.e)


def _call(body, *, grid, in_specs, out_specs, out_shape, scratch=(), sem, name, vmem=VMEM_LIMIT):
    return pl.pallas_call(
        body, grid=grid, in_specs=in_specs, out_specs=out_specs, out_shape=out_shape,
        scratch_shapes=list(scratch),
        compiler_params=pltpu.CompilerParams(dimension_semantics=sem, vmem_limit_bytes=vmem),
        name=name)


def _tile(n, cands):
    for c in cands:
        if n % c == 0:
            return c
    raise ValueError(f"no tile for {n} in {cands}")


def _sigmoid(x):
    return 1.0 / (1.0 + jnp.exp(-x))


def _mod_body(c_ref, w_ref, b_ref, o_ref):
    c = c_ref[...]
    s = (c * _sigmoid(c)).astype(BF16)
    o_ref[0] = jnp.dot(s, w_ref[0].astype(BF16), preferred_element_type=F32) + b_ref[0]


def _modulation(cc, w_mod, b_mod):
    L, D, N = w_mod.shape
    R = cc.shape[0]
    tn = _tile(N, (1024, 512, 256, 128))
    return _call(
        _mod_body, grid=(L, N // tn),
        in_specs=[pl.BlockSpec((R, D), lambda l, j: (0, 0)),
                  pl.BlockSpec((1, D, tn), lambda l, j: (l, 0, j)),
                  pl.BlockSpec((1, 1, tn), lambda l, j: (l, 0, j))],
        out_specs=pl.BlockSpec((1, R, tn), lambda l, j: (l, 0, j)),
        out_shape=jax.ShapeDtypeStruct((L, R, N), F32),
        sem=("parallel", "parallel"), name="modulation")(cc, w_mod, b_mod.reshape(L, 1, N))


def _swap32(a):
    lane = lax.broadcasted_iota(jnp.int32, a.shape, 1)
    return jnp.where((lane & 32) == 0, pltpu.roll(a, 96, 1), pltpu.roll(a, 32, 1))


def _norm_modulate(x, g_ref, sh_ref, sc_ref, eps):
    y = x * lax.rsqrt(jnp.mean(x * x, axis=-1, keepdims=True) + eps) * g_ref[...]
    return y * (1.0 + sc_ref[0]) + sh_ref[0]


HALO = 16


def _halo_prologue(h_ref, x_ref, xp_ref, xn_ref, g_ref, sh_ref, sc_ref, eps):
    i, tm = pl.program_id(1), x_ref.shape[1]
    nm = lambda x: _norm_modulate(x, g_ref, sh_ref, sc_ref, eps)
    keep_prev = (i > 0).astype(F32)
    keep_next = (i < pl.num_programs(1) - 1).astype(F32)
    h_ref[0:HALO, :] = (nm(xp_ref[0]) * keep_prev).astype(BF16)
    h_ref[HALO:HALO + tm, :] = nm(x_ref[0]).astype(BF16)
    h_ref[HALO + tm:, :] = (nm(xn_ref[0]) * keep_next).astype(BF16)


def _conv3_halo(a, cw_ref, cb_ref, tm, period):
    rows = a.shape[0]
    up = pltpu.roll(a, 1, 0)[HALO:HALO + tm]
    un = pltpu.roll(a, rows - 1, 0)[HALO:HALO + tm]
    if period is not None:
        t = (pl.program_id(1) * tm + lax.broadcasted_iota(jnp.int32, up.shape, 0)) % period
        up = jnp.where(t == 0, 0.0, up)
        un = jnp.where(t == period - 1, 0.0, un)
    return up * cw_ref[0:1, :] + a[HALO:HALO + tm] * cw_ref[1:2, :] + un * cw_ref[2:3, :] + cb_ref[...]


def _halo_specs(S, tm, D):
    hb, nh = tm // HALO, S // HALO
    return [pl.BlockSpec((1, tm, D), lambda b, i, j: (b, i, 0)),
            pl.BlockSpec((1, HALO, D), lambda b, i, j: (b, jnp.maximum(i * hb - 1, 0), 0)),
            pl.BlockSpec((1, HALO, D), lambda b, i, j: (b, jnp.minimum((i + 1) * hb, nh - 1), 0)),
            pl.BlockSpec((1, D), lambda b, i, j: (0, 0)),
            pl.BlockSpec((1, 1, D), lambda b, i, j: (b, 0, 0)),
            pl.BlockSpec((1, 1, D), lambda b, i, j: (b, 0, 0))]


def _inproj_body(*refs, nq, nc, rope_chunks, period, eps):
    if rope_chunks:
        x_ref, xp_ref, xn_ref, g_ref, sh_ref, sc_ref, w_ref, cw_ref, cb_ref, cos_ref, sin_ref, q_o, o0, o1, o2, h_ref = refs
    else:
        x_ref, xp_ref, xn_ref, g_ref, sh_ref, sc_ref, w_ref, cw_ref, cb_ref, q_o, o0, o1, o2, h_ref = refs
    j = pl.program_id(2)
    tm = x_ref.shape[1]

    @pl.when(j == 0)
    def _():
        _halo_prologue(h_ref, x_ref, xp_ref, xn_ref, g_ref, sh_ref, sc_ref, eps)

    qkv_dot = lambda: jnp.dot(h_ref[HALO:HALO + tm, :], w_ref[...].astype(BF16), preferred_element_type=F32)
    nch = w_ref.shape[1] // LANES
    rope_tiles = sorted({ch // nch for ch in rope_chunks})
    for jt in rope_tiles:
        @pl.when(j == jt)
        def _(jt=jt):
            acc = qkv_dot()
            c, s = cos_ref[...], sin_ref[...]
            for k in range(nch):
                a = acc[:, k * LANES:(k + 1) * LANES]
                if jt * nch + k in rope_chunks:
                    a = a * c + _swap32(a) * s
                q_o[0, :, k * LANES:(k + 1) * LANES] = a.astype(q_o.dtype)

    @pl.when(functools.reduce(jnp.logical_and, [j != jt for jt in rope_tiles], j < nq))
    def _():
        q_o[0] = qkv_dot().astype(q_o.dtype)

    for k, o in enumerate((o0, o1, o2)):
        @pl.when(jnp.logical_and(j >= nq + k * nc, j < nq + (k + 1) * nc))
        def _(o=o):
            a = jnp.dot(h_ref[...], w_ref[...].astype(BF16), preferred_element_type=F32)
            o[0] = _conv3_halo(a, cw_ref, cb_ref, tm, period)


def _colmap(segs, tn):
    bounds, o = [], 0
    for s, w in segs:
        assert s % tn == 0 and w % tn == 0
        bounds.append(((o + w) // tn, (s - o) // tn))
        o += w

    def f(j):
        r = j + bounds[-1][1]
        for hi, off in reversed(bounds[:-1]):
            r = jnp.where(j < hi, j + off, r)
        return r
    return f, o


def _seg_tile(segs, cands):
    return _tile(functools.reduce(math.gcd, [v for seg in segs for v in seg if v]), cands)


def _in_proj(x, gain, shift, scale, w_in, l, qkv_segs, hy_start, C, conv_w, conv_b, rope, name, period=None):
    B, S, D = x.shape
    tm = _tile(S, (1024, 512, 256, 128))
    segs = list(qkv_segs) + [(hy_start, 3 * C)]
    tn = _seg_tile(segs + [(0, C)], (512, 256, 128))
    cmap, n_out = _colmap(segs, tn)
    nc = C // tn
    nq = n_out // tn - 3 * nc
    in_specs = _halo_specs(S, tm, D) + [
        pl.BlockSpec((None, D, tn), lambda b, i, j: (l, 0, cmap(j))),
        pl.BlockSpec((3, tn), lambda b, i, j: (0, jnp.clip(j - nq, 0, 3 * nc - 1))),
        pl.BlockSpec((1, tn), lambda b, i, j: (0, jnp.clip(j - nq, 0, 3 * nc - 1)))]
    args = [x, x, x, gain.reshape(1, D), shift, scale, w_in, conv_w, conv_b.reshape(1, 3 * C)]
    rope_chunks = ()
    if rope is not None:
        cos_t, sin_t, rope_chunks = rope
        in_specs += [pl.BlockSpec((tm, LANES), lambda b, i, j: (i, 0)),
                     pl.BlockSpec((tm, LANES), lambda b, i, j: (i, 0))]
        args += [cos_t, sin_t]
    hy_spec = lambda k: pl.BlockSpec((1, tm, tn), lambda b, i, j: (b, i, jnp.clip(j - nq - k * nc, 0, nc - 1)))
    return _call(
        functools.partial(_inproj_body, nq=nq, nc=nc, rope_chunks=frozenset(rope_chunks), period=period, eps=NORM_EPS),
        grid=(B, S // tm, n_out // tn), in_specs=in_specs,
        out_specs=[pl.BlockSpec((1, tm, tn), lambda b, i, j: (b, i, jnp.minimum(j, nq - 1))),
                   hy_spec(0), hy_spec(1), hy_spec(2)],
        out_shape=[jax.ShapeDtypeStruct((B, S, nq * tn), BF16)] + [jax.ShapeDtypeStruct((B, S, C), F32)] * 3,
        scratch=[pltpu.VMEM((tm + 2 * HALO, D), BF16)],
        sem=("parallel", "parallel", "arbitrary"), name=name)(*args)


def _ffnup_body(x_ref, xp_ref, xn_ref, g_ref, sh_ref, sc_ref, wa_ref, wu_ref, cw_ref, cb_ref, o_ref, h_ref, *, period, eps):
    tm = x_ref.shape[1]

    @pl.when(pl.program_id(2) == 0)
    def _():
        _halo_prologue(h_ref, x_ref, xp_ref, xn_ref, g_ref, sh_ref, sc_ref, eps)

    a = jnp.dot(h_ref[...], wa_ref[...].astype(BF16), preferred_element_type=F32)
    u = jnp.dot(h_ref[HALO:HALO + tm, :], wu_ref[...].astype(BF16), preferred_element_type=F32)
    c = _conv3_halo(a, cw_ref, cb_ref, tm, period)
    o_ref[0] = (c * _sigmoid(c) * u).astype(o_ref.dtype)


def _ffn_up_glu(x, gain, shift, scale, w_up, l, conv_w, conv_b, period=None):
    B, S, D = x.shape
    Fd = w_up.shape[2] // 2
    tm = _tile(S, (1024, 512, 256, 128))
    tn = _tile(Fd, (512, 256, 128))
    nj = Fd // tn
    return _call(
        functools.partial(_ffnup_body, period=period, eps=NORM_EPS), grid=(B, S // tm, nj),
        in_specs=_halo_specs(S, tm, D) + [
            pl.BlockSpec((None, D, tn), lambda b, i, j: (l, 0, j)),
            pl.BlockSpec((None, D, tn), lambda b, i, j: (l, 0, nj + j)),
            pl.BlockSpec((3, tn), lambda b, i, j: (0, j)),
            pl.BlockSpec((1, tn), lambda b, i, j: (0, j))],
        out_specs=pl.BlockSpec((1, tm, tn), lambda b, i, j: (b, i, j)),
        out_shape=jax.ShapeDtypeStruct((B, S, Fd), BF16),
        scratch=[pltpu.VMEM((tm + 2 * HALO, D), BF16)],
        sem=("parallel", "parallel", "arbitrary"), name="ffn_up_glu")(
            x, x, x, gain.reshape(1, D), shift, scale, w_up, w_up, conv_w, conv_b.reshape(1, Fd))


def _filt_trunk_body(z_ref, w1_ref, b1_ref, fr_ref, w2_ref, b2_ref, o_ref):
    a = jnp.dot(z_ref[...], w1_ref[...], precision=HP, preferred_element_type=F32) + b1_ref[...]
    a = jnp.sin(fr_ref[0:1, :] * a)
    a = jnp.dot(a, w2_ref[...], precision=HP, preferred_element_type=F32) + b2_ref[...]
    o_ref[...] = jnp.sin(fr_ref[1:2, :] * a)


def _filt_main_body(a_ref, wf_ref, wb_ref, t_ref, dl_ref, o_ref, *, n):
    hf = jnp.dot(a_ref[0:n, :], wf_ref[...], precision=HP, preferred_element_type=F32)
    hb = jnp.dot(a_ref[n:2 * n, :], wb_ref[...], precision=HP, preferred_element_type=F32)
    r = lax.broadcasted_iota(jnp.int32, hb.shape, 0)
    k = jnp.concatenate([hf, jnp.where(r == 0, 0.0, hb)], axis=0) * jnp.exp(-t_ref[...] * dl_ref[...])
    o_ref[...] = k / jnp.sum(jnp.abs(k), axis=0, keepdims=True)


def _hyena_filters(n, w1, b1, freq, w2, b2, w3):
    Hd = w1.shape[1]
    OC = w3.shape[1] // 2
    t = jnp.linspace(0.0, 1.0, n, dtype=F32)[:, None]
    w = (2.0 * math.pi / n) * jnp.arange(n, dtype=F32)[:, None]
    f = jnp.linspace(1e-4, HY_BANDS - 1, HY_BANDS, dtype=F32)[None, :]
    z = jnp.concatenate([t, jnp.cos(f * w), -jnp.sin(f * w)], axis=-1)
    fold = lambda a: jnp.concatenate([a, jnp.zeros_like(a[:1]), a[:0:-1]], axis=0)
    EP = 64
    z2 = jnp.pad(fold(z), ((0, 0), (0, EP - HY_EMB)))
    w1p = jnp.pad(w1, ((0, EP - HY_EMB), (0, 0)))
    rt = _tile(2 * n, (1024, 512, 256))
    a2 = _call(
        _filt_trunk_body, grid=(2 * n // rt,),
        in_specs=[pl.BlockSpec((rt, EP), lambda i: (i, 0)),
                  pl.BlockSpec((EP, Hd), lambda i: (0, 0)),
                  pl.BlockSpec((1, Hd), lambda i: (0, 0)),
                  pl.BlockSpec((2, Hd), lambda i: (0, 0)),
                  pl.BlockSpec((Hd, Hd), lambda i: (0, 0)),
                  pl.BlockSpec((1, Hd), lambda i: (0, 0))],
        out_specs=pl.BlockSpec((rt, Hd), lambda i: (i, 0)),
        out_shape=jax.ShapeDtypeStruct((2 * n, Hd), F32),
        sem=("parallel",), name="hyena_filter_trunk")(z2, w1p, b1.reshape(1, Hd), freq, w2, b2.reshape(1, Hd))
    tc = LANES
    t2 = jnp.broadcast_to(fold(t), (2 * n, tc))
    deltas = jnp.abs(jnp.linspace(HY_MIN_DECAY, HY_MAX_DECAY, OC, dtype=F32))[None, :]
    nc = OC // tc
    return _call(
        functools.partial(_filt_main_body, n=n), grid=(nc,),
        in_specs=[pl.BlockSpec((2 * n, Hd), lambda c: (0, 0)),
                  pl.BlockSpec((Hd, tc), lambda c: (0, c)),
                  pl.BlockSpec((Hd, tc), lambda c: (0, nc + c)),
                  pl.BlockSpec((2 * n, tc), lambda c: (0, 0)),
                  pl.BlockSpec((1, tc), lambda c: (0, c))],
        out_specs=pl.BlockSpec((2 * n, tc), lambda c: (0, c)),
        out_shape=jax.ShapeDtypeStruct((2 * n, OC), F32),
        sem=("parallel",), name="hyena_filter")(a2, w3, w3, t2, deltas)


def _embed(re, im):
    return np.block([[re, -im], [im, re]])


@functools.lru_cache(maxsize=None)
def _fft_tables(N1, N2):
    N = N1 * N2
    S1 = N1 // 2
    i1 = np.arange(N1)
    ang = -2.0 * np.pi * ((i1[:, None] * i1[None, :]) % N1) / N1
    fr, fi = np.cos(ang), np.sin(ang)
    f1_pair = _embed(fr[:, :S1], fi[:, :S1])
    f1_real = np.concatenate([fr, fi], axis=0)
    i2 = np.arange(N2)
    fidx = i1[:, None, None] + N1 * i2[None, :, None]
    ang = -2.0 * np.pi * ((fidx * i2[None, None, :]) % N) / N
    mr, mi = np.cos(ang), np.sin(ang)
    m_fwd = np.stack([_embed(mr[a], mi[a]) for a in range(N1)])
    m_inv = np.stack([_embed(mr[a].T, -mi[a].T) for a in range(N1)])
    ang = 2.0 * np.pi * ((i1[:S1, None] * i1[None, :]) % N1) / N1
    f1_inv = _embed(np.cos(ang) / N, np.sin(ang) / N)
    cvt = lambda a: jnp.asarray(a, dtype=BF16)
    return cvt(f1_pair), cvt(f1_real), cvt(m_fwd), cvt(m_inv), cvt(f1_inv)


def _lmat_body(f_ref, x_ref, o_ref):
    o_ref[0] = jnp.dot(f_ref[...], x_ref[0].astype(BF16), preferred_element_type=F32).astype(o_ref.dtype)


def _left_matmul(fm, x, out_dtype, name):
    P, K, W = x.shape
    R = fm.shape[0]
    tw = _tile(W, (4096, 2048, 1024, 512, 256, 128))
    return _call(
        _lmat_body, grid=(P, W // tw),
        in_specs=[pl.BlockSpec((R, K), lambda p, j: (0, 0)),
                  pl.BlockSpec((1, K, tw), lambda p, j: (p, 0, j))],
        out_specs=pl.BlockSpec((1, R, tw), lambda p, j: (p, 0, j)),
        out_shape=jax.ShapeDtypeStruct((P, R, W), out_dtype),
        sem=("parallel", "parallel"), name=name)(fm, x)


def _cmul(xr, xi, kr, ki):
    return xr * kr - xi * ki, xr * ki + xi * kr


def _fftmid_body(a_ref, m_ref, mi_ref, k_ref, o_ref, *, FB, N2):
    for t in range(FB):
        a = a_ref[0, :, t].reshape(2 * N2, a_ref.shape[-1])
        x = jnp.dot(m_ref[t], a, preferred_element_type=F32)
        yr, yi = _cmul(x[:N2], x[N2:], k_ref[0, t].astype(F32), k_ref[1, t].astype(F32))
        y = jnp.concatenate([yr, yi], axis=0).astype(BF16)
        g = jnp.dot(mi_ref[t], y, preferred_element_type=F32)
        o_ref[0, :, t] = g.reshape(2, N2, g.shape[-1]).astype(o_ref.dtype)


def _fftfwd_body(a_ref, m_ref, o_ref, *, FB, N2):
    for t in range(FB):
        a = a_ref[:, t].reshape(2 * N2, a_ref.shape[-1])
        x = jnp.dot(m_ref[t], a, preferred_element_type=F32)
        o_ref[:, t] = x.reshape(2, N2, x.shape[-1]).astype(o_ref.dtype)


SUB_BLOCK = 16


def _dft1_body(f_ref, x_ref, o_ref):
    xt = pltpu.einshape("ksc->skc", x_ref[0])
    r = jnp.stack([jnp.dot(f_ref[...], xt[k].astype(BF16), preferred_element_type=F32)
                   for k in range(xt.shape[0])], axis=0)
    o_ref[0] = pltpu.einshape("skc->ksc", r).astype(o_ref.dtype)


def _outer_dft(fm, x4, out_dtype, name):
    P, K, N2, C = x4.shape
    R = fm.shape[0]
    tc = _tile(C, (512, 256, 128))
    return _call(
        _dft1_body, grid=(P, N2 // SUB_BLOCK, C // tc),
        in_specs=[pl.BlockSpec((R, K), lambda p, s, c: (0, 0)),
                  pl.BlockSpec((1, K, SUB_BLOCK, tc), lambda p, s, c: (p, 0, s, c))],
        out_specs=pl.BlockSpec((1, R, SUB_BLOCK, tc), lambda p, s, c: (p, 0, s, c)),
        out_shape=jax.ShapeDtypeStruct((P, R, N2, C), out_dtype),
        sem=("parallel", "parallel", "parallel"), name=name)(fm, x4)


def _idft1_gate_body(f_ref, g_ref, z_ref, x1_ref, sk_ref, o_ref):
    gt = pltpu.einshape("ksc->skc", g_ref[0])
    y = jnp.stack([jnp.dot(f_ref[...], gt[k], preferred_element_type=F32) for k in range(gt.shape[0])], axis=0)
    y = pltpu.einshape("skc->ksc", y)
    o_ref[0] = (x1_ref[0] * (y + sk_ref[...] * z_ref[0])).astype(o_ref.dtype)


def _filter_spectrum_2stage(k, N1, N2):
    N, OC = k.shape
    _, f1_real, m_fwd, _, _ = _fft_tables(N1, N2)
    a = _outer_dft(f1_real, k.reshape(1, N1, N2, OC), BF16, "filter_dft1").reshape(2, N1, N2, OC)
    FB = _tile(N1, (8, 4, 2, 1))
    tc = _tile(OC, (512, 256, 128))
    return _call(
        functools.partial(_fftfwd_body, FB=FB, N2=N2), grid=(N1 // FB, OC // tc),
        in_specs=[pl.BlockSpec((2, FB, N2, tc), lambda f, c: (0, f, 0, c)),
                  pl.BlockSpec((FB, 2 * N2, 2 * N2), lambda f, c: (f, 0, 0))],
        out_specs=pl.BlockSpec((2, FB, N2, tc), lambda f, c: (0, f, 0, c)),
        out_shape=jax.ShapeDtypeStruct((2, N1, N2, OC), BF16),
        sem=("parallel", "parallel"), name="filter_dft2")(a, m_fwd)


def _long_conv_gate_2stage(z, x1, skip, kf, order, out_dtype, N1, N2):
    B, n, C = z.shape
    assert B % 2 == 0
    P, S1 = B // 2, N1 // 2
    f1_pair, _, m_fwd, m_inv, f1_inv = _fft_tables(N1, N2)
    z4 = z.reshape(P, 2 * S1, N2, C)
    a = _outer_dft(f1_pair, z4, BF16, "conv_dft1").reshape(P, 2, N1, N2, C)
    FB = _tile(N1, (8, 4, 2, 1))
    tc = _tile(C, (512, 256, 128))
    oc = order * (C // tc)
    g = _call(
        functools.partial(_fftmid_body, FB=FB, N2=N2), grid=(N1 // FB, C // tc, P),
        in_specs=[pl.BlockSpec((1, 2, FB, N2, tc), lambda f, c, p: (p, 0, f, 0, c)),
                  pl.BlockSpec((FB, 2 * N2, 2 * N2), lambda f, c, p: (f, 0, 0)),
                  pl.BlockSpec((FB, 2 * N2, 2 * N2), lambda f, c, p: (f, 0, 0)),
                  pl.BlockSpec((2, FB, N2, tc), lambda f, c, p: (0, f, 0, oc + c))],
        out_specs=pl.BlockSpec((1, 2, FB, N2, tc), lambda f, c, p: (p, 0, f, 0, c)),
        out_shape=jax.ShapeDtypeStruct((P, 2, N1, N2, C), BF16),
        sem=("parallel", "parallel", "arbitrary"), name="conv_dft2_mul_idft2")(a, m_fwd, m_inv, kf)
    blk = lambda rows: pl.BlockSpec((1, rows, SUB_BLOCK, tc), lambda p, s, c: (p, 0, s, c))
    out = _call(
        _idft1_gate_body, grid=(P, N2 // SUB_BLOCK, C // tc),
        in_specs=[pl.BlockSpec((2 * S1, 2 * N1), lambda p, s, c: (0, 0)),
                  blk(2 * N1), blk(2 * S1), blk(2 * S1),
                  pl.BlockSpec((1, tc), lambda p, s, c: (0, c))],
        out_specs=blk(2 * S1),
        out_shape=jax.ShapeDtypeStruct((P, 2 * S1, N2, C), out_dtype),
        sem=("parallel", "parallel", "parallel"), name="conv_idft1_gate")(
            f1_inv, g.reshape(P, 2 * N1, N2, C), z4, x1.reshape(P, 2 * S1, N2, C), skip[None, :])
    return out.reshape(B, n, C)


@functools.lru_cache(maxsize=None)
def _dft_tables(n):
    N = 2 * n
    f = np.arange(N)
    ang = -2.0 * np.pi * ((f[:, None] * f[None, :]) % N) / N
    fr, fi = np.cos(ang), np.sin(ang)
    fwd_full = np.concatenate([fr, fi], axis=0)
    fwd_half = fwd_full[:, :n]
    inv = np.concatenate([fr[:n, :], fi[:n, :]], axis=1) / N
    cvt = lambda a: jnp.asarray(a, dtype=BF16)
    return cvt(fwd_full), cvt(fwd_half), cvt(inv)


def _dftconv_body(z_ref, x1_ref, sk_ref, f_ref, fi_ref, k_ref, o_ref, *, N):
    z = z_ref[0]
    x = jnp.dot(f_ref[...], z.astype(BF16), preferred_element_type=F32)
    yr, yi = _cmul(x[:N], x[N:], k_ref[0], k_ref[1])
    y = jnp.concatenate([yr, yi], axis=0).astype(BF16)
    y = jnp.dot(fi_ref[...], y, preferred_element_type=F32)
    o_ref[0] = (x1_ref[0] * (y + sk_ref[...] * z)).astype(o_ref.dtype)


def _long_conv_gate_dense(z, x1, skip, kf, order, out_dtype):
    B, n, C = z.shape
    N = 2 * n
    _, fwd_half, inv = _dft_tables(n)
    tc = _tile(C, (256, 128))
    oc = order * (C // tc)
    return _call(
        functools.partial(_dftconv_body, N=N), grid=(C // tc, B),
        in_specs=[pl.BlockSpec((1, n, tc), lambda c, b: (b, 0, c)),
                  pl.BlockSpec((1, n, tc), lambda c, b: (b, 0, c)),
                  pl.BlockSpec((1, tc), lambda c, b: (0, c)),
                  pl.BlockSpec((2 * N, n), lambda c, b: (0, 0)),
                  pl.BlockSpec((n, 2 * N), lambda c, b: (0, 0)),
                  pl.BlockSpec((2, N, tc), lambda c, b: (0, 0, oc + c))],
        out_specs=pl.BlockSpec((1, n, tc), lambda c, b: (b, 0, c)),
        out_shape=jax.ShapeDtypeStruct((B, n, C), out_dtype),
        sem=("parallel", "arbitrary"), name="conv_dense_dft")(z, x1, skip[None, :], fwd_half, inv, kf)


def _hyena_mixer(v, x1, x2, w1, b1, freq, w2, b2, w3, skip):
    B, n, _ = v.shape
    k = _hyena_filters(n, w1, b1, freq, w2, b2, w3)
    N = 2 * n
    if N % FFT_N2 == 0 and (N // FFT_N2) >= 16:
        N1 = N // FFT_N2
        kf = _filter_spectrum_2stage(k, N1, FFT_N2)
        z = _long_conv_gate_2stage(v, x1, skip[0], kf, 0, F32, N1, FFT_N2)
        return _long_conv_gate_2stage(z, x2, skip[1], kf, 1, BF16, N1, FFT_N2)
    fwd_full, _, _ = _dft_tables(n)
    kf = _left_matmul(fwd_full, k[None], F32, "filter_dense_dft").reshape(2, N, k.shape[1])
    z = _long_conv_gate_dense(v, x1, skip[0], kf, 0, F32)
    return _long_conv_gate_dense(z, x2, skip[1], kf, 1, BF16)


def _nt(a, b):
    return lax.dot_general(a, b, (((1,), (1,)), ((), ())), preferred_element_type=F32)


def _na_body(q_ref, k_ref, v_ref, kc_ref, vc_ref, b_ref, o_ref, *, R, KR, rows, HPS, scale):
    j = pl.program_id(2)
    start = pl.multiple_of(jnp.clip(j * R - NA_WIN_R // 2, 0, rows - KR) * GRID_W, GRID_W)
    for h in range(HPS):
        hs = slice(h * HEAD_DIM, (h + 1) * HEAD_DIM)
        q = q_ref[0, :, hs]
        kw = k_ref[0, pl.ds(start, KR * GRID_W), hs]
        vw = v_ref[0, pl.ds(start, KR * GRID_W), hs]
        s = _nt(q, kw) + b_ref[h, 0]
        sc = _nt(q, kc_ref[0, :, hs])
        m = jnp.maximum(jnp.max(s, axis=-1, keepdims=True), jnp.max(sc, axis=-1, keepdims=True))
        p = jnp.exp2((s - m) * (scale * LOG2_E))
        pc = jnp.exp2((sc - m) * (scale * LOG2_E))
        l = jnp.sum(p, axis=-1, keepdims=True) + jnp.sum(pc, axis=-1, keepdims=True)
        o = jnp.dot(p.astype(BF16), vw, preferred_element_type=F32)
        o = o + jnp.dot(pc.astype(BF16), vc_ref[0, :, hs], preferred_element_type=F32)
        o_ref[0, :, hs] = (o / l).astype(o_ref.dtype)


def _na_geometry(S):
    rows = S // GRID_W
    kr = min(NA_WIN_R, rows)
    R = min(8, rows)
    KR = min(rows, R + kr)
    nb = rows // R
    assert rows % R == 0
    types = sorted({0, min(1, nb - 1), nb - 1})
    if nb > 3:
        offs = {int(np.clip(j * R - NA_WIN_R // 2, 0, rows - KR)) - j * R for j in range(1, nb - 1)}
        assert len(offs) == 1
    return rows, kr, R, KR, nb, types


def _nabias_body(rpb_ref, o_ref, tw_ref, *, plan, R, KR, inv_scale):
    W = GRID_W
    nd_r, nd_c = 2 * NA_WIN_R - 1, 2 * NA_WIN_C - 1
    base = pl.program_id(0) * (nd_r * nd_c)
    qc = lax.broadcasted_iota(jnp.int32, (W, 2 * W), 0)
    lane = lax.broadcasted_iota(jnp.int32, (W, 2 * W), 1)
    kc = lane % W
    cs = jnp.clip(qc - NA_WIN_C // 2, 0, W - NA_WIN_C)
    col_ok = jnp.logical_and(kc >= cs, kc < cs + NA_WIN_C)
    dcm = kc - qc + (NA_WIN_C - 1)
    neg = jnp.full((W, 2 * W), NEG_INF, F32)
    for dr in range(nd_r):
        acc = neg
        for dc in range(nd_c):
            acc = jnp.where(dcm == dc, rpb_ref[base + dr * nd_c + dc] * inv_scale, acc)
        tw_ref[dr] = jnp.where(col_ok, acc, NEG_INF)
    left = lane < W
    for t, per_q in enumerate(plan):
        for qr in range(R):
            for kp in range(KR // 2):
                d0, d1 = per_q[qr][kp]
                a = neg if d0 is None else tw_ref[d0]
                b = neg if d1 is None else tw_ref[d1]
                blk = neg if (d0 is None and d1 is None) else jnp.where(left, a, b)
                o_ref[0, t, qr * W:(qr + 1) * W, kp * 2 * W:(kp + 1) * 2 * W] = blk


def _na_bias_tables(rpb, S):
    rows, kr, R, KR, nb, types = _na_geometry(S)
    assert 2 * GRID_W == LANES and KR % 2 == 0
    plan = []
    for jt in types:
        start = int(np.clip(jt * R - NA_WIN_R // 2, 0, rows - KR))
        per_q = []
        for q in range(R):
            qra = jt * R + q
            ws = int(np.clip(qra - kr // 2, 0, rows - kr))
            d = [(start + k) - qra + (NA_WIN_R - 1) if ws <= start + k < ws + kr else None for k in range(KR)]
            per_q.append([(d[2 * p], d[2 * p + 1]) for p in range(KR // 2)])
        plan.append(per_q)
    L, H, nd_r, nd_c = rpb.shape
    T, QB, KB = len(types), R * GRID_W, KR * GRID_W
    return _call(
        functools.partial(_nabias_body, plan=plan, R=R, KR=KR, inv_scale=HEAD_DIM ** 0.5), grid=(L * H,),
        in_specs=[pl.BlockSpec(memory_space=pltpu.SMEM)],
        out_specs=pl.BlockSpec((1, T, QB, KB), lambda i: (i, 0, 0, 0)),
        out_shape=jax.ShapeDtypeStruct((L * H, T, QB, KB), F32),
        scratch=[pltpu.VMEM((nd_r, GRID_W, 2 * GRID_W), F32)],
        sem=("parallel",), name="na_bias_table")(rpb.reshape(-1).astype(F32))


def _na_attention(qkv, qkv_c, bias, l, offs, H):
    B, S, _ = qkv.shape
    CTX = qkv_c.shape[1]
    rows, kr, R, KR, nb, types = _na_geometry(S)
    T = len(types)
    QB, KB = R * GRID_W, KR * GRID_W
    HPS = next(n for n in (4, 2, 1) if H % n == 0)
    HW = HPS * HEAD_DIM
    assert all(offs[n] % HW == 0 for n in ("na_k", "na_v", "na_q"))
    ok_, ov_, oq_ = (offs[n] // HW for n in ("na_k", "na_v", "na_q"))

    def btype(j):
        if T == nb:
            return j
        return jnp.where(j == 0, 0, jnp.where(j == nb - 1, T - 1, 1))

    return _call(
        functools.partial(_na_body, R=R, KR=KR, rows=rows, HPS=HPS, scale=HEAD_DIM ** -0.5),
        grid=(B, H // HPS, nb),
        in_specs=[pl.BlockSpec((1, QB, HW), lambda b, h, j: (b, j, oq_ + h)),
                  pl.BlockSpec((1, S, HW), lambda b, h, j: (b, 0, ok_ + h)),
                  pl.BlockSpec((1, S, HW), lambda b, h, j: (b, 0, ov_ + h)),
                  pl.BlockSpec((1, CTX, HW), lambda b, h, j: (b, 0, ok_ + h)),
                  pl.BlockSpec((1, CTX, HW), lambda b, h, j: (b, 0, ov_ + h)),
                  pl.BlockSpec((HPS, 1, QB, KB), lambda b, h, j: (l * (H // HPS) + h, btype(j), 0, 0))],
        out_specs=pl.BlockSpec((1, QB, HW), lambda b, h, j: (b, j, h)),
        out_shape=jax.ShapeDtypeStruct((B, S, H * HEAD_DIM), BF16),
        sem=("parallel", "parallel", "arbitrary"), name="na_attention")(qkv, qkv, qkv, qkv_c, qkv_c, bias)


def _stack_heads(q2, G):
    return jnp.concatenate([q2[:, g * HEAD_DIM:(g + 1) * HEAD_DIM] for g in range(G)], axis=0)


def _unstack_heads(o, G, n):
    return jnp.concatenate([o[g * n:(g + 1) * n] for g in range(G)], axis=1)


def _sink_column(sink_ref, h0, G, n):
    return jnp.concatenate([jnp.full((n, 1), sink_ref[h0 + g], F32) for g in range(G)], axis=0)


def _swa_body(sink_ref, q_ref, k_ref, v_ref, kc_ref, vc_ref, o_ref, *, QB, KB, S, G, HPS, scale):
    hb, j = pl.program_id(1), pl.program_id(2)
    start = pl.multiple_of(jnp.clip(j * QB - GQA_WINDOW, 0, S - KB), LANES)
    qpos = j * QB + lax.broadcasted_iota(jnp.int32, (QB, KB), 0)
    kpos = start + lax.broadcasted_iota(jnp.int32, (QB, KB), 1)
    mask = jnp.where(jnp.abs(qpos - kpos) <= GQA_WINDOW, 0.0, NEG_INF)
    mask = jnp.concatenate([mask] * G, axis=0)
    for h in range(HPS):
        hs = slice(h * HEAD_DIM, (h + 1) * HEAD_DIM)
        qs = slice(h * G * HEAD_DIM, (h + 1) * G * HEAD_DIM)
        q = _stack_heads(q_ref[0, :, qs], G)
        kw = k_ref[0, pl.ds(start, KB), hs]
        vw = v_ref[0, pl.ds(start, KB), hs]
        s = _nt(q, kw) + mask
        sc = _nt(q, kc_ref[0, :, hs])
        sk = _sink_column(sink_ref, (hb * HPS + h) * G, G, QB) * (1.0 / scale)
        m = jnp.maximum(jnp.maximum(jnp.max(s, axis=-1, keepdims=True), jnp.max(sc, axis=-1, keepdims=True)), sk)
        p = jnp.exp2((s - m) * (scale * LOG2_E))
        pc = jnp.exp2((sc - m) * (scale * LOG2_E))
        l = (jnp.sum(p, axis=-1, keepdims=True) + jnp.sum(pc, axis=-1, keepdims=True)
             + jnp.exp2((sk - m) * (scale * LOG2_E)))
        o = jnp.dot(p.astype(BF16), vw, preferred_element_type=F32)
        o = o + jnp.dot(pc.astype(BF16), vc_ref[0, :, hs], preferred_element_type=F32)
        o_ref[0, :, qs] = _unstack_heads(o / l, G, QB).astype(o_ref.dtype)


def _swa_attention(qkv, qkv_c, sink, offs, KVH):
    B, S, _ = qkv.shape
    CTX = qkv_c.shape[1]
    G = GQA_GROUP
    QB = _tile(S, (512, 256, 128))
    KB = min(S, QB + 2 * GQA_WINDOW)
    HPS = 2 if KVH % 2 == 0 else 1
    HW = HPS * HEAD_DIM
    assert offs["sw_k"] % HW == 0 and offs["sw_v"] % HW == 0 and offs["sw_q"] % (G * HW) == 0
    ok_, ov_, oq_ = offs["sw_k"] // HW, offs["sw_v"] // HW, offs["sw_q"] // (G * HW)
    return _call(
        functools.partial(_swa_body, QB=QB, KB=KB, S=S, G=G, HPS=HPS, scale=HEAD_DIM ** -0.5),
        grid=(B, KVH // HPS, S // QB),
        in_specs=[pl.BlockSpec(memory_space=pltpu.SMEM),
                  pl.BlockSpec((1, QB, G * HW), lambda b, h, j: (b, j, oq_ + h)),
                  pl.BlockSpec((1, S, HW), lambda b, h, j: (b, 0, ok_ + h)),
                  pl.BlockSpec((1, S, HW), lambda b, h, j: (b, 0, ov_ + h)),
                  pl.BlockSpec((1, CTX, HW), lambda b, h, j: (b, 0, ok_ + h)),
                  pl.BlockSpec((1, CTX, HW), lambda b, h, j: (b, 0, ov_ + h))],
        out_specs=pl.BlockSpec((1, QB, G * HW), lambda b, h, j: (b, j, h)),
        out_shape=jax.ShapeDtypeStruct((B, S, KVH * G * HEAD_DIM), BF16),
        sem=("parallel", "parallel", "arbitrary"), name="swa_attention")(sink, qkv, qkv, qkv, qkv_c, qkv_c)


def _cattn_body(sink_ref, q_ref, k_ref, v_ref, o_ref, *, G, use_sink, scale):
    h = pl.program_id(1)
    n = q_ref.shape[1]
    q = _stack_heads(q_ref[0], G)
    s = _nt(q, k_ref[0]) * scale
    m = jnp.max(s, axis=-1, keepdims=True)
    if use_sink:
        sk = _sink_column(sink_ref, h * G, G, n)
        m = jnp.maximum(m, sk)
    p = jnp.exp(s - m)
    l = jnp.sum(p, axis=-1, keepdims=True)
    if use_sink:
        l = l + jnp.exp(sk - m)
    o = jnp.dot(p.astype(BF16), v_ref[0], preferred_element_type=F32)
    o_ref[0] = _unstack_heads(o / l, G, n).astype(o_ref.dtype)


def _ctx_attention(qkv_c, sink, oq, ok, ov, KVH, G, use_sink):
    B, n, _ = qkv_c.shape
    oq_, ok_, ov_ = oq // (G * HEAD_DIM), ok // HEAD_DIM, ov // HEAD_DIM
    return _call(
        functools.partial(_cattn_body, G=G, use_sink=use_sink, scale=HEAD_DIM ** -0.5),
        grid=(B, KVH),
        in_specs=[pl.BlockSpec(memory_space=pltpu.SMEM),
                  pl.BlockSpec((1, n, G * HEAD_DIM), lambda b, h: (b, 0, oq_ + h)),
                  pl.BlockSpec((1, n, HEAD_DIM), lambda b, h: (b, 0, ok_ + h)),
                  pl.BlockSpec((1, n, HEAD_DIM), lambda b, h: (b, 0, ov_ + h))],
        out_specs=pl.BlockSpec((1, n, G * HEAD_DIM), lambda b, h: (b, 0, h)),
        out_shape=jax.ShapeDtypeStruct((B, n, KVH * G * HEAD_DIM), BF16),
        sem=("parallel", "parallel"), name="ctx_attention")(sink, qkv_c, qkv_c, qkv_c)


def _merge_body(x_ref, g_ref, sh_ref, sc_ref, yh, yn, ys, wgh, wgn, wgs, wh, wn, ws, o_ref, h_ref, *, eps):
    @pl.when(pl.program_id(2) == 0)
    def _():
        h_ref[...] = _norm_modulate(x_ref[0], g_ref, sh_ref, sc_ref, eps).astype(BF16)

    h = h_ref[...]
    gate = lambda wg: _sigmoid(jnp.dot(h, wg[...].astype(BF16), preferred_element_type=F32))
    m = gate(wgh) * jnp.dot(yh[0], wh[...], preferred_element_type=F32)
    m = m + gate(wgn) * jnp.dot(yn[0], wn[...], preferred_element_type=F32)
    m = m + gate(wgs) * jnp.dot(ys[0], ws[...], preferred_element_type=F32)
    o_ref[0] = m.astype(o_ref.dtype)


def _merge_branches(x, gain, shift, scale, w_in, gate_start, y_hy, y_na, y_sw, w_br, l):
    B, S, D = x.shape
    widths = (y_hy.shape[2], y_na.shape[2], y_sw.shape[2])
    starts = (0, widths[0], widths[0] + widths[1])
    assert all(s % w == 0 for s, w in zip(starts, widths))
    tm = _tile(S, (512, 256, 128))
    tn = _seg_tile([(gate_start, D)], (512, 256, 128))
    nj, gj = D // tn, gate_start // tn
    yspec = lambda y: pl.BlockSpec((1, tm, y.shape[2]), lambda b, i, j: (b, i, 0))
    gspec = lambda k: pl.BlockSpec((None, D, tn), lambda b, i, j: (l, 0, gj + k * nj + j))
    wspec = lambda k: pl.BlockSpec((None, widths[k], tn), lambda b, i, j: (l, starts[k] // widths[k], j))
    return _call(
        functools.partial(_merge_body, eps=NORM_EPS), grid=(B, S // tm, nj),
        in_specs=[pl.BlockSpec((1, tm, D), lambda b, i, j: (b, i, 0)),
                  pl.BlockSpec((1, D), lambda b, i, j: (0, 0)),
                  pl.BlockSpec((1, 1, D), lambda b, i, j: (b, 0, 0)),
                  pl.BlockSpec((1, 1, D), lambda b, i, j: (b, 0, 0)),
                  yspec(y_hy), yspec(y_na), yspec(y_sw), gspec(0), gspec(1), gspec(2),
                  wspec(0), wspec(1), wspec(2)],
        out_specs=pl.BlockSpec((1, tm, tn), lambda b, i, j: (b, i, j)),
        out_shape=jax.ShapeDtypeStruct((B, S, D), BF16),
        scratch=[pltpu.VMEM((tm, D), BF16)],
        sem=("parallel", "parallel", "arbitrary"), name="gated_merge")(
            x, gain.reshape(1, D), shift, scale, y_hy, y_na, y_sw, w_in, w_in, w_in, w_br, w_br, w_br)


def _mmres_body(a_ref, w_ref, x_ref, g_ref, mg_ref, o_ref, *, nk, eps):
    k = pl.program_id(2)
    part = lambda: jnp.dot(a_ref[0], w_ref[...], preferred_element_type=F32)

    def finish():
        y = o_ref[0]
        yn = y * lax.rsqrt(jnp.mean(y * y, axis=-1, keepdims=True) + eps) * g_ref[...]
        o_ref[0] = x_ref[0] + mg_ref[0] * yn

    if nk == 1:
        o_ref[0] = part()
        finish()
        return

    @pl.when(k == 0)
    def _():
        o_ref[0] = part()

    @pl.when(k > 0)
    def _():
        o_ref[0] += part()

    @pl.when(k == nk - 1)
    def _():
        finish()


def _matmul_norm_residual(a, w, l, x, gain, mgate, name):
    B, S, K = a.shape
    D = w.shape[2]
    if K <= 2048:
        tm, tk = _tile(S, (512, 256, 128)), K
    else:
        tm = _tile(S, (1024, 512, 256, 128))
        tk = max(t for t in range(LANES, 1537, LANES) if K % t == 0)
    nk = K // tk
    est = 2 * (tm * tk * 2 + tk * D * 2 + 2 * tm * D * 4)
    return _call(
        functools.partial(_mmres_body, nk=nk, eps=NORM_EPS), grid=(B, S // tm, nk),
        vmem=VMEM_LIMIT_LARGE if est > VMEM_LIMIT - VMEM_TEMP_RESERVE else VMEM_LIMIT,
        in_specs=[pl.BlockSpec((1, tm, tk), lambda b, i, k: (b, i, k)),
                  pl.BlockSpec((None, tk, D), lambda b, i, k: (l, k, 0)),
                  pl.BlockSpec((1, tm, D), lambda b, i, k: (b, i, 0)),
                  pl.BlockSpec((1, D), lambda b, i, k: (0, 0)),
                  pl.BlockSpec((1, 1, D), lambda b, i, k: (b, 0, 0))],
        out_specs=pl.BlockSpec((1, tm, D), lambda b, i, k: (b, i, 0)),
        out_shape=jax.ShapeDtypeStruct((B, S, D), F32),
        sem=("parallel", "parallel", "arbitrary"), name=name)(a, w, x, gain.reshape(1, D), mgate)


def _rope_tables(n):
    t = jnp.arange(n)
    row = (t // GRID_W).astype(F32)
    col = (t % GRID_W).astype(F32)
    per_axis = HEAD_DIM // 2
    inv = ROPE_BASE ** (-jnp.arange(0, per_axis, 2, dtype=F32) / per_axis)
    ar, ac = row[:, None] * inv, col[:, None] * inv
    cos_t = jnp.concatenate([jnp.cos(ar), jnp.cos(ar), jnp.cos(ac), jnp.cos(ac)], axis=1)
    sin_t = jnp.concatenate([-jnp.sin(ar), jnp.sin(ar), -jnp.sin(ac), jnp.sin(ac)], axis=1)
    return cos_t, sin_t


def kernel(x, c, ctx, c_ctx, w_mod, b_mod, norm_gains, w_in, hy_conv_w, hy_conv_b, hy_w1, hy_b1, hy_freq, hy_w2, hy_b2, hy_w3, hy_skip, na_rpb, swa_sink, w_branch, w_out, ffn_w_up, ffn_conv_w, ffn_conv_b, ffn_w_down):
    B, S, D = x.shape
    L = w_mod.shape[0]
    C = hy_skip.shape[-1]
    H_na = na_rpb.shape[1]
    H_q = swa_sink.shape[1]
    KVH = H_q // GQA_GROUP
    NA_W, QW, KVW = H_na * HEAD_DIM, H_q * HEAD_DIM, KVH * HEAD_DIM
    KV_COLS = 2 * NA_W + 2 * KVW
    qkv_segs = [(0, KV_COLS), (KV_COLS + 3 * C, NA_W + QW)]
    gate_start = KV_COLS + 3 * C + NA_W + QW
    offs = {"na_k": 0, "na_v": NA_W, "sw_k": 2 * NA_W, "sw_v": 2 * NA_W + KVW,
            "na_q": KV_COLS, "sw_q": KV_COLS + NA_W}
    rope_chunks = (list(range(offs["sw_k"] // LANES, (offs["sw_k"] + KVW) // LANES))
                   + list(range(offs["sw_q"] // LANES, (offs["sw_q"] + QW) // LANES)))
    cos_t, sin_t = _rope_tables(S)

    w_out_b, w_br_b, w_dn_b = w_out.astype(BF16), w_branch.astype(BF16), ffn_w_down.astype(BF16)
    na_bias = _na_bias_tables(na_rpb, S)
    CTX = ctx.shape[1]
    flat = lambda t: t.reshape(1, -1, t.shape[-1])
    unflat = lambda t: t.reshape(B, CTX, t.shape[-1])

    R = -(-(B + 1) // 8) * 8
    cc = jnp.concatenate([c, c_ctx[None, :], jnp.zeros((R - B - 1, D), F32)], axis=0)
    mods = _modulation(cc, w_mod, b_mod)

    xc = ctx
    for l in range(L):
        mod = [mods[l, :B, k * D:(k + 1) * D].reshape(B, 1, D) for k in range(6)]
        mod_c = [mods[l, B, k * D:(k + 1) * D].reshape(1, 1, D) for k in range(6)]
        g = norm_gains[l]
        hy_p = (hy_w1[l], hy_b1[l], hy_freq[l], hy_w2[l], hy_b2[l], hy_w3[l], hy_skip[l])

        qkv_c, vc, x1c, x2c = map(unflat, _in_proj(flat(xc), g[0], mod_c[0], mod_c[1], w_in, l, qkv_segs, KV_COLS, C,
                                                   hy_conv_w[l], hy_conv_b[l], None, "ctx_in_proj", period=CTX))
        qkv, v, x1, x2 = _in_proj(x, g[0], mod[0], mod[1], w_in, l, qkv_segs, KV_COLS, C,
                                  hy_conv_w[l], hy_conv_b[l], (cos_t, sin_t, rope_chunks), "in_proj")
        y_hy = _hyena_mixer(v, x1, x2, *hy_p)
        y_na = _na_attention(qkv, qkv_c, na_bias, l, offs, H_na)
        y_sw = _swa_attention(qkv, qkv_c, swa_sink[l], offs, KVH)
        m = _merge_branches(x, g[0], mod[0], mod[1], w_in, gate_start, y_hy, y_na, y_sw, w_br_b, l)
        x = _matmul_norm_residual(m, w_out_b, l, x, g[1], mod[2], "out_proj_residual")
        gl = _ffn_up_glu(x, g[2], mod[3], mod[4], ffn_w_up, l, ffn_conv_w[l], ffn_conv_b[l])
        x = _matmul_norm_residual(gl, w_dn_b, l, x, g[3], mod[5], "ffn_down_residual")

        if l < L - 1:
            yc_hy = _hyena_mixer(vc, x1c, x2c, *hy_p)
            yc_na = _ctx_attention(qkv_c, swa_sink[l], offs["na_q"], offs["na_k"], offs["na_v"], H_na, 1, False)
            yc_sw = _ctx_attention(qkv_c, swa_sink[l], offs["sw_q"], offs["sw_k"], offs["sw_v"], KVH, GQA_GROUP, True)
            mc = _merge_branches(flat(xc), g[0], mod_c[0], mod_c[1], w_in, gate_start,
                                 flat(yc_hy), flat(yc_na), flat(yc_sw), w_br_b, l)
            xcf = _matmul_norm_residual(mc, w_out_b, l, flat(xc), g[1], mod_c[2], "ctx_out_proj_residual")
            gl_c = _ffn_up_glu(xcf, g[2], mod_c[3], mod_c[4], ffn_w_up, l, ffn_conv_w[l], ffn_conv_b[l], period=CTX)
            xc = unflat(_matmul_norm_residual(gl_c, w_dn_b, l, xcf, g[3], mod_c[5], "ctx_ffn_down_residual"))
    return x
```

```python
import functools
import math

import numpy as np
import jax
import jax.numpy as jnp
from jax import lax
from jax.experimental import pallas as pl
from jax.experimental.pallas import tpu as pltpu

F32 = jnp.float32
BF16 = jnp.bfloat16
NEG_INF = -1e30

GRID_W = 64
HEAD_DIM = 128
HY_EMB = 33
HY_BANDS = (HY_EMB - 1) // 2
HY_FAST_DECAY = 0.3
HY_SLOW_DECAY = 1.5
HY_DECAY_TARGET = 1e-2
HY_MAX_DECAY = math.log(HY_DECAY_TARGET) / HY_FAST_DECAY
HY_MIN_DECAY = math.log(HY_DECAY_TARGET) / HY_SLOW_DECAY
NA_WIN_R = 8
NA_WIN_C = 16
GQA_GROUP = 2
GQA_WINDOW = 128
ROPE_BASE = 10000.0
NORM_EPS = 1e-6

LANES = 128
FFT_N2 = 128
VMEM_LIMIT = 56 * 1024 * 1024
VMEM_LIMIT_LARGE = 60 * 1024 * 1024
VMEM_TEMP_RESERVE = 8 * 1024 * 1024
HP = lax.Precision.HIGHEST
LOG2_E = math.log2(math.e)


def _call(body, *, grid, in_specs, out_specs, out_shape, scratch=(), sem, name, vmem=VMEM_LIMIT):
    return pl.pallas_call(
        body, grid=grid, in_specs=in_specs, out_specs=out_specs, out_shape=out_shape,
        scratch_shapes=list(scratch),
        compiler_params=pltpu.CompilerParams(dimension_semantics=sem, vmem_limit_bytes=vmem),
        name=name)


def _tile(n, cands):
    for c in cands:
        if n % c == 0:
            return c
    raise ValueError(f"no tile for {n} in {cands}")


def _sigmoid(x):
    return 1.0 / (1.0 + jnp.exp(-x))


def _mod_body(c_ref, w_ref, b_ref, o_ref):
    c = c_ref[...]
    s = (c * _sigmoid(c)).astype(BF16)
    o_ref[0] = jnp.dot(s, w_ref[0].astype(BF16), preferred_element_type=F32) + b_ref[0]


def _modulation(cc, w_mod, b_mod):
    L, D, N = w_mod.shape
    R = cc.shape[0]
    tn = _tile(N, (1024, 512, 256, 128))
    return _call(
        _mod_body, grid=(L, N // tn),
        in_specs=[pl.BlockSpec((R, D), lambda l, j: (0, 0)),
                  pl.BlockSpec((1, D, tn), lambda l, j: (l, 0, j)),
                  pl.BlockSpec((1, 1, tn), lambda l, j: (l, 0, j))],
        out_specs=pl.BlockSpec((1, R, tn), lambda l, j: (l, 0, j)),
        out_shape=jax.ShapeDtypeStruct((L, R, N), F32),
        sem=("parallel", "parallel"), name="modulation")(cc, w_mod, b_mod.reshape(L, 1, N))


def _swap32(a):
    lane = lax.broadcasted_iota(jnp.int32, a.shape, 1)
    return jnp.where((lane & 32) == 0, pltpu.roll(a, 96, 1), pltpu.roll(a, 32, 1))


def _norm_modulate(x, g_ref, sh_ref, sc_ref, eps):
    y = x * lax.rsqrt(jnp.mean(x * x, axis=-1, keepdims=True) + eps) * g_ref[...]
    return y * (1.0 + sc_ref[0]) + sh_ref[0]


HALO = 16


def _halo_prologue(h_ref, x_ref, xp_ref, xn_ref, g_ref, sh_ref, sc_ref, eps):
    i, tm = pl.program_id(1), x_ref.shape[1]
    nm = lambda x: _norm_modulate(x, g_ref, sh_ref, sc_ref, eps)
    keep_prev = (i > 0).astype(F32)
    keep_next = (i < pl.num_programs(1) - 1).astype(F32)
    h_ref[0:HALO, :] = (nm(xp_ref[0]) * keep_prev).astype(BF16)
    h_ref[HALO:HALO + tm, :] = nm(x_ref[0]).astype(BF16)
    h_ref[HALO + tm:, :] = (nm(xn_ref[0]) * keep_next).astype(BF16)


def _conv3_halo(a, cw_ref, cb_ref, tm, period):
    rows = a.shape[0]
    up = pltpu.roll(a, 1, 0)[HALO:HALO + tm]
    un = pltpu.roll(a, rows - 1, 0)[HALO:HALO + tm]
    if period is not None:
        t = (pl.program_id(1) * tm + lax.broadcasted_iota(jnp.int32, up.shape, 0)) % period
        up = jnp.where(t == 0, 0.0, up)
        un = jnp.where(t == period - 1, 0.0, un)
    return up * cw_ref[0:1, :] + a[HALO:HALO + tm] * cw_ref[1:2, :] + un * cw_ref[2:3, :] + cb_ref[...]


def _halo_specs(S, tm, D):
    hb, nh = tm // HALO, S // HALO
    return [pl.BlockSpec((1, tm, D), lambda b, i, j: (b, i, 0)),
            pl.BlockSpec((1, HALO, D), lambda b, i, j: (b, jnp.maximum(i * hb - 1, 0), 0)),
            pl.BlockSpec((1, HALO, D), lambda b, i, j: (b, jnp.minimum((i + 1) * hb, nh - 1), 0)),
            pl.BlockSpec((1, D), lambda b, i, j: (0, 0)),
            pl.BlockSpec((1, 1, D), lambda b, i, j: (b, 0, 0)),
            pl.BlockSpec((1, 1, D), lambda b, i, j: (b, 0, 0))]


def _inproj_body(*refs, nq, nc, rope_chunks, period, eps):
    if rope_chunks:
        x_ref, xp_ref, xn_ref, g_ref, sh_ref, sc_ref, w_ref, cw_ref, cb_ref, cos_ref, sin_ref, q_o, o0, o1, o2, h_ref = refs
    else:
        x_ref, xp_ref, xn_ref, g_ref, sh_ref, sc_ref, w_ref, cw_ref, cb_ref, q_o, o0, o1, o2, h_ref = refs
    j = pl.program_id(2)
    tm = x_ref.shape[1]

    @pl.when(j == 0)
    def _():
        _halo_prologue(h_ref, x_ref, xp_ref, xn_ref, g_ref, sh_ref, sc_ref, eps)

    qkv_dot = lambda: jnp.dot(h_ref[HALO:HALO + tm, :], w_ref[...].astype(BF16), preferred_element_type=F32)
    nch = w_ref.shape[1] // LANES
    rope_tiles = sorted({ch // nch for ch in rope_chunks})
    for jt in rope_tiles:
        @pl.when(j == jt)
        def _(jt=jt):
            acc = qkv_dot()
            c, s = cos_ref[...], sin_ref[...]
            for k in range(nch):
                a = acc[:, k * LANES:(k + 1) * LANES]
                if jt * nch + k in rope_chunks:
                    a = a * c + _swap32(a) * s
                q_o[0, :, k * LANES:(k + 1) * LANES] = a.astype(q_o.dtype)

    @pl.when(functools.reduce(jnp.logical_and, [j != jt for jt in rope_tiles], j < nq))
    def _():
        q_o[0] = qkv_dot().astype(q_o.dtype)

    for k, o in enumerate((o0, o1, o2)):
        @pl.when(jnp.logical_and(j >= nq + k * nc, j < nq + (k + 1) * nc))
        def _(o=o):
            a = jnp.dot(h_ref[...], w_ref[...].astype(BF16), preferred_element_type=F32)
            o[0] = _conv3_halo(a, cw_ref, cb_ref, tm, period)


def _colmap(segs, tn):
    bounds, o = [], 0
    for s, w in segs:
        assert s % tn == 0 and w % tn == 0
        bounds.append(((o + w) // tn, (s - o) // tn))
        o += w

    def f(j):
        r = j + bounds[-1][1]
        for hi, off in reversed(bounds[:-1]):
            r = jnp.where(j < hi, j + off, r)
        return r
    return f, o


def _seg_tile(segs, cands):
    return _tile(functools.reduce(math.gcd, [v for seg in segs for v in seg if v]), cands)


def _in_proj(x, gain, shift, scale, w_in, l, qkv_segs, hy_start, C, conv_w, conv_b, rope, name, period=None):
    B, S, D = x.shape
    tm = _tile(S, (1024, 512, 256, 128))
    segs = list(qkv_segs) + [(hy_start, 3 * C)]
    tn = _seg_tile(segs + [(0, C)], (512, 256, 128))
    cmap, n_out = _colmap(segs, tn)
    nc = C // tn
    nq = n_out // tn - 3 * nc
    in_specs = _halo_specs(S, tm, D) + [
        pl.BlockSpec((None, D, tn), lambda b, i, j: (l, 0, cmap(j))),
        pl.BlockSpec((3, tn), lambda b, i, j: (0, jnp.clip(j - nq, 0, 3 * nc - 1))),
        pl.BlockSpec((1, tn), lambda b, i, j: (0, jnp.clip(j - nq, 0, 3 * nc - 1)))]
    args = [x, x, x, gain.reshape(1, D), shift, scale, w_in, conv_w, conv_b.reshape(1, 3 * C)]
    rope_chunks = ()
    if rope is not None:
        cos_t, sin_t, rope_chunks = rope
        in_specs += [pl.BlockSpec((tm, LANES), lambda b, i, j: (i, 0)),
                     pl.BlockSpec((tm, LANES), lambda b, i, j: (i, 0))]
        args += [cos_t, sin_t]
    hy_spec = lambda k: pl.BlockSpec((1, tm, tn), lambda b, i, j: (b, i, jnp.clip(j - nq - k * nc, 0, nc - 1)))
    return _call(
        functools.partial(_inproj_body, nq=nq, nc=nc, rope_chunks=frozenset(rope_chunks), period=period, eps=NORM_EPS),
        grid=(B, S // tm, n_out // tn), in_specs=in_specs,
        out_specs=[pl.BlockSpec((1, tm, tn), lambda b, i, j: (b, i, jnp.minimum(j, nq - 1))),
                   hy_spec(0), hy_spec(1), hy_spec(2)],
        out_shape=[jax.ShapeDtypeStruct((B, S, nq * tn), BF16)] + [jax.ShapeDtypeStruct((B, S, C), F32)] * 3,
        scratch=[pltpu.VMEM((tm + 2 * HALO, D), BF16)],
        sem=("parallel", "parallel", "arbitrary"), name=name)(*args)


def _ffnup_body(x_ref, xp_ref, xn_ref, g_ref, sh_ref, sc_ref, wa_ref, wu_ref, cw_ref, cb_ref, o_ref, h_ref, *, period, eps):
    tm = x_ref.shape[1]

    @pl.when(pl.program_id(2) == 0)
    def _():
        _halo_prologue(h_ref, x_ref, xp_ref, xn_ref, g_ref, sh_ref, sc_ref, eps)

    a = jnp.dot(h_ref[...], wa_ref[...].astype(BF16), preferred_element_type=F32)
    u = jnp.dot(h_ref[HALO:HALO + tm, :], wu_ref[...].astype(BF16), preferred_element_type=F32)
    c = _conv3_halo(a, cw_ref, cb_ref, tm, period)
    o_ref[0] = (c * _sigmoid(c) * u).astype(o_ref.dtype)


def _ffn_up_glu(x, gain, shift, scale, w_up, l, conv_w, conv_b, period=None):
    B, S, D = x.shape
    Fd = w_up.shape[2] // 2
    tm = _tile(S, (1024, 512, 256, 128))
    tn = _tile(Fd, (512, 256, 128))
    nj = Fd // tn
    return _call(
        functools.partial(_ffnup_body, period=period, eps=NORM_EPS), grid=(B, S // tm, nj),
        in_specs=_halo_specs(S, tm, D) + [
            pl.BlockSpec((None, D, tn), lambda b, i, j: (l, 0, j)),
            pl.BlockSpec((None, D, tn), lambda b, i, j: (l, 0, nj + j)),
            pl.BlockSpec((3, tn), lambda b, i, j: (0, j)),
            pl.BlockSpec((1, tn), lambda b, i, j: (0, j))],
        out_specs=pl.BlockSpec((1, tm, tn), lambda b, i, j: (b, i, j)),
        out_shape=jax.ShapeDtypeStruct((B, S, Fd), BF16),
        scratch=[pltpu.VMEM((tm + 2 * HALO, D), BF16)],
        sem=("parallel", "parallel", "arbitrary"), name="ffn_up_glu")(
            x, x, x, gain.reshape(1, D), shift, scale, w_up, w_up, conv_w, conv_b.reshape(1, Fd))


def _filt_trunk_body(z_ref, w1_ref, b1_ref, fr_ref, w2_ref, b2_ref, o_ref):
    a = jnp.dot(z_ref[...], w1_ref[...], precision=HP, preferred_element_type=F32) + b1_ref[...]
    a = jnp.sin(fr_ref[0:1, :] * a)
    a = jnp.dot(a, w2_ref[...], precision=HP, preferred_element_type=F32) + b2_ref[...]
    o_ref[...] = jnp.sin(fr_ref[1:2, :] * a)


def _filt_main_body(a_ref, wf_ref, wb_ref, t_ref, dl_ref, o_ref, *, n):
    hf = jnp.dot(a_ref[0:n, :], wf_ref[...], precision=HP, preferred_element_type=F32)
    hb = jnp.dot(a_ref[n:2 * n, :], wb_ref[...], precision=HP, preferred_element_type=F32)
    r = lax.broadcasted_iota(jnp.int32, hb.shape, 0)
    k = jnp.concatenate([hf, jnp.where(r == 0, 0.0, hb)], axis=0) * jnp.exp(-t_ref[...] * dl_ref[...])
    o_ref[...] = k / jnp.sum(jnp.abs(k), axis=0, keepdims=True)


def _hyena_filters(n, w1, b1, freq, w2, b2, w3):
    Hd = w1.shape[1]
    OC = w3.shape[1] // 2
    t = jnp.linspace(0.0, 1.0, n, dtype=F32)[:, None]
    w = (2.0 * math.pi / n) * jnp.arange(n, dtype=F32)[:, None]
    f = jnp.linspace(1e-4, HY_BANDS - 1, HY_BANDS, dtype=F32)[None, :]
    z = jnp.concatenate([t, jnp.cos(f * w), -jnp.sin(f * w)], axis=-1)
    fold = lambda a: jnp.concatenate([a, jnp.zeros_like(a[:1]), a[:0:-1]], axis=0)
    EP = 64
    z2 = jnp.pad(fold(z), ((0, 0), (0, EP - HY_EMB)))
    w1p = jnp.pad(w1, ((0, EP - HY_EMB), (0, 0)))
    rt = _tile(2 * n, (1024, 512, 256))
    a2 = _call(
        _filt_trunk_body, grid=(2 * n // rt,),
        in_specs=[pl.BlockSpec((rt, EP), lambda i: (i, 0)),
                  pl.BlockSpec((EP, Hd), lambda i: (0, 0)),
                  pl.BlockSpec((1, Hd), lambda i: (0, 0)),
                  pl.BlockSpec((2, Hd), lambda i: (0, 0)),
                  pl.BlockSpec((Hd, Hd), lambda i: (0, 0)),
                  pl.BlockSpec((1, Hd), lambda i: (0, 0))],
        out_specs=pl.BlockSpec((rt, Hd), lambda i: (i, 0)),
        out_shape=jax.ShapeDtypeStruct((2 * n, Hd), F32),
        sem=("parallel",), name="hyena_filter_trunk")(z2, w1p, b1.reshape(1, Hd), freq, w2, b2.reshape(1, Hd))
    tc = LANES
    t2 = jnp.broadcast_to(fold(t), (2 * n, tc))
    deltas = jnp.abs(jnp.linspace(HY_MIN_DECAY, HY_MAX_DECAY, OC, dtype=F32))[None, :]
    nc = OC // tc
    return _call(
        functools.partial(_filt_main_body, n=n), grid=(nc,),
        in_specs=[pl.BlockSpec((2 * n, Hd), lambda c: (0, 0)),
                  pl.BlockSpec((Hd, tc), lambda c: (0, c)),
                  pl.BlockSpec((Hd, tc), lambda c: (0, nc + c)),
                  pl.BlockSpec((2 * n, tc), lambda c: (0, 0)),
                  pl.BlockSpec((1, tc), lambda c: (0, c))],
        out_specs=pl.BlockSpec((2 * n, tc), lambda c: (0, c)),
        out_shape=jax.ShapeDtypeStruct((2 * n, OC), F32),
        sem=("parallel",), name="hyena_filter")(a2, w3, w3, t2, deltas)


def _embed(re, im):
    return np.block([[re, -im], [im, re]])


@functools.lru_cache(maxsize=None)
def _fft_tables(N1, N2):
    N = N1 * N2
    S1 = N1 // 2
    i1 = np.arange(N1)
    ang = -2.0 * np.pi * ((i1[:, None] * i1[None, :]) % N1) / N1
    fr, fi = np.cos(ang), np.sin(ang)
    f1_pair = _embed(fr[:, :S1], fi[:, :S1])
    f1_real = np.concatenate([fr, fi], axis=0)
    i2 = np.arange(N2)
    fidx = i1[:, None, None] + N1 * i2[None, :, None]
    ang = -2.0 * np.pi * ((fidx * i2[None, None, :]) % N) / N
    mr, mi = np.cos(ang), np.sin(ang)
    m_fwd = np.stack([_embed(mr[a], mi[a]) for a in range(N1)])
    m_inv = np.stack([_embed(mr[a].T, -mi[a].T) for a in range(N1)])
    ang = 2.0 * np.pi * ((i1[:S1, None] * i1[None, :]) % N1) / N1
    f1_inv = _embed(np.cos(ang) / N, np.sin(ang) / N)
    cvt = lambda a: jnp.asarray(a, dtype=BF16)
    return cvt(f1_pair), cvt(f1_real), cvt(m_fwd), cvt(m_inv), cvt(f1_inv)


def _lmat_body(f_ref, x_ref, o_ref):
    o_ref[0] = jnp.dot(f_ref[...], x_ref[0].astype(BF16), preferred_element_type=F32).astype(o_ref.dtype)


def _left_matmul(fm, x, out_dtype, name):
    P, K, W = x.shape
    R = fm.shape[0]
    tw = _tile(W, (4096, 2048, 1024, 512, 256, 128))
    return _call(
        _lmat_body, grid=(P, W // tw),
        in_specs=[pl.BlockSpec((R, K), lambda p, j: (0, 0)),
                  pl.BlockSpec((1, K, tw), lambda p, j: (p, 0, j))],
        out_specs=pl.BlockSpec((1, R, tw), lambda p, j: (p, 0, j)),
        out_shape=jax.ShapeDtypeStruct((P, R, W), out_dtype),
        sem=("parallel", "parallel"), name=name)(fm, x)


def _cmul(xr, xi, kr, ki):
    return xr * kr - xi * ki, xr * ki + xi * kr


def _fftmid_body(a_ref, m_ref, mi_ref, k_ref, o_ref, *, FB, N2):
    for t in range(FB):
        a = a_ref[0, :, t].reshape(2 * N2, a_ref.shape[-1])
        x = jnp.dot(m_ref[t], a, preferred_element_type=F32)
        yr, yi = _cmul(x[:N2], x[N2:], k_ref[0, t].astype(F32), k_ref[1, t].astype(F32))
        y = jnp.concatenate([yr, yi], axis=0).astype(BF16)
        g = jnp.dot(mi_ref[t], y, preferred_element_type=F32)
        o_ref[0, :, t] = g.reshape(2, N2, g.shape[-1]).astype(o_ref.dtype)


def _fftfwd_body(a_ref, m_ref, o_ref, *, FB, N2):
    for t in range(FB):
        a = a_ref[:, t].reshape(2 * N2, a_ref.shape[-1])
        x = jnp.dot(m_ref[t], a, preferred_element_type=F32)
        o_ref[:, t] = x.reshape(2, N2, x.shape[-1]).astype(o_ref.dtype)


SUB_BLOCK = 16


def _dft1_body(f_ref, x_ref, o_ref):
    xt = pltpu.einshape("ksc->skc", x_ref[0])
    r = jnp.stack([jnp.dot(f_ref[...], xt[k].astype(BF16), preferred_element_type=F32)
                   for k in range(xt.shape[0])], axis=0)
    o_ref[0] = pltpu.einshape("skc->ksc", r).astype(o_ref.dtype)


def _outer_dft(fm, x4, out_dtype, name):
    P, K, N2, C = x4.shape
    R = fm.shape[0]
    tc = _tile(C, (512, 256, 128))
    return _call(
        _dft1_body, grid=(P, N2 // SUB_BLOCK, C // tc),
        in_specs=[pl.BlockSpec((R, K), lambda p, s, c: (0, 0)),
                  pl.BlockSpec((1, K, SUB_BLOCK, tc), lambda p, s, c: (p, 0, s, c))],
        out_specs=pl.BlockSpec((1, R, SUB_BLOCK, tc), lambda p, s, c: (p, 0, s, c)),
        out_shape=jax.ShapeDtypeStruct((P, R, N2, C), out_dtype),
        sem=("parallel", "parallel", "parallel"), name=name)(fm, x4)


def _idft1_gate_body(f_ref, g_ref, z_ref, x1_ref, sk_ref, o_ref):
    gt = pltpu.einshape("ksc->skc", g_ref[0])
    y = jnp.stack([jnp.dot(f_ref[...], gt[k], preferred_element_type=F32) for k in range(gt.shape[0])], axis=0)
    y = pltpu.einshape("skc->ksc", y)
    o_ref[0] = (x1_ref[0] * (y + sk_ref[...] * z_ref[0])).astype(o_ref.dtype)


def _filter_spectrum_2stage(k, N1, N2):
    N, OC = k.shape
    _, f1_real, m_fwd, _, _ = _fft_tables(N1, N2)
    a = _outer_dft(f1_real, k.reshape(1, N1, N2, OC), BF16, "filter_dft1").reshape(2, N1, N2, OC)
    FB = _tile(N1, (8, 4, 2, 1))
    tc = _tile(OC, (512, 256, 128))
    return _call(
        functools.partial(_fftfwd_body, FB=FB, N2=N2), grid=(N1 // FB, OC // tc),
        in_specs=[pl.BlockSpec((2, FB, N2, tc), lambda f, c: (0, f, 0, c)),
                  pl.BlockSpec((FB, 2 * N2, 2 * N2), lambda f, c: (f, 0, 0))],
        out_specs=pl.BlockSpec((2, FB, N2, tc), lambda f, c: (0, f, 0, c)),
        out_shape=jax.ShapeDtypeStruct((2, N1, N2, OC), BF16),
        sem=("parallel", "parallel"), name="filter_dft2")(a, m_fwd)


def _long_conv_gate_2stage(z, x1, skip, kf, order, out_dtype, N1, N2):
    B, n, C = z.shape
    assert B % 2 == 0
    P, S1 = B // 2, N1 // 2
    f1_pair, _, m_fwd, m_inv, f1_inv = _fft_tables(N1, N2)
    z4 = z.reshape(P, 2 * S1, N2, C)
    a = _outer_dft(f1_pair, z4, BF16, "conv_dft1").reshape(P, 2, N1, N2, C)
    FB = _tile(N1, (8, 4, 2, 1))
    tc = _tile(C, (512, 256, 128))
    oc = order * (C // tc)
    g = _call(
        functools.partial(_fftmid_body, FB=FB, N2=N2), grid=(N1 // FB, C // tc, P),
        in_specs=[pl.BlockSpec((1, 2, FB, N2, tc), lambda f, c, p: (p, 0, f, 0, c)),
                  pl.BlockSpec((FB, 2 * N2, 2 * N2), lambda f, c, p: (f, 0, 0)),
                  pl.BlockSpec((FB, 2 * N2, 2 * N2), lambda f, c, p: (f, 0, 0)),
                  pl.BlockSpec((2, FB, N2, tc), lambda f, c, p: (0, f, 0, oc + c))],
        out_specs=pl.BlockSpec((1, 2, FB, N2, tc), lambda f, c, p: (p, 0, f, 0, c)),
        out_shape=jax.ShapeDtypeStruct((P, 2, N1, N2, C), BF16),
        sem=("parallel", "parallel", "arbitrary"), name="conv_dft2_mul_idft2")(a, m_fwd, m_inv, kf)
    blk = lambda rows: pl.BlockSpec((1, rows, SUB_BLOCK, tc), lambda p, s, c: (p, 0, s, c))
    out = _call(
        _idft1_gate_body, grid=(P, N2 // SUB_BLOCK, C // tc),
        in_specs=[pl.BlockSpec((2 * S1, 2 * N1), lambda p, s, c: (0, 0)),
                  blk(2 * N1), blk(2 * S1), blk(2 * S1),
                  pl.BlockSpec((1, tc), lambda p, s, c: (0, c))],
        out_specs=blk(2 * S1),
        out_shape=jax.ShapeDtypeStruct((P, 2 * S1, N2, C), out_dtype),
        sem=("parallel", "parallel", "parallel"), name="conv_idft1_gate")(
            f1_inv, g.reshape(P, 2 * N1, N2, C), z4, x1.reshape(P, 2 * S1, N2, C), skip[None, :])
    return out.reshape(B, n, C)


@functools.lru_cache(maxsize=None)
def _dft_tables(n):
    N = 2 * n
    f = np.arange(N)
    ang = -2.0 * np.pi * ((f[:, None] * f[None, :]) % N) / N
    fr, fi = np.cos(ang), np.sin(ang)
    fwd_full = np.concatenate([fr, fi], axis=0)
    fwd_half = fwd_full[:, :n]
    inv = np.concatenate([fr[:n, :], fi[:n, :]], axis=1) / N
    cvt = lambda a: jnp.asarray(a, dtype=BF16)
    return cvt(fwd_full), cvt(fwd_half), cvt(inv)


def _dftconv_body(z_ref, x1_ref, sk_ref, f_ref, fi_ref, k_ref, o_ref, *, N):
    z = z_ref[0]
    x = jnp.dot(f_ref[...], z.astype(BF16), preferred_element_type=F32)
    yr, yi = _cmul(x[:N], x[N:], k_ref[0], k_ref[1])
    y = jnp.concatenate([yr, yi], axis=0).astype(BF16)
    y = jnp.dot(fi_ref[...], y, preferred_element_type=F32)
    o_ref[0] = (x1_ref[0] * (y + sk_ref[...] * z)).astype(o_ref.dtype)


def _long_conv_gate_dense(z, x1, skip, kf, order, out_dtype):
    B, n, C = z.shape
    N = 2 * n
    _, fwd_half, inv = _dft_tables(n)
    tc = _tile(C, (256, 128))
    oc = order * (C // tc)
    return _call(
        functools.partial(_dftconv_body, N=N), grid=(C // tc, B),
        in_specs=[pl.BlockSpec((1, n, tc), lambda c, b: (b, 0, c)),
                  pl.BlockSpec((1, n, tc), lambda c, b: (b, 0, c)),
                  pl.BlockSpec((1, tc), lambda c, b: (0, c)),
                  pl.BlockSpec((2 * N, n), lambda c, b: (0, 0)),
                  pl.BlockSpec((n, 2 * N), lambda c, b: (0, 0)),
                  pl.BlockSpec((2, N, tc), lambda c, b: (0, 0, oc + c))],
        out_specs=pl.BlockSpec((1, n, tc), lambda c, b: (b, 0, c)),
        out_shape=jax.ShapeDtypeStruct((B, n, C), out_dtype),
        sem=("parallel", "arbitrary"), name="conv_dense_dft")(z, x1, skip[None, :], fwd_half, inv, kf)


def _hyena_mixer(v, x1, x2, w1, b1, freq, w2, b2, w3, skip):
    B, n, _ = v.shape
    k = _hyena_filters(n, w1, b1, freq, w2, b2, w3)
    N = 2 * n
    if N % FFT_N2 == 0 and (N // FFT_N2) >= 16:
        N1 = N // FFT_N2
        kf = _filter_spectrum_2stage(k, N1, FFT_N2)
        z = _long_conv_gate_2stage(v, x1, skip[0], kf, 0, F32, N1, FFT_N2)
        return _long_conv_gate_2stage(z, x2, skip[1], kf, 1, BF16, N1, FFT_N2)
    fwd_full, _, _ = _dft_tables(n)
    kf = _left_matmul(fwd_full, k[None], F32, "filter_dense_dft").reshape(2, N, k.shape[1])
    z = _long_conv_gate_dense(v, x1, skip[0], kf, 0, F32)
    return _long_conv_gate_dense(z, x2, skip[1], kf, 1, BF16)


def _nt(a, b):
    return lax.dot_general(a, b, (((1,), (1,)), ((), ())), preferred_element_type=F32)


def _na_body(q_ref, k_ref, v_ref, kc_ref, vc_ref, b_ref, o_ref, *, R, KR, rows, HPS, scale):
    j = pl.program_id(2)
    start = pl.multiple_of(jnp.clip(j * R - NA_WIN_R // 2, 0, rows - KR) * GRID_W, GRID_W)
    for h in range(HPS):
        hs = slice(h * HEAD_DIM, (h + 1) * HEAD_DIM)
        q = q_ref[0, :, hs]
        kw = k_ref[0, pl.ds(start, KR * GRID_W), hs]
        vw = v_ref[0, pl.ds(start, KR * GRID_W), hs]
        s = _nt(q, kw) + b_ref[h, 0]
        sc = _nt(q, kc_ref[0, :, hs])
        m = jnp.maximum(jnp.max(s, axis=-1, keepdims=True), jnp.max(sc, axis=-1, keepdims=True))
        p = jnp.exp2((s - m) * (scale * LOG2_E))
        pc = jnp.exp2((sc - m) * (scale * LOG2_E))
        l = jnp.sum(p, axis=-1, keepdims=True) + jnp.sum(pc, axis=-1, keepdims=True)
        o = jnp.dot(p.astype(BF16), vw, preferred_element_type=F32)
        o = o + jnp.dot(pc.astype(BF16), vc_ref[0, :, hs], preferred_element_type=F32)
        o_ref[0, :, hs] = (o / l).astype(o_ref.dtype)


def _na_geometry(S):
    rows = S // GRID_W
    kr = min(NA_WIN_R, rows)
    R = min(8, rows)
    KR = min(rows, R + kr)
    nb = rows // R
    assert rows % R == 0
    types = sorted({0, min(1, nb - 1), nb - 1})
    if nb > 3:
        offs = {int(np.clip(j * R - NA_WIN_R // 2, 0, rows - KR)) - j * R for j in range(1, nb - 1)}
        assert len(offs) == 1
    return rows, kr, R, KR, nb, types


def _nabias_body(rpb_ref, o_ref, tw_ref, *, plan, R, KR, inv_scale):
    W = GRID_W
    nd_r, nd_c = 2 * NA_WIN_R - 1, 2 * NA_WIN_C - 1
    base = pl.program_id(0) * (nd_r * nd_c)
    qc = lax.broadcasted_iota(jnp.int32, (W, 2 * W), 0)
    lane = lax.broadcasted_iota(jnp.int32, (W, 2 * W), 1)
    kc = lane % W
    cs = jnp.clip(qc - NA_WIN_C // 2, 0, W - NA_WIN_C)
    col_ok = jnp.logical_and(kc >= cs, kc < cs + NA_WIN_C)
    dcm = kc - qc + (NA_WIN_C - 1)
    neg = jnp.full((W, 2 * W), NEG_INF, F32)
    for dr in range(nd_r):
        acc = neg
        for dc in range(nd_c):
            acc = jnp.where(dcm == dc, rpb_ref[base + dr * nd_c + dc] * inv_scale, acc)
        tw_ref[dr] = jnp.where(col_ok, acc, NEG_INF)
    left = lane < W
    for t, per_q in enumerate(plan):
        for qr in range(R):
            for kp in range(KR // 2):
                d0, d1 = per_q[qr][kp]
                a = neg if d0 is None else tw_ref[d0]
                b = neg if d1 is None else tw_ref[d1]
                blk = neg if (d0 is None and d1 is None) else jnp.where(left, a, b)
                o_ref[0, t, qr * W:(qr + 1) * W, kp * 2 * W:(kp + 1) * 2 * W] = blk


def _na_bias_tables(rpb, S):
    rows, kr, R, KR, nb, types = _na_geometry(S)
    assert 2 * GRID_W == LANES and KR % 2 == 0
    plan = []
    for jt in types:
        start = int(np.clip(jt * R - NA_WIN_R // 2, 0, rows - KR))
        per_q = []
        for q in range(R):
            qra = jt * R + q
            ws = int(np.clip(qra - kr // 2, 0, rows - kr))
            d = [(start + k) - qra + (NA_WIN_R - 1) if ws <= start + k < ws + kr else None for k in range(KR)]
            per_q.append([(d[2 * p], d[2 * p + 1]) for p in range(KR // 2)])
        plan.append(per_q)
    L, H, nd_r, nd_c = rpb.shape
    T, QB, KB = len(types), R * GRID_W, KR * GRID_W
    return _call(
        functools.partial(_nabias_body, plan=plan, R=R, KR=KR, inv_scale=HEAD_DIM ** 0.5), grid=(L * H,),
        in_specs=[pl.BlockSpec(memory_space=pltpu.SMEM)],
        out_specs=pl.BlockSpec((1, T, QB, KB), lambda i: (i, 0, 0, 0)),
        out_shape=jax.ShapeDtypeStruct((L * H, T, QB, KB), F32),
        scratch=[pltpu.VMEM((nd_r, GRID_W, 2 * GRID_W), F32)],
        sem=("parallel",), name="na_bias_table")(rpb.reshape(-1).astype(F32))


def _na_attention(qkv, qkv_c, bias, l, offs, H):
    B, S, _ = qkv.shape
    CTX = qkv_c.shape[1]
    rows, kr, R, KR, nb, types = _na_geometry(S)
    T = len(types)
    QB, KB = R * GRID_W, KR * GRID_W
    HPS = next(n for n in (4, 2, 1) if H % n == 0)
    HW = HPS * HEAD_DIM
    assert all(offs[n] % HW == 0 for n in ("na_k", "na_v", "na_q"))
    ok_, ov_, oq_ = (offs[n] // HW for n in ("na_k", "na_v", "na_q"))

    def btype(j):
        if T == nb:
            return j
        return jnp.where(j == 0, 0, jnp.where(j == nb - 1, T - 1, 1))

    return _call(
        functools.partial(_na_body, R=R, KR=KR, rows=rows, HPS=HPS, scale=HEAD_DIM ** -0.5),
        grid=(B, H // HPS, nb),
        in_specs=[pl.BlockSpec((1, QB, HW), lambda b, h, j: (b, j, oq_ + h)),
                  pl.BlockSpec((1, S, HW), lambda b, h, j: (b, 0, ok_ + h)),
                  pl.BlockSpec((1, S, HW), lambda b, h, j: (b, 0, ov_ + h)),
                  pl.BlockSpec((1, CTX, HW), lambda b, h, j: (b, 0, ok_ + h)),
                  pl.BlockSpec((1, CTX, HW), lambda b, h, j: (b, 0, ov_ + h)),
                  pl.BlockSpec((HPS, 1, QB, KB), lambda b, h, j: (l * (H // HPS) + h, btype(j), 0, 0))],
        out_specs=pl.BlockSpec((1, QB, HW), lambda b, h, j: (b, j, h)),
        out_shape=jax.ShapeDtypeStruct((B, S, H * HEAD_DIM), BF16),
        sem=("parallel", "parallel", "arbitrary"), name="na_attention")(qkv, qkv, qkv, qkv_c, qkv_c, bias)


def _stack_heads(q2, G):
    return jnp.concatenate([q2[:, g * HEAD_DIM:(g + 1) * HEAD_DIM] for g in range(G)], axis=0)


def _unstack_heads(o, G, n):
    return jnp.concatenate([o[g * n:(g + 1) * n] for g in range(G)], axis=1)


def _sink_column(sink_ref, h0, G, n):
    return jnp.concatenate([jnp.full((n, 1), sink_ref[h0 + g], F32) for g in range(G)], axis=0)


def _swa_body(sink_ref, q_ref, k_ref, v_ref, kc_ref, vc_ref, o_ref, *, QB, KB, S, G, HPS, scale):
    hb, j = pl.program_id(1), pl.program_id(2)
    start = pl.multiple_of(jnp.clip(j * QB - GQA_WINDOW, 0, S - KB), LANES)
    qpos = j * QB + lax.broadcasted_iota(jnp.int32, (QB, KB), 0)
    kpos = start + lax.broadcasted_iota(jnp.int32, (QB, KB), 1)
    mask = jnp.where(jnp.abs(qpos - kpos) <= GQA_WINDOW, 0.0, NEG_INF)
    mask = jnp.concatenate([mask] * G, axis=0)
    for h in range(HPS):
        hs = slice(h * HEAD_DIM, (h + 1) * HEAD_DIM)
        qs = slice(h * G * HEAD_DIM, (h + 1) * G * HEAD_DIM)
        q = _stack_heads(q_ref[0, :, qs], G)
        kw = k_ref[0, pl.ds(start, KB), hs]
        vw = v_ref[0, pl.ds(start, KB), hs]
        s = _nt(q, kw) + mask
        sc = _nt(q, kc_ref[0, :, hs])
        sk = _sink_column(sink_ref, (hb * HPS + h) * G, G, QB) * (1.0 / scale)
        m = jnp.maximum(jnp.maximum(jnp.max(s, axis=-1, keepdims=True), jnp.max(sc, axis=-1, keepdims=True)), sk)
        p = jnp.exp2((s - m) * (scale * LOG2_E))
        pc = jnp.exp2((sc - m) * (scale * LOG2_E))
        l = (jnp.sum(p, axis=-1, keepdims=True) + jnp.sum(pc, axis=-1, keepdims=True)
             + jnp.exp2((sk - m) * (scale * LOG2_E)))
        o = jnp.dot(p.astype(BF16), vw, preferred_element_type=F32)
        o = o + jnp.dot(pc.astype(BF16), vc_ref[0, :, hs], preferred_element_type=F32)
        o_ref[0, :, qs] = _unstack_heads(o / l, G, QB).astype(o_ref.dtype)


def _swa_attention(qkv, qkv_c, sink, offs, KVH):
    B, S, _ = qkv.shape
    CTX = qkv_c.shape[1]
    G = GQA_GROUP
    QB = _tile(S, (512, 256, 128))
    KB = min(S, QB + 2 * GQA_WINDOW)
    HPS = 2 if KVH % 2 == 0 else 1
    HW = HPS * HEAD_DIM
    assert offs["sw_k"] % HW == 0 and offs["sw_v"] % HW == 0 and offs["sw_q"] % (G * HW) == 0
    ok_, ov_, oq_ = offs["sw_k"] // HW, offs["sw_v"] // HW, offs["sw_q"] // (G * HW)
    return _call(
        functools.partial(_swa_body, QB=QB, KB=KB, S=S, G=G, HPS=HPS, scale=HEAD_DIM ** -0.5),
        grid=(B, KVH // HPS, S // QB),
        in_specs=[pl.BlockSpec(memory_space=pltpu.SMEM),
                  pl.BlockSpec((1, QB, G * HW), lambda b, h, j: (b, j, oq_ + h)),
                  pl.BlockSpec((1, S, HW), lambda b, h, j: (b, 0, ok_ + h)),
                  pl.BlockSpec((1, S, HW), lambda b, h, j: (b, 0, ov_ + h)),
                  pl.BlockSpec((1, CTX, HW), lambda b, h, j: (b, 0, ok_ + h)),
                  pl.BlockSpec((1, CTX, HW), lambda b, h, j: (b, 0, ov_ + h))],
        out_specs=pl.BlockSpec((1, QB, G * HW), lambda b, h, j: (b, j, h)),
        out_shape=jax.ShapeDtypeStruct((B, S, KVH * G * HEAD_DIM), BF16),
        sem=("parallel", "parallel", "arbitrary"), name="swa_attention")(sink, qkv, qkv, qkv, qkv_c, qkv_c)


def _cattn_body(sink_ref, q_ref, k_ref, v_ref, o_ref, *, G, use_sink, scale):
    h = pl.program_id(1)
    n = q_ref.shape[1]
    q = _stack_heads(q_ref[0], G)
    s = _nt(q, k_ref[0]) * scale
    m = jnp.max(s, axis=-1, keepdims=True)
    if use_sink:
        sk = _sink_column(sink_ref, h * G, G, n)
        m = jnp.maximum(m, sk)
    p = jnp.exp(s - m)
    l = jnp.sum(p, axis=-1, keepdims=True)
    if use_sink:
        l = l + jnp.exp(sk - m)
    o = jnp.dot(p.astype(BF16), v_ref[0], preferred_element_type=F32)
    o_ref[0] = _unstack_heads(o / l, G, n).astype(o_ref.dtype)


def _ctx_attention(qkv_c, sink, oq, ok, ov, KVH, G, use_sink):
    B, n, _ = qkv_c.shape
    oq_, ok_, ov_ = oq // (G * HEAD_DIM), ok // HEAD_DIM, ov // HEAD_DIM
    return _call(
        functools.partial(_cattn_body, G=G, use_sink=use_sink, scale=HEAD_DIM ** -0.5),
        grid=(B, KVH),
        in_specs=[pl.BlockSpec(memory_space=pltpu.SMEM),
                  pl.BlockSpec((1, n, G * HEAD_DIM), lambda b, h: (b, 0, oq_ + h)),
                  pl.BlockSpec((1, n, HEAD_DIM), lambda b, h: (b, 0, ok_ + h)),
                  pl.BlockSpec((1, n, HEAD_DIM), lambda b, h: (b, 0, ov_ + h))],
        out_specs=pl.BlockSpec((1, n, G * HEAD_DIM), lambda b, h: (b, 0, h)),
        out_shape=jax.ShapeDtypeStruct((B, n, KVH * G * HEAD_DIM), BF16),
        sem=("parallel", "parallel"), name="ctx_attention")(sink, qkv_c, qkv_c, qkv_c)


def _merge_body(x_ref, g_ref, sh_ref, sc_ref, yh, yn, ys, wgh, wgn, wgs, wh, wn, ws, o_ref, h_ref, *, eps):
    @pl.when(pl.program_id(2) == 0)
    def _():
        h_ref[...] = _norm_modulate(x_ref[0], g_ref, sh_ref, sc_ref, eps).astype(BF16)

    h = h_ref[...]
    gate = lambda wg: _sigmoid(jnp.dot(h, wg[...], preferred_element_type=F32))
    m = gate(wgh) * jnp.dot(yh[0], wh[...], preferred_element_type=F32)
    m = m + gate(wgn) * jnp.dot(yn[0], wn[...], preferred_element_type=F32)
    m = m + gate(wgs) * jnp.dot(ys[0], ws[...], preferred_element_type=F32)
    o_ref[0] = m.astype(o_ref.dtype)


def _merge_branches(x, gain, shift, scale, w_gate, y_hy, y_na, y_sw, w_br, l):
    B, S, D = x.shape
    widths = (y_hy.shape[2], y_na.shape[2], y_sw.shape[2])
    starts = (0, widths[0], widths[0] + widths[1])
    assert all(s % w == 0 for s, w in zip(starts, widths))
    tm = _tile(S, (512, 256, 128))
    tn = _tile(D, (512, 256, 128))
    nj = D // tn
    yspec = lambda y: pl.BlockSpec((1, tm, y.shape[2]), lambda b, i, j: (b, i, 0))
    gspec = lambda k: pl.BlockSpec((None, D, tn), lambda b, i, j: (l, 0, k * nj + j))
    wspec = lambda k: pl.BlockSpec((None, widths[k], tn), lambda b, i, j: (l, starts[k] // widths[k], j))
    return _call(
        functools.partial(_merge_body, eps=NORM_EPS), grid=(B, S // tm, nj),
        in_specs=[pl.BlockSpec((1, tm, D), lambda b, i, j: (b, i, 0)),
                  pl.BlockSpec((1, D), lambda b, i, j: (0, 0)),
                  pl.BlockSpec((1, 1, D), lambda b, i, j: (b, 0, 0)),
                  pl.BlockSpec((1, 1, D), lambda b, i, j: (b, 0, 0)),
                  yspec(y_hy), yspec(y_na), yspec(y_sw), gspec(0), gspec(1), gspec(2),
                  wspec(0), wspec(1), wspec(2)],
        out_specs=pl.BlockSpec((1, tm, tn), lambda b, i, j: (b, i, j)),
        out_shape=jax.ShapeDtypeStruct((B, S, D), BF16),
        scratch=[pltpu.VMEM((tm, D), BF16)],
        sem=("parallel", "parallel", "arbitrary"), name="gated_merge")(
            x, gain.reshape(1, D), shift, scale, y_hy, y_na, y_sw, w_gate, w_gate, w_gate, w_br, w_br, w_br)


def _mmres_body(a_ref, w_ref, x_ref, g_ref, mg_ref, o_ref, *, nk, eps):
    k = pl.program_id(2)
    part = lambda: jnp.dot(a_ref[0], w_ref[...], preferred_element_type=F32)

    def finish():
        y = o_ref[0]
        yn = y * lax.rsqrt(jnp.mean(y * y, axis=-1, keepdims=True) + eps) * g_ref[...]
        o_ref[0] = x_ref[0] + mg_ref[0] * yn

    if nk == 1:
        o_ref[0] = part()
        finish()
        return

    @pl.when(k == 0)
    def _():
        o_ref[0] = part()

    @pl.when(k > 0)
    def _():
        o_ref[0] += part()

    @pl.when(k == nk - 1)
    def _():
        finish()


def _matmul_norm_residual(a, w, l, x, gain, mgate, name):
    B, S, K = a.shape
    D = w.shape[2]
    if K <= 2048:
        tm, tk = _tile(S, (512, 256, 128)), K
    else:
        tm = _tile(S, (1024, 512, 256, 128))
        tk = max(t for t in range(LANES, 1537, LANES) if K % t == 0)
    nk = K // tk
    est = 2 * (tm * tk * 2 + tk * D * 2 + 2 * tm * D * 4)
    return _call(
        functools.partial(_mmres_body, nk=nk, eps=NORM_EPS), grid=(B, S // tm, nk),
        vmem=VMEM_LIMIT_LARGE if est > VMEM_LIMIT - VMEM_TEMP_RESERVE else VMEM_LIMIT,
        in_specs=[pl.BlockSpec((1, tm, tk), lambda b, i, k: (b, i, k)),
                  pl.BlockSpec((None, tk, D), lambda b, i, k: (l, k, 0)),
                  pl.BlockSpec((1, tm, D), lambda b, i, k: (b, i, 0)),
                  pl.BlockSpec((1, D), lambda b, i, k: (0, 0)),
                  pl.BlockSpec((1, 1, D), lambda b, i, k: (b, 0, 0))],
        out_specs=pl.BlockSpec((1, tm, D), lambda b, i, k: (b, i, 0)),
        out_shape=jax.ShapeDtypeStruct((B, S, D), F32),
        sem=("parallel", "parallel", "arbitrary"), name=name)(a, w, x, gain.reshape(1, D), mgate)


def _rope_tables(n):
    t = jnp.arange(n)
    row = (t // GRID_W).astype(F32)
    col = (t % GRID_W).astype(F32)
    per_axis = HEAD_DIM // 2
    inv = ROPE_BASE ** (-jnp.arange(0, per_axis, 2, dtype=F32) / per_axis)
    ar, ac = row[:, None] * inv, col[:, None] * inv
    cos_t = jnp.concatenate([jnp.cos(ar), jnp.cos(ar), jnp.cos(ac), jnp.cos(ac)], axis=1)
    sin_t = jnp.concatenate([-jnp.sin(ar), jnp.sin(ar), -jnp.sin(ac), jnp.sin(ac)], axis=1)
    return cos_t, sin_t


def kernel(x, c, ctx, c_ctx, w_mod, b_mod, norm_gains, w_in, hy_conv_w, hy_conv_b, hy_w1, hy_b1, hy_freq, hy_w2, hy_b2, hy_w3, hy_skip, na_rpb, swa_sink, w_branch, w_out, ffn_w_up, ffn_conv_w, ffn_conv_b, ffn_w_down):
    B, S, D = x.shape
    L = w_mod.shape[0]
    C = hy_skip.shape[-1]
    H_na = na_rpb.shape[1]
    H_q = swa_sink.shape[1]
    KVH = H_q // GQA_GROUP
    NA_W, QW, KVW = H_na * HEAD_DIM, H_q * HEAD_DIM, KVH * HEAD_DIM
    KV_COLS = 2 * NA_W + 2 * KVW
    qkv_segs = [(0, KV_COLS), (KV_COLS + 3 * C, NA_W + QW)]
    gate_start = KV_COLS + 3 * C + NA_W + QW
    offs = {"na_k": 0, "na_v": NA_W, "sw_k": 2 * NA_W, "sw_v": 2 * NA_W + KVW,
            "na_q": KV_COLS, "sw_q": KV_COLS + NA_W}
    rope_chunks = (list(range(offs["sw_k"] // LANES, (offs["sw_k"] + KVW) // LANES))
                   + list(range(offs["sw_q"] // LANES, (offs["sw_q"] + QW) // LANES)))
    cos_t, sin_t = _rope_tables(S)

    w_out_b, w_br_b, w_dn_b = w_out.astype(BF16), w_branch.astype(BF16), ffn_w_down.astype(BF16)
    w_gate_b = w_in[:, :, gate_start:].astype(BF16)
    na_bias = _na_bias_tables(na_rpb, S)
    CTX = ctx.shape[1]
    flat = lambda t: t.reshape(1, -1, t.shape[-1])
    unflat = lambda t: t.reshape(B, CTX, t.shape[-1])

    R = -(-(B + 1) // 8) * 8
    cc = jnp.concatenate([c, c_ctx[None, :], jnp.zeros((R - B - 1, D), F32)], axis=0)
    mods = _modulation(cc, w_mod, b_mod)

    xc = ctx
    for l in range(L):
        mod = [mods[l, :B, k * D:(k + 1) * D].reshape(B, 1, D) for k in range(6)]
        mod_c = [mods[l, B, k * D:(k + 1) * D].reshape(1, 1, D) for k in range(6)]
        g = norm_gains[l]
        hy_p = (hy_w1[l], hy_b1[l], hy_freq[l], hy_w2[l], hy_b2[l], hy_w3[l], hy_skip[l])

        qkv_c, vc, x1c, x2c = map(unflat, _in_proj(flat(xc), g[0], mod_c[0], mod_c[1], w_in, l, qkv_segs, KV_COLS, C,
                                                   hy_conv_w[l], hy_conv_b[l], None, "ctx_in_proj", period=CTX))
        qkv, v, x1, x2 = _in_proj(x, g[0], mod[0], mod[1], w_in, l, qkv_segs, KV_COLS, C,
                                  hy_conv_w[l], hy_conv_b[l], (cos_t, sin_t, rope_chunks), "in_proj")
        y_hy = _hyena_mixer(v, x1, x2, *hy_p)
        y_na = _na_attention(qkv, qkv_c, na_bias, l, offs, H_na)
        y_sw = _swa_attention(qkv, qkv_c, swa_sink[l], offs, KVH)
        m = _merge_branches(x, g[0], mod[0], mod[1], w_gate_b, y_hy, y_na, y_sw, w_br_b, l)
        x = _matmul_norm_residual(m, w_out_b, l, x, g[1], mod[2], "out_proj_residual")
        gl = _ffn_up_glu(x, g[2], mod[3], mod[4], ffn_w_up, l, ffn_conv_w[l], ffn_conv_b[l])
        x = _matmul_norm_residual(gl, w_dn_b, l, x, g[3], mod[5], "ffn_down_residual")

        if l < L - 1:
            yc_hy = _hyena_mixer(vc, x1c, x2c, *hy_p)
            yc_na = _ctx_attention(qkv_c, swa_sink[l], offs["na_q"], offs["na_k"], offs["na_v"], H_na, 1, False)
            yc_sw = _ctx_attention(qkv_c, swa_sink[l], offs["sw_q"], offs["sw_k"], offs["sw_v"], KVH, GQA_GROUP, True)
            mc = _merge_branches(flat(xc), g[0], mod_c[0], mod_c[1], w_gate_b,
                                 flat(yc_hy), flat(yc_na), flat(yc_sw), w_br_b, l)
            xcf = _matmul_norm_residual(mc, w_out_b, l, flat(xc), g[1], mod_c[2], "ctx_out_proj_residual")
            gl_c = _ffn_up_glu(xcf, g[2], mod_c[3], mod_c[4], ffn_w_up, l, ffn_conv_w[l], ffn_conv_b[l], period=CTX)
            xc = unflat(_matmul_norm_residual(gl_c, w_dn_b, l, xcf, g[3], mod_c[5], "ctx_ffn_down_residual"))
    return x
```

```python
import functools
import math

import numpy as np
import jax
import jax.numpy as jnp
from jax import lax
from jax.experimental import pallas as pl
from jax.experimental.pallas import tpu as pltpu

F32 = jnp.float32
BF16 = jnp.bfloat16
NEG_INF = -1e30

GRID_W = 64
HEAD_DIM = 128
HY_EMB = 33
HY_BANDS = (HY_EMB - 1) // 2
HY_FAST_DECAY = 0.3
HY_SLOW_DECAY = 1.5
HY_DECAY_TARGET = 1e-2
HY_MAX_DECAY = math.log(HY_DECAY_TARGET) / HY_FAST_DECAY
HY_MIN_DECAY = math.log(HY_DECAY_TARGET) / HY_SLOW_DECAY
NA_WIN_R = 8
NA_WIN_C = 16
GQA_GROUP = 2
GQA_WINDOW = 128
ROPE_BASE = 10000.0
NORM_EPS = 1e-6

LANES = 128
FFT_N2 = 128
VMEM_LIMIT = 56 * 1024 * 1024
VMEM_LIMIT_LARGE = 60 * 1024 * 1024
VMEM_TEMP_RESERVE = 8 * 1024 * 1024
HP = lax.Precision.HIGHEST
LOG2_E = math.log2(math.e)


def _call(body, *, grid, in_specs, out_specs, out_shape, scratch=(), sem, name, vmem=VMEM_LIMIT):
    return pl.pallas_call(
        body, grid=grid, in_specs=in_specs, out_specs=out_specs, out_shape=out_shape,
        scratch_shapes=list(scratch),
        compiler_params=pltpu.CompilerParams(dimension_semantics=sem, vmem_limit_bytes=vmem),
        name=name)


def _tile(n, cands):
    for c in cands:
        if n % c == 0:
            return c
    raise ValueError(f"no tile for {n} in {cands}")


def _sigmoid(x):
    return 1.0 / (1.0 + jnp.exp(-x))


def _mod_body(c_ref, w_ref, b_ref, o_ref):
    c = c_ref[...]
    s = (c * _sigmoid(c)).astype(BF16)
    o_ref[0] = jnp.dot(s, w_ref[0].astype(BF16), preferred_element_type=F32) + b_ref[0]


def _modulation(cc, w_mod, b_mod):
    L, D, N = w_mod.shape
    R = cc.shape[0]
    tn = _tile(N, (1024, 512, 256, 128))
    return _call(
        _mod_body, grid=(L, N // tn),
        in_specs=[pl.BlockSpec((R, D), lambda l, j: (0, 0)),
                  pl.BlockSpec((1, D, tn), lambda l, j: (l, 0, j)),
                  pl.BlockSpec((1, 1, tn), lambda l, j: (l, 0, j))],
        out_specs=pl.BlockSpec((1, R, tn), lambda l, j: (l, 0, j)),
        out_shape=jax.ShapeDtypeStruct((L, R, N), F32),
        sem=("parallel", "parallel"), name="modulation")(cc, w_mod, b_mod.reshape(L, 1, N))


def _swap32(a):
    lane = lax.broadcasted_iota(jnp.int32, a.shape, 1)
    return jnp.where((lane & 32) == 0, pltpu.roll(a, 96, 1), pltpu.roll(a, 32, 1))


def _norm_modulate(x, g_ref, sh_ref, sc_ref, eps):
    y = x * lax.rsqrt(jnp.mean(x * x, axis=-1, keepdims=True) + eps) * g_ref[...]
    return y * (1.0 + sc_ref[0]) + sh_ref[0]


HALO = 16


def _halo_prologue(h_ref, x_ref, xp_ref, xn_ref, g_ref, sh_ref, sc_ref, eps):
    i, tm = pl.program_id(1), x_ref.shape[1]
    nm = lambda x: _norm_modulate(x, g_ref, sh_ref, sc_ref, eps)
    keep_prev = (i > 0).astype(F32)
    keep_next = (i < pl.num_programs(1) - 1).astype(F32)
    h_ref[0:HALO, :] = (nm(xp_ref[0]) * keep_prev).astype(BF16)
    h_ref[HALO:HALO + tm, :] = nm(x_ref[0]).astype(BF16)
    h_ref[HALO + tm:, :] = (nm(xn_ref[0]) * keep_next).astype(BF16)


def _conv3_halo(a, cw_ref, cb_ref, tm, period):
    rows = a.shape[0]
    up = pltpu.roll(a, 1, 0)[HALO:HALO + tm]
    un = pltpu.roll(a, rows - 1, 0)[HALO:HALO + tm]
    if period is not None:
        t = (pl.program_id(1) * tm + lax.broadcasted_iota(jnp.int32, up.shape, 0)) % period
        up = jnp.where(t == 0, 0.0, up)
        un = jnp.where(t == period - 1, 0.0, un)
    return up * cw_ref[0:1, :] + a[HALO:HALO + tm] * cw_ref[1:2, :] + un * cw_ref[2:3, :] + cb_ref[...]


def _halo_specs(S, tm, D):
    hb, nh = tm // HALO, S // HALO
    return [pl.BlockSpec((1, tm, D), lambda b, i, j: (b, i, 0)),
            pl.BlockSpec((1, HALO, D), lambda b, i, j: (b, jnp.maximum(i * hb - 1, 0), 0)),
            pl.BlockSpec((1, HALO, D), lambda b, i, j: (b, jnp.minimum((i + 1) * hb, nh - 1), 0)),
            pl.BlockSpec((1, D), lambda b, i, j: (0, 0)),
            pl.BlockSpec((1, 1, D), lambda b, i, j: (b, 0, 0)),
            pl.BlockSpec((1, 1, D), lambda b, i, j: (b, 0, 0))]


def _inproj_body(*refs, nq, nc, rope_chunks, period, eps):
    if rope_chunks:
        x_ref, xp_ref, xn_ref, g_ref, sh_ref, sc_ref, w_ref, cw_ref, cb_ref, cos_ref, sin_ref, q_o, o0, o1, o2, h_ref = refs
    else:
        x_ref, xp_ref, xn_ref, g_ref, sh_ref, sc_ref, w_ref, cw_ref, cb_ref, q_o, o0, o1, o2, h_ref = refs
    j = pl.program_id(2)
    tm = x_ref.shape[1]

    @pl.when(j == 0)
    def _():
        _halo_prologue(h_ref, x_ref, xp_ref, xn_ref, g_ref, sh_ref, sc_ref, eps)

    qkv_dot = lambda: jnp.dot(h_ref[HALO:HALO + tm, :], w_ref[...].astype(BF16), preferred_element_type=F32)
    nch = w_ref.shape[1] // LANES
    rope_tiles = sorted({ch // nch for ch in rope_chunks})
    for jt in rope_tiles:
        @pl.when(j == jt)
        def _(jt=jt):
            acc = qkv_dot()
            c, s = cos_ref[...], sin_ref[...]
            for k in range(nch):
                a = acc[:, k * LANES:(k + 1) * LANES]
                if jt * nch + k in rope_chunks:
                    a = a * c + _swap32(a) * s
                q_o[0, :, k * LANES:(k + 1) * LANES] = a.astype(q_o.dtype)

    @pl.when(functools.reduce(jnp.logical_and, [j != jt for jt in rope_tiles], j < nq))
    def _():
        q_o[0] = qkv_dot().astype(q_o.dtype)

    for k, o in enumerate((o0, o1, o2)):
        @pl.when(jnp.logical_and(j >= nq + k * nc, j < nq + (k + 1) * nc))
        def _(o=o):
            a = jnp.dot(h_ref[...], w_ref[...].astype(BF16), preferred_element_type=F32)
            o[0] = _conv3_halo(a, cw_ref, cb_ref, tm, period)


def _colmap(segs, tn):
    bounds, o = [], 0
    for s, w in segs:
        assert s % tn == 0 and w % tn == 0
        bounds.append(((o + w) // tn, (s - o) // tn))
        o += w

    def f(j):
        r = j + bounds[-1][1]
        for hi, off in reversed(bounds[:-1]):
            r = jnp.where(j < hi, j + off, r)
        return r
    return f, o


def _seg_tile(segs, cands):
    return _tile(functools.reduce(math.gcd, [v for seg in segs for v in seg if v]), cands)


def _in_proj(x, gain, shift, scale, w_in, l, qkv_segs, hy_start, C, conv_w, conv_b, rope, name, period=None):
    B, S, D = x.shape
    tm = _tile(S, (1024, 512, 256, 128))
    segs = list(qkv_segs) + [(hy_start, 3 * C)]
    tn = _seg_tile(segs + [(0, C)], (512, 256, 128))
    cmap, n_out = _colmap(segs, tn)
    nc = C // tn
    nq = n_out // tn - 3 * nc
    in_specs = _halo_specs(S, tm, D) + [
        pl.BlockSpec((None, D, tn), lambda b, i, j: (l, 0, cmap(j))),
        pl.BlockSpec((3, tn), lambda b, i, j: (0, jnp.clip(j - nq, 0, 3 * nc - 1))),
        pl.BlockSpec((1, tn), lambda b, i, j: (0, jnp.clip(j - nq, 0, 3 * nc - 1)))]
    args = [x, x, x, gain.reshape(1, D), shift, scale, w_in, conv_w, conv_b.reshape(1, 3 * C)]
    rope_chunks = ()
    if rope is not None:
        cos_t, sin_t, rope_chunks = rope
        in_specs += [pl.BlockSpec((tm, LANES), lambda b, i, j: (i, 0)),
                     pl.BlockSpec((tm, LANES), lambda b, i, j: (i, 0))]
        args += [cos_t, sin_t]
    hy_spec = lambda k: pl.BlockSpec((1, tm, tn), lambda b, i, j: (b, i, jnp.clip(j - nq - k * nc, 0, nc - 1)))
    return _call(
        functools.partial(_inproj_body, nq=nq, nc=nc, rope_chunks=frozenset(rope_chunks), period=period, eps=NORM_EPS),
        grid=(B, S // tm, n_out // tn), in_specs=in_specs,
        out_specs=[pl.BlockSpec((1, tm, tn), lambda b, i, j: (b, i, jnp.minimum(j, nq - 1))),
                   hy_spec(0), hy_spec(1), hy_spec(2)],
        out_shape=[jax.ShapeDtypeStruct((B, S, nq * tn), BF16)] + [jax.ShapeDtypeStruct((B, S, C), F32)] * 3,
        scratch=[pltpu.VMEM((tm + 2 * HALO, D), BF16)],
        sem=("parallel", "parallel", "arbitrary"), name=name)(*args)


def _ffnup_body(x_ref, xp_ref, xn_ref, g_ref, sh_ref, sc_ref, wa_ref, wu_ref, cw_ref, cb_ref, o_ref, h_ref, *, period, eps):
    tm = x_ref.shape[1]

    @pl.when(pl.program_id(2) == 0)
    def _():
        _halo_prologue(h_ref, x_ref, xp_ref, xn_ref, g_ref, sh_ref, sc_ref, eps)

    a = jnp.dot(h_ref[...], wa_ref[...].astype(BF16), preferred_element_type=F32)
    u = jnp.dot(h_ref[HALO:HALO + tm, :], wu_ref[...].astype(BF16), preferred_element_type=F32)
    c = _conv3_halo(a, cw_ref, cb_ref, tm, period)
    o_ref[0] = (c * _sigmoid(c) * u).astype(o_ref.dtype)


def _ffn_up_glu(x, gain, shift, scale, w_up, l, conv_w, conv_b, period=None):
    B, S, D = x.shape
    Fd = w_up.shape[2] // 2
    tm = _tile(S, (1024, 512, 256, 128))
    tn = _tile(Fd, (512, 256, 128))
    nj = Fd // tn
    return _call(
        functools.partial(_ffnup_body, period=period, eps=NORM_EPS), grid=(B, S // tm, nj),
        in_specs=_halo_specs(S, tm, D) + [
            pl.BlockSpec((None, D, tn), lambda b, i, j: (l, 0, j)),
            pl.BlockSpec((None, D, tn), lambda b, i, j: (l, 0, nj + j)),
            pl.BlockSpec((3, tn), lambda b, i, j: (0, j)),
            pl.BlockSpec((1, tn), lambda b, i, j: (0, j))],
        out_specs=pl.BlockSpec((1, tm, tn), lambda b, i, j: (b, i, j)),
        out_shape=jax.ShapeDtypeStruct((B, S, Fd), BF16),
        scratch=[pltpu.VMEM((tm + 2 * HALO, D), BF16)],
        sem=("parallel", "parallel", "arbitrary"), name="ffn_up_glu")(
            x, x, x, gain.reshape(1, D), shift, scale, w_up, w_up, conv_w, conv_b.reshape(1, Fd))


def _filt_trunk_body(z_ref, w1_ref, b1_ref, fr_ref, w2_ref, b2_ref, o_ref):
    a = jnp.dot(z_ref[...], w1_ref[...], precision=HP, preferred_element_type=F32) + b1_ref[...]
    a = jnp.sin(fr_ref[0:1, :] * a)
    a = jnp.dot(a, w2_ref[...], precision=HP, preferred_element_type=F32) + b2_ref[...]
    o_ref[...] = jnp.sin(fr_ref[1:2, :] * a)


def _filt_main_body(a_ref, wf_ref, wb_ref, t_ref, dl_ref, o_ref, *, n):
    hf = jnp.dot(a_ref[0:n, :], wf_ref[...], precision=HP, preferred_element_type=F32)
    hb = jnp.dot(a_ref[n:2 * n, :], wb_ref[...], precision=HP, preferred_element_type=F32)
    r = lax.broadcasted_iota(jnp.int32, hb.shape, 0)
    k = jnp.concatenate([hf, jnp.where(r == 0, 0.0, hb)], axis=0) * jnp.exp(-t_ref[...] * dl_ref[...])
    o_ref[...] = k / jnp.sum(jnp.abs(k), axis=0, keepdims=True)


def _hyena_filters(n, w1, b1, freq, w2, b2, w3):
    Hd = w1.shape[1]
    OC = w3.shape[1] // 2
    t = jnp.linspace(0.0, 1.0, n, dtype=F32)[:, None]
    w = (2.0 * math.pi / n) * jnp.arange(n, dtype=F32)[:, None]
    f = jnp.linspace(1e-4, HY_BANDS - 1, HY_BANDS, dtype=F32)[None, :]
    z = jnp.concatenate([t, jnp.cos(f * w), -jnp.sin(f * w)], axis=-1)
    fold = lambda a: jnp.concatenate([a, jnp.zeros_like(a[:1]), a[:0:-1]], axis=0)
    EP = 64
    z2 = jnp.pad(fold(z), ((0, 0), (0, EP - HY_EMB)))
    w1p = jnp.pad(w1, ((0, EP - HY_EMB), (0, 0)))
    rt = _tile(2 * n, (1024, 512, 256))
    a2 = _call(
        _filt_trunk_body, grid=(2 * n // rt,),
        in_specs=[pl.BlockSpec((rt, EP), lambda i: (i, 0)),
                  pl.BlockSpec((EP, Hd), lambda i: (0, 0)),
                  pl.BlockSpec((1, Hd), lambda i: (0, 0)),
                  pl.BlockSpec((2, Hd), lambda i: (0, 0)),
                  pl.BlockSpec((Hd, Hd), lambda i: (0, 0)),
                  pl.BlockSpec((1, Hd), lambda i: (0, 0))],
        out_specs=pl.BlockSpec((rt, Hd), lambda i: (i, 0)),
        out_shape=jax.ShapeDtypeStruct((2 * n, Hd), F32),
        sem=("parallel",), name="hyena_filter_trunk")(z2, w1p, b1.reshape(1, Hd), freq, w2, b2.reshape(1, Hd))
    tc = LANES
    t2 = jnp.broadcast_to(fold(t), (2 * n, tc))
    deltas = jnp.abs(jnp.linspace(HY_MIN_DECAY, HY_MAX_DECAY, OC, dtype=F32))[None, :]
    nc = OC // tc
    return _call(
        functools.partial(_filt_main_body, n=n), grid=(nc,),
        in_specs=[pl.BlockSpec((2 * n, Hd), lambda c: (0, 0)),
                  pl.BlockSpec((Hd, tc), lambda c: (0, c)),
                  pl.BlockSpec((Hd, tc), lambda c: (0, nc + c)),
                  pl.BlockSpec((2 * n, tc), lambda c: (0, 0)),
                  pl.BlockSpec((1, tc), lambda c: (0, c))],
        out_specs=pl.BlockSpec((2 * n, tc), lambda c: (0, c)),
        out_shape=jax.ShapeDtypeStruct((2 * n, OC), F32),
        sem=("parallel",), name="hyena_filter")(a2, w3, w3, t2, deltas)


def _embed(re, im):
    return np.block([[re, -im], [im, re]])


@functools.lru_cache(maxsize=None)
def _fft_tables(N1, N2):
    N = N1 * N2
    S1 = N1 // 2
    i1 = np.arange(N1)
    ang = -2.0 * np.pi * ((i1[:, None] * i1[None, :]) % N1) / N1
    fr, fi = np.cos(ang), np.sin(ang)
    f1_pair = _embed(fr[:, :S1], fi[:, :S1])
    f1_real = np.concatenate([fr, fi], axis=0)
    i2 = np.arange(N2)
    fidx = i1[:, None, None] + N1 * i2[None, :, None]
    ang = -2.0 * np.pi * ((fidx * i2[None, None, :]) % N) / N
    mr, mi = np.cos(ang), np.sin(ang)
    m_fwd = np.stack([_embed(mr[a], mi[a]) for a in range(N1)])
    m_inv = np.stack([_embed(mr[a].T, -mi[a].T) for a in range(N1)])
    ang = 2.0 * np.pi * ((i1[:S1, None] * i1[None, :]) % N1) / N1
    f1_inv = _embed(np.cos(ang) / N, np.sin(ang) / N)
    cvt = lambda a: jnp.asarray(a, dtype=BF16)
    return cvt(f1_pair), cvt(f1_real), cvt(m_fwd), cvt(m_inv), cvt(f1_inv)


def _lmat_body(f_ref, x_ref, o_ref):
    o_ref[0] = jnp.dot(f_ref[...], x_ref[0].astype(BF16), preferred_element_type=F32).astype(o_ref.dtype)


def _left_matmul(fm, x, out_dtype, name):
    P, K, W = x.shape
    R = fm.shape[0]
    tw = _tile(W, (4096, 2048, 1024, 512, 256, 128))
    return _call(
        _lmat_body, grid=(P, W // tw),
        in_specs=[pl.BlockSpec((R, K), lambda p, j: (0, 0)),
                  pl.BlockSpec((1, K, tw), lambda p, j: (p, 0, j))],
        out_specs=pl.BlockSpec((1, R, tw), lambda p, j: (p, 0, j)),
        out_shape=jax.ShapeDtypeStruct((P, R, W), out_dtype),
        sem=("parallel", "parallel"), name=name)(fm, x)


def _cmul(xr, xi, kr, ki):
    return xr * kr - xi * ki, xr * ki + xi * kr


def _fftmid_body(a_ref, m_ref, mi_ref, k_ref, o_ref, *, FB, N2):
    for t in range(FB):
        a = a_ref[0, :, t].reshape(2 * N2, a_ref.shape[-1])
        x = jnp.dot(m_ref[t], a, preferred_element_type=F32)
        yr, yi = _cmul(x[:N2], x[N2:], k_ref[0, t].astype(F32), k_ref[1, t].astype(F32))
        y = jnp.concatenate([yr, yi], axis=0).astype(BF16)
        g = jnp.dot(mi_ref[t], y, preferred_element_type=F32)
        o_ref[0, :, t] = g.reshape(2, N2, g.shape[-1]).astype(o_ref.dtype)


def _fftfwd_body(a_ref, m_ref, o_ref, *, FB, N2):
    for t in range(FB):
        a = a_ref[:, t].reshape(2 * N2, a_ref.shape[-1])
        x = jnp.dot(m_ref[t], a, preferred_element_type=F32)
        o_ref[:, t] = x.reshape(2, N2, x.shape[-1]).astype(o_ref.dtype)


SUB_BLOCK = 16


def _dft1_body(f_ref, x_ref, o_ref):
    xt = pltpu.einshape("ksc->skc", x_ref[0])
    r = jnp.stack([jnp.dot(f_ref[...], xt[k].astype(BF16), preferred_element_type=F32)
                   for k in range(xt.shape[0])], axis=0)
    o_ref[0] = pltpu.einshape("skc->ksc", r).astype(o_ref.dtype)


def _outer_dft(fm, x4, out_dtype, name):
    P, K, N2, C = x4.shape
    R = fm.shape[0]
    tc = _tile(C, (512, 256, 128))
    return _call(
        _dft1_body, grid=(P, N2 // SUB_BLOCK, C // tc),
        in_specs=[pl.BlockSpec((R, K), lambda p, s, c: (0, 0)),
                  pl.BlockSpec((1, K, SUB_BLOCK, tc), lambda p, s, c: (p, 0, s, c))],
        out_specs=pl.BlockSpec((1, R, SUB_BLOCK, tc), lambda p, s, c: (p, 0, s, c)),
        out_shape=jax.ShapeDtypeStruct((P, R, N2, C), out_dtype),
        sem=("parallel", "parallel", "parallel"), name=name)(fm, x4)


def _idft1_gate_body(f_ref, g_ref, z_ref, x1_ref, sk_ref, o_ref):
    gt = pltpu.einshape("ksc->skc", g_ref[0])
    y = jnp.stack([jnp.dot(f_ref[...], gt[k], preferred_element_type=F32) for k in range(gt.shape[0])], axis=0)
    y = pltpu.einshape("skc->ksc", y)
    o_ref[0] = (x1_ref[0] * (y + sk_ref[...] * z_ref[0])).astype(o_ref.dtype)


def _filter_spectrum_2stage(k, N1, N2):
    N, OC = k.shape
    _, f1_real, m_fwd, _, _ = _fft_tables(N1, N2)
    a = _outer_dft(f1_real, k.reshape(1, N1, N2, OC), BF16, "filter_dft1").reshape(2, N1, N2, OC)
    FB = _tile(N1, (8, 4, 2, 1))
    tc = _tile(OC, (512, 256, 128))
    return _call(
        functools.partial(_fftfwd_body, FB=FB, N2=N2), grid=(N1 // FB, OC // tc),
        in_specs=[pl.BlockSpec((2, FB, N2, tc), lambda f, c: (0, f, 0, c)),
                  pl.BlockSpec((FB, 2 * N2, 2 * N2), lambda f, c: (f, 0, 0))],
        out_specs=pl.BlockSpec((2, FB, N2, tc), lambda f, c: (0, f, 0, c)),
        out_shape=jax.ShapeDtypeStruct((2, N1, N2, OC), BF16),
        sem=("parallel", "parallel"), name="filter_dft2")(a, m_fwd)


def _long_conv_gate_2stage(z, x1, skip, kf, order, out_dtype, N1, N2):
    B, n, C = z.shape
    assert B % 2 == 0
    P, S1 = B // 2, N1 // 2
    f1_pair, _, m_fwd, m_inv, f1_inv = _fft_tables(N1, N2)
    z4 = z.reshape(P, 2 * S1, N2, C)
    a = _outer_dft(f1_pair, z4, BF16, "conv_dft1").reshape(P, 2, N1, N2, C)
    FB = _tile(N1, (8, 4, 2, 1))
    tc = _tile(C, (512, 256, 128))
    oc = order * (C // tc)
    g = _call(
        functools.partial(_fftmid_body, FB=FB, N2=N2), grid=(N1 // FB, C // tc, P),
        in_specs=[pl.BlockSpec((1, 2, FB, N2, tc), lambda f, c, p: (p, 0, f, 0, c)),
                  pl.BlockSpec((FB, 2 * N2, 2 * N2), lambda f, c, p: (f, 0, 0)),
                  pl.BlockSpec((FB, 2 * N2, 2 * N2), lambda f, c, p: (f, 0, 0)),
                  pl.BlockSpec((2, FB, N2, tc), lambda f, c, p: (0, f, 0, oc + c))],
        out_specs=pl.BlockSpec((1, 2, FB, N2, tc), lambda f, c, p: (p, 0, f, 0, c)),
        out_shape=jax.ShapeDtypeStruct((P, 2, N1, N2, C), BF16),
        sem=("parallel", "parallel", "arbitrary"), name="conv_dft2_mul_idft2")(a, m_fwd, m_inv, kf)
    blk = lambda rows: pl.BlockSpec((1, rows, SUB_BLOCK, tc), lambda p, s, c: (p, 0, s, c))
    out = _call(
        _idft1_gate_body, grid=(P, N2 // SUB_BLOCK, C // tc),
        in_specs=[pl.BlockSpec((2 * S1, 2 * N1), lambda p, s, c: (0, 0)),
                  blk(2 * N1), blk(2 * S1), blk(2 * S1),
                  pl.BlockSpec((1, tc), lambda p, s, c: (0, c))],
        out_specs=blk(2 * S1),
        out_shape=jax.ShapeDtypeStruct((P, 2 * S1, N2, C), out_dtype),
        sem=("parallel", "parallel", "parallel"), name="conv_idft1_gate")(
            f1_inv, g.reshape(P, 2 * N1, N2, C), z4, x1.reshape(P, 2 * S1, N2, C), skip[None, :])
    return out.reshape(B, n, C)


@functools.lru_cache(maxsize=None)
def _dft_tables(n):
    N = 2 * n
    f = np.arange(N)
    ang = -2.0 * np.pi * ((f[:, None] * f[None, :]) % N) / N
    fr, fi = np.cos(ang), np.sin(ang)
    fwd_full = np.concatenate([fr, fi], axis=0)
    fwd_half = fwd_full[:, :n]
    inv = np.concatenate([fr[:n, :], fi[:n, :]], axis=1) / N
    cvt = lambda a: jnp.asarray(a, dtype=BF16)
    return cvt(fwd_full), cvt(fwd_half), cvt(inv)


def _dftconv_body(z_ref, x1_ref, sk_ref, f_ref, fi_ref, k_ref, o_ref, *, N):
    z = z_ref[0]
    x = jnp.dot(f_ref[...], z.astype(BF16), preferred_element_type=F32)
    yr, yi = _cmul(x[:N], x[N:], k_ref[0], k_ref[1])
    y = jnp.concatenate([yr, yi], axis=0).astype(BF16)
    y = jnp.dot(fi_ref[...], y, preferred_element_type=F32)
    o_ref[0] = (x1_ref[0] * (y + sk_ref[...] * z)).astype(o_ref.dtype)


def _long_conv_gate_dense(z, x1, skip, kf, order, out_dtype):
    B, n, C = z.shape
    N = 2 * n
    _, fwd_half, inv = _dft_tables(n)
    tc = _tile(C, (256, 128))
    oc = order * (C // tc)
    return _call(
        functools.partial(_dftconv_body, N=N), grid=(C // tc, B),
        in_specs=[pl.BlockSpec((1, n, tc), lambda c, b: (b, 0, c)),
                  pl.BlockSpec((1, n, tc), lambda c, b: (b, 0, c)),
                  pl.BlockSpec((1, tc), lambda c, b: (0, c)),
                  pl.BlockSpec((2 * N, n), lambda c, b: (0, 0)),
                  pl.BlockSpec((n, 2 * N), lambda c, b: (0, 0)),
                  pl.BlockSpec((2, N, tc), lambda c, b: (0, 0, oc + c))],
        out_specs=pl.BlockSpec((1, n, tc), lambda c, b: (b, 0, c)),
        out_shape=jax.ShapeDtypeStruct((B, n, C), out_dtype),
        sem=("parallel", "arbitrary"), name="conv_dense_dft")(z, x1, skip[None, :], fwd_half, inv, kf)


def _hyena_mixer(v, x1, x2, w1, b1, freq, w2, b2, w3, skip):
    B, n, _ = v.shape
    k = _hyena_filters(n, w1, b1, freq, w2, b2, w3)
    N = 2 * n
    if N % FFT_N2 == 0 and (N // FFT_N2) >= 16:
        N1 = N // FFT_N2
        kf = _filter_spectrum_2stage(k, N1, FFT_N2)
        z = _long_conv_gate_2stage(v, x1, skip[0], kf, 0, F32, N1, FFT_N2)
        return _long_conv_gate_2stage(z, x2, skip[1], kf, 1, BF16, N1, FFT_N2)
    fwd_full, _, _ = _dft_tables(n)
    kf = _left_matmul(fwd_full, k[None], F32, "filter_dense_dft").reshape(2, N, k.shape[1])
    z = _long_conv_gate_dense(v, x1, skip[0], kf, 0, F32)
    return _long_conv_gate_dense(z, x2, skip[1], kf, 1, BF16)


def _nt(a, b):
    return lax.dot_general(a, b, (((1,), (1,)), ((), ())), preferred_element_type=F32)


def _na_body(q_ref, k_ref, v_ref, kc_ref, vc_ref, b_ref, o_ref, *, R, KR, rows, HPS, scale):
    j = pl.program_id(2)
    start = pl.multiple_of(jnp.clip(j * R - NA_WIN_R // 2, 0, rows - KR) * GRID_W, GRID_W)
    for h in range(HPS):
        hs = slice(h * HEAD_DIM, (h + 1) * HEAD_DIM)
        q = q_ref[0, :, hs]
        kw = k_ref[0, pl.ds(start, KR * GRID_W), hs]
        vw = v_ref[0, pl.ds(start, KR * GRID_W), hs]
        s = _nt(q, kw) + b_ref[h, 0]
        sc = _nt(q, kc_ref[0, :, hs])
        m = jnp.maximum(jnp.max(s, axis=-1, keepdims=True), jnp.max(sc, axis=-1, keepdims=True))
        p = jnp.exp2((s - m) * (scale * LOG2_E))
        pc = jnp.exp2((sc - m) * (scale * LOG2_E))
        l = jnp.sum(p, axis=-1, keepdims=True) + jnp.sum(pc, axis=-1, keepdims=True)
        o = jnp.dot(p.astype(BF16), vw, preferred_element_type=F32)
        o = o + jnp.dot(pc.astype(BF16), vc_ref[0, :, hs], preferred_element_type=F32)
        o_ref[0, :, hs] = (o / l).astype(o_ref.dtype)


def _na_geometry(S):
    rows = S // GRID_W
    kr = min(NA_WIN_R, rows)
    R = min(8, rows)
    KR = min(rows, R + kr)
    nb = rows // R
    assert rows % R == 0
    types = sorted({0, min(1, nb - 1), nb - 1})
    if nb > 3:
        offs = {int(np.clip(j * R - NA_WIN_R // 2, 0, rows - KR)) - j * R for j in range(1, nb - 1)}
        assert len(offs) == 1
    return rows, kr, R, KR, nb, types


def _nabias_body(rpb_ref, o_ref, tw_ref, *, plan, R, KR, inv_scale):
    W = GRID_W
    nd_r, nd_c = 2 * NA_WIN_R - 1, 2 * NA_WIN_C - 1
    base = pl.program_id(0) * (nd_r * nd_c)
    qc = lax.broadcasted_iota(jnp.int32, (W, 2 * W), 0)
    lane = lax.broadcasted_iota(jnp.int32, (W, 2 * W), 1)
    kc = lane % W
    cs = jnp.clip(qc - NA_WIN_C // 2, 0, W - NA_WIN_C)
    col_ok = jnp.logical_and(kc >= cs, kc < cs + NA_WIN_C)
    dcm = kc - qc + (NA_WIN_C - 1)
    neg = jnp.full((W, 2 * W), NEG_INF, F32)
    for dr in range(nd_r):
        acc = neg
        for dc in range(nd_c):
            acc = jnp.where(dcm == dc, rpb_ref[base + dr * nd_c + dc] * inv_scale, acc)
        tw_ref[dr] = jnp.where(col_ok, acc, NEG_INF)
    left = lane < W
    for t, per_q in enumerate(plan):
        for qr in range(R):
            for kp in range(KR // 2):
                d0, d1 = per_q[qr][kp]
                a = neg if d0 is None else tw_ref[d0]
                b = neg if d1 is None else tw_ref[d1]
                blk = neg if (d0 is None and d1 is None) else jnp.where(left, a, b)
                o_ref[0, t, qr * W:(qr + 1) * W, kp * 2 * W:(kp + 1) * 2 * W] = blk


def _na_bias_tables(rpb, S):
    rows, kr, R, KR, nb, types = _na_geometry(S)
    assert 2 * GRID_W == LANES and KR % 2 == 0
    plan = []
    for jt in types:
        start = int(np.clip(jt * R - NA_WIN_R // 2, 0, rows - KR))
        per_q = []
        for q in range(R):
            qra = jt * R + q
            ws = int(np.clip(qra - kr // 2, 0, rows - kr))
            d = [(start + k) - qra + (NA_WIN_R - 1) if ws <= start + k < ws + kr else None for k in range(KR)]
            per_q.append([(d[2 * p], d[2 * p + 1]) for p in range(KR // 2)])
        plan.append(per_q)
    L, H, nd_r, nd_c = rpb.shape
    T, QB, KB = len(types), R * GRID_W, KR * GRID_W
    return _call(
        functools.partial(_nabias_body, plan=plan, R=R, KR=KR, inv_scale=HEAD_DIM ** 0.5), grid=(L * H,),
        in_specs=[pl.BlockSpec(memory_space=pltpu.SMEM)],
        out_specs=pl.BlockSpec((1, T, QB, KB), lambda i: (i, 0, 0, 0)),
        out_shape=jax.ShapeDtypeStruct((L * H, T, QB, KB), F32),
        scratch=[pltpu.VMEM((nd_r, GRID_W, 2 * GRID_W), F32)],
        sem=("parallel",), name="na_bias_table")(rpb.reshape(-1).astype(F32))


def _na_attention(qkv, qkv_c, bias, l, offs, H):
    B, S, _ = qkv.shape
    CTX = qkv_c.shape[1]
    rows, kr, R, KR, nb, types = _na_geometry(S)
    T = len(types)
    QB, KB = R * GRID_W, KR * GRID_W
    HPS = next(n for n in (4, 2, 1) if H % n == 0)
    HW = HPS * HEAD_DIM
    assert all(offs[n] % HW == 0 for n in ("na_k", "na_v", "na_q"))
    ok_, ov_, oq_ = (offs[n] // HW for n in ("na_k", "na_v", "na_q"))

    def btype(j):
        if T == nb:
            return j
        return jnp.where(j == 0, 0, jnp.where(j == nb - 1, T - 1, 1))

    return _call(
        functools.partial(_na_body, R=R, KR=KR, rows=rows, HPS=HPS, scale=HEAD_DIM ** -0.5),
        grid=(B, H // HPS, nb),
        in_specs=[pl.BlockSpec((1, QB, HW), lambda b, h, j: (b, j, oq_ + h)),
                  pl.BlockSpec((1, S, HW), lambda b, h, j: (b, 0, ok_ + h)),
                  pl.BlockSpec((1, S, HW), lambda b, h, j: (b, 0, ov_ + h)),
                  pl.BlockSpec((1, CTX, HW), lambda b, h, j: (b, 0, ok_ + h)),
                  pl.BlockSpec((1, CTX, HW), lambda b, h, j: (b, 0, ov_ + h)),
                  pl.BlockSpec((HPS, 1, QB, KB), lambda b, h, j: (l * (H // HPS) + h, btype(j), 0, 0))],
        out_specs=pl.BlockSpec((1, QB, HW), lambda b, h, j: (b, j, h)),
        out_shape=jax.ShapeDtypeStruct((B, S, H * HEAD_DIM), BF16),
        sem=("parallel", "parallel", "arbitrary"), name="na_attention")(qkv, qkv, qkv, qkv_c, qkv_c, bias)


def _stack_heads(q2, G):
    return jnp.concatenate([q2[:, g * HEAD_DIM:(g + 1) * HEAD_DIM] for g in range(G)], axis=0)


def _unstack_heads(o, G, n):
    return jnp.concatenate([o[g * n:(g + 1) * n] for g in range(G)], axis=1)


def _sink_column(sink_ref, h0, G, n):
    return jnp.concatenate([jnp.full((n, 1), sink_ref[h0 + g], F32) for g in range(G)], axis=0)


def _swa_body(sink_ref, q_ref, k_ref, v_ref, kc_ref, vc_ref, o_ref, *, QB, KB, S, G, HPS, QPS, scale):
    hb = pl.program_id(1)
    for qb in range(QPS):
        j = pl.program_id(2) * QPS + qb
        rows = slice(qb * QB, (qb + 1) * QB)
        start = pl.multiple_of(jnp.clip(j * QB - GQA_WINDOW, 0, S - KB), LANES)
        qpos = j * QB + lax.broadcasted_iota(jnp.int32, (QB, KB), 0)
        kpos = start + lax.broadcasted_iota(jnp.int32, (QB, KB), 1)
        mask = jnp.where(jnp.abs(qpos - kpos) <= GQA_WINDOW, 0.0, NEG_INF)
        mask = jnp.concatenate([mask] * G, axis=0)
        for h in range(HPS):
            hs = slice(h * HEAD_DIM, (h + 1) * HEAD_DIM)
            qs = slice(h * G * HEAD_DIM, (h + 1) * G * HEAD_DIM)
            q = _stack_heads(q_ref[0, rows, qs], G)
            kw = k_ref[0, pl.ds(start, KB), hs]
            vw = v_ref[0, pl.ds(start, KB), hs]
            s = _nt(q, kw) + mask
            sc = _nt(q, kc_ref[0, :, hs])
            sk = _sink_column(sink_ref, (hb * HPS + h) * G, G, QB) * (1.0 / scale)
            m = jnp.maximum(jnp.maximum(jnp.max(s, axis=-1, keepdims=True), jnp.max(sc, axis=-1, keepdims=True)), sk)
            p = jnp.exp2((s - m) * (scale * LOG2_E))
            pc = jnp.exp2((sc - m) * (scale * LOG2_E))
            l = (jnp.sum(p, axis=-1, keepdims=True) + jnp.sum(pc, axis=-1, keepdims=True)
                 + jnp.exp2((sk - m) * (scale * LOG2_E)))
            o = jnp.dot(p.astype(BF16), vw, preferred_element_type=F32)
            o = o + jnp.dot(pc.astype(BF16), vc_ref[0, :, hs], preferred_element_type=F32)
            o_ref[0, rows, qs] = _unstack_heads(o / l, G, QB).astype(o_ref.dtype)


def _swa_attention(qkv, qkv_c, sink, offs, KVH):
    B, S, _ = qkv.shape
    CTX = qkv_c.shape[1]
    G = GQA_GROUP
    QB = _tile(S, (512, 256, 128))
    KB = min(S, QB + 2 * GQA_WINDOW)
    HPS = 2 if KVH % 2 == 0 else 1
    QPS = 2 if (S // QB) % 2 == 0 else 1
    HW = HPS * HEAD_DIM
    assert offs["sw_k"] % HW == 0 and offs["sw_v"] % HW == 0 and offs["sw_q"] % (G * HW) == 0
    ok_, ov_, oq_ = offs["sw_k"] // HW, offs["sw_v"] // HW, offs["sw_q"] // (G * HW)
    return _call(
        functools.partial(_swa_body, QB=QB, KB=KB, S=S, G=G, HPS=HPS, QPS=QPS, scale=HEAD_DIM ** -0.5),
        grid=(B, KVH // HPS, S // (QB * QPS)),
        in_specs=[pl.BlockSpec(memory_space=pltpu.SMEM),
                  pl.BlockSpec((1, QPS * QB, G * HW), lambda b, h, j: (b, j, oq_ + h)),
                  pl.BlockSpec((1, S, HW), lambda b, h, j: (b, 0, ok_ + h)),
                  pl.BlockSpec((1, S, HW), lambda b, h, j: (b, 0, ov_ + h)),
                  pl.BlockSpec((1, CTX, HW), lambda b, h, j: (b, 0, ok_ + h)),
                  pl.BlockSpec((1, CTX, HW), lambda b, h, j: (b, 0, ov_ + h))],
        out_specs=pl.BlockSpec((1, QPS * QB, G * HW), lambda b, h, j: (b, j, h)),
        out_shape=jax.ShapeDtypeStruct((B, S, KVH * G * HEAD_DIM), BF16),
        sem=("parallel", "parallel", "arbitrary"), name="swa_attention")(sink, qkv, qkv, qkv, qkv_c, qkv_c)


def _cattn_body(sink_ref, q_ref, k_ref, v_ref, o_ref, *, G, use_sink, scale):
    h = pl.program_id(1)
    n = q_ref.shape[1]
    q = _stack_heads(q_ref[0], G)
    s = _nt(q, k_ref[0]) * scale
    m = jnp.max(s, axis=-1, keepdims=True)
    if use_sink:
        sk = _sink_column(sink_ref, h * G, G, n)
        m = jnp.maximum(m, sk)
    p = jnp.exp(s - m)
    l = jnp.sum(p, axis=-1, keepdims=True)
    if use_sink:
        l = l + jnp.exp(sk - m)
    o = jnp.dot(p.astype(BF16), v_ref[0], preferred_element_type=F32)
    o_ref[0] = _unstack_heads(o / l, G, n).astype(o_ref.dtype)


def _ctx_attention(qkv_c, sink, oq, ok, ov, KVH, G, use_sink):
    B, n, _ = qkv_c.shape
    oq_, ok_, ov_ = oq // (G * HEAD_DIM), ok // HEAD_DIM, ov // HEAD_DIM
    return _call(
        functools.partial(_cattn_body, G=G, use_sink=use_sink, scale=HEAD_DIM ** -0.5),
        grid=(B, KVH),
        in_specs=[pl.BlockSpec(memory_space=pltpu.SMEM),
                  pl.BlockSpec((1, n, G * HEAD_DIM), lambda b, h: (b, 0, oq_ + h)),
                  pl.BlockSpec((1, n, HEAD_DIM), lambda b, h: (b, 0, ok_ + h)),
                  pl.BlockSpec((1, n, HEAD_DIM), lambda b, h: (b, 0, ov_ + h))],
        out_specs=pl.BlockSpec((1, n, G * HEAD_DIM), lambda b, h: (b, 0, h)),
        out_shape=jax.ShapeDtypeStruct((B, n, KVH * G * HEAD_DIM), BF16),
        sem=("parallel", "parallel"), name="ctx_attention")(sink, qkv_c, qkv_c, qkv_c)


def _merge_body(x_ref, g_ref, sh_ref, sc_ref, yh, yn, ys, wgh, wgn, wgs, wh, wn, ws, o_ref, h_ref, *, eps):
    @pl.when(pl.program_id(2) == 0)
    def _():
        h_ref[...] = _norm_modulate(x_ref[0], g_ref, sh_ref, sc_ref, eps).astype(BF16)

    h = h_ref[...]
    gate = lambda wg: _sigmoid(jnp.dot(h, wg[...], preferred_element_type=F32))
    m = gate(wgh) * jnp.dot(yh[0], wh[...], preferred_element_type=F32)
    m = m + gate(wgn) * jnp.dot(yn[0], wn[...], preferred_element_type=F32)
    m = m + gate(wgs) * jnp.dot(ys[0], ws[...], preferred_element_type=F32)
    o_ref[0] = m.astype(o_ref.dtype)


def _merge_branches(x, gain, shift, scale, w_gate, y_hy, y_na, y_sw, w_br, l):
    B, S, D = x.shape
    widths = (y_hy.shape[2], y_na.shape[2], y_sw.shape[2])
    starts = (0, widths[0], widths[0] + widths[1])
    assert all(s % w == 0 for s, w in zip(starts, widths))
    tm = _tile(S, (512, 256, 128))
    tn = _tile(D, (512, 256, 128))
    nj = D // tn
    yspec = lambda y: pl.BlockSpec((1, tm, y.shape[2]), lambda b, i, j: (b, i, 0))
    gspec = lambda k: pl.BlockSpec((None, D, tn), lambda b, i, j: (l, 0, k * nj + j))
    wspec = lambda k: pl.BlockSpec((None, widths[k], tn), lambda b, i, j: (l, starts[k] // widths[k], j))
    return _call(
        functools.partial(_merge_body, eps=NORM_EPS), grid=(B, S // tm, nj),
        in_specs=[pl.BlockSpec((1, tm, D), lambda b, i, j: (b, i, 0)),
                  pl.BlockSpec((1, D), lambda b, i, j: (0, 0)),
                  pl.BlockSpec((1, 1, D), lambda b, i, j: (b, 0, 0)),
                  pl.BlockSpec((1, 1, D), lambda b, i, j: (b, 0, 0)),
                  yspec(y_hy), yspec(y_na), yspec(y_sw), gspec(0), gspec(1), gspec(2),
                  wspec(0), wspec(1), wspec(2)],
        out_specs=pl.BlockSpec((1, tm, tn), lambda b, i, j: (b, i, j)),
        out_shape=jax.ShapeDtypeStruct((B, S, D), BF16),
        scratch=[pltpu.VMEM((tm, D), BF16)],
        sem=("parallel", "parallel", "arbitrary"), name="gated_merge")(
            x, gain.reshape(1, D), shift, scale, y_hy, y_na, y_sw, w_gate, w_gate, w_gate, w_br, w_br, w_br)


def _mmres_body(a_ref, w_ref, x_ref, g_ref, mg_ref, o_ref, *, nk, eps):
    k = pl.program_id(2)
    part = lambda: jnp.dot(a_ref[0], w_ref[...], preferred_element_type=F32)

    def finish():
        y = o_ref[0]
        yn = y * lax.rsqrt(jnp.mean(y * y, axis=-1, keepdims=True) + eps) * g_ref[...]
        o_ref[0] = x_ref[0] + mg_ref[0] * yn

    if nk == 1:
        o_ref[0] = part()
        finish()
        return

    @pl.when(k == 0)
    def _():
        o_ref[0] = part()

    @pl.when(k > 0)
    def _():
        o_ref[0] += part()

    @pl.when(k == nk - 1)
    def _():
        finish()


def _matmul_norm_residual(a, w, l, x, gain, mgate, name):
    B, S, K = a.shape
    D = w.shape[2]
    if K <= 2048:
        tm, tk = _tile(S, (512, 256, 128)), K
    else:
        tm = _tile(S, (1024, 512, 256, 128))
        tk = max(t for t in range(LANES, 1537, LANES) if K % t == 0)
    nk = K // tk
    est = 2 * (tm * tk * 2 + tk * D * 2 + 2 * tm * D * 4)
    return _call(
        functools.partial(_mmres_body, nk=nk, eps=NORM_EPS), grid=(B, S // tm, nk),
        vmem=VMEM_LIMIT_LARGE if est > VMEM_LIMIT - VMEM_TEMP_RESERVE else VMEM_LIMIT,
        in_specs=[pl.BlockSpec((1, tm, tk), lambda b, i, k: (b, i, k)),
                  pl.BlockSpec((None, tk, D), lambda b, i, k: (l, k, 0)),
                  pl.BlockSpec((1, tm, D), lambda b, i, k: (b, i, 0)),
                  pl.BlockSpec((1, D), lambda b, i, k: (0, 0)),
                  pl.BlockSpec((1, 1, D), lambda b, i, k: (b, 0, 0))],
        out_specs=pl.BlockSpec((1, tm, D), lambda b, i, k: (b, i, 0)),
        out_shape=jax.ShapeDtypeStruct((B, S, D), F32),
        sem=("parallel", "parallel", "arbitrary"), name=name)(a, w, x, gain.reshape(1, D), mgate)


def _rope_tables(n):
    t = jnp.arange(n)
    row = (t // GRID_W).astype(F32)
    col = (t % GRID_W).astype(F32)
    per_axis = HEAD_DIM // 2
    inv = ROPE_BASE ** (-jnp.arange(0, per_axis, 2, dtype=F32) / per_axis)
    ar, ac = row[:, None] * inv, col[:, None] * inv
    cos_t = jnp.concatenate([jnp.cos(ar), jnp.cos(ar), jnp.cos(ac), jnp.cos(ac)], axis=1)
    sin_t = jnp.concatenate([-jnp.sin(ar), jnp.sin(ar), -jnp.sin(ac), jnp.sin(ac)], axis=1)
    return cos_t, sin_t


def kernel(x, c, ctx, c_ctx, w_mod, b_mod, norm_gains, w_in, hy_conv_w, hy_conv_b, hy_w1, hy_b1, hy_freq, hy_w2, hy_b2, hy_w3, hy_skip, na_rpb, swa_sink, w_branch, w_out, ffn_w_up, ffn_conv_w, ffn_conv_b, ffn_w_down):
    B, S, D = x.shape
    L = w_mod.shape[0]
    C = hy_skip.shape[-1]
    H_na = na_rpb.shape[1]
    H_q = swa_sink.shape[1]
    KVH = H_q // GQA_GROUP
    NA_W, QW, KVW = H_na * HEAD_DIM, H_q * HEAD_DIM, KVH * HEAD_DIM
    KV_COLS = 2 * NA_W + 2 * KVW
    qkv_segs = [(0, KV_COLS), (KV_COLS + 3 * C, NA_W + QW)]
    gate_start = KV_COLS + 3 * C + NA_W + QW
    offs = {"na_k": 0, "na_v": NA_W, "sw_k": 2 * NA_W, "sw_v": 2 * NA_W + KVW,
            "na_q": KV_COLS, "sw_q": KV_COLS + NA_W}
    rope_chunks = (list(range(offs["sw_k"] // LANES, (offs["sw_k"] + KVW) // LANES))
                   + list(range(offs["sw_q"] // LANES, (offs["sw_q"] + QW) // LANES)))
    cos_t, sin_t = _rope_tables(S)

    w_out_b, w_br_b, w_dn_b = w_out.astype(BF16), w_branch.astype(BF16), ffn_w_down.astype(BF16)
    w_gate_b = w_in[:, :, gate_start:].astype(BF16)
    na_bias = _na_bias_tables(na_rpb, S)
    CTX = ctx.shape[1]
    flat = lambda t: t.reshape(1, -1, t.shape[-1])
    unflat = lambda t: t.reshape(B, CTX, t.shape[-1])

    R = -(-(B + 1) // 8) * 8
    cc = jnp.concatenate([c, c_ctx[None, :], jnp.zeros((R - B - 1, D), F32)], axis=0)
    mods = _modulation(cc, w_mod, b_mod)

    xc = ctx
    for l in range(L):
        mod = [mods[l, :B, k * D:(k + 1) * D].reshape(B, 1, D) for k in range(6)]
        mod_c = [mods[l, B, k * D:(k + 1) * D].reshape(1, 1, D) for k in range(6)]
        g = norm_gains[l]
        hy_p = (hy_w1[l], hy_b1[l], hy_freq[l], hy_w2[l], hy_b2[l], hy_w3[l], hy_skip[l])

        qkv_c, vc, x1c, x2c = map(unflat, _in_proj(flat(xc), g[0], mod_c[0], mod_c[1], w_in, l, qkv_segs, KV_COLS, C,
                                                   hy_conv_w[l], hy_conv_b[l], None, "ctx_in_proj", period=CTX))
        qkv, v, x1, x2 = _in_proj(x, g[0], mod[0], mod[1], w_in, l, qkv_segs, KV_COLS, C,
                                  hy_conv_w[l], hy_conv_b[l], (cos_t, sin_t, rope_chunks), "in_proj")
        y_hy = _hyena_mixer(v, x1, x2, *hy_p)
        y_na = _na_attention(qkv, qkv_c, na_bias, l, offs, H_na)
        y_sw = _swa_attention(qkv, qkv_c, swa_sink[l], offs, KVH)
        m = _merge_branches(x, g[0], mod[0], mod[1], w_gate_b, y_hy, y_na, y_sw, w_br_b, l)
        x = _matmul_norm_residual(m, w_out_b, l, x, g[1], mod[2], "out_proj_residual")
        gl = _ffn_up_glu(x, g[2], mod[3], mod[4], ffn_w_up, l, ffn_conv_w[l], ffn_conv_b[l])
        x = _matmul_norm_residual(gl, w_dn_b, l, x, g[3], mod[5], "ffn_down_residual")

        if l < L - 1:
            yc_hy = _hyena_mixer(vc, x1c, x2c, *hy_p)
            yc_na = _ctx_attention(qkv_c, swa_sink[l], offs["na_q"], offs["na_k"], offs["na_v"], H_na, 1, False)
            yc_sw = _ctx_attention(qkv_c, swa_sink[l], offs["sw_q"], offs["sw_k"], offs["sw_v"], KVH, GQA_GROUP, True)
            mc = _merge_branches(flat(xc), g[0], mod_c[0], mod_c[1], w_gate_b,
                                 flat(yc_hy), flat(yc_na), flat(yc_sw), w_br_b, l)
            xcf = _matmul_norm_residual(mc, w_out_b, l, flat(xc), g[1], mod_c[2], "ctx_out_proj_residual")
            gl_c = _ffn_up_glu(xcf, g[2], mod_c[3], mod_c[4], ffn_w_up, l, ffn_conv_w[l], ffn_conv_b[l], period=CTX)
            xc = unflat(_matmul_norm_residual(gl_c, w_dn_b, l, xcf, g[3], mod_c[5], "ctx_ffn_down_residual"))
    return x
```

```python
import functools
import math

import numpy as np
import jax
import jax.numpy as jnp
from jax import lax
from jax.experimental import pallas as pl
from jax.experimental.pallas import tpu as pltpu

F32 = jnp.float32
BF16 = jnp.bfloat16
NEG_INF = -1e30

GRID_W = 64
HEAD_DIM = 128
HY_EMB = 33
HY_BANDS = (HY_EMB - 1) // 2
HY_FAST_DECAY = 0.3
HY_SLOW_DECAY = 1.5
HY_DECAY_TARGET = 1e-2
HY_MAX_DECAY = math.log(HY_DECAY_TARGET) / HY_FAST_DECAY
HY_MIN_DECAY = math.log(HY_DECAY_TARGET) / HY_SLOW_DECAY
NA_WIN_R = 8
NA_WIN_C = 16
GQA_GROUP = 2
GQA_WINDOW = 128
ROPE_BASE = 10000.0
NORM_EPS = 1e-6

LANES = 128
FFT_N2 = 128
VMEM_LIMIT = 56 * 1024 * 1024
VMEM_LIMIT_LARGE = 60 * 1024 * 1024
VMEM_TEMP_RESERVE = 8 * 1024 * 1024
LOG2_E = math.log2(math.e)


def _call(body, *, grid, in_specs, out_specs, out_shape, scratch=(), sem, name, vmem=VMEM_LIMIT):
    return pl.pallas_call(
        body, grid=grid, in_specs=in_specs, out_specs=out_specs, out_shape=out_shape,
        scratch_shapes=list(scratch),
        compiler_params=pltpu.CompilerParams(dimension_semantics=sem, vmem_limit_bytes=vmem),
        name=name)


def _tile(n, cands):
    for c in cands:
        if n % c == 0:
            return c
    raise ValueError(f"no tile for {n} in {cands}")


def _sigmoid(x):
    return 1.0 / (1.0 + jnp.exp(-x))


def _mod_body(c_ref, w_ref, b_ref, o_ref):
    c = c_ref[...]
    s = (c * _sigmoid(c)).astype(BF16)
    o_ref[0] = jnp.dot(s, w_ref[0].astype(BF16), preferred_element_type=F32) + b_ref[0]


def _modulation(cc, w_mod, b_mod):
    L, D, N = w_mod.shape
    R = cc.shape[0]
    tn = _tile(N, (1024, 512, 256, 128))
    return _call(
        _mod_body, grid=(L, N // tn),
        in_specs=[pl.BlockSpec((R, D), lambda l, j: (0, 0)),
                  pl.BlockSpec((1, D, tn), lambda l, j: (l, 0, j)),
                  pl.BlockSpec((1, 1, tn), lambda l, j: (l, 0, j))],
        out_specs=pl.BlockSpec((1, R, tn), lambda l, j: (l, 0, j)),
        out_shape=jax.ShapeDtypeStruct((L, R, N), F32),
        sem=("parallel", "parallel"), name="modulation")(cc, w_mod, b_mod.reshape(L, 1, N))


def _swap32(a):
    lane = lax.broadcasted_iota(jnp.int32, a.shape, 1)
    return jnp.where((lane & 32) == 0, pltpu.roll(a, 96, 1), pltpu.roll(a, 32, 1))


def _norm_modulate(x, g_ref, sh_ref, sc_ref, eps):
    y = x * lax.rsqrt(jnp.mean(x * x, axis=-1, keepdims=True) + eps) * g_ref[...]
    return y * (1.0 + sc_ref[0]) + sh_ref[0]


HALO = 16


def _halo_prologue(h_ref, x_ref, xp_ref, xn_ref, g_ref, sh_ref, sc_ref, eps):
    i, tm = pl.program_id(1), x_ref.shape[1]
    nm = lambda x: _norm_modulate(x, g_ref, sh_ref, sc_ref, eps)
    keep_prev = (i > 0).astype(F32)
    keep_next = (i < pl.num_programs(1) - 1).astype(F32)
    h_ref[0:HALO, :] = (nm(xp_ref[0]) * keep_prev).astype(BF16)
    h_ref[HALO:HALO + tm, :] = nm(x_ref[0]).astype(BF16)
    h_ref[HALO + tm:, :] = (nm(xn_ref[0]) * keep_next).astype(BF16)


def _conv3_halo(a, cw_ref, cb_ref, tm, period):
    rows = a.shape[0]
    up = pltpu.roll(a, 1, 0)[HALO:HALO + tm]
    un = pltpu.roll(a, rows - 1, 0)[HALO:HALO + tm]
    if period is not None:
        t = (pl.program_id(1) * tm + lax.broadcasted_iota(jnp.int32, up.shape, 0)) % period
        up = jnp.where(t == 0, 0.0, up)
        un = jnp.where(t == period - 1, 0.0, un)
    return up * cw_ref[0:1, :] + a[HALO:HALO + tm] * cw_ref[1:2, :] + un * cw_ref[2:3, :] + cb_ref[...]


def _halo_specs(S, tm, D):
    hb, nh = tm // HALO, S // HALO
    return [pl.BlockSpec((1, tm, D), lambda b, i, j: (b, i, 0)),
            pl.BlockSpec((1, HALO, D), lambda b, i, j: (b, jnp.maximum(i * hb - 1, 0), 0)),
            pl.BlockSpec((1, HALO, D), lambda b, i, j: (b, jnp.minimum((i + 1) * hb, nh - 1), 0)),
            pl.BlockSpec((1, D), lambda b, i, j: (0, 0)),
            pl.BlockSpec((1, 1, D), lambda b, i, j: (b, 0, 0)),
            pl.BlockSpec((1, 1, D), lambda b, i, j: (b, 0, 0))]


def _inproj_body(*refs, nq, nc, rope_chunks, period, eps):
    if rope_chunks:
        x_ref, xp_ref, xn_ref, g_ref, sh_ref, sc_ref, w_ref, cw_ref, cb_ref, cos_ref, sin_ref, q_o, o0, o1, o2, h_ref = refs
    else:
        x_ref, xp_ref, xn_ref, g_ref, sh_ref, sc_ref, w_ref, cw_ref, cb_ref, q_o, o0, o1, o2, h_ref = refs
    j = pl.program_id(2)
    tm = x_ref.shape[1]

    @pl.when(j == 0)
    def _():
        _halo_prologue(h_ref, x_ref, xp_ref, xn_ref, g_ref, sh_ref, sc_ref, eps)

    qkv_dot = lambda: jnp.dot(h_ref[HALO:HALO + tm, :], w_ref[...].astype(BF16), preferred_element_type=F32)
    nch = w_ref.shape[1] // LANES
    rope_tiles = sorted({ch // nch for ch in rope_chunks})
    for jt in rope_tiles:
        @pl.when(j == jt)
        def _(jt=jt):
            acc = qkv_dot()
            c, s = cos_ref[...], sin_ref[...]
            for k in range(nch):
                a = acc[:, k * LANES:(k + 1) * LANES]
                if jt * nch + k in rope_chunks:
                    a = a * c + _swap32(a) * s
                q_o[0, :, k * LANES:(k + 1) * LANES] = a.astype(q_o.dtype)

    @pl.when(functools.reduce(jnp.logical_and, [j != jt for jt in rope_tiles], j < nq))
    def _():
        q_o[0] = qkv_dot().astype(q_o.dtype)

    for k, o in enumerate((o0, o1, o2)):
        @pl.when(jnp.logical_and(j >= nq + k * nc, j < nq + (k + 1) * nc))
        def _(o=o):
            a = jnp.dot(h_ref[...], w_ref[...].astype(BF16), preferred_element_type=F32)
            o[0] = _conv3_halo(a, cw_ref, cb_ref, tm, period)


def _colmap(segs, tn):
    bounds, o = [], 0
    for s, w in segs:
        assert s % tn == 0 and w % tn == 0
        bounds.append(((o + w) // tn, (s - o) // tn))
        o += w

    def f(j):
        r = j + bounds[-1][1]
        for hi, off in reversed(bounds[:-1]):
            r = jnp.where(j < hi, j + off, r)
        return r
    return f, o


def _seg_tile(segs, cands):
    return _tile(functools.reduce(math.gcd, [v for seg in segs for v in seg if v]), cands)


def _in_proj(x, gain, shift, scale, w_in, l, qkv_segs, hy_start, C, conv_w, conv_b, rope, name, period=None):
    B, S, D = x.shape
    tm = _tile(S, (1024, 512, 256, 128))
    segs = list(qkv_segs) + [(hy_start, 3 * C)]
    tn = _seg_tile(segs + [(0, C)], (512, 256, 128))
    cmap, n_out = _colmap(segs, tn)
    nc = C // tn
    nq = n_out // tn - 3 * nc
    in_specs = _halo_specs(S, tm, D) + [
        pl.BlockSpec((None, D, tn), lambda b, i, j: (l, 0, cmap(j))),
        pl.BlockSpec((3, tn), lambda b, i, j: (0, jnp.clip(j - nq, 0, 3 * nc - 1))),
        pl.BlockSpec((1, tn), lambda b, i, j: (0, jnp.clip(j - nq, 0, 3 * nc - 1)))]
    args = [x, x, x, gain.reshape(1, D), shift, scale, w_in, conv_w, conv_b.reshape(1, 3 * C)]
    rope_chunks = ()
    if rope is not None:
        cos_t, sin_t, rope_chunks = rope
        in_specs += [pl.BlockSpec((tm, LANES), lambda b, i, j: (i, 0)),
                     pl.BlockSpec((tm, LANES), lambda b, i, j: (i, 0))]
        args += [cos_t, sin_t]
    hy_spec = lambda k: pl.BlockSpec((1, tm, tn), lambda b, i, j: (b, i, jnp.clip(j - nq - k * nc, 0, nc - 1)))
    return _call(
        functools.partial(_inproj_body, nq=nq, nc=nc, rope_chunks=frozenset(rope_chunks), period=period, eps=NORM_EPS),
        grid=(B, S // tm, n_out // tn), in_specs=in_specs,
        out_specs=[pl.BlockSpec((1, tm, tn), lambda b, i, j: (b, i, jnp.minimum(j, nq - 1))),
                   hy_spec(0), hy_spec(1), hy_spec(2)],
        out_shape=[jax.ShapeDtypeStruct((B, S, nq * tn), BF16)] + [jax.ShapeDtypeStruct((B, S, C), F32)] * 3,
        scratch=[pltpu.VMEM((tm + 2 * HALO, D), BF16)],
        sem=("parallel", "parallel", "arbitrary"), name=name)(*args)


def _ffnup_body(x_ref, xp_ref, xn_ref, g_ref, sh_ref, sc_ref, wa_ref, wu_ref, cw_ref, cb_ref, o_ref, h_ref, *, period, eps):
    tm = x_ref.shape[1]

    @pl.when(pl.program_id(2) == 0)
    def _():
        _halo_prologue(h_ref, x_ref, xp_ref, xn_ref, g_ref, sh_ref, sc_ref, eps)

    a = jnp.dot(h_ref[...], wa_ref[...].astype(BF16), preferred_element_type=F32)
    u = jnp.dot(h_ref[HALO:HALO + tm, :], wu_ref[...].astype(BF16), preferred_element_type=F32)
    c = _conv3_halo(a, cw_ref, cb_ref, tm, period)
    o_ref[0] = (c * _sigmoid(c) * u).astype(o_ref.dtype)


def _ffn_up_glu(x, gain, shift, scale, w_up, l, conv_w, conv_b, period=None):
    B, S, D = x.shape
    Fd = w_up.shape[2] // 2
    tm = _tile(S, (1024, 512, 256, 128))
    tn = _tile(Fd, (512, 256, 128))
    nj = Fd // tn
    return _call(
        functools.partial(_ffnup_body, period=period, eps=NORM_EPS), grid=(B, S // tm, nj),
        in_specs=_halo_specs(S, tm, D) + [
            pl.BlockSpec((None, D, tn), lambda b, i, j: (l, 0, j)),
            pl.BlockSpec((None, D, tn), lambda b, i, j: (l, 0, nj + j)),
            pl.BlockSpec((3, tn), lambda b, i, j: (0, j)),
            pl.BlockSpec((1, tn), lambda b, i, j: (0, j))],
        out_specs=pl.BlockSpec((1, tm, tn), lambda b, i, j: (b, i, j)),
        out_shape=jax.ShapeDtypeStruct((B, S, Fd), BF16),
        scratch=[pltpu.VMEM((tm + 2 * HALO, D), BF16)],
        sem=("parallel", "parallel", "arbitrary"), name="ffn_up_glu")(
            x, x, x, gain.reshape(1, D), shift, scale, w_up, w_up, conv_w, conv_b.reshape(1, Fd))


def _dot3(a, w):
    ah, wh = a.astype(BF16), w.astype(BF16)
    al, wl = (a - ah.astype(F32)).astype(BF16), (w - wh.astype(F32)).astype(BF16)
    d = lambda p, q: jnp.dot(p, q, preferred_element_type=F32)
    return d(ah, wh) + d(al, wh) + d(ah, wl)


def _filt_trunk_body(z_ref, w1_ref, b1_ref, fr_ref, w2_ref, b2_ref, o_ref):
    a = _dot3(z_ref[...], w1_ref[...]) + b1_ref[...]
    a = jnp.sin(fr_ref[0:1, :] * a)
    a = _dot3(a, w2_ref[...]) + b2_ref[...]
    o_ref[...] = jnp.sin(fr_ref[1:2, :] * a)


def _filt_main_body(a_ref, wf_ref, wb_ref, t_ref, dl_ref, o_ref, *, n):
    hf = _dot3(a_ref[0:n, :], wf_ref[...])
    hb = _dot3(a_ref[n:2 * n, :], wb_ref[...])
    r = lax.broadcasted_iota(jnp.int32, hb.shape, 0)
    k = jnp.concatenate([hf, jnp.where(r == 0, 0.0, hb)], axis=0) * jnp.exp(-t_ref[...] * dl_ref[...])
    o_ref[...] = k / jnp.sum(jnp.abs(k), axis=0, keepdims=True)


def _hyena_filters(n, w1, b1, freq, w2, b2, w3):
    Hd = w1.shape[1]
    OC = w3.shape[1] // 2
    t = jnp.linspace(0.0, 1.0, n, dtype=F32)[:, None]
    w = (2.0 * math.pi / n) * jnp.arange(n, dtype=F32)[:, None]
    f = jnp.linspace(1e-4, HY_BANDS - 1, HY_BANDS, dtype=F32)[None, :]
    z = jnp.concatenate([t, jnp.cos(f * w), -jnp.sin(f * w)], axis=-1)
    fold = lambda a: jnp.concatenate([a, jnp.zeros_like(a[:1]), a[:0:-1]], axis=0)
    EP = 64
    z2 = jnp.pad(fold(z), ((0, 0), (0, EP - HY_EMB)))
    w1p = jnp.pad(w1, ((0, EP - HY_EMB), (0, 0)))
    rt = _tile(2 * n, (1024, 512, 256))
    a2 = _call(
        _filt_trunk_body, grid=(2 * n // rt,),
        in_specs=[pl.BlockSpec((rt, EP), lambda i: (i, 0)),
                  pl.BlockSpec((EP, Hd), lambda i: (0, 0)),
                  pl.BlockSpec((1, Hd), lambda i: (0, 0)),
                  pl.BlockSpec((2, Hd), lambda i: (0, 0)),
                  pl.BlockSpec((Hd, Hd), lambda i: (0, 0)),
                  pl.BlockSpec((1, Hd), lambda i: (0, 0))],
        out_specs=pl.BlockSpec((rt, Hd), lambda i: (i, 0)),
        out_shape=jax.ShapeDtypeStruct((2 * n, Hd), F32),
        sem=("parallel",), name="hyena_filter_trunk")(z2, w1p, b1.reshape(1, Hd), freq, w2, b2.reshape(1, Hd))
    tc = LANES
    t2 = jnp.broadcast_to(fold(t), (2 * n, tc))
    deltas = jnp.abs(jnp.linspace(HY_MIN_DECAY, HY_MAX_DECAY, OC, dtype=F32))[None, :]
    nc = OC // tc
    return _call(
        functools.partial(_filt_main_body, n=n), grid=(nc,),
        in_specs=[pl.BlockSpec((2 * n, Hd), lambda c: (0, 0)),
                  pl.BlockSpec((Hd, tc), lambda c: (0, c)),
                  pl.BlockSpec((Hd, tc), lambda c: (0, nc + c)),
                  pl.BlockSpec((2 * n, tc), lambda c: (0, 0)),
                  pl.BlockSpec((1, tc), lambda c: (0, c))],
        out_specs=pl.BlockSpec((2 * n, tc), lambda c: (0, c)),
        out_shape=jax.ShapeDtypeStruct((2 * n, OC), F32),
        sem=("parallel",), name="hyena_filter")(a2, w3, w3, t2, deltas)


def _embed(re, im):
    return np.block([[re, -im], [im, re]])


@functools.lru_cache(maxsize=None)
def _fft_tables(N1, N2):
    N = N1 * N2
    S1 = N1 // 2
    i1 = np.arange(N1)
    ang = -2.0 * np.pi * ((i1[:, None] * i1[None, :]) % N1) / N1
    fr, fi = np.cos(ang), np.sin(ang)
    f1_pair = _embed(fr[:, :S1], fi[:, :S1])
    f1_real = np.concatenate([fr, fi], axis=0)
    i2 = np.arange(N2)
    fidx = i1[:, None, None] + N1 * i2[None, :, None]
    ang = -2.0 * np.pi * ((fidx * i2[None, None, :]) % N) / N
    mr, mi = np.cos(ang), np.sin(ang)
    m_fwd = np.stack([_embed(mr[a], mi[a]) for a in range(N1)])
    m_inv = np.stack([_embed(mr[a].T, -mi[a].T) for a in range(N1)])
    ang = 2.0 * np.pi * ((i1[:S1, None] * i1[None, :]) % N1) / N1
    f1_inv = _embed(np.cos(ang) / N, np.sin(ang) / N)
    cvt = lambda a: jnp.asarray(a, dtype=BF16)
    return cvt(f1_pair), cvt(f1_real), cvt(m_fwd), cvt(m_inv), cvt(f1_inv)


def _lmat_body(f_ref, x_ref, o_ref):
    o_ref[0] = jnp.dot(f_ref[...], x_ref[0].astype(BF16), preferred_element_type=F32).astype(o_ref.dtype)


def _left_matmul(fm, x, out_dtype, name):
    P, K, W = x.shape
    R = fm.shape[0]
    tw = _tile(W, (4096, 2048, 1024, 512, 256, 128))
    return _call(
        _lmat_body, grid=(P, W // tw),
        in_specs=[pl.BlockSpec((R, K), lambda p, j: (0, 0)),
                  pl.BlockSpec((1, K, tw), lambda p, j: (p, 0, j))],
        out_specs=pl.BlockSpec((1, R, tw), lambda p, j: (p, 0, j)),
        out_shape=jax.ShapeDtypeStruct((P, R, W), out_dtype),
        sem=("parallel", "parallel"), name=name)(fm, x)


def _cmul(xr, xi, kr, ki):
    return xr * kr - xi * ki, xr * ki + xi * kr


def _fftmid_body(a_ref, m_ref, mi_ref, k_ref, o_ref, *, FB, N2):
    for t in range(FB):
        a = a_ref[0, :, t].reshape(2 * N2, a_ref.shape[-1])
        x = jnp.dot(m_ref[t], a, preferred_element_type=F32)
        yr, yi = _cmul(x[:N2], x[N2:], k_ref[0, t].astype(F32), k_ref[1, t].astype(F32))
        y = jnp.concatenate([yr, yi], axis=0).astype(BF16)
        g = jnp.dot(mi_ref[t], y, preferred_element_type=F32)
        o_ref[0, :, t] = g.reshape(2, N2, g.shape[-1]).astype(o_ref.dtype)


def _fftfwd_body(a_ref, m_ref, o_ref, *, FB, N2):
    for t in range(FB):
        a = a_ref[:, t].reshape(2 * N2, a_ref.shape[-1])
        x = jnp.dot(m_ref[t], a, preferred_element_type=F32)
        o_ref[:, t] = x.reshape(2, N2, x.shape[-1]).astype(o_ref.dtype)


SUB_BLOCK = 16


def _dft1_body(f_ref, x_ref, o_ref):
    xt = pltpu.einshape("ksc->skc", x_ref[0])
    r = jnp.stack([jnp.dot(f_ref[...], xt[k].astype(BF16), preferred_element_type=F32)
                   for k in range(xt.shape[0])], axis=0)
    o_ref[0] = pltpu.einshape("skc->ksc", r).astype(o_ref.dtype)


def _outer_dft(fm, x4, out_dtype, name):
    P, K, N2, C = x4.shape
    R = fm.shape[0]
    tc = _tile(C, (512, 256, 128))
    return _call(
        _dft1_body, grid=(P, N2 // SUB_BLOCK, C // tc),
        in_specs=[pl.BlockSpec((R, K), lambda p, s, c: (0, 0)),
                  pl.BlockSpec((1, K, SUB_BLOCK, tc), lambda p, s, c: (p, 0, s, c))],
        out_specs=pl.BlockSpec((1, R, SUB_BLOCK, tc), lambda p, s, c: (p, 0, s, c)),
        out_shape=jax.ShapeDtypeStruct((P, R, N2, C), out_dtype),
        sem=("parallel", "parallel", "parallel"), name=name)(fm, x4)


def _idft1_gate_body(f_ref, *refs, chain):
    if chain:
        fn_ref, g_ref, z_ref, x1_ref, sk_ref, o_ref, a_ref = refs
    else:
        g_ref, z_ref, x1_ref, sk_ref, o_ref = refs
    gt = pltpu.einshape("ksc->skc", g_ref[0])
    y = jnp.stack([jnp.dot(f_ref[...], gt[k], preferred_element_type=F32) for k in range(gt.shape[0])], axis=0)
    y = pltpu.einshape("skc->ksc", y)
    z1 = x1_ref[0] * (y + sk_ref[...] * z_ref[0])
    o_ref[0] = z1.astype(o_ref.dtype)
    if chain:
        zt = pltpu.einshape("ksc->skc", z1)
        r = jnp.stack([jnp.dot(fn_ref[...], zt[k].astype(BF16), preferred_element_type=F32)
                       for k in range(zt.shape[0])], axis=0)
        a_ref[0] = pltpu.einshape("skc->ksc", r).astype(a_ref.dtype)


def _filter_spectrum_2stage(k, N1, N2):
    N, OC = k.shape
    _, f1_real, m_fwd, _, _ = _fft_tables(N1, N2)
    a = _outer_dft(f1_real, k.reshape(1, N1, N2, OC), BF16, "filter_dft1").reshape(2, N1, N2, OC)
    FB = _tile(N1, (8, 4, 2, 1))
    tc = _tile(OC, (512, 256, 128))
    return _call(
        functools.partial(_fftfwd_body, FB=FB, N2=N2), grid=(N1 // FB, OC // tc),
        in_specs=[pl.BlockSpec((2, FB, N2, tc), lambda f, c: (0, f, 0, c)),
                  pl.BlockSpec((FB, 2 * N2, 2 * N2), lambda f, c: (f, 0, 0))],
        out_specs=pl.BlockSpec((2, FB, N2, tc), lambda f, c: (0, f, 0, c)),
        out_shape=jax.ShapeDtypeStruct((2, N1, N2, OC), BF16),
        sem=("parallel", "parallel"), name="filter_dft2")(a, m_fwd)


def _conv_mid(a, kf, order, N1, N2):
    P, _, N2_, C = a.shape
    _, _, m_fwd, m_inv, _ = _fft_tables(N1, N2)
    FB = _tile(N1, (8, 4, 2, 1))
    tc = _tile(C, (512, 256, 128))
    oc = order * (C // tc)
    g = _call(
        functools.partial(_fftmid_body, FB=FB, N2=N2), grid=(N1 // FB, C // tc, P),
        in_specs=[pl.BlockSpec((1, 2, FB, N2, tc), lambda f, c, p: (p, 0, f, 0, c)),
                  pl.BlockSpec((FB, 2 * N2, 2 * N2), lambda f, c, p: (f, 0, 0)),
                  pl.BlockSpec((FB, 2 * N2, 2 * N2), lambda f, c, p: (f, 0, 0)),
                  pl.BlockSpec((2, FB, N2, tc), lambda f, c, p: (0, f, 0, oc + c))],
        out_specs=pl.BlockSpec((1, 2, FB, N2, tc), lambda f, c, p: (p, 0, f, 0, c)),
        out_shape=jax.ShapeDtypeStruct((P, 2, N1, N2, C), BF16),
        sem=("parallel", "parallel", "arbitrary"), name="conv_dft2_mul_idft2")(
            a.reshape(P, 2, N1, N2, C), m_fwd, m_inv, kf)
    return g.reshape(P, 2 * N1, N2, C)


def _conv_finish(g, z4, x14, skip, out_dtype, N1, N2, chain):
    P, K, _, C = z4.shape
    f1_pair, _, _, _, f1_inv = _fft_tables(N1, N2)
    tc = _tile(C, (512, 256, 128))
    blk = lambda rows: pl.BlockSpec((1, rows, SUB_BLOCK, tc), lambda p, s, c: (p, 0, s, c))
    tab = lambda t: pl.BlockSpec(t.shape, lambda p, s, c: (0, 0))
    tables = [f1_inv, f1_pair] if chain else [f1_inv]
    out_specs = [blk(K), blk(2 * N1)] if chain else blk(K)
    z_sds = jax.ShapeDtypeStruct((P, K, N2, C), out_dtype)
    out_shape = [z_sds, jax.ShapeDtypeStruct((P, 2 * N1, N2, C), BF16)] if chain else z_sds
    return _call(
        functools.partial(_idft1_gate_body, chain=chain), grid=(P, N2 // SUB_BLOCK, C // tc),
        in_specs=[tab(t) for t in tables] + [blk(2 * N1), blk(K), blk(K), pl.BlockSpec((1, tc), lambda p, s, c: (0, c))],
        out_specs=out_specs, out_shape=out_shape,
        sem=("parallel", "parallel", "parallel"), name="conv_idft1_gate_dft1" if chain else "conv_idft1_gate")(
            *tables, g, z4, x14, skip[None, :])


def _hyena_long_convs_2stage(v, x1, x2, skip, kf, N1, N2):
    B, n, C = v.shape
    assert B % 2 == 0
    P, S1 = B // 2, N1 // 2
    f1_pair = _fft_tables(N1, N2)[0]
    view = lambda t: t.reshape(P, 2 * S1, N2, C)
    a0 = _outer_dft(f1_pair, view(v), BF16, "conv_dft1")
    z1, a1 = _conv_finish(_conv_mid(a0, kf, 0, N1, N2), view(v), view(x1), skip[0], F32, N1, N2, True)
    y = _conv_finish(_conv_mid(a1, kf, 1, N1, N2), z1, view(x2), skip[1], BF16, N1, N2, False)
    return y.reshape(B, n, C)


@functools.lru_cache(maxsize=None)
def _dft_tables(n):
    N = 2 * n
    f = np.arange(N)
    ang = -2.0 * np.pi * ((f[:, None] * f[None, :]) % N) / N
    fr, fi = np.cos(ang), np.sin(ang)
    fwd_full = np.concatenate([fr, fi], axis=0)
    fwd_half = fwd_full[:, :n]
    inv = np.concatenate([fr[:n, :], fi[:n, :]], axis=1) / N
    cvt = lambda a: jnp.asarray(a, dtype=BF16)
    return cvt(fwd_full), cvt(fwd_half), cvt(inv)


def _dftconv_body(z_ref, x1_ref, sk_ref, f_ref, fi_ref, k_ref, o_ref, *, N):
    z = z_ref[0]
    x = jnp.dot(f_ref[...], z.astype(BF16), preferred_element_type=F32)
    yr, yi = _cmul(x[:N], x[N:], k_ref[0], k_ref[1])
    y = jnp.concatenate([yr, yi], axis=0).astype(BF16)
    y = jnp.dot(fi_ref[...], y, preferred_element_type=F32)
    o_ref[0] = (x1_ref[0] * (y + sk_ref[...] * z)).astype(o_ref.dtype)


def _long_conv_gate_dense(z, x1, skip, kf, order, out_dtype):
    B, n, C = z.shape
    N = 2 * n
    _, fwd_half, inv = _dft_tables(n)
    tc = _tile(C, (256, 128))
    oc = order * (C // tc)
    return _call(
        functools.partial(_dftconv_body, N=N), grid=(C // tc, B),
        in_specs=[pl.BlockSpec((1, n, tc), lambda c, b: (b, 0, c)),
                  pl.BlockSpec((1, n, tc), lambda c, b: (b, 0, c)),
                  pl.BlockSpec((1, tc), lambda c, b: (0, c)),
                  pl.BlockSpec((2 * N, n), lambda c, b: (0, 0)),
                  pl.BlockSpec((n, 2 * N), lambda c, b: (0, 0)),
                  pl.BlockSpec((2, N, tc), lambda c, b: (0, 0, oc + c))],
        out_specs=pl.BlockSpec((1, n, tc), lambda c, b: (b, 0, c)),
        out_shape=jax.ShapeDtypeStruct((B, n, C), out_dtype),
        sem=("parallel", "arbitrary"), name="conv_dense_dft")(z, x1, skip[None, :], fwd_half, inv, kf)


def _hyena_mixer(v, x1, x2, w1, b1, freq, w2, b2, w3, skip):
    B, n, _ = v.shape
    k = _hyena_filters(n, w1, b1, freq, w2, b2, w3)
    N = 2 * n
    if N % FFT_N2 == 0 and (N // FFT_N2) >= 16:
        N1 = N // FFT_N2
        kf = _filter_spectrum_2stage(k, N1, FFT_N2)
        return _hyena_long_convs_2stage(v, x1, x2, skip, kf, N1, FFT_N2)
    fwd_full, _, _ = _dft_tables(n)
    kf = _left_matmul(fwd_full, k[None], F32, "filter_dense_dft").reshape(2, N, k.shape[1])
    z = _long_conv_gate_dense(v, x1, skip[0], kf, 0, F32)
    return _long_conv_gate_dense(z, x2, skip[1], kf, 1, BF16)


def _nt(a, b):
    return lax.dot_general(a, b, (((1,), (1,)), ((), ())), preferred_element_type=F32)


def _na_body(q_ref, k_ref, v_ref, kc_ref, vc_ref, b_ref, o_ref, *, R, KR, rows, HPS, scale):
    j = pl.program_id(2)
    start = pl.multiple_of(jnp.clip(j * R - NA_WIN_R // 2, 0, rows - KR) * GRID_W, GRID_W)
    for h in range(HPS):
        hs = slice(h * HEAD_DIM, (h + 1) * HEAD_DIM)
        q = q_ref[0, :, hs]
        kw = k_ref[0, pl.ds(start, KR * GRID_W), hs]
        vw = v_ref[0, pl.ds(start, KR * GRID_W), hs]
        s = _nt(q, kw) + b_ref[h, 0]
        sc = _nt(q, kc_ref[0, :, hs])
        m = jnp.maximum(jnp.max(s, axis=-1, keepdims=True), jnp.max(sc, axis=-1, keepdims=True))
        p = jnp.exp2((s - m) * (scale * LOG2_E))
        pc = jnp.exp2((sc - m) * (scale * LOG2_E))
        l = jnp.sum(p, axis=-1, keepdims=True) + jnp.sum(pc, axis=-1, keepdims=True)
        o = jnp.dot(p.astype(BF16), vw, preferred_element_type=F32)
        o = o + jnp.dot(pc.astype(BF16), vc_ref[0, :, hs], preferred_element_type=F32)
        o_ref[0, :, hs] = (o / l).astype(o_ref.dtype)


def _na_geometry(S):
    rows = S // GRID_W
    kr = min(NA_WIN_R, rows)
    R = min(8, rows)
    KR = min(rows, R + kr)
    nb = rows // R
    assert rows % R == 0
    types = sorted({0, min(1, nb - 1), nb - 1})
    if nb > 3:
        offs = {int(np.clip(j * R - NA_WIN_R // 2, 0, rows - KR)) - j * R for j in range(1, nb - 1)}
        assert len(offs) == 1
    return rows, kr, R, KR, nb, types


def _nabias_body(rpb_ref, o_ref, tw_ref, *, plan, R, KR, inv_scale):
    W = GRID_W
    nd_r, nd_c = 2 * NA_WIN_R - 1, 2 * NA_WIN_C - 1
    base = pl.program_id(0) * (nd_r * nd_c)
    qc = lax.broadcasted_iota(jnp.int32, (W, 2 * W), 0)
    lane = lax.broadcasted_iota(jnp.int32, (W, 2 * W), 1)
    kc = lane % W
    cs = jnp.clip(qc - NA_WIN_C // 2, 0, W - NA_WIN_C)
    col_ok = jnp.logical_and(kc >= cs, kc < cs + NA_WIN_C)
    dcm = kc - qc + (NA_WIN_C - 1)
    neg = jnp.full((W, 2 * W), NEG_INF, F32)
    for dr in range(nd_r):
        acc = neg
        for dc in range(nd_c):
            acc = jnp.where(dcm == dc, rpb_ref[base + dr * nd_c + dc] * inv_scale, acc)
        tw_ref[dr] = jnp.where(col_ok, acc, NEG_INF)
    left = lane < W
    for t, per_q in enumerate(plan):
        for qr in range(R):
            for kp in range(KR // 2):
                d0, d1 = per_q[qr][kp]
                a = neg if d0 is None else tw_ref[d0]
                b = neg if d1 is None else tw_ref[d1]
                blk = neg if (d0 is None and d1 is None) else jnp.where(left, a, b)
                o_ref[0, t, qr * W:(qr + 1) * W, kp * 2 * W:(kp + 1) * 2 * W] = blk


def _na_bias_tables(rpb, S):
    rows, kr, R, KR, nb, types = _na_geometry(S)
    assert 2 * GRID_W == LANES and KR % 2 == 0
    plan = []
    for jt in types:
        start = int(np.clip(jt * R - NA_WIN_R // 2, 0, rows - KR))
        per_q = []
        for q in range(R):
            qra = jt * R + q
            ws = int(np.clip(qra - kr // 2, 0, rows - kr))
            d = [(start + k) - qra + (NA_WIN_R - 1) if ws <= start + k < ws + kr else None for k in range(KR)]
            per_q.append([(d[2 * p], d[2 * p + 1]) for p in range(KR // 2)])
        plan.append(per_q)
    L, H, nd_r, nd_c = rpb.shape
    T, QB, KB = len(types), R * GRID_W, KR * GRID_W
    return _call(
        functools.partial(_nabias_body, plan=plan, R=R, KR=KR, inv_scale=HEAD_DIM ** 0.5), grid=(L * H,),
        in_specs=[pl.BlockSpec(memory_space=pltpu.SMEM)],
        out_specs=pl.BlockSpec((1, T, QB, KB), lambda i: (i, 0, 0, 0)),
        out_shape=jax.ShapeDtypeStruct((L * H, T, QB, KB), F32),
        scratch=[pltpu.VMEM((nd_r, GRID_W, 2 * GRID_W), F32)],
        sem=("parallel",), name="na_bias_table")(rpb.reshape(-1).astype(F32))


def _na_attention(qkv, qkv_c, bias, l, offs, H):
    B, S, _ = qkv.shape
    CTX = qkv_c.shape[1]
    rows, kr, R, KR, nb, types = _na_geometry(S)
    T = len(types)
    QB, KB = R * GRID_W, KR * GRID_W
    HPS = next(n for n in (4, 2, 1) if H % n == 0)
    HW = HPS * HEAD_DIM
    assert all(offs[n] % HW == 0 for n in ("na_k", "na_v", "na_q"))
    ok_, ov_, oq_ = (offs[n] // HW for n in ("na_k", "na_v", "na_q"))

    def btype(j):
        if T == nb:
            return j
        return jnp.where(j == 0, 0, jnp.where(j == nb - 1, T - 1, 1))

    return _call(
        functools.partial(_na_body, R=R, KR=KR, rows=rows, HPS=HPS, scale=HEAD_DIM ** -0.5),
        grid=(B, H // HPS, nb),
        in_specs=[pl.BlockSpec((1, QB, HW), lambda b, h, j: (b, j, oq_ + h)),
                  pl.BlockSpec((1, S, HW), lambda b, h, j: (b, 0, ok_ + h)),
                  pl.BlockSpec((1, S, HW), lambda b, h, j: (b, 0, ov_ + h)),
                  pl.BlockSpec((1, CTX, HW), lambda b, h, j: (b, 0, ok_ + h)),
                  pl.BlockSpec((1, CTX, HW), lambda b, h, j: (b, 0, ov_ + h)),
                  pl.BlockSpec((HPS, 1, QB, KB), lambda b, h, j: (l * (H // HPS) + h, btype(j), 0, 0))],
        out_specs=pl.BlockSpec((1, QB, HW), lambda b, h, j: (b, j, h)),
        out_shape=jax.ShapeDtypeStruct((B, S, H * HEAD_DIM), BF16),
        sem=("parallel", "parallel", "arbitrary"), name="na_attention")(qkv, qkv, qkv, qkv_c, qkv_c, bias)


def _stack_heads(q2, G):
    return jnp.concatenate([q2[:, g * HEAD_DIM:(g + 1) * HEAD_DIM] for g in range(G)], axis=0)


def _unstack_heads(o, G, n):
    return jnp.concatenate([o[g * n:(g + 1) * n] for g in range(G)], axis=1)


def _sink_column(sink_ref, h0, G, n):
    return jnp.concatenate([jnp.full((n, 1), sink_ref[h0 + g], F32) for g in range(G)], axis=0)


def _swa_body(sink_ref, q_ref, k_ref, v_ref, kc_ref, vc_ref, o_ref, *, QB, KB, S, G, HPS, QPS, scale):
    hb = pl.program_id(1)
    for qb in range(QPS):
        j = pl.program_id(2) * QPS + qb
        rows = slice(qb * QB, (qb + 1) * QB)
        start = pl.multiple_of(jnp.clip(j * QB - GQA_WINDOW, 0, S - KB), LANES)
        qpos = j * QB + lax.broadcasted_iota(jnp.int32, (QB, KB), 0)
        kpos = start + lax.broadcasted_iota(jnp.int32, (QB, KB), 1)
        mask = jnp.where(jnp.abs(qpos - kpos) <= GQA_WINDOW, 0.0, NEG_INF)
        mask = jnp.concatenate([mask] * G, axis=0)
        for h in range(HPS):
            hs = slice(h * HEAD_DIM, (h + 1) * HEAD_DIM)
            qs = slice(h * G * HEAD_DIM, (h + 1) * G * HEAD_DIM)
            q = _stack_heads(q_ref[0, rows, qs], G)
            kw = k_ref[0, pl.ds(start, KB), hs]
            vw = v_ref[0, pl.ds(start, KB), hs]
            s = _nt(q, kw) + mask
            sc = _nt(q, kc_ref[0, :, hs])
            sk = _sink_column(sink_ref, (hb * HPS + h) * G, G, QB) * (1.0 / scale)
            m = jnp.maximum(jnp.maximum(jnp.max(s, axis=-1, keepdims=True), jnp.max(sc, axis=-1, keepdims=True)), sk)
            p = jnp.exp2((s - m) * (scale * LOG2_E))
            pc = jnp.exp2((sc - m) * (scale * LOG2_E))
            l = (jnp.sum(p, axis=-1, keepdims=True) + jnp.sum(pc, axis=-1, keepdims=True)
                 + jnp.exp2((sk - m) * (scale * LOG2_E)))
            o = jnp.dot(p.astype(BF16), vw, preferred_element_type=F32)
            o = o + jnp.dot(pc.astype(BF16), vc_ref[0, :, hs], preferred_element_type=F32)
            o_ref[0, rows, qs] = _unstack_heads(o / l, G, QB).astype(o_ref.dtype)


def _swa_attention(qkv, qkv_c, sink, offs, KVH):
    B, S, _ = qkv.shape
    CTX = qkv_c.shape[1]
    G = GQA_GROUP
    QB = _tile(S, (512, 256, 128))
    KB = min(S, QB + 2 * GQA_WINDOW)
    HPS = 2 if KVH % 2 == 0 else 1
    QPS = 2 if (S // QB) % 2 == 0 else 1
    HW = HPS * HEAD_DIM
    assert offs["sw_k"] % HW == 0 and offs["sw_v"] % HW == 0 and offs["sw_q"] % (G * HW) == 0
    ok_, ov_, oq_ = offs["sw_k"] // HW, offs["sw_v"] // HW, offs["sw_q"] // (G * HW)
    return _call(
        functools.partial(_swa_body, QB=QB, KB=KB, S=S, G=G, HPS=HPS, QPS=QPS, scale=HEAD_DIM ** -0.5),
        grid=(B, KVH // HPS, S // (QB * QPS)),
        in_specs=[pl.BlockSpec(memory_space=pltpu.SMEM),
                  pl.BlockSpec((1, QPS * QB, G * HW), lambda b, h, j: (b, j, oq_ + h)),
                  pl.BlockSpec((1, S, HW), lambda b, h, j: (b, 0, ok_ + h)),
                  pl.BlockSpec((1, S, HW), lambda b, h, j: (b, 0, ov_ + h)),
                  pl.BlockSpec((1, CTX, HW), lambda b, h, j: (b, 0, ok_ + h)),
                  pl.BlockSpec((1, CTX, HW), lambda b, h, j: (b, 0, ov_ + h))],
        out_specs=pl.BlockSpec((1, QPS * QB, G * HW), lambda b, h, j: (b, j, h)),
        out_shape=jax.ShapeDtypeStruct((B, S, KVH * G * HEAD_DIM), BF16),
        sem=("parallel", "parallel", "arbitrary"), name="swa_attention")(sink, qkv, qkv, qkv, qkv_c, qkv_c)


def _cattn_body(sink_ref, q_ref, k_ref, v_ref, o_ref, *, G, use_sink, scale):
    h = pl.program_id(1)
    n = q_ref.shape[1]
    q = _stack_heads(q_ref[0], G)
    s = _nt(q, k_ref[0]) * scale
    m = jnp.max(s, axis=-1, keepdims=True)
    if use_sink:
        sk = _sink_column(sink_ref, h * G, G, n)
        m = jnp.maximum(m, sk)
    p = jnp.exp(s - m)
    l = jnp.sum(p, axis=-1, keepdims=True)
    if use_sink:
        l = l + jnp.exp(sk - m)
    o = jnp.dot(p.astype(BF16), v_ref[0], preferred_element_type=F32)
    o_ref[0] = _unstack_heads(o / l, G, n).astype(o_ref.dtype)


def _ctx_attention(qkv_c, sink, oq, ok, ov, KVH, G, use_sink):
    B, n, _ = qkv_c.shape
    oq_, ok_, ov_ = oq // (G * HEAD_DIM), ok // HEAD_DIM, ov // HEAD_DIM
    return _call(
        functools.partial(_cattn_body, G=G, use_sink=use_sink, scale=HEAD_DIM ** -0.5),
        grid=(B, KVH),
        in_specs=[pl.BlockSpec(memory_space=pltpu.SMEM),
                  pl.BlockSpec((1, n, G * HEAD_DIM), lambda b, h: (b, 0, oq_ + h)),
                  pl.BlockSpec((1, n, HEAD_DIM), lambda b, h: (b, 0, ok_ + h)),
                  pl.BlockSpec((1, n, HEAD_DIM), lambda b, h: (b, 0, ov_ + h))],
        out_specs=pl.BlockSpec((1, n, G * HEAD_DIM), lambda b, h: (b, 0, h)),
        out_shape=jax.ShapeDtypeStruct((B, n, KVH * G * HEAD_DIM), BF16),
        sem=("parallel", "parallel"), name="ctx_attention")(sink, qkv_c, qkv_c, qkv_c)


def _merge_body(x_ref, g_ref, sh_ref, sc_ref, yh, yn, ys, wgh, wgn, wgs, wh, wn, ws, o_ref, h_ref, *, eps):
    @pl.when(pl.program_id(2) == 0)
    def _():
        h_ref[...] = _norm_modulate(x_ref[0], g_ref, sh_ref, sc_ref, eps).astype(BF16)

    h = h_ref[...]
    gate = lambda wg: _sigmoid(jnp.dot(h, wg[...], preferred_element_type=F32))
    m = gate(wgh) * jnp.dot(yh[0], wh[...], preferred_element_type=F32)
    m = m + gate(wgn) * jnp.dot(yn[0], wn[...], preferred_element_type=F32)
    m = m + gate(wgs) * jnp.dot(ys[0], ws[...], preferred_element_type=F32)
    o_ref[0] = m.astype(o_ref.dtype)


def _merge_branches(x, gain, shift, scale, w_gate, y_hy, y_na, y_sw, w_br, l):
    B, S, D = x.shape
    widths = (y_hy.shape[2], y_na.shape[2], y_sw.shape[2])
    starts = (0, widths[0], widths[0] + widths[1])
    assert all(s % w == 0 for s, w in zip(starts, widths))
    tm = _tile(S, (512, 256, 128))
    tn = _tile(D, (512, 256, 128))
    nj = D // tn
    yspec = lambda y: pl.BlockSpec((1, tm, y.shape[2]), lambda b, i, j: (b, i, 0))
    gspec = lambda k: pl.BlockSpec((None, D, tn), lambda b, i, j: (l, 0, k * nj + j))
    wspec = lambda k: pl.BlockSpec((None, widths[k], tn), lambda b, i, j: (l, starts[k] // widths[k], j))
    return _call(
        functools.partial(_merge_body, eps=NORM_EPS), grid=(B, S // tm, nj),
        in_specs=[pl.BlockSpec((1, tm, D), lambda b, i, j: (b, i, 0)),
                  pl.BlockSpec((1, D), lambda b, i, j: (0, 0)),
                  pl.BlockSpec((1, 1, D), lambda b, i, j: (b, 0, 0)),
                  pl.BlockSpec((1, 1, D), lambda b, i, j: (b, 0, 0)),
                  yspec(y_hy), yspec(y_na), yspec(y_sw), gspec(0), gspec(1), gspec(2),
                  wspec(0), wspec(1), wspec(2)],
        out_specs=pl.BlockSpec((1, tm, tn), lambda b, i, j: (b, i, j)),
        out_shape=jax.ShapeDtypeStruct((B, S, D), BF16),
        scratch=[pltpu.VMEM((tm, D), BF16)],
        sem=("parallel", "parallel", "arbitrary"), name="gated_merge")(
            x, gain.reshape(1, D), shift, scale, y_hy, y_na, y_sw, w_gate, w_gate, w_gate, w_br, w_br, w_br)


def _mmres_body(a_ref, w_ref, x_ref, g_ref, mg_ref, o_ref, *, nk, eps):
    k = pl.program_id(2)
    part = lambda: jnp.dot(a_ref[0], w_ref[...], preferred_element_type=F32)

    def finish():
        y = o_ref[0]
        yn = y * lax.rsqrt(jnp.mean(y * y, axis=-1, keepdims=True) + eps) * g_ref[...]
        o_ref[0] = x_ref[0] + mg_ref[0] * yn

    if nk == 1:
        o_ref[0] = part()
        finish()
        return

    @pl.when(k == 0)
    def _():
        o_ref[0] = part()

    @pl.when(k > 0)
    def _():
        o_ref[0] += part()

    @pl.when(k == nk - 1)
    def _():
        finish()


def _matmul_norm_residual(a, w, l, x, gain, mgate, name):
    B, S, K = a.shape
    D = w.shape[2]
    if K <= 2048:
        tm, tk = _tile(S, (512, 256, 128)), K
    else:
        tm = _tile(S, (1024, 512, 256, 128))
        tk = max(t for t in range(LANES, 1537, LANES) if K % t == 0)
    nk = K // tk
    est = 2 * (tm * tk * 2 + tk * D * 2 + 2 * tm * D * 4)
    return _call(
        functools.partial(_mmres_body, nk=nk, eps=NORM_EPS), grid=(B, S // tm, nk),
        vmem=VMEM_LIMIT_LARGE if est > VMEM_LIMIT - VMEM_TEMP_RESERVE else VMEM_LIMIT,
        in_specs=[pl.BlockSpec((1, tm, tk), lambda b, i, k: (b, i, k)),
                  pl.BlockSpec((None, tk, D), lambda b, i, k: (l, k, 0)),
                  pl.BlockSpec((1, tm, D), lambda b, i, k: (b, i, 0)),
                  pl.BlockSpec((1, D), lambda b, i, k: (0, 0)),
                  pl.BlockSpec((1, 1, D), lambda b, i, k: (b, 0, 0))],
        out_specs=pl.BlockSpec((1, tm, D), lambda b, i, k: (b, i, 0)),
        out_shape=jax.ShapeDtypeStruct((B, S, D), F32),
        sem=("parallel", "parallel", "arbitrary"), name=name)(a, w, x, gain.reshape(1, D), mgate)


def _rope_tables(n):
    t = jnp.arange(n)
    row = (t // GRID_W).astype(F32)
    col = (t % GRID_W).astype(F32)
    per_axis = HEAD_DIM // 2
    inv = ROPE_BASE ** (-jnp.arange(0, per_axis, 2, dtype=F32) / per_axis)
    ar, ac = row[:, None] * inv, col[:, None] * inv
    cos_t = jnp.concatenate([jnp.cos(ar), jnp.cos(ar), jnp.cos(ac), jnp.cos(ac)], axis=1)
    sin_t = jnp.concatenate([-jnp.sin(ar), jnp.sin(ar), -jnp.sin(ac), jnp.sin(ac)], axis=1)
    return cos_t, sin_t


def kernel(x, c, ctx, c_ctx, w_mod, b_mod, norm_gains, w_in, hy_conv_w, hy_conv_b, hy_w1, hy_b1, hy_freq, hy_w2, hy_b2, hy_w3, hy_skip, na_rpb, swa_sink, w_branch, w_out, ffn_w_up, ffn_conv_w, ffn_conv_b, ffn_w_down):
    B, S, D = x.shape
    L = w_mod.shape[0]
    C = hy_skip.shape[-1]
    H_na = na_rpb.shape[1]
    H_q = swa_sink.shape[1]
    KVH = H_q // GQA_GROUP
    NA_W, QW, KVW = H_na * HEAD_DIM, H_q * HEAD_DIM, KVH * HEAD_DIM
    KV_COLS = 2 * NA_W + 2 * KVW
    qkv_segs = [(0, KV_COLS), (KV_COLS + 3 * C, NA_W + QW)]
    gate_start = KV_COLS + 3 * C + NA_W + QW
    offs = {"na_k": 0, "na_v": NA_W, "sw_k": 2 * NA_W, "sw_v": 2 * NA_W + KVW,
            "na_q": KV_COLS, "sw_q": KV_COLS + NA_W}
    rope_chunks = (list(range(offs["sw_k"] // LANES, (offs["sw_k"] + KVW) // LANES))
                   + list(range(offs["sw_q"] // LANES, (offs["sw_q"] + QW) // LANES)))
    cos_t, sin_t = _rope_tables(S)

    w_out_b, w_br_b, w_dn_b = w_out.astype(BF16), w_branch.astype(BF16), ffn_w_down.astype(BF16)
    w_gate_b = w_in[:, :, gate_start:].astype(BF16)
    na_bias = _na_bias_tables(na_rpb, S)
    CTX = ctx.shape[1]
    flat = lambda t: t.reshape(1, -1, t.shape[-1])
    unflat = lambda t: t.reshape(B, CTX, t.shape[-1])

    R = -(-(B + 1) // 8) * 8
    cc = jnp.concatenate([c, c_ctx[None, :], jnp.zeros((R - B - 1, D), F32)], axis=0)
    mods = _modulation(cc, w_mod, b_mod)

    xc = ctx
    for l in range(L):
        mod = [mods[l, :B, k * D:(k + 1) * D].reshape(B, 1, D) for k in range(6)]
        mod_c = [mods[l, B, k * D:(k + 1) * D].reshape(1, 1, D) for k in range(6)]
        g = norm_gains[l]
        hy_p = (hy_w1[l], hy_b1[l], hy_freq[l], hy_w2[l], hy_b2[l], hy_w3[l], hy_skip[l])

        qkv_c, vc, x1c, x2c = map(unflat, _in_proj(flat(xc), g[0], mod_c[0], mod_c[1], w_in, l, qkv_segs, KV_COLS, C,
                                                   hy_conv_w[l], hy_conv_b[l], None, "ctx_in_proj", period=CTX))
        qkv, v, x1, x2 = _in_proj(x, g[0], mod[0], mod[1], w_in, l, qkv_segs, KV_COLS, C,
                                  hy_conv_w[l], hy_conv_b[l], (cos_t, sin_t, rope_chunks), "in_proj")
        y_hy = _hyena_mixer(v, x1, x2, *hy_p)
        y_na = _na_attention(qkv, qkv_c, na_bias, l, offs, H_na)
        y_sw = _swa_attention(qkv, qkv_c, swa_sink[l], offs, KVH)
        m = _merge_branches(x, g[0], mod[0], mod[1], w_gate_b, y_hy, y_na, y_sw, w_br_b, l)
        x = _matmul_norm_residual(m, w_out_b, l, x, g[1], mod[2], "out_proj_residual")
        gl = _ffn_up_glu(x, g[2], mod[3], mod[4], ffn_w_up, l, ffn_conv_w[l], ffn_conv_b[l])
        x = _matmul_norm_residual(gl, w_dn_b, l, x, g[3], mod[5], "ffn_down_residual")

        if l < L - 1:
            yc_hy = _hyena_mixer(vc, x1c, x2c, *hy_p)
            yc_na = _ctx_attention(qkv_c, swa_sink[l], offs["na_q"], offs["na_k"], offs["na_v"], H_na, 1, False)
            yc_sw = _ctx_attention(qkv_c, swa_sink[l], offs["sw_q"], offs["sw_k"], offs["sw_v"], KVH, GQA_GROUP, True)
            mc = _merge_branches(flat(xc), g[0], mod_c[0], mod_c[1], w_gate_b,
                                 flat(yc_hy), flat(yc_na), flat(yc_sw), w_br_b, l)
            xcf = _matmul_norm_residual(mc, w_out_b, l, flat(xc), g[1], mod_c[2], "ctx_out_proj_residual")
            gl_c = _ffn_up_glu(xcf, g[2], mod_c[3], mod_c[4], ffn_w_up, l, ffn_conv_w[l], ffn_conv_b[l], period=CTX)
            xc = unflat(_matmul_norm_residual(gl_c, w_dn_b, l, xcf, g[3], mod_c[5], "ctx_ffn_down_residual"))
    return x
```

```python
import functools
import math

import numpy as np
import jax
import jax.numpy as jnp
from jax import lax
from jax.experimental import pallas as pl
from jax.experimental.pallas import tpu as pltpu

F32 = jnp.float32
BF16 = jnp.bfloat16
NEG_INF = -1e30

GRID_W = 64
HEAD_DIM = 128
HY_EMB = 33
HY_BANDS = (HY_EMB - 1) // 2
HY_FAST_DECAY = 0.3
HY_SLOW_DECAY = 1.5
HY_DECAY_TARGET = 1e-2
HY_MAX_DECAY = math.log(HY_DECAY_TARGET) / HY_FAST_DECAY
HY_MIN_DECAY = math.log(HY_DECAY_TARGET) / HY_SLOW_DECAY
NA_WIN_R = 8
NA_WIN_C = 16
GQA_GROUP = 2
GQA_WINDOW = 128
ROPE_BASE = 10000.0
NORM_EPS = 1e-6

LANES = 128
FFT_N2 = 128
VMEM_LIMIT = 56 * 1024 * 1024
VMEM_LIMIT_LARGE = 60 * 1024 * 1024
VMEM_TEMP_RESERVE = 8 * 1024 * 1024
LOG2_E = math.log2(math.e)


def _call(body, *, grid, in_specs, out_specs, out_shape, scratch=(), sem, name, vmem=VMEM_LIMIT):
    return pl.pallas_call(
        body, grid=grid, in_specs=in_specs, out_specs=out_specs, out_shape=out_shape,
        scratch_shapes=list(scratch),
        compiler_params=pltpu.CompilerParams(dimension_semantics=sem, vmem_limit_bytes=vmem),
        name=name)


def _tile(n, cands):
    for c in cands:
        if n % c == 0:
            return c
    raise ValueError(f"no tile for {n} in {cands}")


def _sigmoid(x):
    return 1.0 / (1.0 + jnp.exp(-x))


def _mod_body(c_ref, w_ref, b_ref, o_ref):
    c = c_ref[...]
    s = (c * _sigmoid(c)).astype(BF16)
    o_ref[0] = jnp.dot(s, w_ref[0].astype(BF16), preferred_element_type=F32) + b_ref[0]


def _modulation(cc, w_mod, b_mod):
    L, D, N = w_mod.shape
    R = cc.shape[0]
    tn = _tile(N, (1024, 512, 256, 128))
    return _call(
        _mod_body, grid=(L, N // tn),
        in_specs=[pl.BlockSpec((R, D), lambda l, j: (0, 0)),
                  pl.BlockSpec((1, D, tn), lambda l, j: (l, 0, j)),
                  pl.BlockSpec((1, 1, tn), lambda l, j: (l, 0, j))],
        out_specs=pl.BlockSpec((1, R, tn), lambda l, j: (l, 0, j)),
        out_shape=jax.ShapeDtypeStruct((L, R, N), F32),
        sem=("parallel", "parallel"), name="modulation")(cc, w_mod, b_mod.reshape(L, 1, N))


def _swap32(a):
    lane = lax.broadcasted_iota(jnp.int32, a.shape, 1)
    return jnp.where((lane & 32) == 0, pltpu.roll(a, 96, 1), pltpu.roll(a, 32, 1))


def _norm_modulate(x, g_ref, sh_ref, sc_ref, eps):
    y = x * lax.rsqrt(jnp.mean(x * x, axis=-1, keepdims=True) + eps) * g_ref[...]
    return y * (1.0 + sc_ref[0]) + sh_ref[0]


HALO = 16


def _halo_prologue(h_ref, x_ref, xp_ref, xn_ref, g_ref, sh_ref, sc_ref, eps):
    i, tm = pl.program_id(1), x_ref.shape[1]
    nm = lambda x: _norm_modulate(x, g_ref, sh_ref, sc_ref, eps)
    keep_prev = (i > 0).astype(F32)
    keep_next = (i < pl.num_programs(1) - 1).astype(F32)
    h_ref[0:HALO, :] = (nm(xp_ref[0]) * keep_prev).astype(BF16)
    h_ref[HALO:HALO + tm, :] = nm(x_ref[0]).astype(BF16)
    h_ref[HALO + tm:, :] = (nm(xn_ref[0]) * keep_next).astype(BF16)


def _conv3_halo(a, cw_ref, cb_ref, tm, period):
    rows = a.shape[0]
    up = pltpu.roll(a, 1, 0)[HALO:HALO + tm]
    un = pltpu.roll(a, rows - 1, 0)[HALO:HALO + tm]
    if period is not None:
        t = (pl.program_id(1) * tm + lax.broadcasted_iota(jnp.int32, up.shape, 0)) % period
        up = jnp.where(t == 0, 0.0, up)
        un = jnp.where(t == period - 1, 0.0, un)
    return up * cw_ref[0:1, :] + a[HALO:HALO + tm] * cw_ref[1:2, :] + un * cw_ref[2:3, :] + cb_ref[...]


def _halo_specs(S, tm, D):
    hb, nh = tm // HALO, S // HALO
    return [pl.BlockSpec((1, tm, D), lambda b, i, j: (b, i, 0)),
            pl.BlockSpec((1, HALO, D), lambda b, i, j: (b, jnp.maximum(i * hb - 1, 0), 0)),
            pl.BlockSpec((1, HALO, D), lambda b, i, j: (b, jnp.minimum((i + 1) * hb, nh - 1), 0)),
            pl.BlockSpec((1, D), lambda b, i, j: (0, 0)),
            pl.BlockSpec((1, 1, D), lambda b, i, j: (b, 0, 0)),
            pl.BlockSpec((1, 1, D), lambda b, i, j: (b, 0, 0))]


def _inproj_body(*refs, nq, nc, rope_chunks, period, eps):
    if rope_chunks:
        x_ref, xp_ref, xn_ref, g_ref, sh_ref, sc_ref, w_ref, cw_ref, cb_ref, cos_ref, sin_ref, q_o, o0, o1, o2, h_ref = refs
    else:
        x_ref, xp_ref, xn_ref, g_ref, sh_ref, sc_ref, w_ref, cw_ref, cb_ref, q_o, o0, o1, o2, h_ref = refs
    j = pl.program_id(2)
    tm = x_ref.shape[1]

    @pl.when(j == 0)
    def _():
        _halo_prologue(h_ref, x_ref, xp_ref, xn_ref, g_ref, sh_ref, sc_ref, eps)

    qkv_dot = lambda: jnp.dot(h_ref[HALO:HALO + tm, :], w_ref[...], preferred_element_type=F32)
    nch = w_ref.shape[1] // LANES
    rope_tiles = sorted({ch // nch for ch in rope_chunks})
    for jt in rope_tiles:
        @pl.when(j == jt)
        def _(jt=jt):
            acc = qkv_dot()
            c, s = cos_ref[...], sin_ref[...]
            for k in range(nch):
                a = acc[:, k * LANES:(k + 1) * LANES]
                if jt * nch + k in rope_chunks:
                    a = a * c + _swap32(a) * s
                q_o[0, :, k * LANES:(k + 1) * LANES] = a.astype(q_o.dtype)

    @pl.when(functools.reduce(jnp.logical_and, [j != jt for jt in rope_tiles], j < nq))
    def _():
        q_o[0] = qkv_dot().astype(q_o.dtype)

    for k, o in enumerate((o0, o1, o2)):
        @pl.when(jnp.logical_and(j >= nq + k * nc, j < nq + (k + 1) * nc))
        def _(o=o):
            a = jnp.dot(h_ref[...], w_ref[...], preferred_element_type=F32)
            o[0] = _conv3_halo(a, cw_ref, cb_ref, tm, period)


def _colmap(segs, tn):
    bounds, o = [], 0
    for s, w in segs:
        assert s % tn == 0 and w % tn == 0
        bounds.append(((o + w) // tn, (s - o) // tn))
        o += w

    def f(j):
        r = j + bounds[-1][1]
        for hi, off in reversed(bounds[:-1]):
            r = jnp.where(j < hi, j + off, r)
        return r
    return f, o


def _seg_tile(segs, cands):
    return _tile(functools.reduce(math.gcd, [v for seg in segs for v in seg if v]), cands)


def _in_proj(x, gain, shift, scale, w_in, l, qkv_segs, hy_start, C, conv_w, conv_b, rope, name, period=None):
    B, S, D = x.shape
    tm = _tile(S, (1024, 512, 256, 128))
    segs = list(qkv_segs) + [(hy_start, 3 * C)]
    tn = _seg_tile(segs + [(0, C)], (512, 256, 128))
    cmap, n_out = _colmap(segs, tn)
    nc = C // tn
    nq = n_out // tn - 3 * nc
    in_specs = _halo_specs(S, tm, D) + [
        pl.BlockSpec((None, D, tn), lambda b, i, j: (l, 0, cmap(j))),
        pl.BlockSpec((3, tn), lambda b, i, j: (0, jnp.clip(j - nq, 0, 3 * nc - 1))),
        pl.BlockSpec((1, tn), lambda b, i, j: (0, jnp.clip(j - nq, 0, 3 * nc - 1)))]
    args = [x, x, x, gain.reshape(1, D), shift, scale, w_in, conv_w, conv_b.reshape(1, 3 * C)]
    rope_chunks = ()
    if rope is not None:
        cos_t, sin_t, rope_chunks = rope
        in_specs += [pl.BlockSpec((tm, LANES), lambda b, i, j: (i, 0)),
                     pl.BlockSpec((tm, LANES), lambda b, i, j: (i, 0))]
        args += [cos_t, sin_t]
    hy_spec = lambda k: pl.BlockSpec((1, tm, tn), lambda b, i, j: (b, i, jnp.clip(j - nq - k * nc, 0, nc - 1)))
    return _call(
        functools.partial(_inproj_body, nq=nq, nc=nc, rope_chunks=frozenset(rope_chunks), period=period, eps=NORM_EPS),
        grid=(B, S // tm, n_out // tn), in_specs=in_specs,
        out_specs=[pl.BlockSpec((1, tm, tn), lambda b, i, j: (b, i, jnp.minimum(j, nq - 1))),
                   hy_spec(0), hy_spec(1), hy_spec(2)],
        out_shape=[jax.ShapeDtypeStruct((B, S, nq * tn), BF16)] + [jax.ShapeDtypeStruct((B, S, C), F32)] * 3,
        scratch=[pltpu.VMEM((tm + 2 * HALO, D), BF16)],
        sem=("parallel", "parallel", "arbitrary"), name=name)(*args)


def _ffnup_body(x_ref, xp_ref, xn_ref, g_ref, sh_ref, sc_ref, wa_ref, wu_ref, cw_ref, cb_ref, o_ref, h_ref, *, period, eps):
    tm = x_ref.shape[1]

    @pl.when(pl.program_id(2) == 0)
    def _():
        _halo_prologue(h_ref, x_ref, xp_ref, xn_ref, g_ref, sh_ref, sc_ref, eps)

    a = jnp.dot(h_ref[...], wa_ref[...].astype(BF16), preferred_element_type=F32)
    u = jnp.dot(h_ref[HALO:HALO + tm, :], wu_ref[...].astype(BF16), preferred_element_type=F32)
    c = _conv3_halo(a, cw_ref, cb_ref, tm, period)
    o_ref[0] = (c * _sigmoid(c) * u).astype(o_ref.dtype)


def _ffn_up_glu(x, gain, shift, scale, w_up, l, conv_w, conv_b, period=None):
    B, S, D = x.shape
    Fd = w_up.shape[2] // 2
    tm = _tile(S, (1024, 512, 256, 128))
    tn = _tile(Fd, (512, 256, 128))
    nj = Fd // tn
    return _call(
        functools.partial(_ffnup_body, period=period, eps=NORM_EPS), grid=(B, S // tm, nj),
        in_specs=_halo_specs(S, tm, D) + [
            pl.BlockSpec((None, D, tn), lambda b, i, j: (l, 0, j)),
            pl.BlockSpec((None, D, tn), lambda b, i, j: (l, 0, nj + j)),
            pl.BlockSpec((3, tn), lambda b, i, j: (0, j)),
            pl.BlockSpec((1, tn), lambda b, i, j: (0, j))],
        out_specs=pl.BlockSpec((1, tm, tn), lambda b, i, j: (b, i, j)),
        out_shape=jax.ShapeDtypeStruct((B, S, Fd), BF16),
        scratch=[pltpu.VMEM((tm + 2 * HALO, D), BF16)],
        sem=("parallel", "parallel", "arbitrary"), name="ffn_up_glu")(
            x, x, x, gain.reshape(1, D), shift, scale, w_up, w_up, conv_w, conv_b.reshape(1, Fd))


def _dot3(a, w):
    ah, wh = a.astype(BF16), w.astype(BF16)
    al, wl = (a - ah.astype(F32)).astype(BF16), (w - wh.astype(F32)).astype(BF16)
    d = lambda p, q: jnp.dot(p, q, preferred_element_type=F32)
    return d(ah, wh) + d(al, wh) + d(ah, wl)


def _filt_trunk_body(z_ref, w1_ref, b1_ref, fr_ref, w2_ref, b2_ref, o_ref):
    a = _dot3(z_ref[...], w1_ref[...]) + b1_ref[...]
    a = jnp.sin(fr_ref[0:1, :] * a)
    a = _dot3(a, w2_ref[...]) + b2_ref[...]
    o_ref[...] = jnp.sin(fr_ref[1:2, :] * a)


def _filt_main_body(a_ref, wf_ref, wb_ref, t_ref, dl_ref, o_ref, *, n):
    hf = _dot3(a_ref[0:n, :], wf_ref[...])
    hb = _dot3(a_ref[n:2 * n, :], wb_ref[...])
    r = lax.broadcasted_iota(jnp.int32, hb.shape, 0)
    k = jnp.concatenate([hf, jnp.where(r == 0, 0.0, hb)], axis=0) * jnp.exp(-t_ref[...] * dl_ref[...])
    o_ref[...] = k / jnp.sum(jnp.abs(k), axis=0, keepdims=True)


def _hyena_filters(n, w1, b1, freq, w2, b2, w3):
    Hd = w1.shape[1]
    OC = w3.shape[1] // 2
    t = jnp.linspace(0.0, 1.0, n, dtype=F32)[:, None]
    w = (2.0 * math.pi / n) * jnp.arange(n, dtype=F32)[:, None]
    f = jnp.linspace(1e-4, HY_BANDS - 1, HY_BANDS, dtype=F32)[None, :]
    z = jnp.concatenate([t, jnp.cos(f * w), -jnp.sin(f * w)], axis=-1)
    fold = lambda a: jnp.concatenate([a, jnp.zeros_like(a[:1]), a[:0:-1]], axis=0)
    EP = 64
    z2 = jnp.pad(fold(z), ((0, 0), (0, EP - HY_EMB)))
    w1p = jnp.pad(w1, ((0, EP - HY_EMB), (0, 0)))
    rt = _tile(2 * n, (1024, 512, 256))
    a2 = _call(
        _filt_trunk_body, grid=(2 * n // rt,),
        in_specs=[pl.BlockSpec((rt, EP), lambda i: (i, 0)),
                  pl.BlockSpec((EP, Hd), lambda i: (0, 0)),
                  pl.BlockSpec((1, Hd), lambda i: (0, 0)),
                  pl.BlockSpec((2, Hd), lambda i: (0, 0)),
                  pl.BlockSpec((Hd, Hd), lambda i: (0, 0)),
                  pl.BlockSpec((1, Hd), lambda i: (0, 0))],
        out_specs=pl.BlockSpec((rt, Hd), lambda i: (i, 0)),
        out_shape=jax.ShapeDtypeStruct((2 * n, Hd), F32),
        sem=("parallel",), name="hyena_filter_trunk")(z2, w1p, b1.reshape(1, Hd), freq, w2, b2.reshape(1, Hd))
    tc = LANES
    t2 = jnp.broadcast_to(fold(t), (2 * n, tc))
    deltas = jnp.abs(jnp.linspace(HY_MIN_DECAY, HY_MAX_DECAY, OC, dtype=F32))[None, :]
    nc = OC // tc
    return _call(
        functools.partial(_filt_main_body, n=n), grid=(nc,),
        in_specs=[pl.BlockSpec((2 * n, Hd), lambda c: (0, 0)),
                  pl.BlockSpec((Hd, tc), lambda c: (0, c)),
                  pl.BlockSpec((Hd, tc), lambda c: (0, nc + c)),
                  pl.BlockSpec((2 * n, tc), lambda c: (0, 0)),
                  pl.BlockSpec((1, tc), lambda c: (0, c))],
        out_specs=pl.BlockSpec((2 * n, tc), lambda c: (0, c)),
        out_shape=jax.ShapeDtypeStruct((2 * n, OC), F32),
        sem=("parallel",), name="hyena_filter")(a2, w3, w3, t2, deltas)


def _embed(re, im):
    return np.block([[re, -im], [im, re]])


@functools.lru_cache(maxsize=None)
def _fft_tables(N1, N2):
    N = N1 * N2
    S1 = N1 // 2
    i1 = np.arange(N1)
    ang = -2.0 * np.pi * ((i1[:, None] * i1[None, :]) % N1) / N1
    fr, fi = np.cos(ang), np.sin(ang)
    f1_pair = _embed(fr[:, :S1], fi[:, :S1])
    f1_real = np.concatenate([fr, fi], axis=0)
    i2 = np.arange(N2)
    fidx = i1[:, None, None] + N1 * i2[None, :, None]
    ang = -2.0 * np.pi * ((fidx * i2[None, None, :]) % N) / N
    mr, mi = np.cos(ang), np.sin(ang)
    m_fwd = np.stack([_embed(mr[a], mi[a]) for a in range(N1)])
    m_inv = np.stack([_embed(mr[a].T, -mi[a].T) for a in range(N1)])
    ang = 2.0 * np.pi * ((i1[:S1, None] * i1[None, :]) % N1) / N1
    f1_inv = _embed(np.cos(ang) / N, np.sin(ang) / N)
    cvt = lambda a: jnp.asarray(a, dtype=BF16)
    return cvt(f1_pair), cvt(f1_real), cvt(m_fwd), cvt(m_inv), cvt(f1_inv)


def _lmat_body(f_ref, x_ref, o_ref):
    o_ref[0] = jnp.dot(f_ref[...], x_ref[0].astype(BF16), preferred_element_type=F32).astype(o_ref.dtype)


def _left_matmul(fm, x, out_dtype, name):
    P, K, W = x.shape
    R = fm.shape[0]
    tw = _tile(W, (4096, 2048, 1024, 512, 256, 128))
    return _call(
        _lmat_body, grid=(P, W // tw),
        in_specs=[pl.BlockSpec((R, K), lambda p, j: (0, 0)),
                  pl.BlockSpec((1, K, tw), lambda p, j: (p, 0, j))],
        out_specs=pl.BlockSpec((1, R, tw), lambda p, j: (p, 0, j)),
        out_shape=jax.ShapeDtypeStruct((P, R, W), out_dtype),
        sem=("parallel", "parallel"), name=name)(fm, x)


def _cmul(xr, xi, kr, ki):
    return xr * kr - xi * ki, xr * ki + xi * kr


def _fftmid_body(a_ref, m_ref, mi_ref, k_ref, o_ref, *, FB, N2):
    for t in range(FB):
        a = a_ref[0, :, t].reshape(2 * N2, a_ref.shape[-1])
        x = jnp.dot(m_ref[t], a, preferred_element_type=F32)
        yr, yi = _cmul(x[:N2], x[N2:], k_ref[0, t].astype(F32), k_ref[1, t].astype(F32))
        y = jnp.concatenate([yr, yi], axis=0).astype(BF16)
        g = jnp.dot(mi_ref[t], y, preferred_element_type=F32)
        o_ref[0, :, t] = g.reshape(2, N2, g.shape[-1]).astype(o_ref.dtype)


def _fftfwd_body(a_ref, m_ref, o_ref, *, FB, N2):
    for t in range(FB):
        a = a_ref[:, t].reshape(2 * N2, a_ref.shape[-1])
        x = jnp.dot(m_ref[t], a, preferred_element_type=F32)
        o_ref[:, t] = x.reshape(2, N2, x.shape[-1]).astype(o_ref.dtype)


SUB_BLOCK = 16


def _dft1_body(f_ref, x_ref, o_ref):
    xt = pltpu.einshape("ksc->skc", x_ref[0])
    r = jnp.stack([jnp.dot(f_ref[...], xt[k].astype(BF16), preferred_element_type=F32)
                   for k in range(xt.shape[0])], axis=0)
    o_ref[0] = pltpu.einshape("skc->ksc", r).astype(o_ref.dtype)


def _outer_dft(fm, x4, out_dtype, name):
    P, K, N2, C = x4.shape
    R = fm.shape[0]
    tc = _tile(C, (512, 256, 128))
    return _call(
        _dft1_body, grid=(P, N2 // SUB_BLOCK, C // tc),
        in_specs=[pl.BlockSpec((R, K), lambda p, s, c: (0, 0)),
                  pl.BlockSpec((1, K, SUB_BLOCK, tc), lambda p, s, c: (p, 0, s, c))],
        out_specs=pl.BlockSpec((1, R, SUB_BLOCK, tc), lambda p, s, c: (p, 0, s, c)),
        out_shape=jax.ShapeDtypeStruct((P, R, N2, C), out_dtype),
        sem=("parallel", "parallel", "parallel"), name=name)(fm, x4)


def _idft1_gate_body(f_ref, *refs, chain):
    if chain:
        fn_ref, g_ref, z_ref, x1_ref, sk_ref, o_ref, a_ref = refs
    else:
        g_ref, z_ref, x1_ref, sk_ref, o_ref = refs
    gt = pltpu.einshape("ksc->skc", g_ref[0])
    y = jnp.stack([jnp.dot(f_ref[...], gt[k], preferred_element_type=F32) for k in range(gt.shape[0])], axis=0)
    y = pltpu.einshape("skc->ksc", y)
    z1 = x1_ref[0] * (y + sk_ref[...] * z_ref[0])
    o_ref[0] = z1.astype(o_ref.dtype)
    if chain:
        zt = pltpu.einshape("ksc->skc", z1)
        r = jnp.stack([jnp.dot(fn_ref[...], zt[k].astype(BF16), preferred_element_type=F32)
                       for k in range(zt.shape[0])], axis=0)
        a_ref[0] = pltpu.einshape("skc->ksc", r).astype(a_ref.dtype)


def _filter_spectrum_2stage(k, N1, N2):
    N, OC = k.shape
    _, f1_real, m_fwd, _, _ = _fft_tables(N1, N2)
    a = _outer_dft(f1_real, k.reshape(1, N1, N2, OC), BF16, "filter_dft1").reshape(2, N1, N2, OC)
    FB = _tile(N1, (8, 4, 2, 1))
    tc = _tile(OC, (512, 256, 128))
    return _call(
        functools.partial(_fftfwd_body, FB=FB, N2=N2), grid=(N1 // FB, OC // tc),
        in_specs=[pl.BlockSpec((2, FB, N2, tc), lambda f, c: (0, f, 0, c)),
                  pl.BlockSpec((FB, 2 * N2, 2 * N2), lambda f, c: (f, 0, 0))],
        out_specs=pl.BlockSpec((2, FB, N2, tc), lambda f, c: (0, f, 0, c)),
        out_shape=jax.ShapeDtypeStruct((2, N1, N2, OC), BF16),
        sem=("parallel", "parallel"), name="filter_dft2")(a, m_fwd)


def _conv_mid(a, kf, order, N1, N2):
    P, _, N2_, C = a.shape
    _, _, m_fwd, m_inv, _ = _fft_tables(N1, N2)
    FB = _tile(N1, (8, 4, 2, 1))
    tc = _tile(C, (512, 256, 128))
    oc = order * (C // tc)
    g = _call(
        functools.partial(_fftmid_body, FB=FB, N2=N2), grid=(N1 // FB, C // tc, P),
        in_specs=[pl.BlockSpec((1, 2, FB, N2, tc), lambda f, c, p: (p, 0, f, 0, c)),
                  pl.BlockSpec((FB, 2 * N2, 2 * N2), lambda f, c, p: (f, 0, 0)),
                  pl.BlockSpec((FB, 2 * N2, 2 * N2), lambda f, c, p: (f, 0, 0)),
                  pl.BlockSpec((2, FB, N2, tc), lambda f, c, p: (0, f, 0, oc + c))],
        out_specs=pl.BlockSpec((1, 2, FB, N2, tc), lambda f, c, p: (p, 0, f, 0, c)),
        out_shape=jax.ShapeDtypeStruct((P, 2, N1, N2, C), BF16),
        sem=("parallel", "parallel", "arbitrary"), name="conv_dft2_mul_idft2")(
            a.reshape(P, 2, N1, N2, C), m_fwd, m_inv, kf)
    return g.reshape(P, 2 * N1, N2, C)


def _conv_finish(g, z4, x14, skip, out_dtype, N1, N2, chain):
    P, K, _, C = z4.shape
    f1_pair, _, _, _, f1_inv = _fft_tables(N1, N2)
    tc = _tile(C, (512, 256, 128))
    blk = lambda rows: pl.BlockSpec((1, rows, SUB_BLOCK, tc), lambda p, s, c: (p, 0, s, c))
    tab = lambda t: pl.BlockSpec(t.shape, lambda p, s, c: (0, 0))
    tables = [f1_inv, f1_pair] if chain else [f1_inv]
    out_specs = [blk(K), blk(2 * N1)] if chain else blk(K)
    z_sds = jax.ShapeDtypeStruct((P, K, N2, C), out_dtype)
    out_shape = [z_sds, jax.ShapeDtypeStruct((P, 2 * N1, N2, C), BF16)] if chain else z_sds
    return _call(
        functools.partial(_idft1_gate_body, chain=chain), grid=(P, N2 // SUB_BLOCK, C // tc),
        in_specs=[tab(t) for t in tables] + [blk(2 * N1), blk(K), blk(K), pl.BlockSpec((1, tc), lambda p, s, c: (0, c))],
        out_specs=out_specs, out_shape=out_shape,
        sem=("parallel", "parallel", "parallel"), name="conv_idft1_gate_dft1" if chain else "conv_idft1_gate")(
            *tables, g, z4, x14, skip[None, :])


def _hyena_long_convs_2stage(v, x1, x2, skip, kf, N1, N2):
    B, n, C = v.shape
    assert B % 2 == 0
    P, S1 = B // 2, N1 // 2
    f1_pair = _fft_tables(N1, N2)[0]
    view = lambda t: t.reshape(P, 2 * S1, N2, C)
    a0 = _outer_dft(f1_pair, view(v), BF16, "conv_dft1")
    z1, a1 = _conv_finish(_conv_mid(a0, kf, 0, N1, N2), view(v), view(x1), skip[0], F32, N1, N2, True)
    y = _conv_finish(_conv_mid(a1, kf, 1, N1, N2), z1, view(x2), skip[1], BF16, N1, N2, False)
    return y.reshape(B, n, C)


@functools.lru_cache(maxsize=None)
def _dft_tables(n):
    N = 2 * n
    f = np.arange(N)
    ang = -2.0 * np.pi * ((f[:, None] * f[None, :]) % N) / N
    fr, fi = np.cos(ang), np.sin(ang)
    fwd_full = np.concatenate([fr, fi], axis=0)
    fwd_half = fwd_full[:, :n]
    inv = np.concatenate([fr[:n, :], fi[:n, :]], axis=1) / N
    cvt = lambda a: jnp.asarray(a, dtype=BF16)
    return cvt(fwd_full), cvt(fwd_half), cvt(inv)


def _dftconv_body(z_ref, x1_ref, sk_ref, f_ref, fi_ref, k_ref, o_ref, *, N):
    z = z_ref[0]
    x = jnp.dot(f_ref[...], z.astype(BF16), preferred_element_type=F32)
    yr, yi = _cmul(x[:N], x[N:], k_ref[0], k_ref[1])
    y = jnp.concatenate([yr, yi], axis=0).astype(BF16)
    y = jnp.dot(fi_ref[...], y, preferred_element_type=F32)
    o_ref[0] = (x1_ref[0] * (y + sk_ref[...] * z)).astype(o_ref.dtype)


def _long_conv_gate_dense(z, x1, skip, kf, order, out_dtype):
    B, n, C = z.shape
    N = 2 * n
    _, fwd_half, inv = _dft_tables(n)
    tc = _tile(C, (256, 128))
    oc = order * (C // tc)
    return _call(
        functools.partial(_dftconv_body, N=N), grid=(C // tc, B),
        in_specs=[pl.BlockSpec((1, n, tc), lambda c, b: (b, 0, c)),
                  pl.BlockSpec((1, n, tc), lambda c, b: (b, 0, c)),
                  pl.BlockSpec((1, tc), lambda c, b: (0, c)),
                  pl.BlockSpec((2 * N, n), lambda c, b: (0, 0)),
                  pl.BlockSpec((n, 2 * N), lambda c, b: (0, 0)),
                  pl.BlockSpec((2, N, tc), lambda c, b: (0, 0, oc + c))],
        out_specs=pl.BlockSpec((1, n, tc), lambda c, b: (b, 0, c)),
        out_shape=jax.ShapeDtypeStruct((B, n, C), out_dtype),
        sem=("parallel", "arbitrary"), name="conv_dense_dft")(z, x1, skip[None, :], fwd_half, inv, kf)


def _hyena_mixer(v, x1, x2, w1, b1, freq, w2, b2, w3, skip):
    B, n, _ = v.shape
    k = _hyena_filters(n, w1, b1, freq, w2, b2, w3)
    N = 2 * n
    if N % FFT_N2 == 0 and (N // FFT_N2) >= 16:
        N1 = N // FFT_N2
        kf = _filter_spectrum_2stage(k, N1, FFT_N2)
        return _hyena_long_convs_2stage(v, x1, x2, skip, kf, N1, FFT_N2)
    fwd_full, _, _ = _dft_tables(n)
    kf = _left_matmul(fwd_full, k[None], F32, "filter_dense_dft").reshape(2, N, k.shape[1])
    z = _long_conv_gate_dense(v, x1, skip[0], kf, 0, F32)
    return _long_conv_gate_dense(z, x2, skip[1], kf, 1, BF16)


def _nt(a, b):
    return lax.dot_general(a, b, (((1,), (1,)), ((), ())), preferred_element_type=F32)


def _na_body(q_ref, k_ref, v_ref, kc_ref, vc_ref, b_ref, o_ref, *, R, KR, rows, HPS, scale):
    j = pl.program_id(2)
    start = pl.multiple_of(jnp.clip(j * R - NA_WIN_R // 2, 0, rows - KR) * GRID_W, GRID_W)
    for h in range(HPS):
        hs = slice(h * HEAD_DIM, (h + 1) * HEAD_DIM)
        q = q_ref[0, :, hs]
        kw = k_ref[0, pl.ds(start, KR * GRID_W), hs]
        vw = v_ref[0, pl.ds(start, KR * GRID_W), hs]
        s = _nt(q, kw) + b_ref[h, 0]
        sc = _nt(q, kc_ref[0, :, hs])
        m = jnp.maximum(jnp.max(s, axis=-1, keepdims=True), jnp.max(sc, axis=-1, keepdims=True))
        p = jnp.exp2((s - m) * (scale * LOG2_E))
        pc = jnp.exp2((sc - m) * (scale * LOG2_E))
        l = jnp.sum(p, axis=-1, keepdims=True) + jnp.sum(pc, axis=-1, keepdims=True)
        o = jnp.dot(p.astype(BF16), vw, preferred_element_type=F32)
        o = o + jnp.dot(pc.astype(BF16), vc_ref[0, :, hs], preferred_element_type=F32)
        o_ref[0, :, hs] = (o / l).astype(o_ref.dtype)


def _na_geometry(S):
    rows = S // GRID_W
    kr = min(NA_WIN_R, rows)
    R = min(8, rows)
    KR = min(rows, R + kr)
    nb = rows // R
    assert rows % R == 0
    types = sorted({0, min(1, nb - 1), nb - 1})
    if nb > 3:
        offs = {int(np.clip(j * R - NA_WIN_R // 2, 0, rows - KR)) - j * R for j in range(1, nb - 1)}
        assert len(offs) == 1
    return rows, kr, R, KR, nb, types


def _nabias_body(rpb_ref, o_ref, tw_ref, *, plan, R, KR, inv_scale):
    W = GRID_W
    nd_r, nd_c = 2 * NA_WIN_R - 1, 2 * NA_WIN_C - 1
    base = pl.program_id(0) * (nd_r * nd_c)
    qc = lax.broadcasted_iota(jnp.int32, (W, 2 * W), 0)
    lane = lax.broadcasted_iota(jnp.int32, (W, 2 * W), 1)
    kc = lane % W
    cs = jnp.clip(qc - NA_WIN_C // 2, 0, W - NA_WIN_C)
    col_ok = jnp.logical_and(kc >= cs, kc < cs + NA_WIN_C)
    dcm = kc - qc + (NA_WIN_C - 1)
    neg = jnp.full((W, 2 * W), NEG_INF, F32)
    for dr in range(nd_r):
        acc = neg
        for dc in range(nd_c):
            acc = jnp.where(dcm == dc, rpb_ref[base + dr * nd_c + dc] * inv_scale, acc)
        tw_ref[dr] = jnp.where(col_ok, acc, NEG_INF)
    left = lane < W
    for t, per_q in enumerate(plan):
        for qr in range(R):
            for kp in range(KR // 2):
                d0, d1 = per_q[qr][kp]
                a = neg if d0 is None else tw_ref[d0]
                b = neg if d1 is None else tw_ref[d1]
                blk = neg if (d0 is None and d1 is None) else jnp.where(left, a, b)
                o_ref[0, t, qr * W:(qr + 1) * W, kp * 2 * W:(kp + 1) * 2 * W] = blk


def _na_bias_tables(rpb, S):
    rows, kr, R, KR, nb, types = _na_geometry(S)
    assert 2 * GRID_W == LANES and KR % 2 == 0
    plan = []
    for jt in types:
        start = int(np.clip(jt * R - NA_WIN_R // 2, 0, rows - KR))
        per_q = []
        for q in range(R):
            qra = jt * R + q
            ws = int(np.clip(qra - kr // 2, 0, rows - kr))
            d = [(start + k) - qra + (NA_WIN_R - 1) if ws <= start + k < ws + kr else None for k in range(KR)]
            per_q.append([(d[2 * p], d[2 * p + 1]) for p in range(KR // 2)])
        plan.append(per_q)
    L, H, nd_r, nd_c = rpb.shape
    T, QB, KB = len(types), R * GRID_W, KR * GRID_W
    return _call(
        functools.partial(_nabias_body, plan=plan, R=R, KR=KR, inv_scale=HEAD_DIM ** 0.5), grid=(L * H,),
        in_specs=[pl.BlockSpec(memory_space=pltpu.SMEM)],
        out_specs=pl.BlockSpec((1, T, QB, KB), lambda i: (i, 0, 0, 0)),
        out_shape=jax.ShapeDtypeStruct((L * H, T, QB, KB), F32),
        scratch=[pltpu.VMEM((nd_r, GRID_W, 2 * GRID_W), F32)],
        sem=("parallel",), name="na_bias_table")(rpb.reshape(-1).astype(F32))


def _na_attention(qkv, qkv_c, bias, l, offs, H):
    B, S, _ = qkv.shape
    CTX = qkv_c.shape[1]
    rows, kr, R, KR, nb, types = _na_geometry(S)
    T = len(types)
    QB, KB = R * GRID_W, KR * GRID_W
    HPS = next(n for n in (4, 2, 1) if H % n == 0)
    HW = HPS * HEAD_DIM
    assert all(offs[n] % HW == 0 for n in ("na_k", "na_v", "na_q"))
    ok_, ov_, oq_ = (offs[n] // HW for n in ("na_k", "na_v", "na_q"))

    def btype(j):
        if T == nb:
            return j
        return jnp.where(j == 0, 0, jnp.where(j == nb - 1, T - 1, 1))

    return _call(
        functools.partial(_na_body, R=R, KR=KR, rows=rows, HPS=HPS, scale=HEAD_DIM ** -0.5),
        grid=(B, H // HPS, nb),
        in_specs=[pl.BlockSpec((1, QB, HW), lambda b, h, j: (b, j, oq_ + h)),
                  pl.BlockSpec((1, S, HW), lambda b, h, j: (b, 0, ok_ + h)),
                  pl.BlockSpec((1, S, HW), lambda b, h, j: (b, 0, ov_ + h)),
                  pl.BlockSpec((1, CTX, HW), lambda b, h, j: (b, 0, ok_ + h)),
                  pl.BlockSpec((1, CTX, HW), lambda b, h, j: (b, 0, ov_ + h)),
                  pl.BlockSpec((HPS, 1, QB, KB), lambda b, h, j: (l * (H // HPS) + h, btype(j), 0, 0))],
        out_specs=pl.BlockSpec((1, QB, HW), lambda b, h, j: (b, j, h)),
        out_shape=jax.ShapeDtypeStruct((B, S, H * HEAD_DIM), BF16),
        sem=("parallel", "parallel", "arbitrary"), name="na_attention")(qkv, qkv, qkv, qkv_c, qkv_c, bias)


def _stack_heads(q2, G):
    return jnp.concatenate([q2[:, g * HEAD_DIM:(g + 1) * HEAD_DIM] for g in range(G)], axis=0)


def _unstack_heads(o, G, n):
    return jnp.concatenate([o[g * n:(g + 1) * n] for g in range(G)], axis=1)


def _sink_column(sink_ref, h0, G, n):
    return jnp.concatenate([jnp.full((n, 1), sink_ref[h0 + g], F32) for g in range(G)], axis=0)


def _swa_body(sink_ref, q_ref, k_ref, v_ref, kc_ref, vc_ref, o_ref, *, QB, KB, S, G, HPS, QPS, scale):
    hb = pl.program_id(1)
    for qb in range(QPS):
        j = pl.program_id(2) * QPS + qb
        rows = slice(qb * QB, (qb + 1) * QB)
        start = pl.multiple_of(jnp.clip(j * QB - GQA_WINDOW, 0, S - KB), LANES)
        qpos = j * QB + lax.broadcasted_iota(jnp.int32, (QB, KB), 0)
        kpos = start + lax.broadcasted_iota(jnp.int32, (QB, KB), 1)
        mask = jnp.where(jnp.abs(qpos - kpos) <= GQA_WINDOW, 0.0, NEG_INF)
        mask = jnp.concatenate([mask] * G, axis=0)
        for h in range(HPS):
            hs = slice(h * HEAD_DIM, (h + 1) * HEAD_DIM)
            qs = slice(h * G * HEAD_DIM, (h + 1) * G * HEAD_DIM)
            q = _stack_heads(q_ref[0, rows, qs], G)
            kw = k_ref[0, pl.ds(start, KB), hs]
            vw = v_ref[0, pl.ds(start, KB), hs]
            s = _nt(q, kw) + mask
            sc = _nt(q, kc_ref[0, :, hs])
            sk = _sink_column(sink_ref, (hb * HPS + h) * G, G, QB) * (1.0 / scale)
            m = jnp.maximum(jnp.maximum(jnp.max(s, axis=-1, keepdims=True), jnp.max(sc, axis=-1, keepdims=True)), sk)
            p = jnp.exp2((s - m) * (scale * LOG2_E))
            pc = jnp.exp2((sc - m) * (scale * LOG2_E))
            l = (jnp.sum(p, axis=-1, keepdims=True) + jnp.sum(pc, axis=-1, keepdims=True)
                 + jnp.exp2((sk - m) * (scale * LOG2_E)))
            o = jnp.dot(p.astype(BF16), vw, preferred_element_type=F32)
            o = o + jnp.dot(pc.astype(BF16), vc_ref[0, :, hs], preferred_element_type=F32)
            o_ref[0, rows, qs] = _unstack_heads(o / l, G, QB).astype(o_ref.dtype)


def _swa_attention(qkv, qkv_c, sink, offs, KVH):
    B, S, _ = qkv.shape
    CTX = qkv_c.shape[1]
    G = GQA_GROUP
    QB = _tile(S, (512, 256, 128))
    KB = min(S, QB + 2 * GQA_WINDOW)
    HPS = 2 if KVH % 2 == 0 else 1
    QPS = 2 if (S // QB) % 2 == 0 else 1
    HW = HPS * HEAD_DIM
    assert offs["sw_k"] % HW == 0 and offs["sw_v"] % HW == 0 and offs["sw_q"] % (G * HW) == 0
    ok_, ov_, oq_ = offs["sw_k"] // HW, offs["sw_v"] // HW, offs["sw_q"] // (G * HW)
    return _call(
        functools.partial(_swa_body, QB=QB, KB=KB, S=S, G=G, HPS=HPS, QPS=QPS, scale=HEAD_DIM ** -0.5),
        grid=(B, KVH // HPS, S // (QB * QPS)),
        in_specs=[pl.BlockSpec(memory_space=pltpu.SMEM),
                  pl.BlockSpec((1, QPS * QB, G * HW), lambda b, h, j: (b, j, oq_ + h)),
                  pl.BlockSpec((1, S, HW), lambda b, h, j: (b, 0, ok_ + h)),
                  pl.BlockSpec((1, S, HW), lambda b, h, j: (b, 0, ov_ + h)),
                  pl.BlockSpec((1, CTX, HW), lambda b, h, j: (b, 0, ok_ + h)),
                  pl.BlockSpec((1, CTX, HW), lambda b, h, j: (b, 0, ov_ + h))],
        out_specs=pl.BlockSpec((1, QPS * QB, G * HW), lambda b, h, j: (b, j, h)),
        out_shape=jax.ShapeDtypeStruct((B, S, KVH * G * HEAD_DIM), BF16),
        sem=("parallel", "parallel", "arbitrary"), name="swa_attention")(sink, qkv, qkv, qkv, qkv_c, qkv_c)


def _cattn_body(sink_ref, q_ref, k_ref, v_ref, o_ref, *, G, use_sink, scale):
    h = pl.program_id(1)
    n = q_ref.shape[1]
    q = _stack_heads(q_ref[0], G)
    s = _nt(q, k_ref[0]) * scale
    m = jnp.max(s, axis=-1, keepdims=True)
    if use_sink:
        sk = _sink_column(sink_ref, h * G, G, n)
        m = jnp.maximum(m, sk)
    p = jnp.exp(s - m)
    l = jnp.sum(p, axis=-1, keepdims=True)
    if use_sink:
        l = l + jnp.exp(sk - m)
    o = jnp.dot(p.astype(BF16), v_ref[0], preferred_element_type=F32)
    o_ref[0] = _unstack_heads(o / l, G, n).astype(o_ref.dtype)


def _ctx_attention(qkv_c, sink, oq, ok, ov, KVH, G, use_sink):
    B, n, _ = qkv_c.shape
    oq_, ok_, ov_ = oq // (G * HEAD_DIM), ok // HEAD_DIM, ov // HEAD_DIM
    return _call(
        functools.partial(_cattn_body, G=G, use_sink=use_sink, scale=HEAD_DIM ** -0.5),
        grid=(B, KVH),
        in_specs=[pl.BlockSpec(memory_space=pltpu.SMEM),
                  pl.BlockSpec((1, n, G * HEAD_DIM), lambda b, h: (b, 0, oq_ + h)),
                  pl.BlockSpec((1, n, HEAD_DIM), lambda b, h: (b, 0, ok_ + h)),
                  pl.BlockSpec((1, n, HEAD_DIM), lambda b, h: (b, 0, ov_ + h))],
        out_specs=pl.BlockSpec((1, n, G * HEAD_DIM), lambda b, h: (b, 0, h)),
        out_shape=jax.ShapeDtypeStruct((B, n, KVH * G * HEAD_DIM), BF16),
        sem=("parallel", "parallel"), name="ctx_attention")(sink, qkv_c, qkv_c, qkv_c)


def _merge_body(x_ref, g_ref, sh_ref, sc_ref, yh, yn, ys, wgh, wgn, wgs, wh, wn, ws, o_ref, h_ref, *, eps):
    @pl.when(pl.program_id(2) == 0)
    def _():
        h_ref[...] = _norm_modulate(x_ref[0], g_ref, sh_ref, sc_ref, eps).astype(BF16)

    h = h_ref[...]
    gate = lambda wg: _sigmoid(jnp.dot(h, wg[...], preferred_element_type=F32))
    m = gate(wgh) * jnp.dot(yh[0], wh[...], preferred_element_type=F32)
    m = m + gate(wgn) * jnp.dot(yn[0], wn[...], preferred_element_type=F32)
    m = m + gate(wgs) * jnp.dot(ys[0], ws[...], preferred_element_type=F32)
    o_ref[0] = m.astype(o_ref.dtype)


def _merge_branches(x, gain, shift, scale, w_gate, y_hy, y_na, y_sw, w_br, l):
    B, S, D = x.shape
    widths = (y_hy.shape[2], y_na.shape[2], y_sw.shape[2])
    starts = (0, widths[0], widths[0] + widths[1])
    assert all(s % w == 0 for s, w in zip(starts, widths))
    tm = _tile(S, (512, 256, 128))
    gate_start = w_gate.shape[2] - 3 * D
    tn = _seg_tile([(gate_start, D)], (512, 256, 128))
    nj, gj = D // tn, gate_start // tn
    yspec = lambda y: pl.BlockSpec((1, tm, y.shape[2]), lambda b, i, j: (b, i, 0))
    gspec = lambda k: pl.BlockSpec((None, D, tn), lambda b, i, j: (l, 0, gj + k * nj + j))
    wspec = lambda k: pl.BlockSpec((None, widths[k], tn), lambda b, i, j: (l, starts[k] // widths[k], j))
    return _call(
        functools.partial(_merge_body, eps=NORM_EPS), grid=(B, S // tm, nj),
        in_specs=[pl.BlockSpec((1, tm, D), lambda b, i, j: (b, i, 0)),
                  pl.BlockSpec((1, D), lambda b, i, j: (0, 0)),
                  pl.BlockSpec((1, 1, D), lambda b, i, j: (b, 0, 0)),
                  pl.BlockSpec((1, 1, D), lambda b, i, j: (b, 0, 0)),
                  yspec(y_hy), yspec(y_na), yspec(y_sw), gspec(0), gspec(1), gspec(2),
                  wspec(0), wspec(1), wspec(2)],
        out_specs=pl.BlockSpec((1, tm, tn), lambda b, i, j: (b, i, j)),
        out_shape=jax.ShapeDtypeStruct((B, S, D), BF16),
        scratch=[pltpu.VMEM((tm, D), BF16)],
        sem=("parallel", "parallel", "arbitrary"), name="gated_merge")(
            x, gain.reshape(1, D), shift, scale, y_hy, y_na, y_sw, w_gate, w_gate, w_gate, w_br, w_br, w_br)


def _mmres_body(a_ref, w_ref, x_ref, g_ref, mg_ref, o_ref, *, nk, eps):
    k = pl.program_id(2)
    part = lambda: jnp.dot(a_ref[0], w_ref[...], preferred_element_type=F32)

    def finish():
        y = o_ref[0]
        yn = y * lax.rsqrt(jnp.mean(y * y, axis=-1, keepdims=True) + eps) * g_ref[...]
        o_ref[0] = x_ref[0] + mg_ref[0] * yn

    if nk == 1:
        o_ref[0] = part()
        finish()
        return

    @pl.when(k == 0)
    def _():
        o_ref[0] = part()

    @pl.when(k > 0)
    def _():
        o_ref[0] += part()

    @pl.when(k == nk - 1)
    def _():
        finish()


def _matmul_norm_residual(a, w, l, x, gain, mgate, name):
    B, S, K = a.shape
    D = w.shape[2]
    if K <= 2048:
        tm, tk = _tile(S, (512, 256, 128)), K
    else:
        tm = _tile(S, (1024, 512, 256, 128))
        tk = max(t for t in range(LANES, 1537, LANES) if K % t == 0)
    nk = K // tk
    est = 2 * (tm * tk * 2 + tk * D * 2 + 2 * tm * D * 4)
    return _call(
        functools.partial(_mmres_body, nk=nk, eps=NORM_EPS), grid=(B, S // tm, nk),
        vmem=VMEM_LIMIT_LARGE if est > VMEM_LIMIT - VMEM_TEMP_RESERVE else VMEM_LIMIT,
        in_specs=[pl.BlockSpec((1, tm, tk), lambda b, i, k: (b, i, k)),
                  pl.BlockSpec((None, tk, D), lambda b, i, k: (l, k, 0)),
                  pl.BlockSpec((1, tm, D), lambda b, i, k: (b, i, 0)),
                  pl.BlockSpec((1, D), lambda b, i, k: (0, 0)),
                  pl.BlockSpec((1, 1, D), lambda b, i, k: (b, 0, 0))],
        out_specs=pl.BlockSpec((1, tm, D), lambda b, i, k: (b, i, 0)),
        out_shape=jax.ShapeDtypeStruct((B, S, D), F32),
        sem=("parallel", "parallel", "arbitrary"), name=name)(a, w, x, gain.reshape(1, D), mgate)


def _rope_tables(n):
    t = jnp.arange(n)
    row = (t // GRID_W).astype(F32)
    col = (t % GRID_W).astype(F32)
    per_axis = HEAD_DIM // 2
    inv = ROPE_BASE ** (-jnp.arange(0, per_axis, 2, dtype=F32) / per_axis)
    ar, ac = row[:, None] * inv, col[:, None] * inv
    cos_t = jnp.concatenate([jnp.cos(ar), jnp.cos(ar), jnp.cos(ac), jnp.cos(ac)], axis=1)
    sin_t = jnp.concatenate([-jnp.sin(ar), jnp.sin(ar), -jnp.sin(ac), jnp.sin(ac)], axis=1)
    return cos_t, sin_t


def kernel(x, c, ctx, c_ctx, w_mod, b_mod, norm_gains, w_in, hy_conv_w, hy_conv_b, hy_w1, hy_b1, hy_freq, hy_w2, hy_b2, hy_w3, hy_skip, na_rpb, swa_sink, w_branch, w_out, ffn_w_up, ffn_conv_w, ffn_conv_b, ffn_w_down):
    B, S, D = x.shape
    L = w_mod.shape[0]
    C = hy_skip.shape[-1]
    H_na = na_rpb.shape[1]
    H_q = swa_sink.shape[1]
    KVH = H_q // GQA_GROUP
    NA_W, QW, KVW = H_na * HEAD_DIM, H_q * HEAD_DIM, KVH * HEAD_DIM
    KV_COLS = 2 * NA_W + 2 * KVW
    qkv_segs = [(0, KV_COLS), (KV_COLS + 3 * C, NA_W + QW)]
    offs = {"na_k": 0, "na_v": NA_W, "sw_k": 2 * NA_W, "sw_v": 2 * NA_W + KVW,
            "na_q": KV_COLS, "sw_q": KV_COLS + NA_W}
    rope_chunks = (list(range(offs["sw_k"] // LANES, (offs["sw_k"] + KVW) // LANES))
                   + list(range(offs["sw_q"] // LANES, (offs["sw_q"] + QW) // LANES)))
    cos_t, sin_t = _rope_tables(S)

    w_out_b, w_br_b, w_dn_b = w_out.astype(BF16), w_branch.astype(BF16), ffn_w_down.astype(BF16)
    w_in_b = w_in.astype(BF16)
    na_bias = _na_bias_tables(na_rpb, S)
    CTX = ctx.shape[1]
    flat = lambda t: t.reshape(1, -1, t.shape[-1])
    unflat = lambda t: t.reshape(B, CTX, t.shape[-1])

    R = -(-(B + 1) // 8) * 8
    cc = jnp.concatenate([c, c_ctx[None, :], jnp.zeros((R - B - 1, D), F32)], axis=0)
    mods = _modulation(cc, w_mod, b_mod)

    xc = ctx
    for l in range(L):
        mod = [mods[l, :B, k * D:(k + 1) * D].reshape(B, 1, D) for k in range(6)]
        mod_c = [mods[l, B, k * D:(k + 1) * D].reshape(1, 1, D) for k in range(6)]
        g = norm_gains[l]
        hy_p = (hy_w1[l], hy_b1[l], hy_freq[l], hy_w2[l], hy_b2[l], hy_w3[l], hy_skip[l])

        qkv_c, vc, x1c, x2c = map(unflat, _in_proj(flat(xc), g[0], mod_c[0], mod_c[1], w_in_b, l, qkv_segs, KV_COLS, C,
                                                   hy_conv_w[l], hy_conv_b[l], None, "ctx_in_proj", period=CTX))
        qkv, v, x1, x2 = _in_proj(x, g[0], mod[0], mod[1], w_in_b, l, qkv_segs, KV_COLS, C,
                                  hy_conv_w[l], hy_conv_b[l], (cos_t, sin_t, rope_chunks), "in_proj")
        y_hy = _hyena_mixer(v, x1, x2, *hy_p)
        y_na = _na_attention(qkv, qkv_c, na_bias, l, offs, H_na)
        y_sw = _swa_attention(qkv, qkv_c, swa_sink[l], offs, KVH)
        m = _merge_branches(x, g[0], mod[0], mod[1], w_in_b, y_hy, y_na, y_sw, w_br_b, l)
        x = _matmul_norm_residual(m, w_out_b, l, x, g[1], mod[2], "out_proj_residual")
        gl = _ffn_up_glu(x, g[2], mod[3], mod[4], ffn_w_up, l, ffn_conv_w[l], ffn_conv_b[l])
        x = _matmul_norm_residual(gl, w_dn_b, l, x, g[3], mod[5], "ffn_down_residual")

        if l < L - 1:
            yc_hy = _hyena_mixer(vc, x1c, x2c, *hy_p)
            yc_na = _ctx_attention(qkv_c, swa_sink[l], offs["na_q"], offs["na_k"], offs["na_v"], H_na, 1, False)
            yc_sw = _ctx_attention(qkv_c, swa_sink[l], offs["sw_q"], offs["sw_k"], offs["sw_v"], KVH, GQA_GROUP, True)
            mc = _merge_branches(flat(xc), g[0], mod_c[0], mod_c[1], w_in_b,
                                 flat(yc_hy), flat(yc_na), flat(yc_sw), w_br_b, l)
            xcf = _matmul_norm_residual(mc, w_out_b, l, flat(xc), g[1], mod_c[2], "ctx_out_proj_residual")
            gl_c = _ffn_up_glu(xcf, g[2], mod_c[3], mod_c[4], ffn_w_up, l, ffn_conv_w[l], ffn_conv_b[l], period=CTX)
            xc = unflat(_matmul_norm_residual(gl_c, w_dn_b, l, xcf, g[3], mod_c[5], "ctx_ffn_down_residual"))
    return x
```

```python
import functools
import math

import numpy as np
import jax
import jax.numpy as jnp
from jax import lax
from jax.experimental import pallas as pl
from jax.experimental.pallas import tpu as pltpu

F32 = jnp.float32
BF16 = jnp.bfloat16
NEG_INF = -1e30

GRID_W = 64
HEAD_DIM = 128
HY_EMB = 33
HY_BANDS = (HY_EMB - 1) // 2
HY_FAST_DECAY = 0.3
HY_SLOW_DECAY = 1.5
HY_DECAY_TARGET = 1e-2
HY_MAX_DECAY = math.log(HY_DECAY_TARGET) / HY_FAST_DECAY
HY_MIN_DECAY = math.log(HY_DECAY_TARGET) / HY_SLOW_DECAY
NA_WIN_R = 8
NA_WIN_C = 16
GQA_GROUP = 2
GQA_WINDOW = 128
ROPE_BASE = 10000.0
NORM_EPS = 1e-6

LANES = 128
FFT_N2 = 128
VMEM_LIMIT = 56 * 1024 * 1024
VMEM_LIMIT_LARGE = 60 * 1024 * 1024
VMEM_TEMP_RESERVE = 8 * 1024 * 1024
LOG2_E = math.log2(math.e)


def _call(body, *, grid, in_specs, out_specs, out_shape, scratch=(), sem, name, vmem=VMEM_LIMIT):
    return pl.pallas_call(
        body, grid=grid, in_specs=in_specs, out_specs=out_specs, out_shape=out_shape,
        scratch_shapes=list(scratch),
        compiler_params=pltpu.CompilerParams(dimension_semantics=sem, vmem_limit_bytes=vmem),
        name=name)


def _tile(n, cands):
    for c in cands:
        if n % c == 0:
            return c
    raise ValueError(f"no tile for {n} in {cands}")


def _sigmoid(x):
    return 1.0 / (1.0 + jnp.exp(-x))


def _mod_body(c_ref, w_ref, b_ref, o_ref):
    c = c_ref[...]
    s = (c * _sigmoid(c)).astype(BF16)
    o_ref[0] = jnp.dot(s, w_ref[0].astype(BF16), preferred_element_type=F32) + b_ref[0]


def _modulation(cc, w_mod, b_mod):
    L, D, N = w_mod.shape
    R = cc.shape[0]
    tn = _tile(N, (1024, 512, 256, 128))
    return _call(
        _mod_body, grid=(L, N // tn),
        in_specs=[pl.BlockSpec((R, D), lambda l, j: (0, 0)),
                  pl.BlockSpec((1, D, tn), lambda l, j: (l, 0, j)),
                  pl.BlockSpec((1, 1, tn), lambda l, j: (l, 0, j))],
        out_specs=pl.BlockSpec((1, R, tn), lambda l, j: (l, 0, j)),
        out_shape=jax.ShapeDtypeStruct((L, R, N), F32),
        sem=("parallel", "parallel"), name="modulation")(cc, w_mod, b_mod.reshape(L, 1, N))


def _swap32(a):
    lane = lax.broadcasted_iota(jnp.int32, a.shape, 1)
    return jnp.where((lane & 32) == 0, pltpu.roll(a, 96, 1), pltpu.roll(a, 32, 1))


def _norm_modulate(x, g_ref, sh_ref, sc_ref, eps):
    y = x * lax.rsqrt(jnp.mean(x * x, axis=-1, keepdims=True) + eps) * g_ref[...]
    return y * (1.0 + sc_ref[0]) + sh_ref[0]


HALO = 16


def _halo_prologue(h_ref, x_ref, xp_ref, xn_ref, g_ref, sh_ref, sc_ref, eps):
    i, tm = pl.program_id(1), x_ref.shape[1]
    nm = lambda x: _norm_modulate(x, g_ref, sh_ref, sc_ref, eps)
    keep_prev = (i > 0).astype(F32)
    keep_next = (i < pl.num_programs(1) - 1).astype(F32)
    h_ref[0:HALO, :] = (nm(xp_ref[0]) * keep_prev).astype(BF16)
    h_ref[HALO:HALO + tm, :] = nm(x_ref[0]).astype(BF16)
    h_ref[HALO + tm:, :] = (nm(xn_ref[0]) * keep_next).astype(BF16)


def _conv3_halo(a, cw_ref, cb_ref, tm, period):
    rows = a.shape[0]
    up = pltpu.roll(a, 1, 0)[HALO:HALO + tm]
    un = pltpu.roll(a, rows - 1, 0)[HALO:HALO + tm]
    if period is not None:
        t = (pl.program_id(1) * tm + lax.broadcasted_iota(jnp.int32, up.shape, 0)) % period
        up = jnp.where(t == 0, 0.0, up)
        un = jnp.where(t == period - 1, 0.0, un)
    return up * cw_ref[0:1, :] + a[HALO:HALO + tm] * cw_ref[1:2, :] + un * cw_ref[2:3, :] + cb_ref[...]


def _halo_specs(S, tm, D):
    hb, nh = tm // HALO, S // HALO
    return [pl.BlockSpec((1, tm, D), lambda b, i, j: (b, i, 0)),
            pl.BlockSpec((1, HALO, D), lambda b, i, j: (b, jnp.maximum(i * hb - 1, 0), 0)),
            pl.BlockSpec((1, HALO, D), lambda b, i, j: (b, jnp.minimum((i + 1) * hb, nh - 1), 0)),
            pl.BlockSpec((1, D), lambda b, i, j: (0, 0)),
            pl.BlockSpec((1, 1, D), lambda b, i, j: (b, 0, 0)),
            pl.BlockSpec((1, 1, D), lambda b, i, j: (b, 0, 0))]


def _inproj_body(*refs, nq, rope_chunks, period, eps):
    if rope_chunks:
        x_ref, xp_ref, xn_ref, g_ref, sh_ref, sc_ref, w_ref, cw_ref, cb_ref, cos_ref, sin_ref, q_o, hy_o, h_ref = refs
    else:
        x_ref, xp_ref, xn_ref, g_ref, sh_ref, sc_ref, w_ref, cw_ref, cb_ref, q_o, hy_o, h_ref = refs
    j = pl.program_id(2)
    tm = x_ref.shape[1]

    @pl.when(j == 0)
    def _():
        _halo_prologue(h_ref, x_ref, xp_ref, xn_ref, g_ref, sh_ref, sc_ref, eps)

    qkv_dot = lambda: jnp.dot(h_ref[HALO:HALO + tm, :], w_ref[...].astype(BF16), preferred_element_type=F32)
    nch = w_ref.shape[1] // LANES
    rope_tiles = sorted({ch // nch for ch in rope_chunks})
    for jt in rope_tiles:
        @pl.when(j == jt)
        def _(jt=jt):
            acc = qkv_dot()
            c, s = cos_ref[...], sin_ref[...]
            for k in range(nch):
                a = acc[:, k * LANES:(k + 1) * LANES]
                if jt * nch + k in rope_chunks:
                    a = a * c + _swap32(a) * s
                q_o[0, :, k * LANES:(k + 1) * LANES] = a.astype(q_o.dtype)

    @pl.when(functools.reduce(jnp.logical_and, [j != jt for jt in rope_tiles], j < nq))
    def _():
        q_o[0] = qkv_dot().astype(q_o.dtype)

    @pl.when(j >= nq)
    def _():
        a = jnp.dot(h_ref[...], w_ref[...].astype(BF16), preferred_element_type=F32)
        hy_o[0] = _conv3_halo(a, cw_ref, cb_ref, tm, period)


def _colmap(segs, tn):
    bounds, o = [], 0
    for s, w in segs:
        assert s % tn == 0 and w % tn == 0
        bounds.append(((o + w) // tn, (s - o) // tn))
        o += w

    def f(j):
        r = j + bounds[-1][1]
        for hi, off in reversed(bounds[:-1]):
            r = jnp.where(j < hi, j + off, r)
        return r
    return f, o


def _seg_tile(segs, cands):
    return _tile(functools.reduce(math.gcd, [v for seg in segs for v in seg if v]), cands)


def _in_proj(x, gain, shift, scale, w_in, l, qkv_segs, hy_start, C, conv_w, conv_b, rope, name, period=None):
    B, S, D = x.shape
    tm = _tile(S, (1024, 512, 256, 128))
    segs = list(qkv_segs) + [(hy_start, 3 * C)]
    tn = _seg_tile(segs + [(0, C)], (512, 256, 128))
    cmap, n_out = _colmap(segs, tn)
    nc = C // tn
    nq = n_out // tn - 3 * nc
    in_specs = _halo_specs(S, tm, D) + [
        pl.BlockSpec((None, D, tn), lambda b, i, j: (l, 0, cmap(j))),
        pl.BlockSpec((3, tn), lambda b, i, j: (0, jnp.clip(j - nq, 0, 3 * nc - 1))),
        pl.BlockSpec((1, tn), lambda b, i, j: (0, jnp.clip(j - nq, 0, 3 * nc - 1)))]
    args = [x, x, x, gain.reshape(1, D), shift, scale, w_in, conv_w, conv_b.reshape(1, 3 * C)]
    rope_chunks = ()
    if rope is not None:
        cos_t, sin_t, rope_chunks = rope
        in_specs += [pl.BlockSpec((tm, LANES), lambda b, i, j: (i, 0)),
                     pl.BlockSpec((tm, LANES), lambda b, i, j: (i, 0))]
        args += [cos_t, sin_t]
    return _call(
        functools.partial(_inproj_body, nq=nq, rope_chunks=frozenset(rope_chunks), period=period, eps=NORM_EPS),
        grid=(B, S // tm, n_out // tn), in_specs=in_specs,
        out_specs=[pl.BlockSpec((1, tm, tn), lambda b, i, j: (b, i, jnp.minimum(j, nq - 1))),
                   pl.BlockSpec((1, tm, tn), lambda b, i, j: (b, i, jnp.maximum(j - nq, 0)))],
        out_shape=[jax.ShapeDtypeStruct((B, S, nq * tn), BF16), jax.ShapeDtypeStruct((B, S, 3 * C), F32)],
        scratch=[pltpu.VMEM((tm + 2 * HALO, D), BF16)],
        sem=("parallel", "parallel", "arbitrary"), name=name)(*args)


def _ffnup_body(x_ref, xp_ref, xn_ref, g_ref, sh_ref, sc_ref, wa_ref, wu_ref, cw_ref, cb_ref, o_ref, h_ref, *, period, eps):
    tm = x_ref.shape[1]

    @pl.when(pl.program_id(2) == 0)
    def _():
        _halo_prologue(h_ref, x_ref, xp_ref, xn_ref, g_ref, sh_ref, sc_ref, eps)

    a = jnp.dot(h_ref[...], wa_ref[...].astype(BF16), preferred_element_type=F32)
    u = jnp.dot(h_ref[HALO:HALO + tm, :], wu_ref[...].astype(BF16), preferred_element_type=F32)
    c = _conv3_halo(a, cw_ref, cb_ref, tm, period)
    o_ref[0] = (c * _sigmoid(c) * u).astype(o_ref.dtype)


def _ffn_up_glu(x, gain, shift, scale, w_up, l, conv_w, conv_b, period=None):
    B, S, D = x.shape
    Fd = w_up.shape[2] // 2
    tm = _tile(S, (1024, 512, 256, 128))
    tn = _tile(Fd, (512, 256, 128))
    nj = Fd // tn
    return _call(
        functools.partial(_ffnup_body, period=period, eps=NORM_EPS), grid=(B, S // tm, nj),
        in_specs=_halo_specs(S, tm, D) + [
            pl.BlockSpec((None, D, tn), lambda b, i, j: (l, 0, j)),
            pl.BlockSpec((None, D, tn), lambda b, i, j: (l, 0, nj + j)),
            pl.BlockSpec((3, tn), lambda b, i, j: (0, j)),
            pl.BlockSpec((1, tn), lambda b, i, j: (0, j))],
        out_specs=pl.BlockSpec((1, tm, tn), lambda b, i, j: (b, i, j)),
        out_shape=jax.ShapeDtypeStruct((B, S, Fd), BF16),
        scratch=[pltpu.VMEM((tm + 2 * HALO, D), BF16)],
        sem=("parallel", "parallel", "arbitrary"), name="ffn_up_glu")(
            x, x, x, gain.reshape(1, D), shift, scale, w_up, w_up, conv_w, conv_b.reshape(1, Fd))


def _dot3(a, w):
    ah, wh = a.astype(BF16), w.astype(BF16)
    al, wl = (a - ah.astype(F32)).astype(BF16), (w - wh.astype(F32)).astype(BF16)
    d = lambda p, q: jnp.dot(p, q, preferred_element_type=F32)
    return d(ah, wh) + d(al, wh) + d(ah, wl)


def _filt_trunk_body(z_ref, w1_ref, b1_ref, fr_ref, w2_ref, b2_ref, o_ref):
    a = _dot3(z_ref[...], w1_ref[...]) + b1_ref[...]
    a = jnp.sin(fr_ref[0:1, :] * a)
    a = _dot3(a, w2_ref[...]) + b2_ref[...]
    o_ref[...] = jnp.sin(fr_ref[1:2, :] * a)


def _filt_main_body(a_ref, wf_ref, wb_ref, t_ref, dl_ref, o_ref, *, n):
    hf = _dot3(a_ref[0:n, :], wf_ref[...])
    hb = _dot3(a_ref[n:2 * n, :], wb_ref[...])
    r = lax.broadcasted_iota(jnp.int32, hb.shape, 0)
    k = jnp.concatenate([hf, jnp.where(r == 0, 0.0, hb)], axis=0) * jnp.exp(-t_ref[...] * dl_ref[...])
    o_ref[...] = k / jnp.sum(jnp.abs(k), axis=0, keepdims=True)


def _hyena_filters(n, w1, b1, freq, w2, b2, w3):
    Hd = w1.shape[1]
    OC = w3.shape[1] // 2
    t = jnp.linspace(0.0, 1.0, n, dtype=F32)[:, None]
    w = (2.0 * math.pi / n) * jnp.arange(n, dtype=F32)[:, None]
    f = jnp.linspace(1e-4, HY_BANDS - 1, HY_BANDS, dtype=F32)[None, :]
    z = jnp.concatenate([t, jnp.cos(f * w), -jnp.sin(f * w)], axis=-1)
    fold = lambda a: jnp.concatenate([a, jnp.zeros_like(a[:1]), a[:0:-1]], axis=0)
    EP = 64
    z2 = jnp.pad(fold(z), ((0, 0), (0, EP - HY_EMB)))
    w1p = jnp.pad(w1, ((0, EP - HY_EMB), (0, 0)))
    rt = _tile(2 * n, (1024, 512, 256))
    a2 = _call(
        _filt_trunk_body, grid=(2 * n // rt,),
        in_specs=[pl.BlockSpec((rt, EP), lambda i: (i, 0)),
                  pl.BlockSpec((EP, Hd), lambda i: (0, 0)),
                  pl.BlockSpec((1, Hd), lambda i: (0, 0)),
                  pl.BlockSpec((2, Hd), lambda i: (0, 0)),
                  pl.BlockSpec((Hd, Hd), lambda i: (0, 0)),
                  pl.BlockSpec((1, Hd), lambda i: (0, 0))],
        out_specs=pl.BlockSpec((rt, Hd), lambda i: (i, 0)),
        out_shape=jax.ShapeDtypeStruct((2 * n, Hd), F32),
        sem=("parallel",), name="hyena_filter_trunk")(z2, w1p, b1.reshape(1, Hd), freq, w2, b2.reshape(1, Hd))
    tc = LANES
    t2 = jnp.broadcast_to(fold(t), (2 * n, tc))
    deltas = jnp.abs(jnp.linspace(HY_MIN_DECAY, HY_MAX_DECAY, OC, dtype=F32))[None, :]
    nc = OC // tc
    return _call(
        functools.partial(_filt_main_body, n=n), grid=(nc,),
        in_specs=[pl.BlockSpec((2 * n, Hd), lambda c: (0, 0)),
                  pl.BlockSpec((Hd, tc), lambda c: (0, c)),
                  pl.BlockSpec((Hd, tc), lambda c: (0, nc + c)),
                  pl.BlockSpec((2 * n, tc), lambda c: (0, 0)),
                  pl.BlockSpec((1, tc), lambda c: (0, c))],
        out_specs=pl.BlockSpec((2 * n, tc), lambda c: (0, c)),
        out_shape=jax.ShapeDtypeStruct((2 * n, OC), F32),
        sem=("parallel",), name="hyena_filter")(a2, w3, w3, t2, deltas)


def _embed(re, im):
    return np.block([[re, -im], [im, re]])


@functools.lru_cache(maxsize=None)
def _fft_tables(N1, N2):
    N = N1 * N2
    S1 = N1 // 2
    i1 = np.arange(N1)
    ang = -2.0 * np.pi * ((i1[:, None] * i1[None, :]) % N1) / N1
    fr, fi = np.cos(ang), np.sin(ang)
    f1_pair = _embed(fr[:, :S1], fi[:, :S1])
    f1_real = np.concatenate([fr, fi], axis=0)
    i2 = np.arange(N2)
    fidx = i1[:, None, None] + N1 * i2[None, :, None]
    ang = -2.0 * np.pi * ((fidx * i2[None, None, :]) % N) / N
    mr, mi = np.cos(ang), np.sin(ang)
    m_fwd = np.stack([_embed(mr[a], mi[a]) for a in range(N1)])
    m_inv = np.stack([_embed(mr[a].T, -mi[a].T) for a in range(N1)])
    ang = 2.0 * np.pi * ((i1[:S1, None] * i1[None, :]) % N1) / N1
    f1_inv = _embed(np.cos(ang) / N, np.sin(ang) / N)
    cvt = lambda a: jnp.asarray(a, dtype=BF16)
    return cvt(f1_pair), cvt(f1_real), cvt(m_fwd), cvt(m_inv), cvt(f1_inv)


def _lmat_body(f_ref, x_ref, o_ref):
    o_ref[0] = jnp.dot(f_ref[...], x_ref[0].astype(BF16), preferred_element_type=F32).astype(o_ref.dtype)


def _left_matmul(fm, x, out_dtype, name):
    P, K, W = x.shape
    R = fm.shape[0]
    tw = _tile(W, (4096, 2048, 1024, 512, 256, 128))
    return _call(
        _lmat_body, grid=(P, W // tw),
        in_specs=[pl.BlockSpec((R, K), lambda p, j: (0, 0)),
                  pl.BlockSpec((1, K, tw), lambda p, j: (p, 0, j))],
        out_specs=pl.BlockSpec((1, R, tw), lambda p, j: (p, 0, j)),
        out_shape=jax.ShapeDtypeStruct((P, R, W), out_dtype),
        sem=("parallel", "parallel"), name=name)(fm, x)


def _cmul(xr, xi, kr, ki):
    return xr * kr - xi * ki, xr * ki + xi * kr


def _fftmid_body(a_ref, m_ref, mi_ref, k_ref, o_ref, *, FB, N2):
    for t in range(FB):
        a = a_ref[0, :, t].reshape(2 * N2, a_ref.shape[-1])
        x = jnp.dot(m_ref[t], a, preferred_element_type=F32)
        yr, yi = _cmul(x[:N2], x[N2:], k_ref[0, t].astype(F32), k_ref[1, t].astype(F32))
        y = jnp.concatenate([yr, yi], axis=0).astype(BF16)
        g = jnp.dot(mi_ref[t], y, preferred_element_type=F32)
        o_ref[0, :, t] = g.reshape(2, N2, g.shape[-1]).astype(o_ref.dtype)


def _fftfwd_body(a_ref, m_ref, o_ref, *, FB, N2):
    for t in range(FB):
        a = a_ref[:, t].reshape(2 * N2, a_ref.shape[-1])
        x = jnp.dot(m_ref[t], a, preferred_element_type=F32)
        o_ref[:, t] = x.reshape(2, N2, x.shape[-1]).astype(o_ref.dtype)


SUB_BLOCK = 16


def _dft1_body(f_ref, x_ref, o_ref):
    xt = pltpu.einshape("ksc->skc", x_ref[0])
    r = jnp.stack([jnp.dot(f_ref[...], xt[k].astype(BF16), preferred_element_type=F32)
                   for k in range(xt.shape[0])], axis=0)
    o_ref[0] = pltpu.einshape("skc->ksc", r).astype(o_ref.dtype)


def _outer_dft(fm, x4, C, out_dtype, name):
    P, K, N2, _ = x4.shape
    R = fm.shape[0]
    tc = _tile(C, (512, 256, 128))
    return _call(
        _dft1_body, grid=(P, N2 // SUB_BLOCK, C // tc),
        in_specs=[pl.BlockSpec((R, K), lambda p, s, c: (0, 0)),
                  pl.BlockSpec((1, K, SUB_BLOCK, tc), lambda p, s, c: (p, 0, s, c))],
        out_specs=pl.BlockSpec((1, R, SUB_BLOCK, tc), lambda p, s, c: (p, 0, s, c)),
        out_shape=jax.ShapeDtypeStruct((P, R, N2, C), out_dtype),
        sem=("parallel", "parallel", "parallel"), name=name)(fm, x4)


def _idft1_gate_body(f_ref, *refs, chain):
    if chain:
        fn_ref, g_ref, z_ref, x1_ref, sk_ref, o_ref, a_ref = refs
    else:
        g_ref, z_ref, x1_ref, sk_ref, o_ref = refs
    gt = pltpu.einshape("ksc->skc", g_ref[0])
    y = jnp.stack([jnp.dot(f_ref[...], gt[k], preferred_element_type=F32) for k in range(gt.shape[0])], axis=0)
    y = pltpu.einshape("skc->ksc", y)
    z1 = x1_ref[0] * (y + sk_ref[...] * z_ref[0])
    o_ref[0] = z1.astype(o_ref.dtype)
    if chain:
        zt = pltpu.einshape("ksc->skc", z1)
        r = jnp.stack([jnp.dot(fn_ref[...], zt[k].astype(BF16), preferred_element_type=F32)
                       for k in range(zt.shape[0])], axis=0)
        a_ref[0] = pltpu.einshape("skc->ksc", r).astype(a_ref.dtype)


def _filter_spectrum_2stage(k, N1, N2):
    N, OC = k.shape
    _, f1_real, m_fwd, _, _ = _fft_tables(N1, N2)
    a = _outer_dft(f1_real, k.reshape(1, N1, N2, OC), OC, BF16, "filter_dft1").reshape(2, N1, N2, OC)
    FB = _tile(N1, (8, 4, 2, 1))
    tc = _tile(OC, (512, 256, 128))
    return _call(
        functools.partial(_fftfwd_body, FB=FB, N2=N2), grid=(N1 // FB, OC // tc),
        in_specs=[pl.BlockSpec((2, FB, N2, tc), lambda f, c: (0, f, 0, c)),
                  pl.BlockSpec((FB, 2 * N2, 2 * N2), lambda f, c: (f, 0, 0))],
        out_specs=pl.BlockSpec((2, FB, N2, tc), lambda f, c: (0, f, 0, c)),
        out_shape=jax.ShapeDtypeStruct((2, N1, N2, OC), BF16),
        sem=("parallel", "parallel"), name="filter_dft2")(a, m_fwd)


def _conv_mid(a, kf, order, N1, N2):
    P, _, N2_, C = a.shape
    _, _, m_fwd, m_inv, _ = _fft_tables(N1, N2)
    FB = _tile(N1, (8, 4, 2, 1))
    tc = _tile(C, (512, 256, 128))
    oc = order * (C // tc)
    g = _call(
        functools.partial(_fftmid_body, FB=FB, N2=N2), grid=(N1 // FB, C // tc, P),
        in_specs=[pl.BlockSpec((1, 2, FB, N2, tc), lambda f, c, p: (p, 0, f, 0, c)),
                  pl.BlockSpec((FB, 2 * N2, 2 * N2), lambda f, c, p: (f, 0, 0)),
                  pl.BlockSpec((FB, 2 * N2, 2 * N2), lambda f, c, p: (f, 0, 0)),
                  pl.BlockSpec((2, FB, N2, tc), lambda f, c, p: (0, f, 0, oc + c))],
        out_specs=pl.BlockSpec((1, 2, FB, N2, tc), lambda f, c, p: (p, 0, f, 0, c)),
        out_shape=jax.ShapeDtypeStruct((P, 2, N1, N2, C), BF16),
        sem=("parallel", "parallel", "arbitrary"), name="conv_dft2_mul_idft2")(
            a.reshape(P, 2, N1, N2, C), m_fwd, m_inv, kf)
    return g.reshape(P, 2 * N1, N2, C)


def _conv_finish(g, z4, zoff, x14, xoff, skip, out_dtype, N1, N2, chain):
    P, K = z4.shape[:2]
    C = g.shape[3]
    f1_pair, _, _, _, f1_inv = _fft_tables(N1, N2)
    tc = _tile(C, (512, 256, 128))
    blk = lambda rows, off=0: pl.BlockSpec((1, rows, SUB_BLOCK, tc), lambda p, s, c: (p, 0, s, off // tc + c))
    tab = lambda t: pl.BlockSpec(t.shape, lambda p, s, c: (0, 0))
    tables = [f1_inv, f1_pair] if chain else [f1_inv]
    out_specs = [blk(K), blk(2 * N1)] if chain else blk(K)
    z_sds = jax.ShapeDtypeStruct((P, K, N2, C), out_dtype)
    out_shape = [z_sds, jax.ShapeDtypeStruct((P, 2 * N1, N2, C), BF16)] if chain else z_sds
    return _call(
        functools.partial(_idft1_gate_body, chain=chain), grid=(P, N2 // SUB_BLOCK, C // tc),
        in_specs=[tab(t) for t in tables] + [blk(2 * N1), blk(K, zoff), blk(K, xoff),
                                             pl.BlockSpec((1, tc), lambda p, s, c: (0, c))],
        out_specs=out_specs, out_shape=out_shape,
        sem=("parallel", "parallel", "parallel"), name="conv_idft1_gate_dft1" if chain else "conv_idft1_gate")(
            *tables, g, z4, x14, skip[None, :])


def _hyena_long_convs_2stage(hy, skip, kf, N1, N2):
    B, n, C3 = hy.shape
    C = C3 // 3
    assert B % 2 == 0
    P, S1 = B // 2, N1 // 2
    f1_pair = _fft_tables(N1, N2)[0]
    hy4 = hy.reshape(P, 2 * S1, N2, C3)
    a0 = _outer_dft(f1_pair, hy4, C, BF16, "conv_dft1")
    z1, a1 = _conv_finish(_conv_mid(a0, kf, 0, N1, N2), hy4, 0, hy4, C, skip[0], F32, N1, N2, True)
    y = _conv_finish(_conv_mid(a1, kf, 1, N1, N2), z1, 0, hy4, 2 * C, skip[1], BF16, N1, N2, False)
    return y.reshape(B, n, C)


@functools.lru_cache(maxsize=None)
def _dft_tables(n):
    N = 2 * n
    f = np.arange(N)
    ang = -2.0 * np.pi * ((f[:, None] * f[None, :]) % N) / N
    fr, fi = np.cos(ang), np.sin(ang)
    fwd_full = np.concatenate([fr, fi], axis=0)
    fwd_half = fwd_full[:, :n]
    inv = np.concatenate([fr[:n, :], fi[:n, :]], axis=1) / N
    cvt = lambda a: jnp.asarray(a, dtype=BF16)
    return cvt(fwd_full), cvt(fwd_half), cvt(inv)


def _dftconv_body(z_ref, x1_ref, sk_ref, f_ref, fi_ref, k_ref, o_ref, *, N):
    z = z_ref[0]
    x = jnp.dot(f_ref[...], z.astype(BF16), preferred_element_type=F32)
    yr, yi = _cmul(x[:N], x[N:], k_ref[0], k_ref[1])
    y = jnp.concatenate([yr, yi], axis=0).astype(BF16)
    y = jnp.dot(fi_ref[...], y, preferred_element_type=F32)
    o_ref[0] = (x1_ref[0] * (y + sk_ref[...] * z)).astype(o_ref.dtype)


def _long_conv_gate_dense(z, zoff, x1, xoff, C, skip, kf, order, out_dtype):
    B, n, _ = z.shape
    N = 2 * n
    _, fwd_half, inv = _dft_tables(n)
    tc = _tile(C, (256, 128))
    oc = order * (C // tc)
    return _call(
        functools.partial(_dftconv_body, N=N), grid=(C // tc, B),
        in_specs=[pl.BlockSpec((1, n, tc), lambda c, b: (b, 0, zoff // tc + c)),
                  pl.BlockSpec((1, n, tc), lambda c, b: (b, 0, xoff // tc + c)),
                  pl.BlockSpec((1, tc), lambda c, b: (0, c)),
                  pl.BlockSpec((2 * N, n), lambda c, b: (0, 0)),
                  pl.BlockSpec((n, 2 * N), lambda c, b: (0, 0)),
                  pl.BlockSpec((2, N, tc), lambda c, b: (0, 0, oc + c))],
        out_specs=pl.BlockSpec((1, n, tc), lambda c, b: (b, 0, c)),
        out_shape=jax.ShapeDtypeStruct((B, n, C), out_dtype),
        sem=("parallel", "arbitrary"), name="conv_dense_dft")(z, x1, skip[None, :], fwd_half, inv, kf)


def _hyena_mixer(hy, w1, b1, freq, w2, b2, w3, skip):
    B, n, C3 = hy.shape
    C = C3 // 3
    k = _hyena_filters(n, w1, b1, freq, w2, b2, w3)
    N = 2 * n
    if N % FFT_N2 == 0 and (N // FFT_N2) >= 16:
        N1 = N // FFT_N2
        kf = _filter_spectrum_2stage(k, N1, FFT_N2)
        return _hyena_long_convs_2stage(hy, skip, kf, N1, FFT_N2)
    fwd_full, _, _ = _dft_tables(n)
    kf = _left_matmul(fwd_full, k[None], F32, "filter_dense_dft").reshape(2, N, k.shape[1])
    z = _long_conv_gate_dense(hy, 0, hy, C, C, skip[0], kf, 0, F32)
    return _long_conv_gate_dense(z, 0, hy, 2 * C, C, skip[1], kf, 1, BF16)


def _nt(a, b):
    return lax.dot_general(a, b, (((1,), (1,)), ((), ())), preferred_element_type=F32)


def _na_body(q_ref, k_ref, v_ref, kc_ref, vc_ref, b_ref, o_ref, *, R, KR, rows, HPS, scale):
    j = pl.program_id(2)
    start = pl.multiple_of(jnp.clip(j * R - NA_WIN_R // 2, 0, rows - KR) * GRID_W, GRID_W)
    for h in range(HPS):
        hs = slice(h * HEAD_DIM, (h + 1) * HEAD_DIM)
        q = q_ref[0, :, hs]
        kw = k_ref[0, pl.ds(start, KR * GRID_W), hs]
        vw = v_ref[0, pl.ds(start, KR * GRID_W), hs]
        s = _nt(q, kw) + b_ref[h, 0]
        sc = _nt(q, kc_ref[0, :, hs])
        m = jnp.maximum(jnp.max(s, axis=-1, keepdims=True), jnp.max(sc, axis=-1, keepdims=True))
        p = jnp.exp2((s - m) * (scale * LOG2_E))
        pc = jnp.exp2((sc - m) * (scale * LOG2_E))
        l = jnp.sum(p, axis=-1, keepdims=True) + jnp.sum(pc, axis=-1, keepdims=True)
        o = jnp.dot(p.astype(BF16), vw, preferred_element_type=F32)
        o = o + jnp.dot(pc.astype(BF16), vc_ref[0, :, hs], preferred_element_type=F32)
        o_ref[0, :, hs] = (o / l).astype(o_ref.dtype)


def _na_geometry(S):
    rows = S // GRID_W
    kr = min(NA_WIN_R, rows)
    R = min(8, rows)
    KR = min(rows, R + kr)
    nb = rows // R
    assert rows % R == 0
    types = sorted({0, min(1, nb - 1), nb - 1})
    if nb > 3:
        offs = {int(np.clip(j * R - NA_WIN_R // 2, 0, rows - KR)) - j * R for j in range(1, nb - 1)}
        assert len(offs) == 1
    return rows, kr, R, KR, nb, types


def _nabias_body(rpb_ref, o_ref, tw_ref, *, plan, R, KR, inv_scale):
    W = GRID_W
    nd_r, nd_c = 2 * NA_WIN_R - 1, 2 * NA_WIN_C - 1
    base = pl.program_id(0) * (nd_r * nd_c)
    qc = lax.broadcasted_iota(jnp.int32, (W, 2 * W), 0)
    lane = lax.broadcasted_iota(jnp.int32, (W, 2 * W), 1)
    kc = lane % W
    cs = jnp.clip(qc - NA_WIN_C // 2, 0, W - NA_WIN_C)
    col_ok = jnp.logical_and(kc >= cs, kc < cs + NA_WIN_C)
    dcm = kc - qc + (NA_WIN_C - 1)
    neg = jnp.full((W, 2 * W), NEG_INF, F32)
    for dr in range(nd_r):
        acc = neg
        for dc in range(nd_c):
            acc = jnp.where(dcm == dc, rpb_ref[base + dr * nd_c + dc] * inv_scale, acc)
        tw_ref[dr] = jnp.where(col_ok, acc, NEG_INF)
    left = lane < W
    for t, per_q in enumerate(plan):
        for qr in range(R):
            for kp in range(KR // 2):
                d0, d1 = per_q[qr][kp]
                a = neg if d0 is None else tw_ref[d0]
                b = neg if d1 is None else tw_ref[d1]
                blk = neg if (d0 is None and d1 is None) else jnp.where(left, a, b)
                o_ref[0, t, qr * W:(qr + 1) * W, kp * 2 * W:(kp + 1) * 2 * W] = blk


def _na_bias_tables(rpb, S):
    rows, kr, R, KR, nb, types = _na_geometry(S)
    assert 2 * GRID_W == LANES and KR % 2 == 0
    plan = []
    for jt in types:
        start = int(np.clip(jt * R - NA_WIN_R // 2, 0, rows - KR))
        per_q = []
        for q in range(R):
            qra = jt * R + q
            ws = int(np.clip(qra - kr // 2, 0, rows - kr))
            d = [(start + k) - qra + (NA_WIN_R - 1) if ws <= start + k < ws + kr else None for k in range(KR)]
            per_q.append([(d[2 * p], d[2 * p + 1]) for p in range(KR // 2)])
        plan.append(per_q)
    L, H, nd_r, nd_c = rpb.shape
    T, QB, KB = len(types), R * GRID_W, KR * GRID_W
    return _call(
        functools.partial(_nabias_body, plan=plan, R=R, KR=KR, inv_scale=HEAD_DIM ** 0.5), grid=(L * H,),
        in_specs=[pl.BlockSpec(memory_space=pltpu.SMEM)],
        out_specs=pl.BlockSpec((1, T, QB, KB), lambda i: (i, 0, 0, 0)),
        out_shape=jax.ShapeDtypeStruct((L * H, T, QB, KB), F32),
        scratch=[pltpu.VMEM((nd_r, GRID_W, 2 * GRID_W), F32)],
        sem=("parallel",), name="na_bias_table")(rpb.reshape(-1).astype(F32))


def _na_attention(qkv, qkv_c, bias, l, offs, H):
    B, S, _ = qkv.shape
    CTX = qkv_c.shape[1]
    rows, kr, R, KR, nb, types = _na_geometry(S)
    T = len(types)
    QB, KB = R * GRID_W, KR * GRID_W
    HPS = next(n for n in (4, 2, 1) if H % n == 0)
    HW = HPS * HEAD_DIM
    assert all(offs[n] % HW == 0 for n in ("na_k", "na_v", "na_q"))
    ok_, ov_, oq_ = (offs[n] // HW for n in ("na_k", "na_v", "na_q"))

    def btype(j):
        if T == nb:
            return j
        return jnp.where(j == 0, 0, jnp.where(j == nb - 1, T - 1, 1))

    return _call(
        functools.partial(_na_body, R=R, KR=KR, rows=rows, HPS=HPS, scale=HEAD_DIM ** -0.5),
        grid=(B, H // HPS, nb),
        in_specs=[pl.BlockSpec((1, QB, HW), lambda b, h, j: (b, j, oq_ + h)),
                  pl.BlockSpec((1, S, HW), lambda b, h, j: (b, 0, ok_ + h)),
                  pl.BlockSpec((1, S, HW), lambda b, h, j: (b, 0, ov_ + h)),
                  pl.BlockSpec((1, CTX, HW), lambda b, h, j: (b, 0, ok_ + h)),
                  pl.BlockSpec((1, CTX, HW), lambda b, h, j: (b, 0, ov_ + h)),
                  pl.BlockSpec((HPS, 1, QB, KB), lambda b, h, j: (l * (H // HPS) + h, btype(j), 0, 0))],
        out_specs=pl.BlockSpec((1, QB, HW), lambda b, h, j: (b, j, h)),
        out_shape=jax.ShapeDtypeStruct((B, S, H * HEAD_DIM), BF16),
        sem=("parallel", "parallel", "arbitrary"), name="na_attention")(qkv, qkv, qkv, qkv_c, qkv_c, bias)


def _stack_heads(q2, G):
    return jnp.concatenate([q2[:, g * HEAD_DIM:(g + 1) * HEAD_DIM] for g in range(G)], axis=0)


def _unstack_heads(o, G, n):
    return jnp.concatenate([o[g * n:(g + 1) * n] for g in range(G)], axis=1)


def _sink_column(sink_ref, h0, G, n):
    return jnp.concatenate([jnp.full((n, 1), sink_ref[h0 + g], F32) for g in range(G)], axis=0)


def _swa_body(sink_ref, q_ref, k_ref, v_ref, kc_ref, vc_ref, o_ref, *, QB, KB, S, G, HPS, QPS, scale):
    hb = pl.program_id(1)
    for qb in range(QPS):
        j = pl.program_id(2) * QPS + qb
        rows = slice(qb * QB, (qb + 1) * QB)
        start = pl.multiple_of(jnp.clip(j * QB - GQA_WINDOW, 0, S - KB), LANES)
        qpos = j * QB + lax.broadcasted_iota(jnp.int32, (QB, KB), 0)
        kpos = start + lax.broadcasted_iota(jnp.int32, (QB, KB), 1)
        mask = jnp.where(jnp.abs(qpos - kpos) <= GQA_WINDOW, 0.0, NEG_INF)
        mask = jnp.concatenate([mask] * G, axis=0)
        for h in range(HPS):
            hs = slice(h * HEAD_DIM, (h + 1) * HEAD_DIM)
            qs = slice(h * G * HEAD_DIM, (h + 1) * G * HEAD_DIM)
            q = _stack_heads(q_ref[0, rows, qs], G)
            kw = k_ref[0, pl.ds(start, KB), hs]
            vw = v_ref[0, pl.ds(start, KB), hs]
            s = _nt(q, kw) + mask
            sc = _nt(q, kc_ref[0, :, hs])
            sk = _sink_column(sink_ref, (hb * HPS + h) * G, G, QB) * (1.0 / scale)
            m = jnp.maximum(jnp.maximum(jnp.max(s, axis=-1, keepdims=True), jnp.max(sc, axis=-1, keepdims=True)), sk)
            p = jnp.exp2((s - m) * (scale * LOG2_E))
            pc = jnp.exp2((sc - m) * (scale * LOG2_E))
            l = (jnp.sum(p, axis=-1, keepdims=True) + jnp.sum(pc, axis=-1, keepdims=True)
                 + jnp.exp2((sk - m) * (scale * LOG2_E)))
            o = jnp.dot(p.astype(BF16), vw, preferred_element_type=F32)
            o = o + jnp.dot(pc.astype(BF16), vc_ref[0, :, hs], preferred_element_type=F32)
            o_ref[0, rows, qs] = _unstack_heads(o / l, G, QB).astype(o_ref.dtype)


def _swa_attention(qkv, qkv_c, sink, offs, KVH):
    B, S, _ = qkv.shape
    CTX = qkv_c.shape[1]
    G = GQA_GROUP
    QB = _tile(S, (512, 256, 128))
    KB = min(S, QB + 2 * GQA_WINDOW)
    HPS = 2 if KVH % 2 == 0 else 1
    QPS = 2 if (S // QB) % 2 == 0 else 1
    HW = HPS * HEAD_DIM
    assert offs["sw_k"] % HW == 0 and offs["sw_v"] % HW == 0 and offs["sw_q"] % (G * HW) == 0
    ok_, ov_, oq_ = offs["sw_k"] // HW, offs["sw_v"] // HW, offs["sw_q"] // (G * HW)
    return _call(
        functools.partial(_swa_body, QB=QB, KB=KB, S=S, G=G, HPS=HPS, QPS=QPS, scale=HEAD_DIM ** -0.5),
        grid=(B, KVH // HPS, S // (QB * QPS)),
        in_specs=[pl.BlockSpec(memory_space=pltpu.SMEM),
                  pl.BlockSpec((1, QPS * QB, G * HW), lambda b, h, j: (b, j, oq_ + h)),
                  pl.BlockSpec((1, S, HW), lambda b, h, j: (b, 0, ok_ + h)),
                  pl.BlockSpec((1, S, HW), lambda b, h, j: (b, 0, ov_ + h)),
                  pl.BlockSpec((1, CTX, HW), lambda b, h, j: (b, 0, ok_ + h)),
                  pl.BlockSpec((1, CTX, HW), lambda b, h, j: (b, 0, ov_ + h))],
        out_specs=pl.BlockSpec((1, QPS * QB, G * HW), lambda b, h, j: (b, j, h)),
        out_shape=jax.ShapeDtypeStruct((B, S, KVH * G * HEAD_DIM), BF16),
        sem=("parallel", "parallel", "arbitrary"), name="swa_attention")(sink, qkv, qkv, qkv, qkv_c, qkv_c)


def _cattn_body(sink_ref, q_ref, k_ref, v_ref, o_ref, *, G, use_sink, scale):
    h = pl.program_id(1)
    n = q_ref.shape[1]
    q = _stack_heads(q_ref[0], G)
    s = _nt(q, k_ref[0]) * scale
    m = jnp.max(s, axis=-1, keepdims=True)
    if use_sink:
        sk = _sink_column(sink_ref, h * G, G, n)
        m = jnp.maximum(m, sk)
    p = jnp.exp(s - m)
    l = jnp.sum(p, axis=-1, keepdims=True)
    if use_sink:
        l = l + jnp.exp(sk - m)
    o = jnp.dot(p.astype(BF16), v_ref[0], preferred_element_type=F32)
    o_ref[0] = _unstack_heads(o / l, G, n).astype(o_ref.dtype)


def _ctx_attention(qkv_c, sink, oq, ok, ov, KVH, G, use_sink):
    B, n, _ = qkv_c.shape
    oq_, ok_, ov_ = oq // (G * HEAD_DIM), ok // HEAD_DIM, ov // HEAD_DIM
    return _call(
        functools.partial(_cattn_body, G=G, use_sink=use_sink, scale=HEAD_DIM ** -0.5),
        grid=(B, KVH),
        in_specs=[pl.BlockSpec(memory_space=pltpu.SMEM),
                  pl.BlockSpec((1, n, G * HEAD_DIM), lambda b, h: (b, 0, oq_ + h)),
                  pl.BlockSpec((1, n, HEAD_DIM), lambda b, h: (b, 0, ok_ + h)),
                  pl.BlockSpec((1, n, HEAD_DIM), lambda b, h: (b, 0, ov_ + h))],
        out_specs=pl.BlockSpec((1, n, G * HEAD_DIM), lambda b, h: (b, 0, h)),
        out_shape=jax.ShapeDtypeStruct((B, n, KVH * G * HEAD_DIM), BF16),
        sem=("parallel", "parallel"), name="ctx_attention")(sink, qkv_c, qkv_c, qkv_c)


def _merge_body(x_ref, g_ref, sh_ref, sc_ref, yh, yn, ys, wgh, wgn, wgs, wh, wn, ws, o_ref, h_ref, *, eps):
    @pl.when(pl.program_id(2) == 0)
    def _():
        h_ref[...] = _norm_modulate(x_ref[0], g_ref, sh_ref, sc_ref, eps).astype(BF16)

    h = h_ref[...]
    gate = lambda wg: _sigmoid(jnp.dot(h, wg[...], preferred_element_type=F32))
    m = gate(wgh) * jnp.dot(yh[0], wh[...], preferred_element_type=F32)
    m = m + gate(wgn) * jnp.dot(yn[0], wn[...], preferred_element_type=F32)
    m = m + gate(wgs) * jnp.dot(ys[0], ws[...], preferred_element_type=F32)
    o_ref[0] = m.astype(o_ref.dtype)


def _merge_branches(x, gain, shift, scale, w_gate, y_hy, y_na, y_sw, w_br, l):
    B, S, D = x.shape
    widths = (y_hy.shape[2], y_na.shape[2], y_sw.shape[2])
    starts = (0, widths[0], widths[0] + widths[1])
    assert all(s % w == 0 for s, w in zip(starts, widths))
    tm = _tile(S, (512, 256, 128))
    tn = _tile(D, (512, 256, 128))
    nj = D // tn
    yspec = lambda y: pl.BlockSpec((1, tm, y.shape[2]), lambda b, i, j: (b, i, 0))
    gspec = lambda k: pl.BlockSpec((None, D, tn), lambda b, i, j: (l, 0, k * nj + j))
    wspec = lambda k: pl.BlockSpec((None, widths[k], tn), lambda b, i, j: (l, starts[k] // widths[k], j))
    return _call(
        functools.partial(_merge_body, eps=NORM_EPS), grid=(B, S // tm, nj),
        in_specs=[pl.BlockSpec((1, tm, D), lambda b, i, j: (b, i, 0)),
                  pl.BlockSpec((1, D), lambda b, i, j: (0, 0)),
                  pl.BlockSpec((1, 1, D), lambda b, i, j: (b, 0, 0)),
                  pl.BlockSpec((1, 1, D), lambda b, i, j: (b, 0, 0)),
                  yspec(y_hy), yspec(y_na), yspec(y_sw), gspec(0), gspec(1), gspec(2),
                  wspec(0), wspec(1), wspec(2)],
        out_specs=pl.BlockSpec((1, tm, tn), lambda b, i, j: (b, i, j)),
        out_shape=jax.ShapeDtypeStruct((B, S, D), BF16),
        scratch=[pltpu.VMEM((tm, D), BF16)],
        sem=("parallel", "parallel", "arbitrary"), name="gated_merge")(
            x, gain.reshape(1, D), shift, scale, y_hy, y_na, y_sw, w_gate, w_gate, w_gate, w_br, w_br, w_br)


def _mmres_body(a_ref, w_ref, x_ref, g_ref, mg_ref, o_ref, *, nk, eps):
    k = pl.program_id(2)
    part = lambda: jnp.dot(a_ref[0], w_ref[...], preferred_element_type=F32)

    def finish():
        y = o_ref[0]
        yn = y * lax.rsqrt(jnp.mean(y * y, axis=-1, keepdims=True) + eps) * g_ref[...]
        o_ref[0] = x_ref[0] + mg_ref[0] * yn

    if nk == 1:
        o_ref[0] = part()
        finish()
        return

    @pl.when(k == 0)
    def _():
        o_ref[0] = part()

    @pl.when(k > 0)
    def _():
        o_ref[0] += part()

    @pl.when(k == nk - 1)
    def _():
        finish()


def _matmul_norm_residual(a, w, l, x, gain, mgate, name):
    B, S, K = a.shape
    D = w.shape[2]
    if K <= 2048:
        tm, tk = _tile(S, (512, 256, 128)), K
    else:
        tm = _tile(S, (1024, 512, 256, 128))
        tk = max(t for t in range(LANES, 1537, LANES) if K % t == 0)
    nk = K // tk
    est = 2 * (tm * tk * 2 + tk * D * 2 + 2 * tm * D * 4)
    return _call(
        functools.partial(_mmres_body, nk=nk, eps=NORM_EPS), grid=(B, S // tm, nk),
        vmem=VMEM_LIMIT_LARGE if est > VMEM_LIMIT - VMEM_TEMP_RESERVE else VMEM_LIMIT,
        in_specs=[pl.BlockSpec((1, tm, tk), lambda b, i, k: (b, i, k)),
                  pl.BlockSpec((None, tk, D), lambda b, i, k: (l, k, 0)),
                  pl.BlockSpec((1, tm, D), lambda b, i, k: (b, i, 0)),
                  pl.BlockSpec((1, D), lambda b, i, k: (0, 0)),
                  pl.BlockSpec((1, 1, D), lambda b, i, k: (b, 0, 0))],
        out_specs=pl.BlockSpec((1, tm, D), lambda b, i, k: (b, i, 0)),
        out_shape=jax.ShapeDtypeStruct((B, S, D), F32),
        sem=("parallel", "parallel", "arbitrary"), name=name)(a, w, x, gain.reshape(1, D), mgate)


def _rope_tables(n):
    t = jnp.arange(n)
    row = (t // GRID_W).astype(F32)
    col = (t % GRID_W).astype(F32)
    per_axis = HEAD_DIM // 2
    inv = ROPE_BASE ** (-jnp.arange(0, per_axis, 2, dtype=F32) / per_axis)
    ar, ac = row[:, None] * inv, col[:, None] * inv
    cos_t = jnp.concatenate([jnp.cos(ar), jnp.cos(ar), jnp.cos(ac), jnp.cos(ac)], axis=1)
    sin_t = jnp.concatenate([-jnp.sin(ar), jnp.sin(ar), -jnp.sin(ac), jnp.sin(ac)], axis=1)
    return cos_t, sin_t


def kernel(x, c, ctx, c_ctx, w_mod, b_mod, norm_gains, w_in, hy_conv_w, hy_conv_b, hy_w1, hy_b1, hy_freq, hy_w2, hy_b2, hy_w3, hy_skip, na_rpb, swa_sink, w_branch, w_out, ffn_w_up, ffn_conv_w, ffn_conv_b, ffn_w_down):
    B, S, D = x.shape
    L = w_mod.shape[0]
    C = hy_skip.shape[-1]
    H_na = na_rpb.shape[1]
    H_q = swa_sink.shape[1]
    KVH = H_q // GQA_GROUP
    NA_W, QW, KVW = H_na * HEAD_DIM, H_q * HEAD_DIM, KVH * HEAD_DIM
    KV_COLS = 2 * NA_W + 2 * KVW
    qkv_segs = [(0, KV_COLS), (KV_COLS + 3 * C, NA_W + QW)]
    gate_start = KV_COLS + 3 * C + NA_W + QW
    offs = {"na_k": 0, "na_v": NA_W, "sw_k": 2 * NA_W, "sw_v": 2 * NA_W + KVW,
            "na_q": KV_COLS, "sw_q": KV_COLS + NA_W}
    rope_chunks = (list(range(offs["sw_k"] // LANES, (offs["sw_k"] + KVW) // LANES))
                   + list(range(offs["sw_q"] // LANES, (offs["sw_q"] + QW) // LANES)))
    cos_t, sin_t = _rope_tables(S)

    w_out_b, w_br_b, w_dn_b = w_out.astype(BF16), w_branch.astype(BF16), ffn_w_down.astype(BF16)
    w_gate_b = w_in[:, :, gate_start:].astype(BF16)
    na_bias = _na_bias_tables(na_rpb, S)
    CTX = ctx.shape[1]
    flat = lambda t: t.reshape(1, -1, t.shape[-1])
    unflat = lambda t: t.reshape(B, CTX, t.shape[-1])

    R = -(-(B + 1) // 8) * 8
    cc = jnp.concatenate([c, c_ctx[None, :], jnp.zeros((R - B - 1, D), F32)], axis=0)
    mods = _modulation(cc, w_mod, b_mod)

    xc = ctx
    for l in range(L):
        mod = [mods[l, :B, k * D:(k + 1) * D].reshape(B, 1, D) for k in range(6)]
        mod_c = [mods[l, B, k * D:(k + 1) * D].reshape(1, 1, D) for k in range(6)]
        g = norm_gains[l]
        hy_p = (hy_w1[l], hy_b1[l], hy_freq[l], hy_w2[l], hy_b2[l], hy_w3[l], hy_skip[l])

        qkv_c, hy_c = map(unflat, _in_proj(flat(xc), g[0], mod_c[0], mod_c[1], w_in, l, qkv_segs, KV_COLS, C,
                                           hy_conv_w[l], hy_conv_b[l], None, "ctx_in_proj", period=CTX))
        qkv, hy = _in_proj(x, g[0], mod[0], mod[1], w_in, l, qkv_segs, KV_COLS, C,
                           hy_conv_w[l], hy_conv_b[l], (cos_t, sin_t, rope_chunks), "in_proj")
        y_hy = _hyena_mixer(hy, *hy_p)
        y_na = _na_attention(qkv, qkv_c, na_bias, l, offs, H_na)
        y_sw = _swa_attention(qkv, qkv_c, swa_sink[l], offs, KVH)
        m = _merge_branches(x, g[0], mod[0], mod[1], w_gate_b, y_hy, y_na, y_sw, w_br_b, l)
        x = _matmul_norm_residual(m, w_out_b, l, x, g[1], mod[2], "out_proj_residual")
        gl = _ffn_up_glu(x, g[2], mod[3], mod[4], ffn_w_up, l, ffn_conv_w[l], ffn_conv_b[l])
        x = _matmul_norm_residual(gl, w_dn_b, l, x, g[3], mod[5], "ffn_down_residual")

        if l < L - 1:
            yc_hy = _hyena_mixer(hy_c, *hy_p)
            yc_na = _ctx_attention(qkv_c, swa_sink[l], offs["na_q"], offs["na_k"], offs["na_v"], H_na, 1, False)
            yc_sw = _ctx_attention(qkv_c, swa_sink[l], offs["sw_q"], offs["sw_k"], offs["sw_v"], KVH, GQA_GROUP, True)
            mc = _merge_branches(flat(xc), g[0], mod_c[0], mod_c[1], w_gate_b,
                                 flat(yc_hy), flat(yc_na), flat(yc_sw), w_br_b, l)
            xcf = _matmul_norm_residual(mc, w_out_b, l, flat(xc), g[1], mod_c[2], "ctx_out_proj_residual")
            gl_c = _ffn_up_glu(xcf, g[2], mod_c[3], mod_c[4], ffn_w_up, l, ffn_conv_w[l], ffn_conv_b[l], period=CTX)
            xc = unflat(_matmul_norm_residual(gl_c, w_dn_b, l, xcf, g[3], mod_c[5], "ctx_ffn_down_residual"))
    return x
```
